```python
import math
import jax, jax.numpy as jnp
from jax import lax
import numpy as np

D_MODEL = 1024
BATCH = 16
SEQ = 2048
DEPTH = 2
DEC_BATCH = 8
DEC_SEQ = 16
PAST_LEN = 1024

CHUNK = 64
N_EVEN = (DEPTH + 1) // 2
N_ODD = DEPTH // 2
S5_WIDTH = D_MODEL // 2
S5_GROUP = 16
S5_GROUPS = S5_WIDTH // S5_GROUP
S5_STATE = 64
GDN_HEADS = 4
GDN_DK = 128
GDN_DV = 128
GDN_CONV = 4
GDN_QKV = GDN_HEADS * (2 * GDN_DK + GDN_DV)
RET_HEADS = 4
RET_DK = D_MODEL // RET_HEADS
RET_DV = 2 * D_MODEL // RET_HEADS
ROPE_BASE = 10000.0
N_EXPERTS = 32
TOP_K = 4
D_FF = D_MODEL
SWIGLU_LIMIT = 7.0
SWIGLU_ALPHA = 1.702
EXPERT_BLOCK = 256
PLE_DIM = 256
DEEPNORM_ALPHA = (2 * DEPTH) ** 0.25
DEEPNORM_BETA = (8 * DEPTH) ** -0.25
LN_EPS = 1e-5
NORM_EPS = 1e-6
EVEN_IN = S5_WIDTH + GDN_QKV + GDN_HEADS * GDN_DV + 2 * GDN_HEADS
EVEN_MIX = S5_WIDTH + GDN_HEADS * GDN_DV
ODD_IN = 2 * RET_HEADS * RET_DK + 2 * RET_HEADS * RET_DV

kernel_name = 'hybrid_s5_gdn_retention_moe_stream_step'

F32 = jnp.float32


def layer_norm(x, g, b):
    xf = x.astype(F32)
    mu = jnp.mean(xf, -1, keepdims=True)
    var = jnp.mean(jnp.square(xf - mu), -1, keepdims=True)
    return (xf - mu) * lax.rsqrt(var + LN_EPS) * g + b


def l2norm(x):
    return x * lax.rsqrt(jnp.sum(x * x, -1, keepdims=True) + NORM_EPS)


def to_chunks(x, lc):
    return x.reshape(x.shape[0], x.shape[1] // lc, lc, *x.shape[2:])


def complex_affine(e1, e2):
    a1r, a1i, b1r, b1i = e1
    a2r, a2i, b2r, b2i = e2
    return (a2r * a1r - a2i * a1i, a2r * a1i + a2i * a1r,
            a2r * b1r - a2i * b1i + b2r, a2r * b1i + a2i * b1r + b2i)


def s5_scan(u, h0_re, h0_im, a_re, a_im, log_dt, b_re, b_im, c_re, c_im, d_skip, lc):
    bsz, L, G, N = u.shape
    dt = jnp.exp(log_dt)[:, None]
    lr, li = a_re * dt, a_im * dt
    mag = jnp.exp(lr)
    ab_re, ab_im = mag * jnp.cos(li), mag * jnp.sin(li)
    den = a_re * a_re + a_im * a_im
    cf_re = ((ab_re - 1.0) * a_re + ab_im * a_im) / den
    cf_im = (ab_im * a_re - (ab_re - 1.0) * a_im) / den
    bb_re = cf_re[..., None] * b_re - cf_im[..., None] * b_im
    bb_im = cf_re[..., None] * b_im + cf_im[..., None] * b_re
    t = jnp.arange(1, lc + 1, dtype=F32)[:, None, None]
    pw_mag = jnp.exp(lr[None] * t)
    pw_re, pw_im = pw_mag * jnp.cos(li[None] * t), pw_mag * jnp.sin(li[None] * t)
    uc = jnp.swapaxes(to_chunks(u, lc), 0, 1)

    def step(carry, uk):
        h_re, h_im = carry
        bu_re = jnp.einsum('blgn,gpn->blgp', uk, bb_re)
        bu_im = jnp.einsum('blgn,gpn->blgp', uk, bb_im)
        a_full_re = jnp.broadcast_to(ab_re, bu_re.shape)
        a_full_im = jnp.broadcast_to(ab_im, bu_re.shape)
        _, _, s_re, s_im = lax.associative_scan(complex_affine, (a_full_re, a_full_im, bu_re, bu_im), axis=1)
        s_re = s_re + pw_re * h_re[:, None] - pw_im * h_im[:, None]
        s_im = s_im + pw_re * h_im[:, None] + pw_im * h_re[:, None]
        y = jnp.einsum('gnp,blgp->blgn', c_re, s_re) - jnp.einsum('gnp,blgp->blgn', c_im, s_im)
        return (s_re[:, -1], s_im[:, -1]), y

    (h_re, h_im), ys = lax.scan(step, (h0_re, h0_im), uc)
    y = jnp.swapaxes(ys, 0, 1).reshape(bsz, L, G, N) + d_skip.reshape(G, N) * u
    return y, h_re, h_im


def causal_conv(x, ctx, w):
    xp = jnp.concatenate([ctx, x], axis=1)
    C = x.shape[-1]
    y = lax.conv_general_dilated(xp, w[:, None, :].astype(xp.dtype), window_strides=(1,), padding='VALID',
                                 dimension_numbers=('NWC', 'WIO', 'NWC'), feature_group_count=C)
    return y, xp[:, -(GDN_CONV - 1):]


def gated_delta(q, k, v, beta, g, s0, lc):
    bsz, L, H, dv = v.shape
    ch = lambda t: jnp.moveaxis(to_chunks(t, lc), 3, 2)
    qc, kc, vc, bc, gc = ch(q), ch(k), ch(v), ch(beta), ch(g)
    G = jnp.cumsum(gc, axis=-1)
    diff = G[..., :, None] - G[..., None, :]
    idx = jnp.arange(lc)
    incl = idx[:, None] >= idx[None, :]
    strict = idx[:, None] > idx[None, :]
    dec = jnp.where(incl, jnp.exp(jnp.where(incl, diff, 0.0)), 0.0)
    kb = kc * bc[..., None]
    tri = jnp.eye(lc, dtype=F32) + jnp.where(strict, jnp.einsum('bchid,bchjd->bchij', kb, kc) * dec, 0.0)
    rhs = jnp.concatenate([vc * bc[..., None], kb * jnp.exp(G)[..., None]], axis=-1)
    sol = lax.linalg.triangular_solve(tri, rhs, left_side=True, lower=True, unit_diagonal=True)
    u0, w = sol[..., :dv], sol[..., dv:]
    attn = jnp.einsum('bchid,bchjd->bchij', qc, kc) * dec
    q_dec = qc * jnp.exp(G)[..., None]
    g_last = G[..., -1]
    k_dec = kc * jnp.exp(g_last[..., None] - G)[..., None]
    xs = tuple(jnp.moveaxis(t, 1, 0) for t in (u0, w, attn, q_dec, k_dec, jnp.exp(g_last)))

    def step(S, inp):
        u0k, wk, ak, qk, kk, dk = inp
        u = u0k - jnp.einsum('bhik,bhkv->bhiv', wk, S)
        o = jnp.einsum('bhik,bhkv->bhiv', qk, S) + jnp.einsum('bhij,bhjv->bhiv', ak, u)
        S = dk[..., None, None] * S + jnp.einsum('bhik,bhiv->bhkv', kk, u)
        return S, o

    S, os_ = lax.scan(step, s0, xs)
    o = jnp.transpose(os_, (1, 0, 3, 2, 4)).reshape(bsz, L, H, dv)
    return o, S


def rotary(x, pos):
    d = x.shape[-1]
    freq = 1.0 / (ROPE_BASE ** jnp.linspace(0.0, 1.0, d // 2, dtype=F32))
    ang = pos[:, None] * freq[None]
    cos, sin = jnp.cos(ang)[None, :, None, :], jnp.sin(ang)[None, :, None, :]
    x2 = x.reshape(*x.shape[:-1], d // 2, 2)
    x0, x1 = x2[..., 0], x2[..., 1]
    return jnp.stack([x0 * cos - x1 * sin, x0 * sin + x1 * cos], axis=-1).reshape(x.shape)


def retention(q, k, v, r0, lc):
    bsz, L, H, dv = v.shape
    log_g = jnp.log(1.0 - 2.0 ** (-5.0 - jnp.arange(H, dtype=F32)))
    idx = jnp.arange(lc, dtype=F32)
    dec_intra = jnp.exp(log_g[:, None, None] * jnp.abs(idx[:, None] - idx[None, :]))
    q_scale = jnp.exp(log_g[:, None] * (idx + 1.0))
    k_scale = jnp.exp(log_g[:, None] * (lc - 1.0 - idx))
    c_dec = jnp.exp(log_g * lc)
    ch = lambda t: jnp.transpose(to_chunks(t, lc), (1, 0, 3, 2, 4))

    def step(R, inp):
        qk, kk, vk = inp
        s = jnp.einsum('bhid,bhjd->bhij', qk, kk) * dec_intra
        o = jnp.einsum('bhij,bhjv->bhiv', s, vk) + jnp.einsum('bhid,bhdv->bhiv', qk * q_scale[..., None], R)
        R = c_dec[:, None, None] * R + jnp.einsum('bhjd,bhjv->bhdv', kk * k_scale[..., None], vk)
        return R, o

    R, os_ = lax.scan(step, r0, (ch(q), ch(k), ch(v)))
    o = jnp.transpose(os_, (1, 0, 3, 2, 4)).reshape(bsz, L, H, dv)
    return o, R


def even_mixer(x, h_re, h_im, s_gdn, conv_ctx, lc, w_in, a_re, a_im, log_dt, b_re, b_im, c_re, c_im,
               d_skip, w_glu, b_glu, conv_w, a_log, dt_bias, norm_w, w_out):
    bsz, L, _ = x.shape
    proj = x @ w_in
    o1 = S5_WIDTH
    o2 = o1 + GDN_QKV
    o3 = o2 + GDN_HEADS * GDN_DV
    o4 = o3 + GDN_HEADS
    u, qkv, z, b_raw, a_raw = proj[..., :o1], proj[..., o1:o2], proj[..., o2:o3], proj[..., o3:o4], proj[..., o4:]
    yA, h_re, h_im = s5_scan(u.reshape(bsz, L, S5_GROUPS, S5_GROUP), h_re, h_im, a_re, a_im, log_dt,
                             b_re, b_im, c_re, c_im, d_skip, lc)
    yA = jax.nn.gelu(yA.reshape(bsz, L, S5_WIDTH))
    yA = yA * jax.nn.sigmoid(yA @ w_glu + b_glu)
    qkv, conv_new = causal_conv(qkv, conv_ctx, conv_w)
    qkv = jax.nn.silu(qkv)
    nq = GDN_HEADS * GDN_DK
    q = l2norm(qkv[..., :nq].reshape(bsz, L, GDN_HEADS, GDN_DK)) * (GDN_DK ** -0.5)
    k = l2norm(qkv[..., nq:2 * nq].reshape(bsz, L, GDN_HEADS, GDN_DK))
    v = qkv[..., 2 * nq:].reshape(bsz, L, GDN_HEADS, GDN_DV)
    beta = jax.nn.sigmoid(b_raw)
    g = -jnp.exp(a_log) * jax.nn.softplus(a_raw + dt_bias)
    o, S = gated_delta(q, k, v, beta, g, s_gdn, lc)
    o = o * lax.rsqrt(jnp.mean(o * o, -1, keepdims=True) + NORM_EPS) * norm_w
    o = o * jax.nn.silu(z.reshape(bsz, L, GDN_HEADS, GDN_DV))
    yB = o.reshape(bsz, L, GDN_HEADS * GDN_DV)
    out = jnp.concatenate([yA, yB], axis=-1) @ w_out
    return out, h_re, h_im, S, conv_new


def odd_mixer(x, r0, pos, lc, w_in, w_out):
    bsz, L, _ = x.shape
    proj = x @ w_in
    nk = RET_HEADS * RET_DK
    nv = RET_HEADS * RET_DV
    q = rotary(proj[..., :nk].reshape(bsz, L, RET_HEADS, RET_DK), pos)
    k = rotary(proj[..., nk:2 * nk].reshape(bsz, L, RET_HEADS, RET_DK), pos) * (RET_DK ** -0.5)
    v = proj[..., 2 * nk:2 * nk + nv].reshape(bsz, L, RET_HEADS, RET_DV)
    gate = proj[..., 2 * nk + nv:]
    o, R = retention(q, k, v, r0, lc)
    mu = jnp.mean(o, -1, keepdims=True)
    var = jnp.mean(jnp.square(o - mu), -1, keepdims=True)
    o = ((o - mu) * lax.rsqrt(var + LN_EPS)).reshape(bsz, L, nv)
    return (jax.nn.silu(gate) * o) @ w_out, R


def moe(x, router_w, router_b, w1, b1, w2, b2):
    bsz, L, D = x.shape
    xt = x.reshape(-1, D)
    T = xt.shape[0]
    logits = (xt @ router_w + router_b).astype(F32)
    top_v, top_i = lax.top_k(logits, TOP_K)
    gates = jax.nn.softmax(top_v, axis=-1)
    flat_e = top_i.reshape(-1)
    flat_g = gates.reshape(-1)
    order = jnp.argsort(flat_e)
    se = flat_e[order]
    counts = jnp.zeros((N_EXPERTS,), jnp.int32).at[flat_e].add(1)
    padded = (counts + EXPERT_BLOCK - 1) // EXPERT_BLOCK * EXPERT_BLOCK
    pend = jnp.cumsum(padded)
    pstart = pend - padded
    start = jnp.cumsum(counts) - counts
    dest = pstart[se] + (jnp.arange(T * TOP_K, dtype=jnp.int32) - start[se])
    n_pad = -(-(T * TOP_K + N_EXPERTS * (EXPERT_BLOCK - 1)) // EXPERT_BLOCK) * EXPERT_BLOCK
    n_blk = n_pad // EXPERT_BLOCK
    src = jnp.full((n_pad,), T, jnp.int32).at[dest].set((order // TOP_K).astype(jnp.int32))
    gate_pad = jnp.zeros((n_pad,), F32).at[dest].set(flat_g[order])
    blk_start = jnp.arange(n_blk, dtype=jnp.int32) * EXPERT_BLOCK
    blk_e = jnp.minimum(jnp.searchsorted(pend, blk_start, side='right'), N_EXPERTS - 1)
    x_ext = jnp.concatenate([xt, jnp.zeros((1, D), xt.dtype)], axis=0)

    def expert_block(args):
        rows, e = args
        h = x_ext[rows] @ w1[e] + b1[e]
        glu = jnp.minimum(h[:, :D_FF], SWIGLU_LIMIT)
        lin = jnp.clip(h[:, D_FF:], -SWIGLU_LIMIT, SWIGLU_LIMIT)
        return (glu * jax.nn.sigmoid(SWIGLU_ALPHA * glu) * (lin + 1.0)) @ w2[e] + b2[e]

    y_pad = lax.map(expert_block, (src.reshape(n_blk, EXPERT_BLOCK), blk_e)).reshape(n_pad, D)
    y = jax.ops.segment_sum(y_pad * gate_pad[:, None], src, num_segments=T + 1)[:T]
    return y.reshape(bsz, L, D)


def setup_inputs(seed: int = 0) -> dict:
    key = jax.random.key(seed)
    ks = iter(jax.random.split(key, 64))
    nrm = lambda shape, scale: scale * jax.random.normal(next(ks), shape, F32)
    uni = lambda shape, lo, hi: jax.random.uniform(next(ks), shape, F32, lo, hi)
    n = jnp.arange(S5_STATE, dtype=F32)
    gdn_dt = jnp.exp(uni((N_EVEN, GDN_HEADS), math.log(1e-3), math.log(1e-1)))
    return {
        'x_prompt': nrm((BATCH, SEQ, D_MODEL), 1.0),
        'x_sample': nrm((DEC_BATCH, DEC_SEQ, D_MODEL), 1.0),
        'state_s5_re': nrm((N_EVEN, DEC_BATCH, S5_GROUPS, S5_STATE), 0.1),
        'state_s5_im': nrm((N_EVEN, DEC_BATCH, S5_GROUPS, S5_STATE), 0.1),
        'state_gdn': nrm((N_EVEN, DEC_BATCH, GDN_HEADS, GDN_DK, GDN_DV), 0.1),
        'state_gdn_conv': nrm((N_EVEN, DEC_BATCH, GDN_CONV - 1, GDN_QKV), 1.0),
        'state_ret': nrm((N_ODD, DEC_BATCH, RET_HEADS, RET_DK, RET_DV), 0.3),
        'p_prompt': nrm((DEPTH, BATCH, SEQ, PLE_DIM), 1.0),
        'p_sample': nrm((DEPTH, DEC_BATCH, DEC_SEQ, PLE_DIM), 1.0),
        'w_in_even': nrm((N_EVEN, D_MODEL, EVEN_IN), D_MODEL ** -0.5),
        's5_a_re': -0.5 + nrm((N_EVEN, S5_GROUPS, S5_STATE), 0.01),
        's5_a_im': math.pi * n + nrm((N_EVEN, S5_GROUPS, S5_STATE), 0.01),
        's5_log_dt': uni((N_EVEN, S5_GROUPS), math.log(1e-3), math.log(1e-1)),
        's5_b_re': nrm((N_EVEN, S5_GROUPS, S5_STATE, S5_GROUP), (2.0 * S5_GROUP) ** -0.5),
        's5_b_im': nrm((N_EVEN, S5_GROUPS, S5_STATE, S5_GROUP), (2.0 * S5_GROUP) ** -0.5),
        's5_c_re': nrm((N_EVEN, S5_GROUPS, S5_GROUP, S5_STATE), (2.0 * S5_STATE) ** -0.5),
        's5_c_im': nrm((N_EVEN, S5_GROUPS, S5_GROUP, S5_STATE), (2.0 * S5_STATE) ** -0.5),
        's5_d': nrm((N_EVEN, S5_WIDTH), 1.0),
        's5_w_glu': nrm((N_EVEN, S5_WIDTH, S5_WIDTH), S5_WIDTH ** -0.5),
        's5_b_glu': nrm((N_EVEN, S5_WIDTH), 0.01),
        'gdn_conv_w': nrm((N_EVEN, GDN_CONV, GDN_QKV), GDN_CONV ** -0.5),
        'gdn_a_log': jnp.log(uni((N_EVEN, GDN_HEADS), 1.0, 16.0)),
        'gdn_dt_bias': jnp.log(jnp.expm1(gdn_dt)),
        'gdn_norm_w': 1.0 + nrm((N_EVEN, GDN_DV), 0.01),
        'w_out_even': nrm((N_EVEN, EVEN_MIX, D_MODEL), EVEN_MIX ** -0.5 * DEEPNORM_BETA),
        'w_in_odd': nrm((N_ODD, D_MODEL, ODD_IN), D_MODEL ** -0.5),
        'w_out_odd': nrm((N_ODD, RET_HEADS * RET_DV, D_MODEL), (RET_HEADS * RET_DV) ** -0.5 * DEEPNORM_BETA),
        'ln1_g': 1.0 + nrm((DEPTH, D_MODEL), 0.01),
        'ln1_b': nrm((DEPTH, D_MODEL), 0.01),
        'ln2_g': 1.0 + nrm((DEPTH, D_MODEL), 0.01),
        'ln2_b': nrm((DEPTH, D_MODEL), 0.01),
        'router_w': nrm((DEPTH, D_MODEL, N_EXPERTS), D_MODEL ** -0.5),
        'router_b': nrm((DEPTH, N_EXPERTS), 0.01),
        'moe_w1': nrm((DEPTH, N_EXPERTS, D_MODEL, 2 * D_FF), D_MODEL ** -0.5),
        'moe_b1': nrm((DEPTH, N_EXPERTS, 2 * D_FF), 0.01),
        'moe_w2': nrm((DEPTH, N_EXPERTS, D_FF, D_MODEL), D_FF ** -0.5 * DEEPNORM_BETA),
        'moe_b2': nrm((DEPTH, N_EXPERTS, D_MODEL), 0.01),
        'ple_w': nrm((DEPTH, PLE_DIM, D_MODEL), PLE_DIM ** -0.5),
        'ple_gate_w': nrm((DEPTH, D_MODEL, D_MODEL), D_MODEL ** -0.5),
    }


def reference(x_prompt, x_sample, state_s5_re, state_s5_im, state_gdn, state_gdn_conv, state_ret,
              p_prompt, p_sample, w_in_even, s5_a_re, s5_a_im, s5_log_dt, s5_b_re, s5_b_im, s5_c_re,
              s5_c_im, s5_d, s5_w_glu, s5_b_glu, gdn_conv_w, gdn_a_log, gdn_dt_bias, gdn_norm_w,
              w_out_even, w_in_odd, w_out_odd, ln1_g, ln1_b, ln2_g, ln2_b, router_w, router_b,
              moe_w1, moe_b1, moe_w2, moe_b2, ple_w, ple_gate_w):

    def run_group(x, p, s5_re, s5_im, gdn_s, conv_s, ret_s, pos0):
        x = x.astype(F32)
        L = x.shape[1]
        lc = L if L <= CHUNK else CHUNK
        pos = pos0 + jnp.arange(L, dtype=F32)
        new_re, new_im, new_gdn, new_conv, new_ret = [], [], [], [], []
        for i in range(DEPTH):
            j = i // 2
            if i % 2 == 0:
                mix, hr, hi, S, cv = even_mixer(
                    x, s5_re[j].astype(F32), s5_im[j].astype(F32), gdn_s[j].astype(F32),
                    conv_s[j].astype(F32), lc, w_in_even[j], s5_a_re[j], s5_a_im[j], s5_log_dt[j],
                    s5_b_re[j], s5_b_im[j], s5_c_re[j], s5_c_im[j], s5_d[j], s5_w_glu[j], s5_b_glu[j],
                    gdn_conv_w[j], gdn_a_log[j], gdn_dt_bias[j], gdn_norm_w[j], w_out_even[j])
                new_re.append(hr)
                new_im.append(hi)
                new_gdn.append(S)
                new_conv.append(cv)
            else:
                mix, R = odd_mixer(x, ret_s[j].astype(F32), pos, lc, w_in_odd[j], w_out_odd[j])
                new_ret.append(R)
            x = layer_norm(DEEPNORM_ALPHA * x + mix, ln1_g[i], ln1_b[i])
            ff = moe(x, router_w[i], router_b[i], moe_w1[i], moe_b1[i], moe_w2[i], moe_b2[i])
            x = layer_norm(DEEPNORM_ALPHA * x + ff, ln2_g[i], ln2_b[i])
            x = x + (p[i].astype(F32) @ ple_w[i]) * jax.nn.sigmoid(x @ ple_gate_w[i])
        return x, jnp.stack(new_re), jnp.stack(new_im), jnp.stack(new_gdn), jnp.stack(new_conv), jnp.stack(new_ret)

    bp = x_prompt.shape[0]
    zeros = lambda *s: jnp.zeros(s, F32)
    y_p, p_re, p_im, p_gdn, p_conv, p_ret = run_group(
        x_prompt, p_prompt,
        zeros(N_EVEN, bp, S5_GROUPS, S5_STATE), zeros(N_EVEN, bp, S5_GROUPS, S5_STATE),
        zeros(N_EVEN, bp, GDN_HEADS, GDN_DK, GDN_DV), zeros(N_EVEN, bp, GDN_CONV - 1, GDN_QKV),
        zeros(N_ODD, bp, RET_HEADS, RET_DK, RET_DV), 0.0)
    y_s, s_re, s_im, s_gdn, s_conv, s_ret = run_group(
        x_sample, p_sample, state_s5_re, state_s5_im, state_gdn, state_gdn_conv, state_ret, float(PAST_LEN))
    dp = x_prompt.dtype
    return (y_p.astype(dp), y_s.astype(x_sample.dtype),
            p_re.astype(dp), p_im.astype(dp), p_gdn.astype(dp), p_conv.astype(dp), p_ret.astype(dp),
            s_re.astype(state_s5_re.dtype), s_im.astype(state_s5_im.dtype), s_gdn.astype(state_gdn.dtype),
            s_conv.astype(state_gdn_conv.dtype), s_ret.astype(state_ret.dtype))
```

```python
import functools
import math

import jax
import jax.numpy as jnp
import numpy as np
from jax import lax
from jax.experimental import pallas as pl
from jax.experimental.pallas import tpu as pltpu

F32 = jnp.float32
BF16 = jnp.bfloat16
HIGHEST = lax.Precision.HIGHEST

D_MODEL = 1024
CHUNK = 64
S5_WIDTH = 512
S5_GROUP = 16
S5_GROUPS = 32
S5_STATE = 64
S5_TILES = 16
GDN_HEADS = 4
GDN_DK = 128
GDN_DV = 128
GDN_CONV = 4
GDN_QKV = 1536
RET_HEADS = 4
RET_DK = 256
RET_DV = 512
ROPE_BASE = 10000.0
N_EXPERTS = 32
TOP_K = 4
D_FF = 1024
SWIGLU_LIMIT = 7.0
SWIGLU_ALPHA = 1.702
PLE_DIM = 256
DEPTH = 2
PAST_LEN = 1024
DEEPNORM_ALPHA = (2 * DEPTH) ** 0.25
LN_EPS = 1e-5
NORM_EPS = 1e-6

LANES = 128
SUBLANES = 8
VMEM_LIMIT = 48 * 1024 * 1024


def _params(sem):
    return pltpu.CompilerParams(dimension_semantics=sem, vmem_limit_bytes=VMEM_LIMIT)


def _dot(a, b):
    return jnp.dot(a, b, preferred_element_type=F32)


def _dot_hi(a, b):
    return jnp.dot(a, b, preferred_element_type=F32, precision=HIGHEST)


def _dot_nt(a, b, precision=None):
    return lax.dot_general(a, b, (((1,), (1,)), ((), ())), preferred_element_type=F32,
                           precision=precision)


def _dot_tn(a, b):
    return lax.dot_general(a, b, (((0,), (0,)), ((), ())), preferred_element_type=F32)


def _sigmoid(x):
    return 1.0 / (1.0 + jnp.exp(-x))


def _full(shape):
    nd = len(shape)
    return pl.BlockSpec(shape, lambda *_: (0,) * nd)


def _proj_even_kernel(x_ref, wu_ref, wqkv_ref, wz_ref, wba_ref, u_ref, qkv_ref, z_ref, ba_ref):
    xb = x_ref[...].astype(BF16)
    u_ref[...] = _dot(xb, wu_ref[...])
    qkv_ref[...] = _dot(xb, wqkv_ref[...])
    z_ref[...] = _dot(xb, wz_ref[...])
    ba_ref[...] = _dot(xb, wba_ref[...])


def _proj_even(x, wu, wqkv, wz, wba, tm):
    t = x.shape[0]
    row = lambda n: pl.BlockSpec((tm, n), lambda i: (i, 0))
    return pl.pallas_call(
        _proj_even_kernel,
        grid=(t // tm,),
        in_specs=[row(D_MODEL), _full(wu.shape), _full(wqkv.shape), _full(wz.shape), _full(wba.shape)],
        out_specs=[row(S5_WIDTH), row(GDN_QKV), row(GDN_HEADS * GDN_DV), row(LANES)],
        out_shape=[jax.ShapeDtypeStruct((t, S5_WIDTH), F32), jax.ShapeDtypeStruct((t, GDN_QKV), F32),
                   jax.ShapeDtypeStruct((t, GDN_HEADS * GDN_DV), F32), jax.ShapeDtypeStruct((t, LANES), F32)],
        compiler_params=_params(("parallel",)),
        name="proj_even",
    )(x, wu, wqkv, wz, wba)


def _proj_odd_kernel(x_ref, w_ref, cos_ref, sin_ref, o_ref):
    j = pl.program_id(0)
    acc = _dot(x_ref[...].astype(BF16), w_ref[...])

    @pl.when(j == 0)
    def _():
        cos, sin = cos_ref[...], sin_ref[...]
        half = RET_DK // 2
        for h in range(2 * RET_HEADS):
            x0 = acc[:, h * RET_DK:h * RET_DK + half]
            x1 = acc[:, h * RET_DK + half:(h + 1) * RET_DK]
            scale = 1.0 if h < RET_HEADS else RET_DK ** -0.5
            o_ref[:, h * RET_DK:h * RET_DK + half] = (x0 * cos - x1 * sin) * scale
            o_ref[:, h * RET_DK + half:(h + 1) * RET_DK] = (x0 * sin + x1 * cos) * scale

    @pl.when(j != 0)
    def _():
        o_ref[...] = acc


def _proj_odd(x, w, cos, sin, tm):
    t = x.shape[0]
    nblk = w.shape[1] // 2048
    return pl.pallas_call(
        _proj_odd_kernel,
        grid=(nblk, t // tm),
        in_specs=[pl.BlockSpec((tm, D_MODEL), lambda j, i: (i, 0)),
                  pl.BlockSpec((D_MODEL, 2048), lambda j, i: (0, j)),
                  pl.BlockSpec((tm, LANES), lambda j, i: (i, 0)),
                  pl.BlockSpec((tm, LANES), lambda j, i: (i, 0))],
        out_specs=pl.BlockSpec((tm, 2048), lambda j, i: (i, j)),
        out_shape=jax.ShapeDtypeStruct((t, w.shape[1]), F32),
        compiler_params=_params(("parallel", "parallel")),
        name="proj_odd",
    )(x, w, cos, sin)


def _layer_norm(r, g, b):
    mu = jnp.mean(r, -1, keepdims=True)
    d = r - mu
    var = jnp.mean(d * d, -1, keepdims=True)
    return d * lax.rsqrt(var + LN_EPS) * g + b


def _outproj_ln_kernel(*refs, n_in):
    a_refs = refs[:n_in]
    w_refs = refs[n_in:2 * n_in]
    x_ref, g_ref, b_ref, o_ref = refs[2 * n_in:]
    acc = _dot(a_refs[0][...].astype(BF16), w_refs[0][...])
    for a_ref, w_ref in zip(a_refs[1:], w_refs[1:]):
        acc = acc + _dot(a_ref[...].astype(BF16), w_ref[...])
    o_ref[...] = _layer_norm(DEEPNORM_ALPHA * x_ref[...] + acc, g_ref[...], b_ref[...])


def _outproj_ln(acts, ws, x, g, b, tm):
    t = x.shape[0]
    row = lambda n: pl.BlockSpec((tm, n), lambda i: (i, 0))
    return pl.pallas_call(
        functools.partial(_outproj_ln_kernel, n_in=len(acts)),
        grid=(t // tm,),
        in_specs=[row(a.shape[1]) for a in acts] + [_full(w.shape) for w in ws]
                 + [row(D_MODEL), _full((1, D_MODEL)), _full((1, D_MODEL))],
        out_specs=row(D_MODEL),
        out_shape=jax.ShapeDtypeStruct((t, D_MODEL), F32),
        compiler_params=_params(("parallel",)),
        name="outproj_ln",
    )(*acts, *ws, x, g, b)


def _s5_kernel(u_ref, h0_ref, bmat_ref, cmat_ref, acoef_ref, dskip_ref, wglu_ref, bglu_ref,
               y_ref, hout_ref, upad, sre, sim, yacc, hst, *, nb, lt, pitch):
    tb = pl.program_id(0)

    @pl.when(tb == 0)
    def _():
        hst[...] = h0_ref[...]
        upad[...] = jnp.zeros_like(upad)

    for b in range(nb):
        upad[b * pitch:b * pitch + lt, :] = u_ref[b]
    yacc[...] = jnp.zeros_like(yacc)
    halves = nb // SUBLANES

    def tile_body(j, carry):
        q = j // 4
        ub = upad[:, pl.ds(pl.multiple_of(q * LANES, LANES), LANES)].astype(BF16)
        bu = _dot(ub, bmat_ref[j])
        sre[...] = bu[:, :LANES]
        sim[...] = bu[:, LANES:]
        ac = acoef_ref[j]
        ar, ai = ac[:, :LANES], ac[:, LANES:]
        hs = [(hst[j, hf * SUBLANES:(hf + 1) * SUBLANES, 0:LANES],
               hst[j, hf * SUBLANES:(hf + 1) * SUBLANES, LANES:2 * LANES]) for hf in range(halves)]
        for t in range(lt):
            for hf in range(halves):
                hr, hi = hs[hf]
                rows = pl.ds(hf * SUBLANES * pitch + t, SUBLANES, stride=pitch)
                nhr = ar * hr - ai * hi + sre[rows, :]
                nhi = ar * hi + ai * hr + sim[rows, :]
                sre[rows, :] = nhr
                sim[rows, :] = nhi
                hs[hf] = (nhr, nhi)
        for hf in range(halves):
            hst[j, hf * SUBLANES:(hf + 1) * SUBLANES, 0:LANES] = hs[hf][0]
            hst[j, hf * SUBLANES:(hf + 1) * SUBLANES, LANES:2 * LANES] = hs[hf][1]
        st = jnp.concatenate([sre[...], sim[...]], axis=-1).astype(BF16)
        yacc[q] += _dot(st, cmat_ref[j])
        return carry

    lax.fori_loop(0, S5_TILES, tile_body, 0)
    y = jnp.concatenate([yacc[0], yacc[1], yacc[2], yacc[3]], axis=-1) + dskip_ref[...] * upad[...]
    y = jax.nn.gelu(y)
    y = y * _sigmoid(_dot(y.astype(BF16), wglu_ref[...]) + bglu_ref[...])
    for b in range(nb):
        y_ref[b] = y[b * pitch:b * pitch + lt]
    hout_ref[...] = hst[...]


def _s5(u3, h0, bmat, cmat, acoef, dskip, wglu, bglu, lt):
    nb, L, _ = u3.shape
    pitch = lt + SUBLANES
    rows = nb * pitch
    return pl.pallas_call(
        functools.partial(_s5_kernel, nb=nb, lt=lt, pitch=pitch),
        grid=(L // lt,),
        in_specs=[pl.BlockSpec((nb, lt, S5_WIDTH), lambda i: (0, i, 0)),
                  _full(h0.shape), _full(bmat.shape), _full(cmat.shape), _full(acoef.shape),
                  _full(dskip.shape), _full(wglu.shape), _full(bglu.shape)],
        out_specs=[pl.BlockSpec((nb, lt, S5_WIDTH), lambda i: (0, i, 0)), _full(h0.shape)],
        out_shape=[jax.ShapeDtypeStruct((nb, L, S5_WIDTH), F32), jax.ShapeDtypeStruct(h0.shape, F32)],
        scratch_shapes=[pltpu.VMEM((rows, S5_WIDTH), F32), pltpu.VMEM((rows, LANES), F32),
                        pltpu.VMEM((rows, LANES), F32), pltpu.VMEM((4, rows, LANES), F32),
                        pltpu.VMEM(h0.shape, F32)],
        compiler_params=_params(("arbitrary",)),
        name="s5_scan",
    )(u3, h0, bmat, cmat, acoef, dskip, wglu, bglu)


def _s5_weights(a_re, a_im, log_dt, b_re, b_im, c_re, c_im):
    dt = jnp.exp(log_dt)[:, None]
    lr, li = a_re * dt, a_im * dt
    mag = jnp.exp(lr)
    ab_re, ab_im = mag * jnp.cos(li), mag * jnp.sin(li)
    den = a_re * a_re + a_im * a_im
    cf_re = ((ab_re - 1.0) * a_re + ab_im * a_im) / den
    cf_im = (ab_im * a_re - (ab_re - 1.0) * a_im) / den
    bb_re = cf_re[..., None] * b_re - cf_im[..., None] * b_im
    bb_im = cf_re[..., None] * b_im + cf_im[..., None] * b_re
    jj = np.arange(S5_TILES)[:, None, None]
    lg = np.arange(8)[None, :, None]
    gi = np.arange(2)[None, None, :]
    sel = jnp.asarray((lg == 2 * (jj % 4) + gi).astype(np.float32))
    tiles = lambda w: w.reshape(S5_TILES, 2, *w.shape[1:])
    bt = lambda w: jnp.einsum('jlg,jgpn->jlngp', sel, tiles(w)).reshape(S5_TILES, LANES, LANES)
    bmat = jnp.concatenate([bt(bb_re), bt(bb_im)], axis=-1)
    ct = lambda w: jnp.einsum('jlg,jgnp->jgpln', sel, tiles(w)).reshape(S5_TILES, LANES, LANES)
    cmat = jnp.concatenate([ct(c_re), -ct(c_im)], axis=1)
    acoef = jnp.concatenate([ab_re.reshape(S5_TILES, LANES), ab_im.reshape(S5_TILES, LANES)], axis=-1)
    acoef = jnp.broadcast_to(acoef[:, None, :], (S5_TILES, SUBLANES, 2 * LANES))
    return bmat.astype(BF16), cmat.astype(BF16), acoef


def _s5_state_in(h_re, h_im):
    nb = h_re.shape[0]
    h = jnp.concatenate([h_re.reshape(nb, S5_TILES, LANES), h_im.reshape(nb, S5_TILES, LANES)], axis=-1)
    return jnp.transpose(h, (1, 0, 2))


def _s5_state_out(h):
    nb = h.shape[1]
    h = jnp.transpose(h, (1, 0, 2))
    return (h[..., :LANES].reshape(nb, S5_GROUPS, S5_STATE), h[..., LANES:].reshape(nb, S5_GROUPS, S5_STATE))


def _gdn_kernel(qkv_ref, z_ref, ba_ref, ctx_ref, s0_ref, cw_ref, p1_ref, p2_ref, nw_ref, ltri_ref,
                y_ref, sout_ref, cout_ref, xpad, S, *, lc, n_sq):
    c = pl.program_id(1)

    @pl.when(c == 0)
    def _():
        xpad[0:SUBLANES, :] = ctx_ref[0]
        S[...] = s0_ref[0]

    xpad[SUBLANES:SUBLANES + lc, :] = qkv_ref[0]
    cw = cw_ref[...]
    conv = (cw[3:4] * xpad[8:8 + lc, :] + cw[2:3] * xpad[7:7 + lc, :]
            + cw[1:2] * xpad[6:6 + lc, :] + cw[0:1] * xpad[5:5 + lc, :])
    tail = xpad[lc:lc + SUBLANES, :]
    xpad[0:SUBLANES, :] = tail
    cout_ref[0] = tail
    a = conv * _sigmoid(conv)

    ba = ba_ref[0]
    beta_all = _sigmoid(ba)
    sp_in = ba + p2_ref[...]
    softplus = jnp.maximum(sp_in, 0.0) + jnp.log(1.0 + jnp.exp(-jnp.abs(sp_in)))
    g_all = p1_ref[...] * softplus
    G = _dot_hi(ltri_ref[...], g_all)
    GT = G.T
    ri = lax.broadcasted_iota(jnp.int32, (lc, lc), 0)
    ci = lax.broadcasted_iota(jnp.int32, (lc, lc), 1)
    incl = ri >= ci
    strict = ri > ci
    eye = (ri == ci).astype(F32)
    z = z_ref[0]
    nw = nw_ref[...]

    for h in range(GDN_HEADS):
        q = a[:, h * GDN_DK:(h + 1) * GDN_DK]
        k = a[:, (GDN_HEADS + h) * GDN_DK:(GDN_HEADS + h + 1) * GDN_DK]
        v = a[:, (2 * GDN_HEADS + h) * GDN_DK:(2 * GDN_HEADS + h + 1) * GDN_DK]
        q = q * lax.rsqrt(jnp.sum(q * q, -1, keepdims=True) + NORM_EPS) * (GDN_DK ** -0.5)
        k = k * lax.rsqrt(jnp.sum(k * k, -1, keepdims=True) + NORM_EPS)
        beta = beta_all[:, h:h + 1]
        gcol = G[:, GDN_HEADS + h:GDN_HEADS + h + 1]
        grow = GT[GDN_HEADS + h:GDN_HEADS + h + 1, :]
        glast = GT[GDN_HEADS + h:GDN_HEADS + h + 1, lc - 1:lc]
        dec = jnp.where(incl, jnp.exp(jnp.where(incl, gcol - grow, 0.0)), 0.0)
        eg = jnp.exp(gcol)
        kb = k * beta
        nmat = jnp.where(strict, _dot_nt(kb, k, HIGHEST) * dec, 0.0)
        inv = eye - nmat
        pw = nmat
        for _ in range(n_sq):
            pw = _dot_hi(pw, pw)
            inv = inv + _dot_hi(inv, pw)
        rhs = jnp.concatenate([v * beta, kb * eg], axis=-1)
        sol = _dot_hi(inv, rhs)
        u0, w = sol[:, :GDN_DV], sol[:, GDN_DV:]
        attn = _dot_nt(q.astype(BF16), k.astype(BF16)) * dec
        Sh = S[h]
        Sb = Sh.astype(BF16)
        u = u0 - _dot(w.astype(BF16), Sb)
        o = _dot((q * eg).astype(BF16), Sb) + _dot(attn.astype(BF16), u.astype(BF16))
        kd = k * jnp.exp(glast - gcol)
        S[h] = jnp.exp(glast) * Sh + _dot_tn(kd.astype(BF16), u.astype(BF16))
        o = o * lax.rsqrt(jnp.mean(o * o, -1, keepdims=True) + NORM_EPS) * nw
        zh = z[:, h * GDN_DV:(h + 1) * GDN_DV]
        y_ref[0, :, h * GDN_DV:(h + 1) * GDN_DV] = o * (zh * _sigmoid(zh))

    @pl.when(c == pl.num_programs(1) - 1)
    def _():
        sout_ref[0] = S[...]


def _gdn(qkv3, z3, ba3, ctx8, s0, cw, p1, p2, nw, lc):
    nb, L, _ = qkv3.shape
    n_sq = int(math.log2(lc)) - 1
    ltri = jnp.asarray(np.tril(np.ones((lc, lc), np.float32)))
    blk = lambda n: pl.BlockSpec((1, lc, n), lambda b, c: (b, c, 0))
    per_b = lambda shape: pl.BlockSpec((1,) + shape, lambda b, c: (b,) + (0,) * len(shape))
    cst = lambda shape: pl.BlockSpec(shape, lambda b, c: (0,) * len(shape))
    return pl.pallas_call(
        functools.partial(_gdn_kernel, lc=lc, n_sq=n_sq),
        grid=(nb, L // lc),
        in_specs=[blk(GDN_QKV), blk(GDN_HEADS * GDN_DV), blk(LANES), per_b((SUBLANES, GDN_QKV)),
                  per_b((GDN_HEADS, GDN_DK, GDN_DV)), cst(cw.shape), cst(p1.shape), cst(p2.shape),
                  cst(nw.shape), cst(ltri.shape)],
        out_specs=[blk(GDN_HEADS * GDN_DV), per_b((GDN_HEADS, GDN_DK, GDN_DV)), per_b((SUBLANES, GDN_QKV))],
        out_shape=[jax.ShapeDtypeStruct((nb, L, GDN_HEADS * GDN_DV), F32),
                   jax.ShapeDtypeStruct((nb, GDN_HEADS, GDN_DK, GDN_DV), F32),
                   jax.ShapeDtypeStruct((nb, SUBLANES, GDN_QKV), F32)],
        scratch_shapes=[pltpu.VMEM((lc + SUBLANES, GDN_QKV), F32),
                        pltpu.VMEM((GDN_HEADS, GDN_DK, GDN_DV), F32)],
        compiler_params=_params(("parallel", "arbitrary")),
        name="gdn",
    )(qkv3, z3, ba3, ctx8, s0, cw, p1, p2, nw, ltri)


def _ret_kernel(q_ref, k_ref, v_ref, g_ref, r0_ref, dec_ref, qs_ref, ks_ref, o_ref, rout_ref, R, *, cdec):
    c = pl.program_id(1)

    @pl.when(c == 0)
    def _():
        R[...] = r0_ref[0]

    for h in range(RET_HEADS):
        q = q_ref[0, :, h * RET_DK:(h + 1) * RET_DK]
        k = k_ref[0, :, h * RET_DK:(h + 1) * RET_DK]
        v = v_ref[0, :, h * RET_DV:(h + 1) * RET_DV].astype(BF16)
        s = _dot_nt(q.astype(BF16), k.astype(BF16)) * dec_ref[h]
        Rh = R[h]
        o = _dot(s.astype(BF16), v) + _dot((q * qs_ref[h]).astype(BF16), Rh.astype(BF16))
        R[h] = cdec[h] * Rh + _dot_tn((k * ks_ref[h]).astype(BF16), v)
        mu = jnp.mean(o, -1, keepdims=True)
        d = o - mu
        var = jnp.mean(d * d, -1, keepdims=True)
        gt = g_ref[0, :, h * RET_DV:(h + 1) * RET_DV]
        o_ref[0, :, h * RET_DV:(h + 1) * RET_DV] = gt * _sigmoid(gt) * (d * lax.rsqrt(var + LN_EPS))

    @pl.when(c == pl.num_programs(1) - 1)
    def _():
        rout_ref[0] = R[...]


def _retention(proj3, r0, lc):
    nb, L, _ = proj3.shape
    log_g = np.log(1.0 - 2.0 ** (-5.0 - np.arange(RET_HEADS, dtype=np.float64)))
    idx = np.arange(lc, dtype=np.float64)
    dec = np.exp(log_g[:, None, None] * np.abs(idx[:, None] - idx[None, :])).astype(np.float32)
    qs = np.exp(log_g[:, None] * (idx + 1.0)).astype(np.float32)[..., None]
    ks = np.exp(log_g[:, None] * (lc - 1.0 - idx)).astype(np.float32)[..., None]
    cdec = tuple(float(x) for x in np.exp(log_g * lc))
    nqk = RET_HEADS * RET_DK
    nv = RET_HEADS * RET_DV
    cst = lambda shape: pl.BlockSpec(shape, lambda b, c: (0,) * len(shape))
    state = pl.BlockSpec((1, RET_HEADS, RET_DK, RET_DV), lambda b, c: (b, 0, 0, 0))
    return pl.pallas_call(
        functools.partial(_ret_kernel, cdec=cdec),
        grid=(nb, L // lc),
        in_specs=[pl.BlockSpec((1, lc, nqk), lambda b, c: (b, c, 0)),
                  pl.BlockSpec((1, lc, nqk), lambda b, c: (b, c, 1)),
                  pl.BlockSpec((1, lc, nv), lambda b, c: (b, c, 1)),
                  pl.BlockSpec((1, lc, nv), lambda b, c: (b, c, 2)),
                  state, cst(dec.shape), cst(qs.shape), cst(ks.shape)],
        out_specs=[pl.BlockSpec((1, lc, nv), lambda b, c: (b, c, 0)), state],
        out_shape=[jax.ShapeDtypeStruct((nb, L, nv), F32), jax.ShapeDtypeStruct(r0.shape, F32)],
        scratch_shapes=[pltpu.VMEM((RET_HEADS, RET_DK, RET_DV), F32)],
        compiler_params=_params(("parallel", "arbitrary")),
        name="retention",
    )(proj3, proj3, proj3, proj3, r0, jnp.asarray(dec), jnp.asarray(qs), jnp.asarray(ks))


def _router_kernel(x_ref, rw_ref, rb_ref, lst_ref, ri_ref, gt_ref, cnt_ref, run):
    i = pl.program_id(0)

    @pl.when(i == 0)
    def _():
        run[...] = jnp.zeros_like(run)

    tm = x_ref.shape[0]
    lane = lax.broadcasted_iota(jnp.int32, (tm, LANES), 1)
    lane_f = lane.astype(F32)
    logits = _dot_hi(x_ref[...], rw_ref[...]) + rb_ref[...]
    logits = jnp.where(lane < N_EXPERTS, logits, -jnp.inf)
    vals, hots = [], []
    for _ in range(TOP_K):
        m = jnp.max(logits, -1, keepdims=True)
        first = jnp.min(jnp.where(logits == m, lane_f, float(LANES)), -1, keepdims=True)
        hot = lane_f == first
        vals.append(m)
        hots.append(hot)
        logits = jnp.where(hot, -jnp.inf, logits)
    es = [jnp.exp(v - vals[0]) for v in vals]
    den = es[0] + es[1] + es[2] + es[3]
    multi = jnp.zeros((tm, LANES), F32)
    for hot in hots:
        multi = multi + hot.astype(F32)
    before = _dot(lst_ref[...], multi.astype(BF16)) + run[...]
    ri = jnp.zeros((tm, LANES), F32)
    gt = jnp.zeros((tm, LANES), F32)
    for kk in range(TOP_K):
        idx = jnp.sum(jnp.where(hots[kk], lane_f, 0.0), -1, keepdims=True)
        rank = jnp.sum(jnp.where(hots[kk], before, 0.0), -1, keepdims=True)
        ri = jnp.where(lane == kk, idx, ri)
        ri = jnp.where(lane == TOP_K + kk, rank, ri)
        gt = jnp.where(lane == kk, es[kk] / den, gt)
    ri_ref[...] = ri.astype(jnp.int32)
    gt_ref[...] = gt
    run[...] = run[...] + jnp.sum(multi, 0, keepdims=True)
    cnt_ref[...] = run[...]


def _router(x, rw, rb, tm):
    t = x.shape[0]
    lst = jnp.asarray(np.tril(np.ones((tm, tm), np.float32), -1)).astype(BF16)
    row = lambda n: pl.BlockSpec((tm, n), lambda i: (i, 0))
    return pl.pallas_call(
        _router_kernel,
        grid=(t // tm,),
        in_specs=[row(D_MODEL), _full(rw.shape), _full(rb.shape), _full(lst.shape)],
        out_specs=[row(LANES), row(LANES), _full((1, LANES))],
        out_shape=[jax.ShapeDtypeStruct((t, LANES), jnp.int32), jax.ShapeDtypeStruct((t, LANES), F32),
                   jax.ShapeDtypeStruct((1, LANES), F32)],
        scratch_shapes=[pltpu.VMEM((1, LANES), F32)],
        compiler_params=_params(("arbitrary",)),
        name="router",
    )(x, rw, rb, lst)


def _dispatch_kernel(dest_ref, x_ref, xs_in, xs_out, sem):
    del xs_in
    tm = x_ref.shape[0]

    def body(r, carry):
        for kk in range(TOP_K):
            d = dest_ref[TOP_K * r + kk]
            pltpu.make_async_copy(x_ref.at[pl.ds(r, 1)], xs_out.at[pl.ds(d, 1)], sem).start()
        return carry

    lax.fori_loop(0, tm, body, 0)
    for _ in range(TOP_K):
        pltpu.make_async_copy(x_ref, xs_out.at[pl.ds(0, tm)], sem).wait()


def _dispatch(dest_flat, x, n_pad, tm):
    t = x.shape[0]
    xs0 = jnp.zeros((n_pad, D_MODEL), F32)
    return pl.pallas_call(
        _dispatch_kernel,
        grid=(t // tm,),
        in_specs=[pl.BlockSpec((tm * TOP_K,), lambda i: (i,), memory_space=pltpu.SMEM),
                  pl.BlockSpec((tm, D_MODEL), lambda i: (i, 0)),
                  pl.BlockSpec(memory_space=pl.ANY)],
        out_specs=pl.BlockSpec(memory_space=pl.ANY),
        out_shape=jax.ShapeDtypeStruct((n_pad, D_MODEL), F32),
        scratch_shapes=[pltpu.SemaphoreType.DMA],
        input_output_aliases={2: 0},
        compiler_params=_params(("arbitrary",)),
        name="moe_dispatch",
    )(dest_flat, x, xs0)


def _expert_kernel(be_ref, nu_ref, xs_ref, w1_ref, b1_ref, w2_ref, b2_ref, ys_ref):
    del be_ref

    @pl.when(pl.program_id(0) < nu_ref[0])
    def _():
        h = _dot(xs_ref[...].astype(BF16), w1_ref[0]) + b1_ref[0]
        glu = jnp.minimum(h[:, :D_FF], SWIGLU_LIMIT)
        lin = jnp.clip(h[:, D_FF:], -SWIGLU_LIMIT, SWIGLU_LIMIT)
        act = glu * _sigmoid(SWIGLU_ALPHA * glu) * (lin + 1.0)
        ys_ref[...] = _dot(act.astype(BF16), w2_ref[0]) + b2_ref[0]

    @pl.when(pl.program_id(0) >= nu_ref[0])
    def _():
        ys_ref[...] = jnp.zeros_like(ys_ref)


def _experts(blk_e, n_used, xs, w1, b1, w2, b2, blk):
    n_pad = xs.shape[0]
    n_blk = n_pad // blk
    used = lambda i, nu: jnp.minimum(i, nu[0] - 1)
    grid_spec = pltpu.PrefetchScalarGridSpec(
        num_scalar_prefetch=2,
        grid=(n_blk,),
        in_specs=[pl.BlockSpec((blk, D_MODEL), lambda i, be, nu: (used(i, nu), 0)),
                  pl.BlockSpec((1, D_MODEL, 2 * D_FF), lambda i, be, nu: (be[i], 0, 0)),
                  pl.BlockSpec((1, 1, 2 * D_FF), lambda i, be, nu: (be[i], 0, 0)),
                  pl.BlockSpec((1, D_FF, D_MODEL), lambda i, be, nu: (be[i], 0, 0)),
                  pl.BlockSpec((1, 1, D_MODEL), lambda i, be, nu: (be[i], 0, 0))],
        out_specs=pl.BlockSpec((blk, D_MODEL), lambda i, be, nu: (i, 0)),
    )
    return pl.pallas_call(
        _expert_kernel,
        grid_spec=grid_spec,
        out_shape=jax.ShapeDtypeStruct((n_pad, D_MODEL), F32),
        compiler_params=_params(("arbitrary",)),
        name="moe_experts",
    )(blk_e, n_used, xs, w1, b1, w2, b2)


def _combine_kernel(dest_ref, x_ref, gt_ref, p_ref, ys_hbm, g_ref, b_ref, plew_ref, gatew_ref, o_ref,
                    buf, sem):
    tm = x_ref.shape[0]

    def body(r, carry):
        for kk in range(TOP_K):
            d = dest_ref[TOP_K * r + kk]
            pltpu.make_async_copy(ys_hbm.at[pl.ds(d, 1)], buf.at[kk, pl.ds(r, 1)], sem).start()
        return carry

    lax.fori_loop(0, tm, body, 0)
    for kk in range(TOP_K):
        pltpu.make_async_copy(ys_hbm.at[pl.ds(0, tm)], buf.at[kk], sem).wait()
    gt = gt_ref[...]
    y = gt[:, 0:1] * buf[0]
    for kk in range(1, TOP_K):
        y = y + gt[:, kk:kk + 1] * buf[kk]
    x2 = _layer_norm(DEEPNORM_ALPHA * x_ref[...] + y, g_ref[...], b_ref[...])
    pp = _dot(p_ref[...].astype(BF16), plew_ref[...])
    gg = _sigmoid(_dot(x2.astype(BF16), gatew_ref[...]))
    o_ref[...] = x2 + pp * gg


def _combine(dest_flat, x, gt, p, ys, g, b, plew, gatew, tm):
    t = x.shape[0]
    row = lambda n: pl.BlockSpec((tm, n), lambda i: (i, 0))
    return pl.pallas_call(
        _combine_kernel,
        grid=(t // tm,),
        in_specs=[pl.BlockSpec((tm * TOP_K,), lambda i: (i,), memory_space=pltpu.SMEM),
                  row(D_MODEL), row(LANES), row(PLE_DIM), pl.BlockSpec(memory_space=pl.ANY),
                  _full((1, D_MODEL)), _full((1, D_MODEL)), _full(plew.shape), _full(gatew.shape)],
        out_specs=row(D_MODEL),
        out_shape=jax.ShapeDtypeStruct((t, D_MODEL), F32),
        scratch_shapes=[pltpu.VMEM((TOP_K, tm, D_MODEL), F32), pltpu.SemaphoreType.DMA],
        compiler_params=_params(("arbitrary",)),
        name="moe_combine",
    )(dest_flat, x, gt, p, ys, g, b, plew, gatew)


def _moe_ple(x, p, rw, rb, w1, b1, w2, b2, g, b, plew, gatew, tm, blk):
    t = x.shape[0]
    ri, gt, cnt = _router(x, rw, rb, tm)
    idx, rank = ri[:, :TOP_K], ri[:, TOP_K:2 * TOP_K]
    counts = cnt[0, :N_EXPERTS].astype(jnp.int32)
    padded = (counts + blk - 1) // blk * blk
    pend = jnp.cumsum(padded)
    pstart = pend - padded
    dest = (pstart[idx] + rank).reshape(-1).astype(jnp.int32)
    n_blk = -(-(t * TOP_K + N_EXPERTS * (blk - 1)) // blk)
    n_used = (pend[-1] // blk).astype(jnp.int32).reshape(1)
    blk_e = jnp.minimum(jnp.searchsorted(pend, jnp.arange(n_blk, dtype=jnp.int32) * blk, side='right'),
                        N_EXPERTS - 1).astype(jnp.int32)
    xs = _dispatch(dest, x, n_blk * blk, tm)
    ys = _experts(blk_e, n_used, xs, w1, b1, w2, b2, blk)
    return _combine(dest, x, gt, p, ys, g, b, plew, gatew, tm)


_ROT_PERM = np.concatenate([np.arange(0, RET_DK, 2), np.arange(1, RET_DK, 2)])
_ROT_INV = np.argsort(_ROT_PERM)


def _lane_row(vals, offset):
    row = jnp.zeros((1, LANES), F32)
    return row.at[0, offset:offset + vals.shape[0]].set(vals)


def _run_group(x3, p, s5_re, s5_im, gdn_s, conv_s, ret_s, pos0, W, tm, blk):
    nb, L, _ = x3.shape
    t = nb * L
    lc = L if L <= CHUNK else CHUNK
    x = x3.reshape(t, D_MODEL).astype(F32)

    u, qkv, z, ba = _proj_even(x, W['wu'], W['wqkv'], W['wz'], W['wba'], tm)
    yA3, h_new = _s5(u.reshape(nb, L, S5_WIDTH), _s5_state_in(s5_re[0].astype(F32), s5_im[0].astype(F32)),
                     W['bmat'], W['cmat'], W['acoef'], W['dskip'], W['wglu'], W['bglu'], lc)
    new_re, new_im = _s5_state_out(h_new)
    ctx8 = jnp.concatenate([jnp.zeros((nb, SUBLANES - (GDN_CONV - 1), GDN_QKV), F32), conv_s[0].astype(F32)], axis=1)
    yB3, new_gdn, cout = _gdn(qkv.reshape(nb, L, GDN_QKV), z.reshape(nb, L, -1), ba.reshape(nb, L, LANES),
                              ctx8, gdn_s[0].astype(F32), W['convw'], W['p1'], W['p2'], W['normw'], lc)
    new_conv = cout[:, SUBLANES - (GDN_CONV - 1):, :]
    x = _outproj_ln([yA3.reshape(t, -1), yB3.reshape(t, -1)], [W['wout_a'], W['wout_b']], x,
                    W['ln1_g'][0], W['ln1_b'][0], tm)
    x = _moe_ple(x, p[0].reshape(t, PLE_DIM), W['rw'][0], W['rb'][0], W['w1'][0], W['b1'][0], W['w2'][0],
                 W['b2'][0], W['ln2_g'][0], W['ln2_b'][0], W['plew'][0], W['gatew'][0], tm, blk)

    pos = pos0 + jnp.arange(L, dtype=F32)
    freq = 1.0 / (ROPE_BASE ** jnp.linspace(0.0, 1.0, RET_DK // 2, dtype=F32))
    ang = pos[:, None] * freq[None]
    cos = jnp.broadcast_to(jnp.cos(ang)[None], (nb, L, RET_DK // 2)).reshape(t, RET_DK // 2)
    sin = jnp.broadcast_to(jnp.sin(ang)[None], (nb, L, RET_DK // 2)).reshape(t, RET_DK // 2)
    proj = _proj_odd(x, W['win_odd'], cos, sin, tm)
    r0 = ret_s[0].astype(F32)[:, :, _ROT_PERM, :]
    o3, r_new = _retention(proj.reshape(nb, L, -1), r0, lc)
    new_ret = r_new[:, :, _ROT_INV, :]
    x = _outproj_ln([o3.reshape(t, -1)], [W['wout_odd']], x, W['ln1_g'][1], W['ln1_b'][1], tm)
    x = _moe_ple(x, p[1].reshape(t, PLE_DIM), W['rw'][1], W['rb'][1], W['w1'][1], W['b1'][1], W['w2'][1],
                 W['b2'][1], W['ln2_g'][1], W['ln2_b'][1], W['plew'][1], W['gatew'][1], tm, blk)
    return (x.reshape(nb, L, D_MODEL), new_re[None], new_im[None], new_gdn[None], new_conv[None], new_ret[None])


def kernel(x_prompt, x_sample, state_s5_re, state_s5_im, state_gdn, state_gdn_conv, state_ret, p_prompt, p_sample, w_in_even, s5_a_re, s5_a_im, s5_log_dt, s5_b_re, s5_b_im, s5_c_re, s5_c_im, s5_d, s5_w_glu, s5_b_glu, gdn_conv_w, gdn_a_log, gdn_dt_bias, gdn_norm_w, w_out_even, w_in_odd, w_out_odd, ln1_g, ln1_b, ln2_g, ln2_b, router_w, router_b, moe_w1, moe_b1, moe_w2, moe_b2, ple_w, ple_gate_w):
    o1 = S5_WIDTH
    o2 = o1 + GDN_QKV
    o3 = o2 + GDN_HEADS * GDN_DV
    win = w_in_even[0]
    bmat, cmat, acoef = _s5_weights(s5_a_re[0], s5_a_im[0], s5_log_dt[0], s5_b_re[0], s5_b_im[0],
                                    s5_c_re[0], s5_c_im[0])
    wodd = w_in_odd[0]
    nk = RET_HEADS * RET_DK
    perm_cols = lambda w: w.reshape(D_MODEL, RET_HEADS, RET_DK)[:, :, _ROT_PERM].reshape(D_MODEL, nk)
    W = dict(
        wu=win[:, :o1].astype(BF16), wqkv=win[:, o1:o2].astype(BF16), wz=win[:, o2:o3].astype(BF16),
        wba=jnp.pad(win[:, o3:], ((0, 0), (0, LANES - 2 * GDN_HEADS))).astype(BF16),
        bmat=bmat, cmat=cmat, acoef=acoef, dskip=s5_d[0][None], wglu=s5_w_glu[0].astype(BF16),
        bglu=s5_b_glu[0][None], convw=gdn_conv_w[0],
        p1=_lane_row(-jnp.exp(gdn_a_log[0]), GDN_HEADS), p2=_lane_row(gdn_dt_bias[0], GDN_HEADS),
        normw=gdn_norm_w[0][None],
        wout_a=w_out_even[0][:S5_WIDTH].astype(BF16), wout_b=w_out_even[0][S5_WIDTH:].astype(BF16),
        win_odd=jnp.concatenate([perm_cols(wodd[:, :nk]), perm_cols(wodd[:, nk:2 * nk]), wodd[:, 2 * nk:]],
                                axis=1).astype(BF16),
        wout_odd=w_out_odd[0].astype(BF16),
        ln1_g=ln1_g[:, None], ln1_b=ln1_b[:, None], ln2_g=ln2_g[:, None], ln2_b=ln2_b[:, None],
        rw=jnp.pad(router_w, ((0, 0), (0, 0), (0, LANES - N_EXPERTS))),
        rb=jnp.pad(router_b, ((0, 0), (0, LANES - N_EXPERTS)))[:, None],
        w1=moe_w1.astype(BF16), b1=moe_b1[:, :, None], w2=moe_w2.astype(BF16), b2=moe_b2[:, :, None],
        plew=ple_w.astype(BF16), gatew=ple_gate_w.astype(BF16),
    )
    bp = x_prompt.shape[0]
    zeros = lambda *s: jnp.zeros(s, F32)
    outs_p = _run_group(x_prompt, p_prompt, zeros(1, bp, S5_GROUPS, S5_STATE), zeros(1, bp, S5_GROUPS, S5_STATE),
                        zeros(1, bp, GDN_HEADS, GDN_DK, GDN_DV), zeros(1, bp, GDN_CONV - 1, GDN_QKV),
                        zeros(1, bp, RET_HEADS, RET_DK, RET_DV), 0.0, W, tm=512, blk=512)
    outs_s = _run_group(x_sample, p_sample, state_s5_re, state_s5_im, state_gdn, state_gdn_conv, state_ret,
                        float(PAST_LEN), W, tm=128, blk=128)
    dp = x_prompt.dtype
    y_p, p_re, p_im, p_gdn, p_conv, p_ret = outs_p
    y_s, s_re, s_im, s_gdn, s_conv, s_ret = outs_s
    return (y_p.astype(dp), y_s.astype(x_sample.dtype),
            p_re.astype(dp), p_im.astype(dp), p_gdn.astype(dp), p_conv.astype(dp), p_ret.astype(dp),
            s_re.astype(state_s5_re.dtype), s_im.astype(state_s5_im.dtype), s_gdn.astype(state_gdn.dtype),
            s_conv.astype(state_gdn_conv.dtype), s_ret.astype(state_ret.dtype))
```

```python
import functools
import math

import jax
import jax.numpy as jnp
import numpy as np
from jax import lax
from jax.experimental import pallas as pl
from jax.experimental.pallas import tpu as pltpu

F32 = jnp.float32
BF16 = jnp.bfloat16
HIGHEST = lax.Precision.HIGHEST

D_MODEL = 1024
CHUNK = 64
S5_WIDTH = 512
S5_GROUP = 16
S5_GROUPS = 32
S5_STATE = 64
S5_TILES = 16
GDN_HEADS = 4
GDN_DK = 128
GDN_DV = 128
GDN_CONV = 4
GDN_QKV = 1536
RET_HEADS = 4
RET_DK = 256
RET_DV = 512
ROPE_BASE = 10000.0
N_EXPERTS = 32
TOP_K = 4
D_FF = 1024
SWIGLU_LIMIT = 7.0
SWIGLU_ALPHA = 1.702
PLE_DIM = 256
DEPTH = 2
PAST_LEN = 1024
DEEPNORM_ALPHA = (2 * DEPTH) ** 0.25
LN_EPS = 1e-5
NORM_EPS = 1e-6

LANES = 128
SUBLANES = 8
VMEM_LIMIT = 48 * 1024 * 1024


def _params(sem):
    return pltpu.CompilerParams(dimension_semantics=sem, vmem_limit_bytes=VMEM_LIMIT)


def _dot(a, b):
    return jnp.dot(a, b, preferred_element_type=F32)


def _dot_hi(a, b):
    return jnp.dot(a, b, preferred_element_type=F32, precision=HIGHEST)


def _dot_nt(a, b, precision=None):
    return lax.dot_general(a, b, (((1,), (1,)), ((), ())), preferred_element_type=F32,
                           precision=precision)


def _dot_tn(a, b):
    return lax.dot_general(a, b, (((0,), (0,)), ((), ())), preferred_element_type=F32)


def _sigmoid(x):
    return 1.0 / (1.0 + jnp.exp(-x))


def _full(shape):
    nd = len(shape)
    return pl.BlockSpec(shape, lambda *_: (0,) * nd)


def _proj_even_kernel(x_ref, wu_ref, wqkv_ref, wz_ref, wba_ref, u_ref, qkv_ref, z_ref, ba_ref):
    xb = x_ref[...].astype(BF16)
    u_ref[...] = _dot(xb, wu_ref[...])
    qkv_ref[...] = _dot(xb, wqkv_ref[...])
    z_ref[...] = _dot(xb, wz_ref[...])
    ba_ref[...] = _dot(xb, wba_ref[...])


def _proj_even(x, wu, wqkv, wz, wba, tm):
    t = x.shape[0]
    row = lambda n: pl.BlockSpec((tm, n), lambda i: (i, 0))
    return pl.pallas_call(
        _proj_even_kernel,
        grid=(t // tm,),
        in_specs=[row(D_MODEL), _full(wu.shape), _full(wqkv.shape), _full(wz.shape), _full(wba.shape)],
        out_specs=[row(S5_WIDTH), row(GDN_QKV), row(GDN_HEADS * GDN_DV), row(LANES)],
        out_shape=[jax.ShapeDtypeStruct((t, S5_WIDTH), F32), jax.ShapeDtypeStruct((t, GDN_QKV), F32),
                   jax.ShapeDtypeStruct((t, GDN_HEADS * GDN_DV), F32), jax.ShapeDtypeStruct((t, LANES), F32)],
        compiler_params=_params(("parallel",)),
        name="proj_even",
    )(x, wu, wqkv, wz, wba)


def _proj_odd_kernel(x_ref, w_ref, cos_ref, sin_ref, o_ref):
    j = pl.program_id(0)
    acc = _dot(x_ref[...].astype(BF16), w_ref[...])

    @pl.when(j == 0)
    def _():
        cos, sin = cos_ref[...], sin_ref[...]
        half = RET_DK // 2
        for h in range(2 * RET_HEADS):
            x0 = acc[:, h * RET_DK:h * RET_DK + half]
            x1 = acc[:, h * RET_DK + half:(h + 1) * RET_DK]
            scale = 1.0 if h < RET_HEADS else RET_DK ** -0.5
            o_ref[:, h * RET_DK:h * RET_DK + half] = (x0 * cos - x1 * sin) * scale
            o_ref[:, h * RET_DK + half:(h + 1) * RET_DK] = (x0 * sin + x1 * cos) * scale

    @pl.when(j != 0)
    def _():
        o_ref[...] = acc


def _proj_odd(x, w, cos, sin, tm):
    t = x.shape[0]
    nblk = w.shape[1] // 2048
    return pl.pallas_call(
        _proj_odd_kernel,
        grid=(nblk, t // tm),
        in_specs=[pl.BlockSpec((tm, D_MODEL), lambda j, i: (i, 0)),
                  pl.BlockSpec((D_MODEL, 2048), lambda j, i: (0, j)),
                  pl.BlockSpec((tm, LANES), lambda j, i: (i, 0)),
                  pl.BlockSpec((tm, LANES), lambda j, i: (i, 0))],
        out_specs=pl.BlockSpec((tm, 2048), lambda j, i: (i, j)),
        out_shape=jax.ShapeDtypeStruct((t, w.shape[1]), F32),
        compiler_params=_params(("parallel", "parallel")),
        name="proj_odd",
    )(x, w, cos, sin)


def _layer_norm(r, g, b):
    mu = jnp.mean(r, -1, keepdims=True)
    d = r - mu
    var = jnp.mean(d * d, -1, keepdims=True)
    return d * lax.rsqrt(var + LN_EPS) * g + b


def _outproj_ln_kernel(*refs, n_in):
    a_refs = refs[:n_in]
    w_refs = refs[n_in:2 * n_in]
    x_ref, g_ref, b_ref, o_ref = refs[2 * n_in:]
    acc = _dot(a_refs[0][...].astype(BF16), w_refs[0][...])
    for a_ref, w_ref in zip(a_refs[1:], w_refs[1:]):
        acc = acc + _dot(a_ref[...].astype(BF16), w_ref[...])
    o_ref[...] = _layer_norm(DEEPNORM_ALPHA * x_ref[...] + acc, g_ref[...], b_ref[...])


def _outproj_ln(acts, ws, x, g, b, tm):
    t = x.shape[0]
    row = lambda n: pl.BlockSpec((tm, n), lambda i: (i, 0))
    return pl.pallas_call(
        functools.partial(_outproj_ln_kernel, n_in=len(acts)),
        grid=(t // tm,),
        in_specs=[row(a.shape[1]) for a in acts] + [_full(w.shape) for w in ws]
                 + [row(D_MODEL), _full((1, D_MODEL)), _full((1, D_MODEL))],
        out_specs=row(D_MODEL),
        out_shape=jax.ShapeDtypeStruct((t, D_MODEL), F32),
        compiler_params=_params(("parallel",)),
        name="outproj_ln",
    )(*acts, *ws, x, g, b)


def _s5_kernel(u_ref, h0_ref, bmat_ref, cmat_ref, acoef_ref, dskip_ref, wglu_ref, bglu_ref,
               y_ref, hout_ref, upad, sre, sim, yacc, hst, *, nb, lt, pitch):
    tb = pl.program_id(0)

    @pl.when(tb == 0)
    def _():
        hst[...] = h0_ref[...]
        upad[...] = jnp.zeros_like(upad)

    for b in range(nb):
        upad[b * pitch:b * pitch + lt, :] = u_ref[b]
    yacc[...] = jnp.zeros_like(yacc)
    halves = nb // SUBLANES

    def tile_body(j, carry):
        q = j // 4
        ub = upad[:, pl.ds(pl.multiple_of(q * LANES, LANES), LANES)].astype(BF16)
        bu = _dot(ub, bmat_ref[j])
        sre[...] = bu[:, :LANES]
        sim[...] = bu[:, LANES:]
        ac = acoef_ref[j]
        ar, ai = ac[:, :LANES], ac[:, LANES:]
        hs = [(hst[j, hf * SUBLANES:(hf + 1) * SUBLANES, 0:LANES],
               hst[j, hf * SUBLANES:(hf + 1) * SUBLANES, LANES:2 * LANES]) for hf in range(halves)]
        for t in range(lt):
            for hf in range(halves):
                hr, hi = hs[hf]
                rows = pl.ds(hf * SUBLANES * pitch + t, SUBLANES, stride=pitch)
                nhr = ar * hr - ai * hi + sre[rows, :]
                nhi = ar * hi + ai * hr + sim[rows, :]
                sre[rows, :] = nhr
                sim[rows, :] = nhi
                hs[hf] = (nhr, nhi)
        for hf in range(halves):
            hst[j, hf * SUBLANES:(hf + 1) * SUBLANES, 0:LANES] = hs[hf][0]
            hst[j, hf * SUBLANES:(hf + 1) * SUBLANES, LANES:2 * LANES] = hs[hf][1]
        st = jnp.concatenate([sre[...], sim[...]], axis=-1).astype(BF16)
        yacc[q] += _dot(st, cmat_ref[j])
        return carry

    lax.fori_loop(0, S5_TILES, tile_body, 0)
    y = jnp.concatenate([yacc[0], yacc[1], yacc[2], yacc[3]], axis=-1) + dskip_ref[...] * upad[...]
    y = jax.nn.gelu(y)
    y = y * _sigmoid(_dot(y.astype(BF16), wglu_ref[...]) + bglu_ref[...])
    for b in range(nb):
        y_ref[b] = y[b * pitch:b * pitch + lt]
    hout_ref[...] = hst[...]


def _s5(u3, h0, bmat, cmat, acoef, dskip, wglu, bglu, lt):
    nb, L, _ = u3.shape
    pitch = lt + SUBLANES
    rows = nb * pitch
    return pl.pallas_call(
        functools.partial(_s5_kernel, nb=nb, lt=lt, pitch=pitch),
        grid=(L // lt,),
        in_specs=[pl.BlockSpec((nb, lt, S5_WIDTH), lambda i: (0, i, 0)),
                  _full(h0.shape), _full(bmat.shape), _full(cmat.shape), _full(acoef.shape),
                  _full(dskip.shape), _full(wglu.shape), _full(bglu.shape)],
        out_specs=[pl.BlockSpec((nb, lt, S5_WIDTH), lambda i: (0, i, 0)), _full(h0.shape)],
        out_shape=[jax.ShapeDtypeStruct((nb, L, S5_WIDTH), F32), jax.ShapeDtypeStruct(h0.shape, F32)],
        scratch_shapes=[pltpu.VMEM((rows, S5_WIDTH), F32), pltpu.VMEM((rows, LANES), F32),
                        pltpu.VMEM((rows, LANES), F32), pltpu.VMEM((4, rows, LANES), F32),
                        pltpu.VMEM(h0.shape, F32)],
        compiler_params=_params(("arbitrary",)),
        name="s5_scan",
    )(u3, h0, bmat, cmat, acoef, dskip, wglu, bglu)


def _s5_weights(a_re, a_im, log_dt, b_re, b_im, c_re, c_im):
    dt = jnp.exp(log_dt)[:, None]
    lr, li = a_re * dt, a_im * dt
    mag = jnp.exp(lr)
    ab_re, ab_im = mag * jnp.cos(li), mag * jnp.sin(li)
    den = a_re * a_re + a_im * a_im
    cf_re = ((ab_re - 1.0) * a_re + ab_im * a_im) / den
    cf_im = (ab_im * a_re - (ab_re - 1.0) * a_im) / den
    bb_re = cf_re[..., None] * b_re - cf_im[..., None] * b_im
    bb_im = cf_re[..., None] * b_im + cf_im[..., None] * b_re
    jj = np.arange(S5_TILES)[:, None, None]
    lg = np.arange(8)[None, :, None]
    gi = np.arange(2)[None, None, :]
    sel = jnp.asarray((lg == 2 * (jj % 4) + gi).astype(np.float32))
    tiles = lambda w: w.reshape(S5_TILES, 2, *w.shape[1:])
    bt = lambda w: jnp.einsum('jlg,jgpn->jlngp', sel, tiles(w)).reshape(S5_TILES, LANES, LANES)
    bmat = jnp.concatenate([bt(bb_re), bt(bb_im)], axis=-1)
    ct = lambda w: jnp.einsum('jlg,jgnp->jgpln', sel, tiles(w)).reshape(S5_TILES, LANES, LANES)
    cmat = jnp.concatenate([ct(c_re), -ct(c_im)], axis=1)
    acoef = jnp.concatenate([ab_re.reshape(S5_TILES, LANES), ab_im.reshape(S5_TILES, LANES)], axis=-1)
    acoef = jnp.broadcast_to(acoef[:, None, :], (S5_TILES, SUBLANES, 2 * LANES))
    return bmat.astype(BF16), cmat.astype(BF16), acoef


def _s5_state_in(h_re, h_im):
    nb = h_re.shape[0]
    h = jnp.concatenate([h_re.reshape(nb, S5_TILES, LANES), h_im.reshape(nb, S5_TILES, LANES)], axis=-1)
    return jnp.transpose(h, (1, 0, 2))


def _s5_state_out(h):
    nb = h.shape[1]
    h = jnp.transpose(h, (1, 0, 2))
    return (h[..., :LANES].reshape(nb, S5_GROUPS, S5_STATE), h[..., LANES:].reshape(nb, S5_GROUPS, S5_STATE))


def _split_bf16(a):
    hi = a.astype(BF16)
    return hi, (a - hi.astype(F32)).astype(BF16)


def _bdot(a, b):
    return lax.dot_general(a, b, (((2,), (1,)), ((0,), (0,))), preferred_element_type=F32)


def _bdot_nt(a, b):
    return lax.dot_general(a, b, (((2,), (2,)), ((0,), (0,))), preferred_element_type=F32)


def _bdot_split(a, b):
    ah, al = _split_bf16(a)
    bh, bl = _split_bf16(b)
    return _bdot(ah, bh) + _bdot(ah, bl) + _bdot(al, bh)


def _unit_lower_inverse(nmat):
    lc = nmat.shape[-1]
    ri = lax.broadcasted_iota(jnp.int32, nmat.shape, 1)
    ci = lax.broadcasted_iota(jnp.int32, nmat.shape, 2)
    base = 16
    dmat = jnp.where(ri // base == ci // base, nmat, 0.0)
    inv = jnp.where(ri == ci, 1.0, 0.0) - dmat
    pw = dmat
    for _ in range(3):
        pw = _bdot_split(pw, pw)
        inv = inv + _bdot_split(inv, pw)
    size = base
    while size < lc:
        off = jnp.where(ri // (2 * size) == ci // (2 * size), jnp.where(ri // size > ci // size, nmat, 0.0), 0.0)
        inv = inv - _bdot_split(_bdot_split(inv, off), inv)
        size *= 2
    return inv


def _gdn_local_kernel(qkv_ref, ba_ref, ctx_ref, cw_ref, p1_ref, p2_ref, ltri_ref,
                      u0_ref, w_ref, qd_ref, kd_ref, attn_ref, g_ref, cout_ref, xpad, *, lc, cp):
    c = pl.program_id(1)
    rb = lc * cp

    @pl.when(c == 0)
    def _():
        xpad[0:SUBLANES, :] = ctx_ref[0]

    xpad[SUBLANES:SUBLANES + rb, :] = qkv_ref[0]
    cw = cw_ref[...]
    conv = (cw[3:4] * xpad[8:8 + rb, :] + cw[2:3] * xpad[7:7 + rb, :]
            + cw[1:2] * xpad[6:6 + rb, :] + cw[0:1] * xpad[5:5 + rb, :])
    tail = xpad[rb:rb + SUBLANES, :]
    xpad[0:SUBLANES, :] = tail
    cout_ref[0] = tail
    a = conv * _sigmoid(conv)

    ba = ba_ref[0]
    beta_all = _sigmoid(ba)
    sp_in = ba + p2_ref[...]
    softplus = jnp.maximum(sp_in, 0.0) + jnp.log(1.0 + jnp.exp(-jnp.abs(sp_in)))
    g_all = p1_ref[...] * softplus
    g_hi = g_all.astype(BF16)
    g_r = g_all - g_hi.astype(F32)
    g_mid = g_r.astype(BF16)
    g_lo = (g_r - g_mid.astype(F32)).astype(BF16)
    lt = ltri_ref[...]
    G = _dot(lt, g_hi) + _dot(lt, g_mid) + _dot(lt, g_lo)
    g_ref[0] = G
    GT = G.T
    pairs = [(h, cc) for h in range(GDN_HEADS) for cc in range(cp)]
    qs, ks, vs, betas, gcols, grows, glasts = [], [], [], [], [], [], []
    for h in range(GDN_HEADS):
        qa = a[:, h * GDN_DK:(h + 1) * GDN_DK]
        ka = a[:, (GDN_HEADS + h) * GDN_DK:(GDN_HEADS + h + 1) * GDN_DK]
        va = a[:, (2 * GDN_HEADS + h) * GDN_DK:(2 * GDN_HEADS + h + 1) * GDN_DK]
        qa = qa * lax.rsqrt(jnp.sum(qa * qa, -1, keepdims=True) + NORM_EPS) * (GDN_DK ** -0.5)
        ka = ka * lax.rsqrt(jnp.sum(ka * ka, -1, keepdims=True) + NORM_EPS)
        for cc in range(cp):
            rows = slice(cc * lc, (cc + 1) * lc)
            qs.append(qa[rows])
            ks.append(ka[rows])
            vs.append(va[rows])
            betas.append(beta_all[rows, h:h + 1])
            gcols.append(G[rows, GDN_HEADS + h:GDN_HEADS + h + 1])
            grows.append(GT[GDN_HEADS + h:GDN_HEADS + h + 1, cc * lc:(cc + 1) * lc])
            glasts.append(GT[GDN_HEADS + h:GDN_HEADS + h + 1, (cc + 1) * lc - 1:(cc + 1) * lc])
    q3, k3, v3 = jnp.stack(qs), jnp.stack(ks), jnp.stack(vs)
    beta3, gcol3 = jnp.stack(betas), jnp.stack(gcols)
    grow3, glast3 = jnp.stack(grows), jnp.stack(glasts)
    shape3 = (len(pairs), lc, lc)
    ri = lax.broadcasted_iota(jnp.int32, shape3, 1)
    ci = lax.broadcasted_iota(jnp.int32, shape3, 2)
    incl = ri >= ci
    dec3 = jnp.where(incl, jnp.exp(jnp.where(incl, gcol3 - grow3, 0.0)), 0.0)
    eg3 = jnp.exp(gcol3)
    kb3 = k3 * beta3
    kbf3 = k3.astype(BF16)
    nmat3 = jnp.where(ri > ci, _bdot_nt(kb3.astype(BF16), kbf3) * dec3, 0.0)
    inv3 = _unit_lower_inverse(nmat3)
    sol3 = _bdot_split(inv3, jnp.concatenate([v3 * beta3, kb3 * eg3], axis=-1))
    w3 = sol3[:, :, GDN_DV:].astype(BF16)
    qd3 = (q3 * eg3).astype(BF16)
    kd3 = (k3 * jnp.exp(glast3 - gcol3)).astype(BF16)
    attn3 = (_bdot_nt(q3.astype(BF16), kbf3) * dec3).astype(BF16)
    for i, (h, cc) in enumerate(pairs):
        rows = slice(cc * lc, (cc + 1) * lc)
        cols = slice(h * GDN_DV, (h + 1) * GDN_DV)
        u0_ref[0, rows, cols] = sol3[i, :, :GDN_DV]
        w_ref[0, rows, cols] = w3[i]
        qd_ref[0, rows, cols] = qd3[i]
        kd_ref[0, rows, cols] = kd3[i]
        attn_ref[0, rows, h * lc:(h + 1) * lc] = attn3[i]


def _gdn_seq_kernel(u0_ref, w_ref, qd_ref, kd_ref, attn_ref, g_ref, z_ref, s0_ref, nw_ref,
                    y_ref, sout_ref, S, *, lc, nbb):
    c = pl.program_id(1)

    @pl.when(c == 0)
    def _():
        S[...] = s0_ref[...]

    nw = nw_ref[...]
    pairs = [(bb, h) for bb in range(nbb) for h in range(GDN_HEADS)]
    hcols = lambda h: slice(h * GDN_DV, (h + 1) * GDN_DV)
    stack = lambda f: jnp.stack([f(bb, h) for bb, h in pairs])
    dlast = jnp.exp(g_ref[:, lc - 1:lc, :])
    S3 = S[...].reshape(len(pairs), GDN_DK, GDN_DV)
    wq3 = stack(lambda bb, h: jnp.concatenate([w_ref[bb, :, hcols(h)], qd_ref[bb, :, hcols(h)]], axis=0))
    r3 = _bdot(wq3, S3.astype(BF16))
    ub3 = (stack(lambda bb, h: u0_ref[bb, :, hcols(h)]) - r3[:, :lc]).astype(BF16)
    o3 = r3[:, lc:] + _bdot(stack(lambda bb, h: attn_ref[bb, :, h * lc:(h + 1) * lc]), ub3)
    d3 = stack(lambda bb, h: dlast[bb, :, GDN_HEADS + h:GDN_HEADS + h + 1])
    kd3 = stack(lambda bb, h: kd_ref[bb, :, hcols(h)])
    kdu3 = lax.dot_general(kd3, ub3, (((1,), (1,)), ((0,), (0,))), preferred_element_type=F32)
    S[...] = (d3 * S3 + kdu3).reshape(S.shape)
    o3 = o3 * lax.rsqrt(jnp.mean(o3 * o3, -1, keepdims=True) + NORM_EPS) * nw
    for i, (bb, h) in enumerate(pairs):
        zh = z_ref[bb, :, hcols(h)]
        y_ref[bb, :, hcols(h)] = o3[i] * (zh * _sigmoid(zh))

    @pl.when(c == pl.num_programs(1) - 1)
    def _():
        sout_ref[...] = S[...]


def _gdn(qkv3, z3, ba3, ctx8, s0, cw, p1, p2, nw, lc, cp, nbb):
    nb, L, _ = qkv3.shape
    rb = lc * cp
    hd = GDN_HEADS * GDN_DV
    ltri = jnp.asarray(np.kron(np.eye(cp, dtype=np.float32), np.tril(np.ones((lc, lc), np.float32)))).astype(BF16)
    blk = lambda n: pl.BlockSpec((1, rb, n), lambda b, c: (b, c, 0))
    per_b = lambda shape: pl.BlockSpec((1,) + shape, lambda b, c: (b,) + (0,) * len(shape))
    cst = lambda shape: pl.BlockSpec(shape, lambda b, c: (0,) * len(shape))
    sds = lambda n, dt: jax.ShapeDtypeStruct((nb, L, n), dt)
    u0, w, qd, kd, attn, G, cout = pl.pallas_call(
        functools.partial(_gdn_local_kernel, lc=lc, cp=cp),
        grid=(nb, L // rb),
        in_specs=[blk(GDN_QKV), blk(LANES), per_b((SUBLANES, GDN_QKV)), cst(cw.shape), cst(p1.shape),
                  cst(p2.shape), cst(ltri.shape)],
        out_specs=[blk(hd), blk(hd), blk(hd), blk(hd), blk(GDN_HEADS * lc), blk(LANES),
                   per_b((SUBLANES, GDN_QKV))],
        out_shape=[sds(hd, F32), sds(hd, BF16), sds(hd, BF16), sds(hd, BF16), sds(GDN_HEADS * lc, BF16),
                   sds(LANES, F32), jax.ShapeDtypeStruct((nb, SUBLANES, GDN_QKV), F32)],
        scratch_shapes=[pltpu.VMEM((rb + SUBLANES, GDN_QKV), F32)],
        compiler_params=_params(("parallel", "arbitrary")),
        name="gdn_local",
    )(qkv3, ba3, ctx8, cw, p1, p2, ltri)
    sblk = lambda n: pl.BlockSpec((nbb, lc, n), lambda b, c: (b, c, 0))
    state = pl.BlockSpec((nbb, GDN_HEADS, GDN_DK, GDN_DV), lambda b, c: (b, 0, 0, 0))
    y, s_new = pl.pallas_call(
        functools.partial(_gdn_seq_kernel, lc=lc, nbb=nbb),
        grid=(nb // nbb, L // lc),
        in_specs=[sblk(hd), sblk(hd), sblk(hd), sblk(hd), sblk(GDN_HEADS * lc), sblk(LANES), sblk(hd),
                  state, cst(nw.shape)],
        out_specs=[sblk(hd), state],
        out_shape=[sds(hd, F32), jax.ShapeDtypeStruct(s0.shape, F32)],
        scratch_shapes=[pltpu.VMEM((nbb, GDN_HEADS, GDN_DK, GDN_DV), F32)],
        compiler_params=_params(("parallel", "arbitrary")),
        name="gdn_seq",
    )(u0, w, qd, kd, attn, G, z3, s0, nw)
    return y, s_new, cout


def _ret_kernel(q_ref, k_ref, v_ref, g_ref, r0_ref, dec_ref, qs_ref, ks_ref, cd_ref, o_ref, rout_ref, R,
                *, nbb):
    c = pl.program_id(1)

    @pl.when(c == 0)
    def _():
        R[...] = r0_ref[...]

    pairs = [(bb, h) for bb in range(nbb) for h in range(RET_HEADS)]
    stack = lambda f: jnp.stack([f(bb, h) for bb, h in pairs])
    kcols = lambda h: slice(h * RET_DK, (h + 1) * RET_DK)
    vcols = lambda h: slice(h * RET_DV, (h + 1) * RET_DV)
    q3 = stack(lambda bb, h: q_ref[bb, :, kcols(h)])
    k3 = stack(lambda bb, h: k_ref[bb, :, kcols(h)])
    v3 = stack(lambda bb, h: v_ref[bb, :, vcols(h)]).astype(BF16)
    dec3 = stack(lambda bb, h: dec_ref[h])
    qs3 = stack(lambda bb, h: qs_ref[h])
    ks3 = stack(lambda bb, h: ks_ref[h])
    cd3 = stack(lambda bb, h: cd_ref[h])
    R3 = R[...].reshape(len(pairs), RET_DK, RET_DV)
    s3 = _bdot_nt(q3.astype(BF16), k3.astype(BF16)) * dec3
    o3 = _bdot(s3.astype(BF16), v3) + _bdot((q3 * qs3).astype(BF16), R3.astype(BF16))
    kv3 = lax.dot_general((k3 * ks3).astype(BF16), v3, (((1,), (1,)), ((0,), (0,))), preferred_element_type=F32)
    R[...] = (cd3 * R3 + kv3).reshape(R.shape)
    mu = jnp.mean(o3, -1, keepdims=True)
    d3 = o3 - mu
    var = jnp.mean(d3 * d3, -1, keepdims=True)
    on3 = d3 * lax.rsqrt(var + LN_EPS)
    for i, (bb, h) in enumerate(pairs):
        gt = g_ref[bb, :, vcols(h)]
        o_ref[bb, :, vcols(h)] = gt * _sigmoid(gt) * on3[i]

    @pl.when(c == pl.num_programs(1) - 1)
    def _():
        rout_ref[...] = R[...]


def _retention(proj3, r0, lc, nbb):
    nb, L, _ = proj3.shape
    log_g = np.log(1.0 - 2.0 ** (-5.0 - np.arange(RET_HEADS, dtype=np.float64)))
    idx = np.arange(lc, dtype=np.float64)
    dec = np.exp(log_g[:, None, None] * np.abs(idx[:, None] - idx[None, :])).astype(np.float32)
    qs = np.exp(log_g[:, None] * (idx + 1.0)).astype(np.float32)[..., None]
    ks = np.exp(log_g[:, None] * (lc - 1.0 - idx)).astype(np.float32)[..., None]
    cdec = np.exp(log_g * lc).astype(np.float32)[:, None, None]
    nqk = RET_HEADS * RET_DK
    nv = RET_HEADS * RET_DV
    cst = lambda shape: pl.BlockSpec(shape, lambda b, c: (0,) * len(shape))
    state = pl.BlockSpec((nbb, RET_HEADS, RET_DK, RET_DV), lambda b, c: (b, 0, 0, 0))
    return pl.pallas_call(
        functools.partial(_ret_kernel, nbb=nbb),
        grid=(nb // nbb, L // lc),
        in_specs=[pl.BlockSpec((nbb, lc, nqk), lambda b, c: (b, c, 0)),
                  pl.BlockSpec((nbb, lc, nqk), lambda b, c: (b, c, 1)),
                  pl.BlockSpec((nbb, lc, nv), lambda b, c: (b, c, 1)),
                  pl.BlockSpec((nbb, lc, nv), lambda b, c: (b, c, 2)),
                  state, cst(dec.shape), cst(qs.shape), cst(ks.shape), cst(cdec.shape)],
        out_specs=[pl.BlockSpec((nbb, lc, nv), lambda b, c: (b, c, 0)), state],
        out_shape=[jax.ShapeDtypeStruct((nb, L, nv), F32), jax.ShapeDtypeStruct(r0.shape, F32)],
        scratch_shapes=[pltpu.VMEM((nbb, RET_HEADS, RET_DK, RET_DV), F32)],
        compiler_params=_params(("parallel", "arbitrary")),
        name="retention",
    )(proj3, proj3, proj3, proj3, r0, jnp.asarray(dec), jnp.asarray(qs), jnp.asarray(ks), jnp.asarray(cdec))


def _router_kernel(x_ref, rw_ref, rb_ref, lst_ref, ri_ref, gt_ref, cnt_ref, run):
    i = pl.program_id(0)

    @pl.when(i == 0)
    def _():
        run[...] = jnp.zeros_like(run)

    tm = x_ref.shape[0]
    lane = lax.broadcasted_iota(jnp.int32, (tm, LANES), 1)
    lane_f = lane.astype(F32)
    logits = _dot_hi(x_ref[...], rw_ref[...]) + rb_ref[...]
    logits = jnp.where(lane < N_EXPERTS, logits, -jnp.inf)
    vals, hots = [], []
    for _ in range(TOP_K):
        m = jnp.max(logits, -1, keepdims=True)
        first = jnp.min(jnp.where(logits == m, lane_f, float(LANES)), -1, keepdims=True)
        hot = lane_f == first
        vals.append(m)
        hots.append(hot)
        logits = jnp.where(hot, -jnp.inf, logits)
    es = [jnp.exp(v - vals[0]) for v in vals]
    den = es[0] + es[1] + es[2] + es[3]
    multi = jnp.zeros((tm, LANES), F32)
    for hot in hots:
        multi = multi + hot.astype(F32)
    before = _dot(lst_ref[...], multi.astype(BF16)) + run[...]
    ri = jnp.zeros((tm, LANES), F32)
    gt = jnp.zeros((tm, LANES), F32)
    for kk in range(TOP_K):
        idx = jnp.sum(jnp.where(hots[kk], lane_f, 0.0), -1, keepdims=True)
        rank = jnp.sum(jnp.where(hots[kk], before, 0.0), -1, keepdims=True)
        ri = jnp.where(lane == kk, idx, ri)
        ri = jnp.where(lane == TOP_K + kk, rank, ri)
        gt = jnp.where(lane == kk, es[kk] / den, gt)
    ri_ref[...] = ri.astype(jnp.int32)
    gt_ref[...] = gt
    run[...] = run[...] + jnp.sum(multi, 0, keepdims=True)
    cnt_ref[...] = run[...]


def _router(x, rw, rb, tm):
    t = x.shape[0]
    lst = jnp.asarray(np.tril(np.ones((tm, tm), np.float32), -1)).astype(BF16)
    row = lambda n: pl.BlockSpec((tm, n), lambda i: (i, 0))
    return pl.pallas_call(
        _router_kernel,
        grid=(t // tm,),
        in_specs=[row(D_MODEL), _full(rw.shape), _full(rb.shape), _full(lst.shape)],
        out_specs=[row(LANES), row(LANES), _full((1, LANES))],
        out_shape=[jax.ShapeDtypeStruct((t, LANES), jnp.int32), jax.ShapeDtypeStruct((t, LANES), F32),
                   jax.ShapeDtypeStruct((1, LANES), F32)],
        scratch_shapes=[pltpu.VMEM((1, LANES), F32)],
        compiler_params=_params(("arbitrary",)),
        name="router",
    )(x, rw, rb, lst)


def _dispatch_kernel(dest_ref, x_ref, xs_in, xs_out, sem):
    del xs_in
    tm = x_ref.shape[0]

    def body(r, carry):
        for kk in range(TOP_K):
            d = dest_ref[TOP_K * r + kk]
            pltpu.make_async_copy(x_ref.at[pl.ds(r, 1)], xs_out.at[pl.ds(d, 1)], sem).start()
        return carry

    lax.fori_loop(0, tm, body, 0)
    for _ in range(TOP_K):
        pltpu.make_async_copy(x_ref, xs_out.at[pl.ds(0, tm)], sem).wait()


def _dispatch(dest_flat, x, n_pad, tm):
    t = x.shape[0]
    xs0 = jnp.zeros((n_pad, D_MODEL), F32)
    return pl.pallas_call(
        _dispatch_kernel,
        grid=(t // tm,),
        in_specs=[pl.BlockSpec((tm * TOP_K,), lambda i: (i,), memory_space=pltpu.SMEM),
                  pl.BlockSpec((tm, D_MODEL), lambda i: (i, 0)),
                  pl.BlockSpec(memory_space=pl.ANY)],
        out_specs=pl.BlockSpec(memory_space=pl.ANY),
        out_shape=jax.ShapeDtypeStruct((n_pad, D_MODEL), F32),
        scratch_shapes=[pltpu.SemaphoreType.DMA],
        input_output_aliases={2: 0},
        compiler_params=_params(("arbitrary",)),
        name="moe_dispatch",
    )(dest_flat, x, xs0)


def _expert_kernel(be_ref, nu_ref, xs_ref, w1_ref, b1_ref, w2_ref, b2_ref, ys_ref):
    del be_ref

    @pl.when(pl.program_id(0) < nu_ref[0])
    def _():
        h = _dot(xs_ref[...].astype(BF16), w1_ref[0]) + b1_ref[0]
        glu = jnp.minimum(h[:, :D_FF], SWIGLU_LIMIT)
        lin = jnp.clip(h[:, D_FF:], -SWIGLU_LIMIT, SWIGLU_LIMIT)
        act = glu * _sigmoid(SWIGLU_ALPHA * glu) * (lin + 1.0)
        ys_ref[...] = _dot(act.astype(BF16), w2_ref[0]) + b2_ref[0]

    @pl.when(pl.program_id(0) >= nu_ref[0])
    def _():
        ys_ref[...] = jnp.zeros_like(ys_ref)


def _experts(blk_e, n_used, xs, w1, b1, w2, b2, blk):
    n_pad = xs.shape[0]
    n_blk = n_pad // blk
    used = lambda i, nu: jnp.minimum(i, nu[0] - 1)
    grid_spec = pltpu.PrefetchScalarGridSpec(
        num_scalar_prefetch=2,
        grid=(n_blk,),
        in_specs=[pl.BlockSpec((blk, D_MODEL), lambda i, be, nu: (used(i, nu), 0)),
                  pl.BlockSpec((1, D_MODEL, 2 * D_FF), lambda i, be, nu: (be[i], 0, 0)),
                  pl.BlockSpec((1, 1, 2 * D_FF), lambda i, be, nu: (be[i], 0, 0)),
                  pl.BlockSpec((1, D_FF, D_MODEL), lambda i, be, nu: (be[i], 0, 0)),
                  pl.BlockSpec((1, 1, D_MODEL), lambda i, be, nu: (be[i], 0, 0))],
        out_specs=pl.BlockSpec((blk, D_MODEL), lambda i, be, nu: (i, 0)),
    )
    return pl.pallas_call(
        _expert_kernel,
        grid_spec=grid_spec,
        out_shape=jax.ShapeDtypeStruct((n_pad, D_MODEL), F32),
        compiler_params=_params(("arbitrary",)),
        name="moe_experts",
    )(blk_e, n_used, xs, w1, b1, w2, b2)


def _combine_kernel(dest_ref, x_ref, gt_ref, p_ref, ys_hbm, g_ref, b_ref, plew_ref, gatew_ref, o_ref,
                    buf, sem):
    tm = x_ref.shape[0]

    def body(r, carry):
        for kk in range(TOP_K):
            d = dest_ref[TOP_K * r + kk]
            pltpu.make_async_copy(ys_hbm.at[pl.ds(d, 1)], buf.at[kk, pl.ds(r, 1)], sem).start()
        return carry

    lax.fori_loop(0, tm, body, 0)
    for kk in range(TOP_K):
        pltpu.make_async_copy(ys_hbm.at[pl.ds(0, tm)], buf.at[kk], sem).wait()
    gt = gt_ref[...]
    y = gt[:, 0:1] * buf[0]
    for kk in range(1, TOP_K):
        y = y + gt[:, kk:kk + 1] * buf[kk]
    x2 = _layer_norm(DEEPNORM_ALPHA * x_ref[...] + y, g_ref[...], b_ref[...])
    pp = _dot(p_ref[...].astype(BF16), plew_ref[...])
    gg = _sigmoid(_dot(x2.astype(BF16), gatew_ref[...]))
    o_ref[...] = x2 + pp * gg


def _combine(dest_flat, x, gt, p, ys, g, b, plew, gatew, tm):
    t = x.shape[0]
    row = lambda n: pl.BlockSpec((tm, n), lambda i: (i, 0))
    return pl.pallas_call(
        _combine_kernel,
        grid=(t // tm,),
        in_specs=[pl.BlockSpec((tm * TOP_K,), lambda i: (i,), memory_space=pltpu.SMEM),
                  row(D_MODEL), row(LANES), row(PLE_DIM), pl.BlockSpec(memory_space=pl.ANY),
                  _full((1, D_MODEL)), _full((1, D_MODEL)), _full(plew.shape), _full(gatew.shape)],
        out_specs=row(D_MODEL),
        out_shape=jax.ShapeDtypeStruct((t, D_MODEL), F32),
        scratch_shapes=[pltpu.VMEM((TOP_K, tm, D_MODEL), F32), pltpu.SemaphoreType.DMA],
        compiler_params=_params(("arbitrary",)),
        name="moe_combine",
    )(dest_flat, x, gt, p, ys, g, b, plew, gatew)


def _moe_ple(x, p, rw, rb, w1, b1, w2, b2, g, b, plew, gatew, tm, blk):
    t = x.shape[0]
    ri, gt, cnt = _router(x, rw, rb, tm)
    idx, rank = ri[:, :TOP_K], ri[:, TOP_K:2 * TOP_K]
    counts = cnt[0, :N_EXPERTS].astype(jnp.int32)
    padded = (counts + blk - 1) // blk * blk
    pend = jnp.cumsum(padded)
    pstart = pend - padded
    dest = (pstart[idx] + rank).reshape(-1).astype(jnp.int32)
    n_blk = -(-(t * TOP_K + N_EXPERTS * (blk - 1)) // blk)
    n_used = (pend[-1] // blk).astype(jnp.int32).reshape(1)
    blk_start = jnp.arange(n_blk, dtype=jnp.int32) * blk
    blk_e = jnp.minimum(jnp.sum((pend[None, :] <= blk_start[:, None]).astype(jnp.int32), axis=1), N_EXPERTS - 1)
    xs = _dispatch(dest, x, n_blk * blk, tm)
    ys = _experts(blk_e, n_used, xs, w1, b1, w2, b2, blk)
    return _combine(dest, x, gt, p, ys, g, b, plew, gatew, tm)


_ROT_PERM = np.concatenate([np.arange(0, RET_DK, 2), np.arange(1, RET_DK, 2)])
_ROT_INV = np.argsort(_ROT_PERM)


def _lane_row(vals, offset):
    row = jnp.zeros((1, LANES), F32)
    return row.at[0, offset:offset + vals.shape[0]].set(vals)


def _run_group(x3, p, s5_re, s5_im, gdn_s, conv_s, ret_s, pos0, W, tm, blk):
    nb, L, _ = x3.shape
    t = nb * L
    lc = L if L <= CHUNK else CHUNK
    x = x3.reshape(t, D_MODEL).astype(F32)

    u, qkv, z, ba = _proj_even(x, W['wu'], W['wqkv'], W['wz'], W['wba'], tm)
    yA3, h_new = _s5(u.reshape(nb, L, S5_WIDTH), _s5_state_in(s5_re[0].astype(F32), s5_im[0].astype(F32)),
                     W['bmat'], W['cmat'], W['acoef'], W['dskip'], W['wglu'], W['bglu'], lc)
    new_re, new_im = _s5_state_out(h_new)
    ctx8 = jnp.concatenate([jnp.zeros((nb, SUBLANES - (GDN_CONV - 1), GDN_QKV), F32), conv_s[0].astype(F32)], axis=1)
    yB3, new_gdn, cout = _gdn(qkv.reshape(nb, L, GDN_QKV), z.reshape(nb, L, -1), ba.reshape(nb, L, LANES),
                              ctx8, gdn_s[0].astype(F32), W['convw'], W['p1'], W['p2'], W['normw'], lc,
                              cp=min(4, L // lc), nbb=4)
    new_conv = cout[:, SUBLANES - (GDN_CONV - 1):, :]
    x = _outproj_ln([yA3.reshape(t, -1), yB3.reshape(t, -1)], [W['wout_a'], W['wout_b']], x,
                    W['ln1_g'][0], W['ln1_b'][0], tm)
    x = _moe_ple(x, p[0].reshape(t, PLE_DIM), W['rw'][0], W['rb'][0], W['w1'][0], W['b1'][0], W['w2'][0],
                 W['b2'][0], W['ln2_g'][0], W['ln2_b'][0], W['plew'][0], W['gatew'][0], tm, blk)

    pos = pos0 + jnp.arange(L, dtype=F32)
    freq = 1.0 / (ROPE_BASE ** jnp.linspace(0.0, 1.0, RET_DK // 2, dtype=F32))
    ang = pos[:, None] * freq[None]
    cos = jnp.broadcast_to(jnp.cos(ang)[None], (nb, L, RET_DK // 2)).reshape(t, RET_DK // 2)
    sin = jnp.broadcast_to(jnp.sin(ang)[None], (nb, L, RET_DK // 2)).reshape(t, RET_DK // 2)
    proj = _proj_odd(x, W['win_odd'], cos, sin, tm)
    r0 = ret_s[0].astype(F32)[:, :, _ROT_PERM, :]
    o3, r_new = _retention(proj.reshape(nb, L, -1), r0, lc, nbb=2)
    new_ret = r_new[:, :, _ROT_INV, :]
    x = _outproj_ln([o3.reshape(t, -1)], [W['wout_odd']], x, W['ln1_g'][1], W['ln1_b'][1], tm)
    x = _moe_ple(x, p[1].reshape(t, PLE_DIM), W['rw'][1], W['rb'][1], W['w1'][1], W['b1'][1], W['w2'][1],
                 W['b2'][1], W['ln2_g'][1], W['ln2_b'][1], W['plew'][1], W['gatew'][1], tm, blk)
    return (x.reshape(nb, L, D_MODEL), new_re[None], new_im[None], new_gdn[None], new_conv[None], new_ret[None])


def kernel(x_prompt, x_sample, state_s5_re, state_s5_im, state_gdn, state_gdn_conv, state_ret, p_prompt, p_sample, w_in_even, s5_a_re, s5_a_im, s5_log_dt, s5_b_re, s5_b_im, s5_c_re, s5_c_im, s5_d, s5_w_glu, s5_b_glu, gdn_conv_w, gdn_a_log, gdn_dt_bias, gdn_norm_w, w_out_even, w_in_odd, w_out_odd, ln1_g, ln1_b, ln2_g, ln2_b, router_w, router_b, moe_w1, moe_b1, moe_w2, moe_b2, ple_w, ple_gate_w):
    o1 = S5_WIDTH
    o2 = o1 + GDN_QKV
    o3 = o2 + GDN_HEADS * GDN_DV
    win = w_in_even[0]
    bmat, cmat, acoef = _s5_weights(s5_a_re[0], s5_a_im[0], s5_log_dt[0], s5_b_re[0], s5_b_im[0],
                                    s5_c_re[0], s5_c_im[0])
    wodd = w_in_odd[0]
    nk = RET_HEADS * RET_DK
    perm_cols = lambda w: w.reshape(D_MODEL, RET_HEADS, RET_DK)[:, :, _ROT_PERM].reshape(D_MODEL, nk)
    W = dict(
        wu=win[:, :o1].astype(BF16), wqkv=win[:, o1:o2].astype(BF16), wz=win[:, o2:o3].astype(BF16),
        wba=jnp.pad(win[:, o3:], ((0, 0), (0, LANES - 2 * GDN_HEADS))).astype(BF16),
        bmat=bmat, cmat=cmat, acoef=acoef, dskip=s5_d[0][None], wglu=s5_w_glu[0].astype(BF16),
        bglu=s5_b_glu[0][None], convw=gdn_conv_w[0],
        p1=_lane_row(-jnp.exp(gdn_a_log[0]), GDN_HEADS), p2=_lane_row(gdn_dt_bias[0], GDN_HEADS),
        normw=gdn_norm_w[0][None],
        wout_a=w_out_even[0][:S5_WIDTH].astype(BF16), wout_b=w_out_even[0][S5_WIDTH:].astype(BF16),
        win_odd=jnp.concatenate([perm_cols(wodd[:, :nk]), perm_cols(wodd[:, nk:2 * nk]), wodd[:, 2 * nk:]],
                                axis=1).astype(BF16),
        wout_odd=w_out_odd[0].astype(BF16),
        ln1_g=ln1_g[:, None], ln1_b=ln1_b[:, None], ln2_g=ln2_g[:, None], ln2_b=ln2_b[:, None],
        rw=jnp.pad(router_w, ((0, 0), (0, 0), (0, LANES - N_EXPERTS))),
        rb=jnp.pad(router_b, ((0, 0), (0, LANES - N_EXPERTS)))[:, None],
        w1=moe_w1.astype(BF16), b1=moe_b1[:, :, None], w2=moe_w2.astype(BF16), b2=moe_b2[:, :, None],
        plew=ple_w.astype(BF16), gatew=ple_gate_w.astype(BF16),
    )
    bp = x_prompt.shape[0]
    zeros = lambda *s: jnp.zeros(s, F32)
    outs_p = _run_group(x_prompt, p_prompt, zeros(1, bp, S5_GROUPS, S5_STATE), zeros(1, bp, S5_GROUPS, S5_STATE),
                        zeros(1, bp, GDN_HEADS, GDN_DK, GDN_DV), zeros(1, bp, GDN_CONV - 1, GDN_QKV),
                        zeros(1, bp, RET_HEADS, RET_DK, RET_DV), 0.0, W, tm=512, blk=512)
    outs_s = _run_group(x_sample, p_sample, state_s5_re, state_s5_im, state_gdn, state_gdn_conv, state_ret,
                        float(PAST_LEN), W, tm=128, blk=128)
    dp = x_prompt.dtype
    y_p, p_re, p_im, p_gdn, p_conv, p_ret = outs_p
    y_s, s_re, s_im, s_gdn, s_conv, s_ret = outs_s
    return (y_p.astype(dp), y_s.astype(x_sample.dtype),
            p_re.astype(dp), p_im.astype(dp), p_gdn.astype(dp), p_conv.astype(dp), p_ret.astype(dp),
            s_re.astype(state_s5_re.dtype), s_im.astype(state_s5_im.dtype), s_gdn.astype(state_gdn.dtype),
            s_conv.astype(state_gdn_conv.dtype), s_ret.astype(state_ret.dtype))
```

```python
import functools
import math

import jax
import jax.numpy as jnp
import numpy as np
from jax import lax
from jax.experimental import pallas as pl
from jax.experimental.pallas import tpu as pltpu

F32 = jnp.float32
BF16 = jnp.bfloat16
HIGHEST = lax.Precision.HIGHEST

D_MODEL = 1024
CHUNK = 64
S5_WIDTH = 512
S5_GROUP = 16
S5_GROUPS = 32
S5_STATE = 64
S5_TILES = 16
GDN_HEADS = 4
GDN_DK = 128
GDN_DV = 128
GDN_CONV = 4
GDN_QKV = 1536
RET_HEADS = 4
RET_DK = 256
RET_DV = 512
ROPE_BASE = 10000.0
N_EXPERTS = 32
TOP_K = 4
D_FF = 1024
SWIGLU_LIMIT = 7.0
SWIGLU_ALPHA = 1.702
PLE_DIM = 256
DEPTH = 2
PAST_LEN = 1024
DEEPNORM_ALPHA = (2 * DEPTH) ** 0.25
LN_EPS = 1e-5
NORM_EPS = 1e-6

LANES = 128
SUBLANES = 8
VMEM_LIMIT = 48 * 1024 * 1024


def _params(sem):
    return pltpu.CompilerParams(dimension_semantics=sem, vmem_limit_bytes=VMEM_LIMIT)


def _dot(a, b):
    return jnp.dot(a, b, preferred_element_type=F32)


def _dot_hi(a, b):
    return jnp.dot(a, b, preferred_element_type=F32, precision=HIGHEST)


def _dot_nt(a, b, precision=None):
    return lax.dot_general(a, b, (((1,), (1,)), ((), ())), preferred_element_type=F32,
                           precision=precision)


def _dot_tn(a, b):
    return lax.dot_general(a, b, (((0,), (0,)), ((), ())), preferred_element_type=F32)


def _sigmoid(x):
    return 1.0 / (1.0 + jnp.exp(-x))


def _full(shape):
    nd = len(shape)
    return pl.BlockSpec(shape, lambda *_: (0,) * nd)


def _proj_even_kernel(x_ref, wu_ref, wqkv_ref, wz_ref, wba_ref, u_ref, qkv_ref, z_ref, ba_ref):
    xb = x_ref[...].astype(BF16)
    u_ref[...] = _dot(xb, wu_ref[...])
    qkv_ref[...] = _dot(xb, wqkv_ref[...])
    z_ref[...] = _dot(xb, wz_ref[...])
    ba_ref[...] = _dot(xb, wba_ref[...])


def _proj_even(x, wu, wqkv, wz, wba, tm):
    t = x.shape[0]
    row = lambda n: pl.BlockSpec((tm, n), lambda i: (i, 0))
    return pl.pallas_call(
        _proj_even_kernel,
        grid=(t // tm,),
        in_specs=[row(D_MODEL), _full(wu.shape), _full(wqkv.shape), _full(wz.shape), _full(wba.shape)],
        out_specs=[row(S5_WIDTH), row(GDN_QKV), row(GDN_HEADS * GDN_DV), row(LANES)],
        out_shape=[jax.ShapeDtypeStruct((t, S5_WIDTH), F32), jax.ShapeDtypeStruct((t, GDN_QKV), F32),
                   jax.ShapeDtypeStruct((t, GDN_HEADS * GDN_DV), F32), jax.ShapeDtypeStruct((t, LANES), F32)],
        compiler_params=_params(("parallel",)),
        name="proj_even",
    )(x, wu, wqkv, wz, wba)


def _proj_odd_kernel(x_ref, w_ref, cos_ref, sin_ref, o_ref):
    j = pl.program_id(0)
    acc = _dot(x_ref[...].astype(BF16), w_ref[...])

    @pl.when(j == 0)
    def _():
        cos, sin = cos_ref[...], sin_ref[...]
        half = RET_DK // 2
        for h in range(2 * RET_HEADS):
            x0 = acc[:, h * RET_DK:h * RET_DK + half]
            x1 = acc[:, h * RET_DK + half:(h + 1) * RET_DK]
            scale = 1.0 if h < RET_HEADS else RET_DK ** -0.5
            o_ref[:, h * RET_DK:h * RET_DK + half] = (x0 * cos - x1 * sin) * scale
            o_ref[:, h * RET_DK + half:(h + 1) * RET_DK] = (x0 * sin + x1 * cos) * scale

    @pl.when(j != 0)
    def _():
        o_ref[...] = acc


def _proj_odd(x, w, cos, sin, tm):
    t = x.shape[0]
    nblk = w.shape[1] // 2048
    return pl.pallas_call(
        _proj_odd_kernel,
        grid=(nblk, t // tm),
        in_specs=[pl.BlockSpec((tm, D_MODEL), lambda j, i: (i, 0)),
                  pl.BlockSpec((D_MODEL, 2048), lambda j, i: (0, j)),
                  pl.BlockSpec((tm, LANES), lambda j, i: (i, 0)),
                  pl.BlockSpec((tm, LANES), lambda j, i: (i, 0))],
        out_specs=pl.BlockSpec((tm, 2048), lambda j, i: (i, j)),
        out_shape=jax.ShapeDtypeStruct((t, w.shape[1]), F32),
        compiler_params=_params(("parallel", "parallel")),
        name="proj_odd",
    )(x, w, cos, sin)


def _layer_norm(r, g, b):
    mu = jnp.mean(r, -1, keepdims=True)
    d = r - mu
    var = jnp.mean(d * d, -1, keepdims=True)
    return d * lax.rsqrt(var + LN_EPS) * g + b


def _outproj_ln_kernel(*refs, n_in):
    a_refs = refs[:n_in]
    w_refs = refs[n_in:2 * n_in]
    x_ref, g_ref, b_ref, o_ref = refs[2 * n_in:]
    acc = _dot(a_refs[0][...].astype(BF16), w_refs[0][...])
    for a_ref, w_ref in zip(a_refs[1:], w_refs[1:]):
        acc = acc + _dot(a_ref[...].astype(BF16), w_ref[...])
    o_ref[...] = _layer_norm(DEEPNORM_ALPHA * x_ref[...] + acc, g_ref[...], b_ref[...])


def _outproj_ln(acts, ws, x, g, b, tm):
    t = x.shape[0]
    row = lambda n: pl.BlockSpec((tm, n), lambda i: (i, 0))
    return pl.pallas_call(
        functools.partial(_outproj_ln_kernel, n_in=len(acts)),
        grid=(t // tm,),
        in_specs=[row(a.shape[1]) for a in acts] + [_full(w.shape) for w in ws]
                 + [row(D_MODEL), _full((1, D_MODEL)), _full((1, D_MODEL))],
        out_specs=row(D_MODEL),
        out_shape=jax.ShapeDtypeStruct((t, D_MODEL), F32),
        compiler_params=_params(("parallel",)),
        name="outproj_ln",
    )(*acts, *ws, x, g, b)


def _s5_kernel(u_ref, h0_ref, bmat_ref, cmat_ref, acoef_ref, dskip_ref, wglu_ref, bglu_ref,
               y_ref, hout_ref, upad, sre, sim, yacc, hst, *, nb, lt, pitch):
    tb = pl.program_id(0)

    @pl.when(tb == 0)
    def _():
        hst[...] = h0_ref[...]
        upad[...] = jnp.zeros_like(upad)

    for b in range(nb):
        upad[b * pitch:b * pitch + lt, :] = u_ref[b]
    yacc[...] = jnp.zeros_like(yacc)
    halves = nb // SUBLANES

    def tile_body(j, carry):
        q = j // 4
        ub = upad[:, pl.ds(pl.multiple_of(q * LANES, LANES), LANES)].astype(BF16)
        bu = _dot(ub, bmat_ref[j])
        sre[...] = bu[:, :LANES]
        sim[...] = bu[:, LANES:]
        ac = acoef_ref[j]
        ar, ai = ac[:, :LANES], ac[:, LANES:]
        hs = [(hst[j, hf * SUBLANES:(hf + 1) * SUBLANES, 0:LANES],
               hst[j, hf * SUBLANES:(hf + 1) * SUBLANES, LANES:2 * LANES]) for hf in range(halves)]
        for t in range(lt):
            for hf in range(halves):
                hr, hi = hs[hf]
                rows = pl.ds(hf * SUBLANES * pitch + t, SUBLANES, stride=pitch)
                nhr = ar * hr - ai * hi + sre[rows, :]
                nhi = ar * hi + ai * hr + sim[rows, :]
                sre[rows, :] = nhr
                sim[rows, :] = nhi
                hs[hf] = (nhr, nhi)
        for hf in range(halves):
            hst[j, hf * SUBLANES:(hf + 1) * SUBLANES, 0:LANES] = hs[hf][0]
            hst[j, hf * SUBLANES:(hf + 1) * SUBLANES, LANES:2 * LANES] = hs[hf][1]
        st = jnp.concatenate([sre[...], sim[...]], axis=-1).astype(BF16)
        yacc[q] += _dot(st, cmat_ref[j])
        return carry

    lax.fori_loop(0, S5_TILES, tile_body, 0)
    y = jnp.concatenate([yacc[0], yacc[1], yacc[2], yacc[3]], axis=-1) + dskip_ref[...] * upad[...]
    y = jax.nn.gelu(y)
    y = y * _sigmoid(_dot(y.astype(BF16), wglu_ref[...]) + bglu_ref[...])
    for b in range(nb):
        y_ref[b] = y[b * pitch:b * pitch + lt]
    hout_ref[...] = hst[...]


def _s5(u3, h0, bmat, cmat, acoef, dskip, wglu, bglu, lt):
    nb, L, _ = u3.shape
    pitch = lt + SUBLANES
    rows = nb * pitch
    return pl.pallas_call(
        functools.partial(_s5_kernel, nb=nb, lt=lt, pitch=pitch),
        grid=(L // lt,),
        in_specs=[pl.BlockSpec((nb, lt, S5_WIDTH), lambda i: (0, i, 0)),
                  _full(h0.shape), _full(bmat.shape), _full(cmat.shape), _full(acoef.shape),
                  _full(dskip.shape), _full(wglu.shape), _full(bglu.shape)],
        out_specs=[pl.BlockSpec((nb, lt, S5_WIDTH), lambda i: (0, i, 0)), _full(h0.shape)],
        out_shape=[jax.ShapeDtypeStruct((nb, L, S5_WIDTH), F32), jax.ShapeDtypeStruct(h0.shape, F32)],
        scratch_shapes=[pltpu.VMEM((rows, S5_WIDTH), F32), pltpu.VMEM((rows, LANES), F32),
                        pltpu.VMEM((rows, LANES), F32), pltpu.VMEM((4, rows, LANES), F32),
                        pltpu.VMEM(h0.shape, F32)],
        compiler_params=_params(("arbitrary",)),
        name="s5_scan",
    )(u3, h0, bmat, cmat, acoef, dskip, wglu, bglu)


def _s5_weights(a_re, a_im, log_dt, b_re, b_im, c_re, c_im):
    dt = jnp.exp(log_dt)[:, None]
    lr, li = a_re * dt, a_im * dt
    mag = jnp.exp(lr)
    ab_re, ab_im = mag * jnp.cos(li), mag * jnp.sin(li)
    den = a_re * a_re + a_im * a_im
    cf_re = ((ab_re - 1.0) * a_re + ab_im * a_im) / den
    cf_im = (ab_im * a_re - (ab_re - 1.0) * a_im) / den
    bb_re = cf_re[..., None] * b_re - cf_im[..., None] * b_im
    bb_im = cf_re[..., None] * b_im + cf_im[..., None] * b_re
    jj = np.arange(S5_TILES)[:, None, None]
    lg = np.arange(8)[None, :, None]
    gi = np.arange(2)[None, None, :]
    sel = jnp.asarray((lg == 2 * (jj % 4) + gi).astype(np.float32))
    tiles = lambda w: w.reshape(S5_TILES, 2, *w.shape[1:])
    bt = lambda w: jnp.einsum('jlg,jgpn->jlngp', sel, tiles(w)).reshape(S5_TILES, LANES, LANES)
    bmat = jnp.concatenate([bt(bb_re), bt(bb_im)], axis=-1)
    ct = lambda w: jnp.einsum('jlg,jgnp->jgpln', sel, tiles(w)).reshape(S5_TILES, LANES, LANES)
    cmat = jnp.concatenate([ct(c_re), -ct(c_im)], axis=1)
    acoef = jnp.concatenate([ab_re.reshape(S5_TILES, LANES), ab_im.reshape(S5_TILES, LANES)], axis=-1)
    acoef = jnp.broadcast_to(acoef[:, None, :], (S5_TILES, SUBLANES, 2 * LANES))
    return bmat.astype(BF16), cmat.astype(BF16), acoef


def _s5_state_in(h_re, h_im):
    nb = h_re.shape[0]
    h = jnp.concatenate([h_re.reshape(nb, S5_TILES, LANES), h_im.reshape(nb, S5_TILES, LANES)], axis=-1)
    return jnp.transpose(h, (1, 0, 2))


def _s5_state_out(h):
    nb = h.shape[1]
    h = jnp.transpose(h, (1, 0, 2))
    return (h[..., :LANES].reshape(nb, S5_GROUPS, S5_STATE), h[..., LANES:].reshape(nb, S5_GROUPS, S5_STATE))


def _split_bf16(a):
    hi = a.astype(BF16)
    return hi, (a - hi.astype(F32)).astype(BF16)


def _bdot(a, b):
    return lax.dot_general(a, b, (((2,), (1,)), ((0,), (0,))), preferred_element_type=F32)


def _bdot_nt(a, b):
    return lax.dot_general(a, b, (((2,), (2,)), ((0,), (0,))), preferred_element_type=F32)


def _bdot_split(a, b):
    ah, al = _split_bf16(a)
    bh, bl = _split_bf16(b)
    return _bdot(ah, bh) + _bdot(ah, bl) + _bdot(al, bh)


def _unit_lower_inverse(nmat):
    lc = nmat.shape[-1]
    ri = lax.broadcasted_iota(jnp.int32, nmat.shape, 1)
    ci = lax.broadcasted_iota(jnp.int32, nmat.shape, 2)
    base = 16
    dmat = jnp.where(ri // base == ci // base, nmat, 0.0)
    inv = jnp.where(ri == ci, 1.0, 0.0) - dmat
    pw = dmat
    for _ in range(3):
        pw = _bdot_split(pw, pw)
        inv = inv + _bdot_split(inv, pw)
    size = base
    while size < lc:
        off = jnp.where(ri // (2 * size) == ci // (2 * size), jnp.where(ri // size > ci // size, nmat, 0.0), 0.0)
        inv = inv - _bdot_split(_bdot_split(inv, off), inv)
        size *= 2
    return inv


def _gdn_local_kernel(qkv_ref, ba_ref, ctx_ref, cw_ref, p1_ref, p2_ref, ltri_ref,
                      u0_ref, w_ref, qd_ref, kd_ref, attn_ref, g_ref, cout_ref, xpad, *, lc, cp):
    c = pl.program_id(1)
    rb = lc * cp

    @pl.when(c == 0)
    def _():
        xpad[0:SUBLANES, :] = ctx_ref[0]

    xpad[SUBLANES:SUBLANES + rb, :] = qkv_ref[0]
    cw = cw_ref[...]
    conv = (cw[3:4] * xpad[8:8 + rb, :] + cw[2:3] * xpad[7:7 + rb, :]
            + cw[1:2] * xpad[6:6 + rb, :] + cw[0:1] * xpad[5:5 + rb, :])
    tail = xpad[rb:rb + SUBLANES, :]
    xpad[0:SUBLANES, :] = tail
    cout_ref[0] = tail
    a = conv * _sigmoid(conv)

    ba = ba_ref[0]
    beta_all = _sigmoid(ba)
    sp_in = ba + p2_ref[...]
    softplus = jnp.maximum(sp_in, 0.0) + jnp.log(1.0 + jnp.exp(-jnp.abs(sp_in)))
    g_all = p1_ref[...] * softplus
    g_hi = g_all.astype(BF16)
    g_r = g_all - g_hi.astype(F32)
    g_mid = g_r.astype(BF16)
    g_lo = (g_r - g_mid.astype(F32)).astype(BF16)
    lt = ltri_ref[...]
    G = _dot(lt, g_hi) + _dot(lt, g_mid) + _dot(lt, g_lo)
    g_ref[0] = G
    GT = G.T
    pairs = [(h, cc) for h in range(GDN_HEADS) for cc in range(cp)]
    qs, ks, vs, betas, gcols, grows, glasts = [], [], [], [], [], [], []
    for h in range(GDN_HEADS):
        qa = a[:, h * GDN_DK:(h + 1) * GDN_DK]
        ka = a[:, (GDN_HEADS + h) * GDN_DK:(GDN_HEADS + h + 1) * GDN_DK]
        va = a[:, (2 * GDN_HEADS + h) * GDN_DK:(2 * GDN_HEADS + h + 1) * GDN_DK]
        qa = qa * lax.rsqrt(jnp.sum(qa * qa, -1, keepdims=True) + NORM_EPS) * (GDN_DK ** -0.5)
        ka = ka * lax.rsqrt(jnp.sum(ka * ka, -1, keepdims=True) + NORM_EPS)
        for cc in range(cp):
            rows = slice(cc * lc, (cc + 1) * lc)
            qs.append(qa[rows])
            ks.append(ka[rows])
            vs.append(va[rows])
            betas.append(beta_all[rows, h:h + 1])
            gcols.append(G[rows, GDN_HEADS + h:GDN_HEADS + h + 1])
            grows.append(GT[GDN_HEADS + h:GDN_HEADS + h + 1, cc * lc:(cc + 1) * lc])
            glasts.append(GT[GDN_HEADS + h:GDN_HEADS + h + 1, (cc + 1) * lc - 1:(cc + 1) * lc])
    q3, k3, v3 = jnp.stack(qs), jnp.stack(ks), jnp.stack(vs)
    beta3, gcol3 = jnp.stack(betas), jnp.stack(gcols)
    grow3, glast3 = jnp.stack(grows), jnp.stack(glasts)
    shape3 = (len(pairs), lc, lc)
    ri = lax.broadcasted_iota(jnp.int32, shape3, 1)
    ci = lax.broadcasted_iota(jnp.int32, shape3, 2)
    incl = ri >= ci
    dec3 = jnp.where(incl, jnp.exp(jnp.where(incl, gcol3 - grow3, 0.0)), 0.0)
    eg3 = jnp.exp(gcol3)
    kb3 = k3 * beta3
    kbf3 = k3.astype(BF16)
    nmat3 = jnp.where(ri > ci, _bdot_nt(kb3.astype(BF16), kbf3) * dec3, 0.0)
    inv3 = _unit_lower_inverse(nmat3)
    sol3 = _bdot_split(inv3, jnp.concatenate([v3 * beta3, kb3 * eg3], axis=-1))
    w3 = sol3[:, :, GDN_DV:].astype(BF16)
    qd3 = (q3 * eg3).astype(BF16)
    kd3 = (k3 * jnp.exp(glast3 - gcol3)).astype(BF16)
    attn3 = (_bdot_nt(q3.astype(BF16), kbf3) * dec3).astype(BF16)
    for i, (h, cc) in enumerate(pairs):
        rows = slice(cc * lc, (cc + 1) * lc)
        cols = slice(h * GDN_DV, (h + 1) * GDN_DV)
        u0_ref[0, rows, cols] = sol3[i, :, :GDN_DV]
        w_ref[0, rows, cols] = w3[i]
        qd_ref[0, rows, cols] = qd3[i]
        kd_ref[0, rows, cols] = kd3[i]
        attn_ref[0, rows, h * lc:(h + 1) * lc] = attn3[i]


def _gdn_seq_kernel(u0_ref, w_ref, qd_ref, kd_ref, attn_ref, g_ref, z_ref, s0_ref, nw_ref,
                    y_ref, sout_ref, S, *, lc, nbb):
    c = pl.program_id(1)

    @pl.when(c == 0)
    def _():
        S[...] = s0_ref[...]

    nw = nw_ref[...]
    pairs = [(bb, h) for bb in range(nbb) for h in range(GDN_HEADS)]
    hcols = lambda h: slice(h * GDN_DV, (h + 1) * GDN_DV)
    stack = lambda f: jnp.stack([f(bb, h) for bb, h in pairs])
    dlast = jnp.exp(g_ref[:, lc - 1:lc, :])
    S3 = S[...].reshape(len(pairs), GDN_DK, GDN_DV)
    wq3 = stack(lambda bb, h: jnp.concatenate([w_ref[bb, :, hcols(h)], qd_ref[bb, :, hcols(h)]], axis=0))
    r3 = _bdot(wq3, S3.astype(BF16))
    ub3 = (stack(lambda bb, h: u0_ref[bb, :, hcols(h)]) - r3[:, :lc]).astype(BF16)
    o3 = r3[:, lc:] + _bdot(stack(lambda bb, h: attn_ref[bb, :, h * lc:(h + 1) * lc]), ub3)
    d3 = stack(lambda bb, h: dlast[bb, :, GDN_HEADS + h:GDN_HEADS + h + 1])
    kd3 = stack(lambda bb, h: kd_ref[bb, :, hcols(h)])
    kdu3 = lax.dot_general(kd3, ub3, (((1,), (1,)), ((0,), (0,))), preferred_element_type=F32)
    S[...] = (d3 * S3 + kdu3).reshape(S.shape)
    o3 = o3 * lax.rsqrt(jnp.mean(o3 * o3, -1, keepdims=True) + NORM_EPS) * nw
    for i, (bb, h) in enumerate(pairs):
        zh = z_ref[bb, :, hcols(h)]
        y_ref[bb, :, hcols(h)] = o3[i] * (zh * _sigmoid(zh))

    @pl.when(c == pl.num_programs(1) - 1)
    def _():
        sout_ref[...] = S[...]


def _gdn(qkv3, z3, ba3, ctx8, s0, cw, p1, p2, nw, lc, cp, nbb):
    nb, L, _ = qkv3.shape
    rb = lc * cp
    hd = GDN_HEADS * GDN_DV
    ltri = jnp.asarray(np.kron(np.eye(cp, dtype=np.float32), np.tril(np.ones((lc, lc), np.float32)))).astype(BF16)
    blk = lambda n: pl.BlockSpec((1, rb, n), lambda b, c: (b, c, 0))
    per_b = lambda shape: pl.BlockSpec((1,) + shape, lambda b, c: (b,) + (0,) * len(shape))
    cst = lambda shape: pl.BlockSpec(shape, lambda b, c: (0,) * len(shape))
    sds = lambda n, dt: jax.ShapeDtypeStruct((nb, L, n), dt)
    u0, w, qd, kd, attn, G, cout = pl.pallas_call(
        functools.partial(_gdn_local_kernel, lc=lc, cp=cp),
        grid=(nb, L // rb),
        in_specs=[blk(GDN_QKV), blk(LANES), per_b((SUBLANES, GDN_QKV)), cst(cw.shape), cst(p1.shape),
                  cst(p2.shape), cst(ltri.shape)],
        out_specs=[blk(hd), blk(hd), blk(hd), blk(hd), blk(GDN_HEADS * lc), blk(LANES),
                   per_b((SUBLANES, GDN_QKV))],
        out_shape=[sds(hd, F32), sds(hd, BF16), sds(hd, BF16), sds(hd, BF16), sds(GDN_HEADS * lc, BF16),
                   sds(LANES, F32), jax.ShapeDtypeStruct((nb, SUBLANES, GDN_QKV), F32)],
        scratch_shapes=[pltpu.VMEM((rb + SUBLANES, GDN_QKV), F32)],
        compiler_params=_params(("parallel", "arbitrary")),
        name="gdn_local",
    )(qkv3, ba3, ctx8, cw, p1, p2, ltri)
    sblk = lambda n: pl.BlockSpec((nbb, lc, n), lambda b, c: (b, c, 0))
    state = pl.BlockSpec((nbb, GDN_HEADS, GDN_DK, GDN_DV), lambda b, c: (b, 0, 0, 0))
    y, s_new = pl.pallas_call(
        functools.partial(_gdn_seq_kernel, lc=lc, nbb=nbb),
        grid=(nb // nbb, L // lc),
        in_specs=[sblk(hd), sblk(hd), sblk(hd), sblk(hd), sblk(GDN_HEADS * lc), sblk(LANES), sblk(hd),
                  state, cst(nw.shape)],
        out_specs=[sblk(hd), state],
        out_shape=[sds(hd, F32), jax.ShapeDtypeStruct(s0.shape, F32)],
        scratch_shapes=[pltpu.VMEM((nbb, GDN_HEADS, GDN_DK, GDN_DV), F32)],
        compiler_params=_params(("parallel", "arbitrary")),
        name="gdn_seq",
    )(u0, w, qd, kd, attn, G, z3, s0, nw)
    return y, s_new, cout


def _ret_kernel(q_ref, k_ref, v_ref, g_ref, r0_ref, dec_ref, qs_ref, ks_ref, cd_ref, o_ref, rout_ref, R,
                *, nbb):
    c = pl.program_id(1)

    @pl.when(c == 0)
    def _():
        R[...] = r0_ref[...]

    pairs = [(bb, h) for bb in range(nbb) for h in range(RET_HEADS)]
    stack = lambda f: jnp.stack([f(bb, h) for bb, h in pairs])
    kcols = lambda h: slice(h * RET_DK, (h + 1) * RET_DK)
    vcols = lambda h: slice(h * RET_DV, (h + 1) * RET_DV)
    q3 = stack(lambda bb, h: q_ref[bb, :, kcols(h)])
    k3 = stack(lambda bb, h: k_ref[bb, :, kcols(h)])
    v3 = stack(lambda bb, h: v_ref[bb, :, vcols(h)]).astype(BF16)
    dec3 = stack(lambda bb, h: dec_ref[h])
    qs3 = stack(lambda bb, h: qs_ref[h])
    ks3 = stack(lambda bb, h: ks_ref[h])
    cd3 = stack(lambda bb, h: cd_ref[h])
    R3 = R[...].reshape(len(pairs), RET_DK, RET_DV)
    s3 = _bdot_nt(q3.astype(BF16), k3.astype(BF16)) * dec3
    o3 = _bdot(s3.astype(BF16), v3) + _bdot((q3 * qs3).astype(BF16), R3.astype(BF16))
    kv3 = lax.dot_general((k3 * ks3).astype(BF16), v3, (((1,), (1,)), ((0,), (0,))), preferred_element_type=F32)
    R[...] = (cd3 * R3 + kv3).reshape(R.shape)
    mu = jnp.mean(o3, -1, keepdims=True)
    d3 = o3 - mu
    var = jnp.mean(d3 * d3, -1, keepdims=True)
    on3 = d3 * lax.rsqrt(var + LN_EPS)
    for i, (bb, h) in enumerate(pairs):
        gt = g_ref[bb, :, vcols(h)]
        o_ref[bb, :, vcols(h)] = gt * _sigmoid(gt) * on3[i]

    @pl.when(c == pl.num_programs(1) - 1)
    def _():
        rout_ref[...] = R[...]


def _retention(proj3, r0, lc, nbb):
    nb, L, _ = proj3.shape
    log_g = np.log(1.0 - 2.0 ** (-5.0 - np.arange(RET_HEADS, dtype=np.float64)))
    idx = np.arange(lc, dtype=np.float64)
    dec = np.exp(log_g[:, None, None] * np.abs(idx[:, None] - idx[None, :])).astype(np.float32)
    qs = np.exp(log_g[:, None] * (idx + 1.0)).astype(np.float32)[..., None]
    ks = np.exp(log_g[:, None] * (lc - 1.0 - idx)).astype(np.float32)[..., None]
    cdec = np.exp(log_g * lc).astype(np.float32)[:, None, None]
    nqk = RET_HEADS * RET_DK
    nv = RET_HEADS * RET_DV
    cst = lambda shape: pl.BlockSpec(shape, lambda b, c: (0,) * len(shape))
    state = pl.BlockSpec((nbb, RET_HEADS, RET_DK, RET_DV), lambda b, c: (b, 0, 0, 0))
    return pl.pallas_call(
        functools.partial(_ret_kernel, nbb=nbb),
        grid=(nb // nbb, L // lc),
        in_specs=[pl.BlockSpec((nbb, lc, nqk), lambda b, c: (b, c, 0)),
                  pl.BlockSpec((nbb, lc, nqk), lambda b, c: (b, c, 1)),
                  pl.BlockSpec((nbb, lc, nv), lambda b, c: (b, c, 1)),
                  pl.BlockSpec((nbb, lc, nv), lambda b, c: (b, c, 2)),
                  state, cst(dec.shape), cst(qs.shape), cst(ks.shape), cst(cdec.shape)],
        out_specs=[pl.BlockSpec((nbb, lc, nv), lambda b, c: (b, c, 0)), state],
        out_shape=[jax.ShapeDtypeStruct((nb, L, nv), F32), jax.ShapeDtypeStruct(r0.shape, F32)],
        scratch_shapes=[pltpu.VMEM((nbb, RET_HEADS, RET_DK, RET_DV), F32)],
        compiler_params=_params(("parallel", "arbitrary")),
        name="retention",
    )(proj3, proj3, proj3, proj3, r0, jnp.asarray(dec), jnp.asarray(qs), jnp.asarray(ks), jnp.asarray(cdec))


def _router_kernel(x_ref, rw_ref, rb_ref, lst_ref, ri_ref, gt_ref, cnt_ref, run):
    i = pl.program_id(0)

    @pl.when(i == 0)
    def _():
        run[...] = jnp.zeros_like(run)

    tm = x_ref.shape[0]
    lane = lax.broadcasted_iota(jnp.int32, (tm, LANES), 1)
    lane_f = lane.astype(F32)
    logits = _dot_hi(x_ref[...], rw_ref[...]) + rb_ref[...]
    logits = jnp.where(lane < N_EXPERTS, logits, -jnp.inf)
    vals, hots = [], []
    for _ in range(TOP_K):
        m = jnp.max(logits, -1, keepdims=True)
        first = jnp.min(jnp.where(logits == m, lane_f, float(LANES)), -1, keepdims=True)
        hot = lane_f == first
        vals.append(m)
        hots.append(hot)
        logits = jnp.where(hot, -jnp.inf, logits)
    es = [jnp.exp(v - vals[0]) for v in vals]
    den = es[0] + es[1] + es[2] + es[3]
    multi = jnp.zeros((tm, LANES), F32)
    for hot in hots:
        multi = multi + hot.astype(F32)
    before = _dot(lst_ref[...], multi.astype(BF16)) + run[...]
    ri = jnp.zeros((tm, LANES), F32)
    gt = jnp.zeros((tm, LANES), F32)
    for kk in range(TOP_K):
        idx = jnp.sum(jnp.where(hots[kk], lane_f, 0.0), -1, keepdims=True)
        rank = jnp.sum(jnp.where(hots[kk], before, 0.0), -1, keepdims=True)
        ri = jnp.where(lane == kk, idx, ri)
        ri = jnp.where(lane == TOP_K + kk, rank, ri)
        gt = jnp.where(lane == kk, es[kk] / den, gt)
    ri_ref[...] = ri.astype(jnp.int32)
    gt_ref[...] = gt
    run[...] = run[...] + jnp.sum(multi, 0, keepdims=True)
    cnt_ref[...] = run[...]


def _router(x, rw, rb, tm):
    t = x.shape[0]
    lst = jnp.asarray(np.tril(np.ones((tm, tm), np.float32), -1)).astype(BF16)
    row = lambda n: pl.BlockSpec((tm, n), lambda i: (i, 0))
    return pl.pallas_call(
        _router_kernel,
        grid=(t // tm,),
        in_specs=[row(D_MODEL), _full(rw.shape), _full(rb.shape), _full(lst.shape)],
        out_specs=[row(LANES), row(LANES), _full((1, LANES))],
        out_shape=[jax.ShapeDtypeStruct((t, LANES), jnp.int32), jax.ShapeDtypeStruct((t, LANES), F32),
                   jax.ShapeDtypeStruct((1, LANES), F32)],
        scratch_shapes=[pltpu.VMEM((1, LANES), F32)],
        compiler_params=_params(("arbitrary",)),
        name="router",
    )(x, rw, rb, lst)


MOE_CHUNKS = 4


def _moe_block(xbuf, gate, w1_ref, b1_ref, w2_ref, b2_ref, ybuf, act, issue):
    xb = xbuf[...].astype(BF16)
    cw = D_FF // MOE_CHUNKS
    for c in range(MOE_CHUNKS):
        lo, hi = c * cw, (c + 1) * cw
        hg = _dot(xb, w1_ref[0, :, lo:hi]) + b1_ref[0, :, lo:hi]
        hl = _dot(xb, w1_ref[0, :, D_FF + lo:D_FF + hi]) + b1_ref[0, :, D_FF + lo:D_FF + hi]
        glu = jnp.minimum(hg, SWIGLU_LIMIT)
        lin = jnp.clip(hl, -SWIGLU_LIMIT, SWIGLU_LIMIT)
        act[:, lo:hi] = (glu * _sigmoid(SWIGLU_ALPHA * glu) * (lin + 1.0)).astype(BF16)
        issue(c)
    ab = act[...]
    for c in range(MOE_CHUNKS):
        lo, hi = c * cw, (c + 1) * cw
        ybuf[:, lo:hi] = (_dot(ab, w2_ref[0, :, lo:hi]) + b2_ref[0, :, lo:hi]) * gate
        issue(MOE_CHUNKS + c)


def _moe_fused_kernel(be_ref, nu_ref, src0_ref, srcs_ref, dsts_ref, gate_ref, x_hbm,
                      w1a, b1a, w2a, b2a, w1b, b1b, w2b, b2b, yk_hbm,
                      xbuf0, xbuf1, ybuf0, ybuf1, act, gsem, ssem, *, blk, dump0):
    del be_ref
    j = pl.program_id(0)
    nu = nu_ref[0]
    rows_per = blk // (2 * MOE_CHUNKS)

    def gather_row(s, dst, r, sem):
        pltpu.make_async_copy(x_hbm.at[pl.ds(s, 1)], dst.at[pl.ds(r, 1)], sem).start()

    def scatter_row(src, r, d, sem):
        pltpu.make_async_copy(src.at[pl.ds(r, 1)], yk_hbm.at[pl.ds(d, 1)], sem).start()

    def wait_gather(dst, sem):
        pltpu.make_async_copy(x_hbm.at[pl.ds(0, blk)], dst, sem).wait()

    def wait_scatter(src, sem):
        pltpu.make_async_copy(src, yk_hbm.at[pl.ds(0, blk)], sem).wait()

    @pl.when(2 * j < nu)
    def _():
        @pl.when(j == 0)
        def _():
            ybuf1[...] = jnp.zeros_like(ybuf1)

            def prologue(r, carry):
                gather_row(src0_ref[r], xbuf0, r, gsem.at[0])
                scatter_row(ybuf1, r, dump0 + r, ssem.at[1])
                return carry

            lax.fori_loop(0, blk, prologue, 0)
            wait_scatter(ybuf1, ssem.at[1])

        wait_gather(xbuf0, gsem.at[0])

        @pl.when(j > 0)
        def _():
            wait_scatter(ybuf0, ssem.at[0])

        def issue_a(c):
            for r in range(c * rows_per, (c + 1) * rows_per):
                gather_row(srcs_ref[0, 0, r], xbuf1, r, gsem.at[1])
                scatter_row(ybuf1, r, dsts_ref[0, 0, r], ssem.at[1])

        _moe_block(xbuf0, gate_ref[0:blk, :], w1a, b1a, w2a, b2a, ybuf0, act, issue_a)
        wait_gather(xbuf1, gsem.at[1])
        wait_scatter(ybuf1, ssem.at[1])

        def issue_b(c):
            for r in range(c * rows_per, (c + 1) * rows_per):
                gather_row(srcs_ref[0, 0, blk + r], xbuf0, r, gsem.at[0])
                scatter_row(ybuf0, r, dsts_ref[0, 0, blk + r], ssem.at[0])

        _moe_block(xbuf1, gate_ref[blk:2 * blk, :], w1b, b1b, w2b, b2b, ybuf1, act, issue_b)

    @pl.when((2 * j >= nu) & (2 * j < nu + 2))
    def _():
        def flush(r, carry):
            scatter_row(ybuf1, r, dsts_ref[0, 0, r], ssem.at[1])
            return carry

        lax.fori_loop(0, blk, flush, 0)
        wait_gather(xbuf0, gsem.at[0])
        wait_scatter(ybuf0, ssem.at[0])
        wait_scatter(ybuf1, ssem.at[1])


def _moe_fused(blk_e, n_used, src_sorted, dst_sorted, gate_sorted, x, w1, b1, w2, b2, blk, dump0):
    t = x.shape[0]
    n_blk = src_sorted.shape[0] // blk
    nsteps = n_blk // 2 + 1
    pad = lambda v, n: jnp.concatenate([v, jnp.zeros((n,), v.dtype)])
    srcs = pad(src_sorted, 3 * blk)[blk:blk + nsteps * 2 * blk].reshape(nsteps, 1, 2 * blk)
    dump1 = dump0 + blk + jnp.arange(blk, dtype=jnp.int32)
    dsts = jnp.concatenate([dump1, pad(dst_sorted, blk)]).reshape(nsteps, 1, 2 * blk)
    last = n_blk - 1
    wspec = lambda shape, off: pl.BlockSpec(
        (1,) + shape, lambda j, be, nu: (be[jnp.minimum(2 * j + off, last)], 0, 0))
    weights = lambda off: [wspec((D_MODEL, 2 * D_FF), off), wspec((1, 2 * D_FF), off),
                           wspec((D_FF, D_MODEL), off), wspec((1, D_MODEL), off)]
    smem3 = pl.BlockSpec((1, 1, 2 * blk), lambda j, be, nu: (j, 0, 0), memory_space=pltpu.SMEM)
    grid_spec = pltpu.PrefetchScalarGridSpec(
        num_scalar_prefetch=2,
        grid=(nsteps,),
        in_specs=[pl.BlockSpec((blk,), lambda j, be, nu: (0,), memory_space=pltpu.SMEM), smem3, smem3,
                  pl.BlockSpec((2 * blk, 1), lambda j, be, nu: (jnp.minimum(j, n_blk // 2 - 1), 0)),
                  pl.BlockSpec(memory_space=pl.ANY)] + weights(0) + weights(1),
        out_specs=pl.BlockSpec(memory_space=pl.ANY),
        scratch_shapes=[pltpu.VMEM((blk, D_MODEL), F32), pltpu.VMEM((blk, D_MODEL), F32),
                        pltpu.VMEM((blk, D_MODEL), F32), pltpu.VMEM((blk, D_MODEL), F32),
                        pltpu.VMEM((blk, D_FF), BF16),
                        pltpu.SemaphoreType.DMA((2,)), pltpu.SemaphoreType.DMA((2,))],
    )
    return pl.pallas_call(
        functools.partial(_moe_fused_kernel, blk=blk, dump0=dump0),
        grid_spec=grid_spec,
        out_shape=jax.ShapeDtypeStruct((TOP_K * t + 2 * blk, D_MODEL), F32),
        compiler_params=_params(("arbitrary",)),
        name="moe_fused",
    )(blk_e, n_used, src_sorted[:blk], srcs, dsts, gate_sorted, x, w1, b1, w2, b2, w1, b1, w2, b2)


def _combine_kernel(y0_ref, y1_ref, y2_ref, y3_ref, x_ref, p_ref, g_ref, b_ref, plew_ref, gatew_ref, o_ref):
    y = (y0_ref[...] + y1_ref[...]) + (y2_ref[...] + y3_ref[...])
    x2 = _layer_norm(DEEPNORM_ALPHA * x_ref[...] + y, g_ref[...], b_ref[...])
    pp = _dot(p_ref[...].astype(BF16), plew_ref[...])
    gg = _sigmoid(_dot(x2.astype(BF16), gatew_ref[...]))
    o_ref[...] = x2 + pp * gg


def _combine(yk, x, p, g, b, plew, gatew, tm):
    t = x.shape[0]
    row = lambda n: pl.BlockSpec((tm, n), lambda i: (i, 0))
    slot = lambda kk: pl.BlockSpec((tm, D_MODEL), lambda i: (kk * (t // tm) + i, 0))
    return pl.pallas_call(
        _combine_kernel,
        grid=(t // tm,),
        in_specs=[slot(kk) for kk in range(TOP_K)]
                 + [row(D_MODEL), row(PLE_DIM), _full((1, D_MODEL)), _full((1, D_MODEL)),
                    _full(plew.shape), _full(gatew.shape)],
        out_specs=row(D_MODEL),
        out_shape=jax.ShapeDtypeStruct((t, D_MODEL), F32),
        compiler_params=_params(("parallel",)),
        name="moe_combine",
    )(yk, yk, yk, yk, x, p, g, b, plew, gatew)


def _moe_ple(x, p, rw, rb, w1, b1, w2, b2, g, b, plew, gatew, tm, blk):
    t = x.shape[0]
    ri, gt, cnt = _router(x, rw, rb, tm)
    idx, rank = ri[:, :TOP_K], ri[:, TOP_K:2 * TOP_K]
    counts = cnt[0, :N_EXPERTS].astype(jnp.int32)
    padded = (counts + blk - 1) // blk * blk
    pend = jnp.cumsum(padded)
    pstart = pend - padded
    dest = (pstart[idx] + rank).reshape(-1).astype(jnp.int32)
    n_blk = -(-(t * TOP_K + N_EXPERTS * (blk - 1)) // blk)
    n_blk += n_blk % 2
    n_pad = n_blk * blk
    n_used = (pend[-1] // blk).astype(jnp.int32).reshape(1)
    blk_start = jnp.arange(n_blk, dtype=jnp.int32) * blk
    blk_e = jnp.minimum(jnp.sum((pend[None, :] <= blk_start[:, None]).astype(jnp.int32), axis=1), N_EXPERTS - 1)
    inv = jnp.zeros((n_pad,), jnp.int32).at[dest].set(jnp.arange(1, t * TOP_K + 1, dtype=jnp.int32))
    valid = inv > 0
    asg = jnp.maximum(inv - 1, 0)
    pos = jnp.arange(n_pad, dtype=jnp.int32)
    dump0 = TOP_K * t
    src_sorted = jnp.where(valid, asg // TOP_K, 0)
    dst_sorted = jnp.where(valid, (asg % TOP_K) * t + asg // TOP_K, dump0 + (pos // blk % 2) * blk + pos % blk)
    gate_sorted = jnp.where(valid, gt[:, :TOP_K].reshape(-1)[asg], 0.0)[:, None]
    yk = _moe_fused(blk_e, n_used, src_sorted, dst_sorted, gate_sorted, x, w1, b1, w2, b2, blk, dump0)
    return _combine(yk, x, p, g, b, plew, gatew, tm)


_ROT_PERM = np.concatenate([np.arange(0, RET_DK, 2), np.arange(1, RET_DK, 2)])
_ROT_INV = np.argsort(_ROT_PERM)


def _lane_row(vals, offset):
    row = jnp.zeros((1, LANES), F32)
    return row.at[0, offset:offset + vals.shape[0]].set(vals)


def _run_group(x3, p, s5_re, s5_im, gdn_s, conv_s, ret_s, pos0, W, tm, blk):
    nb, L, _ = x3.shape
    t = nb * L
    lc = L if L <= CHUNK else CHUNK
    x = x3.reshape(t, D_MODEL).astype(F32)

    u, qkv, z, ba = _proj_even(x, W['wu'], W['wqkv'], W['wz'], W['wba'], tm)
    yA3, h_new = _s5(u.reshape(nb, L, S5_WIDTH), _s5_state_in(s5_re[0].astype(F32), s5_im[0].astype(F32)),
                     W['bmat'], W['cmat'], W['acoef'], W['dskip'], W['wglu'], W['bglu'], lc)
    new_re, new_im = _s5_state_out(h_new)
    ctx8 = jnp.concatenate([jnp.zeros((nb, SUBLANES - (GDN_CONV - 1), GDN_QKV), F32), conv_s[0].astype(F32)], axis=1)
    yB3, new_gdn, cout = _gdn(qkv.reshape(nb, L, GDN_QKV), z.reshape(nb, L, -1), ba.reshape(nb, L, LANES),
                              ctx8, gdn_s[0].astype(F32), W['convw'], W['p1'], W['p2'], W['normw'], lc,
                              cp=min(4, L // lc), nbb=4)
    new_conv = cout[:, SUBLANES - (GDN_CONV - 1):, :]
    x = _outproj_ln([yA3.reshape(t, -1), yB3.reshape(t, -1)], [W['wout_a'], W['wout_b']], x,
                    W['ln1_g'][0], W['ln1_b'][0], tm)
    x = _moe_ple(x, p[0].reshape(t, PLE_DIM), W['rw'][0], W['rb'][0], W['w1'][0], W['b1'][0], W['w2'][0],
                 W['b2'][0], W['ln2_g'][0], W['ln2_b'][0], W['plew'][0], W['gatew'][0], tm, blk)

    pos = pos0 + jnp.arange(L, dtype=F32)
    freq = 1.0 / (ROPE_BASE ** jnp.linspace(0.0, 1.0, RET_DK // 2, dtype=F32))
    ang = pos[:, None] * freq[None]
    cos = jnp.broadcast_to(jnp.cos(ang)[None], (nb, L, RET_DK // 2)).reshape(t, RET_DK // 2)
    sin = jnp.broadcast_to(jnp.sin(ang)[None], (nb, L, RET_DK // 2)).reshape(t, RET_DK // 2)
    proj = _proj_odd(x, W['win_odd'], cos, sin, tm)
    r0 = ret_s[0].astype(F32)[:, :, _ROT_PERM, :]
    o3, r_new = _retention(proj.reshape(nb, L, -1), r0, lc, nbb=2)
    new_ret = r_new[:, :, _ROT_INV, :]
    x = _outproj_ln([o3.reshape(t, -1)], [W['wout_odd']], x, W['ln1_g'][1], W['ln1_b'][1], tm)
    x = _moe_ple(x, p[1].reshape(t, PLE_DIM), W['rw'][1], W['rb'][1], W['w1'][1], W['b1'][1], W['w2'][1],
                 W['b2'][1], W['ln2_g'][1], W['ln2_b'][1], W['plew'][1], W['gatew'][1], tm, blk)
    return (x.reshape(nb, L, D_MODEL), new_re[None], new_im[None], new_gdn[None], new_conv[None], new_ret[None])


def kernel(x_prompt, x_sample, state_s5_re, state_s5_im, state_gdn, state_gdn_conv, state_ret, p_prompt, p_sample, w_in_even, s5_a_re, s5_a_im, s5_log_dt, s5_b_re, s5_b_im, s5_c_re, s5_c_im, s5_d, s5_w_glu, s5_b_glu, gdn_conv_w, gdn_a_log, gdn_dt_bias, gdn_norm_w, w_out_even, w_in_odd, w_out_odd, ln1_g, ln1_b, ln2_g, ln2_b, router_w, router_b, moe_w1, moe_b1, moe_w2, moe_b2, ple_w, ple_gate_w):
    o1 = S5_WIDTH
    o2 = o1 + GDN_QKV
    o3 = o2 + GDN_HEADS * GDN_DV
    win = w_in_even[0]
    bmat, cmat, acoef = _s5_weights(s5_a_re[0], s5_a_im[0], s5_log_dt[0], s5_b_re[0], s5_b_im[0],
                                    s5_c_re[0], s5_c_im[0])
    wodd = w_in_odd[0]
    nk = RET_HEADS * RET_DK
    perm_cols = lambda w: w.reshape(D_MODEL, RET_HEADS, RET_DK)[:, :, _ROT_PERM].reshape(D_MODEL, nk)
    W = dict(
        wu=win[:, :o1].astype(BF16), wqkv=win[:, o1:o2].astype(BF16), wz=win[:, o2:o3].astype(BF16),
        wba=jnp.pad(win[:, o3:], ((0, 0), (0, LANES - 2 * GDN_HEADS))).astype(BF16),
        bmat=bmat, cmat=cmat, acoef=acoef, dskip=s5_d[0][None], wglu=s5_w_glu[0].astype(BF16),
        bglu=s5_b_glu[0][None], convw=gdn_conv_w[0],
        p1=_lane_row(-jnp.exp(gdn_a_log[0]), GDN_HEADS), p2=_lane_row(gdn_dt_bias[0], GDN_HEADS),
        normw=gdn_norm_w[0][None],
        wout_a=w_out_even[0][:S5_WIDTH].astype(BF16), wout_b=w_out_even[0][S5_WIDTH:].astype(BF16),
        win_odd=jnp.concatenate([perm_cols(wodd[:, :nk]), perm_cols(wodd[:, nk:2 * nk]), wodd[:, 2 * nk:]],
                                axis=1).astype(BF16),
        wout_odd=w_out_odd[0].astype(BF16),
        ln1_g=ln1_g[:, None], ln1_b=ln1_b[:, None], ln2_g=ln2_g[:, None], ln2_b=ln2_b[:, None],
        rw=jnp.pad(router_w, ((0, 0), (0, 0), (0, LANES - N_EXPERTS))),
        rb=jnp.pad(router_b, ((0, 0), (0, LANES - N_EXPERTS)))[:, None],
        w1=[moe_w1[i].astype(BF16) for i in range(DEPTH)], b1=moe_b1[:, :, None],
        w2=[moe_w2[i].astype(BF16) for i in range(DEPTH)], b2=moe_b2[:, :, None],
        plew=ple_w.astype(BF16), gatew=ple_gate_w.astype(BF16),
    )
    bp = x_prompt.shape[0]
    zeros = lambda *s: jnp.zeros(s, F32)
    outs_p = _run_group(x_prompt, p_prompt, zeros(1, bp, S5_GROUPS, S5_STATE), zeros(1, bp, S5_GROUPS, S5_STATE),
                        zeros(1, bp, GDN_HEADS, GDN_DK, GDN_DV), zeros(1, bp, GDN_CONV - 1, GDN_QKV),
                        zeros(1, bp, RET_HEADS, RET_DK, RET_DV), 0.0, W, tm=512, blk=512)
    outs_s = _run_group(x_sample, p_sample, state_s5_re, state_s5_im, state_gdn, state_gdn_conv, state_ret,
                        float(PAST_LEN), W, tm=128, blk=128)
    dp = x_prompt.dtype
    y_p, p_re, p_im, p_gdn, p_conv, p_ret = outs_p
    y_s, s_re, s_im, s_gdn, s_conv, s_ret = outs_s
    return (y_p.astype(dp), y_s.astype(x_sample.dtype),
            p_re.astype(dp), p_im.astype(dp), p_gdn.astype(dp), p_conv.astype(dp), p_ret.astype(dp),
            s_re.astype(state_s5_re.dtype), s_im.astype(state_s5_im.dtype), s_gdn.astype(state_gdn.dtype),
            s_conv.astype(state_gdn_conv.dtype), s_ret.astype(state_ret.dtype))
```

```python
import functools
import math

import jax
import jax.numpy as jnp
import numpy as np
from jax import lax
from jax.experimental import pallas as pl
from jax.experimental.pallas import tpu as pltpu

F32 = jnp.float32
BF16 = jnp.bfloat16
HIGHEST = lax.Precision.HIGHEST

D_MODEL = 1024
CHUNK = 64
S5_WIDTH = 512
S5_GROUP = 16
S5_GROUPS = 32
S5_STATE = 64
S5_TILES = 16
GDN_HEADS = 4
GDN_DK = 128
GDN_DV = 128
GDN_CONV = 4
GDN_QKV = 1536
RET_HEADS = 4
RET_DK = 256
RET_DV = 512
ROPE_BASE = 10000.0
N_EXPERTS = 32
TOP_K = 4
D_FF = 1024
SWIGLU_LIMIT = 7.0
SWIGLU_ALPHA = 1.702
PLE_DIM = 256
DEPTH = 2
PAST_LEN = 1024
DEEPNORM_ALPHA = (2 * DEPTH) ** 0.25
LN_EPS = 1e-5
NORM_EPS = 1e-6

LANES = 128
SUBLANES = 8
VMEM_LIMIT = 48 * 1024 * 1024
EXPERT_VMEM_LIMIT = 58 * 1024 * 1024


def _params(sem):
    return pltpu.CompilerParams(dimension_semantics=sem, vmem_limit_bytes=VMEM_LIMIT)


def _dot(a, b):
    return jnp.dot(a, b, preferred_element_type=F32)


def _dot_hi(a, b):
    return jnp.dot(a, b, preferred_element_type=F32, precision=HIGHEST)


def _dot_nt(a, b, precision=None):
    return lax.dot_general(a, b, (((1,), (1,)), ((), ())), preferred_element_type=F32,
                           precision=precision)


def _dot_tn(a, b):
    return lax.dot_general(a, b, (((0,), (0,)), ((), ())), preferred_element_type=F32)


def _sigmoid(x):
    return 1.0 / (1.0 + jnp.exp(-x))


def _full(shape):
    nd = len(shape)
    return pl.BlockSpec(shape, lambda *_: (0,) * nd)


def _proj_even_kernel(x_ref, wu_ref, wqkv_ref, wz_ref, wba_ref, u_ref, qkv_ref, z_ref, ba_ref):
    xb = x_ref[...].astype(BF16)
    u_ref[...] = _dot(xb, wu_ref[...])
    qkv_ref[...] = _dot(xb, wqkv_ref[...])
    z_ref[...] = _dot(xb, wz_ref[...])
    ba_ref[...] = _dot(xb, wba_ref[...])


def _proj_even(x, wu, wqkv, wz, wba, tm):
    t = x.shape[0]
    row = lambda n: pl.BlockSpec((tm, n), lambda i: (i, 0))
    return pl.pallas_call(
        _proj_even_kernel,
        grid=(t // tm,),
        in_specs=[row(D_MODEL), _full(wu.shape), _full(wqkv.shape), _full(wz.shape), _full(wba.shape)],
        out_specs=[row(S5_WIDTH), row(GDN_QKV), row(GDN_HEADS * GDN_DV), row(LANES)],
        out_shape=[jax.ShapeDtypeStruct((t, S5_WIDTH), F32), jax.ShapeDtypeStruct((t, GDN_QKV), F32),
                   jax.ShapeDtypeStruct((t, GDN_HEADS * GDN_DV), F32), jax.ShapeDtypeStruct((t, LANES), F32)],
        compiler_params=_params(("parallel",)),
        name="proj_even",
    )(x, wu, wqkv, wz, wba)


def _proj_odd_kernel(x_ref, w_ref, cos_ref, sin_ref, o_ref):
    j = pl.program_id(0)
    acc = _dot(x_ref[...].astype(BF16), w_ref[...])

    @pl.when(j == 0)
    def _():
        cos, sin = cos_ref[...], sin_ref[...]
        half = RET_DK // 2
        for h in range(2 * RET_HEADS):
            x0 = acc[:, h * RET_DK:h * RET_DK + half]
            x1 = acc[:, h * RET_DK + half:(h + 1) * RET_DK]
            scale = 1.0 if h < RET_HEADS else RET_DK ** -0.5
            o_ref[:, h * RET_DK:h * RET_DK + half] = (x0 * cos - x1 * sin) * scale
            o_ref[:, h * RET_DK + half:(h + 1) * RET_DK] = (x0 * sin + x1 * cos) * scale

    @pl.when(j != 0)
    def _():
        o_ref[...] = acc


def _proj_odd(x, w, cos, sin, tm):
    t = x.shape[0]
    nblk = w.shape[1] // 2048
    return pl.pallas_call(
        _proj_odd_kernel,
        grid=(nblk, t // tm),
        in_specs=[pl.BlockSpec((tm, D_MODEL), lambda j, i: (i, 0)),
                  pl.BlockSpec((D_MODEL, 2048), lambda j, i: (0, j)),
                  pl.BlockSpec((tm, LANES), lambda j, i: (i, 0)),
                  pl.BlockSpec((tm, LANES), lambda j, i: (i, 0))],
        out_specs=pl.BlockSpec((tm, 2048), lambda j, i: (i, j)),
        out_shape=jax.ShapeDtypeStruct((t, w.shape[1]), F32),
        compiler_params=_params(("parallel", "parallel")),
        name="proj_odd",
    )(x, w, cos, sin)


def _layer_norm(r, g, b):
    mu = jnp.mean(r, -1, keepdims=True)
    d = r - mu
    var = jnp.mean(d * d, -1, keepdims=True)
    return d * lax.rsqrt(var + LN_EPS) * g + b


def _outproj_ln_kernel(*refs, n_in):
    a_refs = refs[:n_in]
    w_refs = refs[n_in:2 * n_in]
    x_ref, g_ref, b_ref, o_ref = refs[2 * n_in:]
    acc = _dot(a_refs[0][...].astype(BF16), w_refs[0][...])
    for a_ref, w_ref in zip(a_refs[1:], w_refs[1:]):
        acc = acc + _dot(a_ref[...].astype(BF16), w_ref[...])
    o_ref[...] = _layer_norm(DEEPNORM_ALPHA * x_ref[...] + acc, g_ref[...], b_ref[...])


def _outproj_ln(acts, ws, x, g, b, tm):
    t = x.shape[0]
    row = lambda n: pl.BlockSpec((tm, n), lambda i: (i, 0))
    return pl.pallas_call(
        functools.partial(_outproj_ln_kernel, n_in=len(acts)),
        grid=(t // tm,),
        in_specs=[row(a.shape[1]) for a in acts] + [_full(w.shape) for w in ws]
                 + [row(D_MODEL), _full((1, D_MODEL)), _full((1, D_MODEL))],
        out_specs=row(D_MODEL),
        out_shape=jax.ShapeDtypeStruct((t, D_MODEL), F32),
        compiler_params=_params(("parallel",)),
        name="outproj_ln",
    )(*acts, *ws, x, g, b)


def _s5_kernel(u_ref, h0_ref, bmat_ref, cmat_ref, acoef_ref, dskip_ref, wglu_ref, bglu_ref,
               y_ref, hout_ref, upad, sre, sim, yacc, hst, *, nb, lt, pitch):
    tb = pl.program_id(0)

    @pl.when(tb == 0)
    def _():
        hst[...] = h0_ref[...]
        upad[...] = jnp.zeros_like(upad)

    for b in range(nb):
        upad[b * pitch:b * pitch + lt, :] = u_ref[b]
    yacc[...] = jnp.zeros_like(yacc)
    halves = nb // SUBLANES

    def tile_body(j, carry):
        q = j // 4
        ub = upad[:, pl.ds(pl.multiple_of(q * LANES, LANES), LANES)].astype(BF16)
        bu = _dot(ub, bmat_ref[j])
        sre[...] = bu[:, :LANES]
        sim[...] = bu[:, LANES:]
        ac = acoef_ref[j]
        ar, ai = ac[:, :LANES], ac[:, LANES:]
        hs = [(hst[j, hf * SUBLANES:(hf + 1) * SUBLANES, 0:LANES],
               hst[j, hf * SUBLANES:(hf + 1) * SUBLANES, LANES:2 * LANES]) for hf in range(halves)]
        for t in range(lt):
            for hf in range(halves):
                hr, hi = hs[hf]
                rows = pl.ds(hf * SUBLANES * pitch + t, SUBLANES, stride=pitch)
                nhr = ar * hr - ai * hi + sre[rows, :]
                nhi = ar * hi + ai * hr + sim[rows, :]
                sre[rows, :] = nhr
                sim[rows, :] = nhi
                hs[hf] = (nhr, nhi)
        for hf in range(halves):
            hst[j, hf * SUBLANES:(hf + 1) * SUBLANES, 0:LANES] = hs[hf][0]
            hst[j, hf * SUBLANES:(hf + 1) * SUBLANES, LANES:2 * LANES] = hs[hf][1]
        st = jnp.concatenate([sre[...], sim[...]], axis=-1).astype(BF16)
        yacc[q] += _dot(st, cmat_ref[j])
        return carry

    lax.fori_loop(0, S5_TILES, tile_body, 0)
    y = jnp.concatenate([yacc[0], yacc[1], yacc[2], yacc[3]], axis=-1) + dskip_ref[...] * upad[...]
    y = jax.nn.gelu(y)
    y = y * _sigmoid(_dot(y.astype(BF16), wglu_ref[...]) + bglu_ref[...])
    for b in range(nb):
        y_ref[b] = y[b * pitch:b * pitch + lt]
    hout_ref[...] = hst[...]


def _s5(u3, h0, bmat, cmat, acoef, dskip, wglu, bglu, lt):
    nb, L, _ = u3.shape
    pitch = lt + SUBLANES
    rows = nb * pitch
    return pl.pallas_call(
        functools.partial(_s5_kernel, nb=nb, lt=lt, pitch=pitch),
        grid=(L // lt,),
        in_specs=[pl.BlockSpec((nb, lt, S5_WIDTH), lambda i: (0, i, 0)),
                  _full(h0.shape), _full(bmat.shape), _full(cmat.shape), _full(acoef.shape),
                  _full(dskip.shape), _full(wglu.shape), _full(bglu.shape)],
        out_specs=[pl.BlockSpec((nb, lt, S5_WIDTH), lambda i: (0, i, 0)), _full(h0.shape)],
        out_shape=[jax.ShapeDtypeStruct((nb, L, S5_WIDTH), F32), jax.ShapeDtypeStruct(h0.shape, F32)],
        scratch_shapes=[pltpu.VMEM((rows, S5_WIDTH), F32), pltpu.VMEM((rows, LANES), F32),
                        pltpu.VMEM((rows, LANES), F32), pltpu.VMEM((4, rows, LANES), F32),
                        pltpu.VMEM(h0.shape, F32)],
        compiler_params=_params(("arbitrary",)),
        name="s5_scan",
    )(u3, h0, bmat, cmat, acoef, dskip, wglu, bglu)


def _s5_weights(a_re, a_im, log_dt, b_re, b_im, c_re, c_im):
    dt = jnp.exp(log_dt)[:, None]
    lr, li = a_re * dt, a_im * dt
    mag = jnp.exp(lr)
    ab_re, ab_im = mag * jnp.cos(li), mag * jnp.sin(li)
    den = a_re * a_re + a_im * a_im
    cf_re = ((ab_re - 1.0) * a_re + ab_im * a_im) / den
    cf_im = (ab_im * a_re - (ab_re - 1.0) * a_im) / den
    bb_re = cf_re[..., None] * b_re - cf_im[..., None] * b_im
    bb_im = cf_re[..., None] * b_im + cf_im[..., None] * b_re
    jj = np.arange(S5_TILES)[:, None, None]
    lg = np.arange(8)[None, :, None]
    gi = np.arange(2)[None, None, :]
    sel = jnp.asarray((lg == 2 * (jj % 4) + gi).astype(np.float32))
    tiles = lambda w: w.reshape(S5_TILES, 2, *w.shape[1:])
    bt = lambda w: jnp.einsum('jlg,jgpn->jlngp', sel, tiles(w)).reshape(S5_TILES, LANES, LANES)
    bmat = jnp.concatenate([bt(bb_re), bt(bb_im)], axis=-1)
    ct = lambda w: jnp.einsum('jlg,jgnp->jgpln', sel, tiles(w)).reshape(S5_TILES, LANES, LANES)
    cmat = jnp.concatenate([ct(c_re), -ct(c_im)], axis=1)
    acoef = jnp.concatenate([ab_re.reshape(S5_TILES, LANES), ab_im.reshape(S5_TILES, LANES)], axis=-1)
    acoef = jnp.broadcast_to(acoef[:, None, :], (S5_TILES, SUBLANES, 2 * LANES))
    return bmat.astype(BF16), cmat.astype(BF16), acoef


def _s5_state_in(h_re, h_im):
    nb = h_re.shape[0]
    h = jnp.concatenate([h_re.reshape(nb, S5_TILES, LANES), h_im.reshape(nb, S5_TILES, LANES)], axis=-1)
    return jnp.transpose(h, (1, 0, 2))


def _s5_state_out(h):
    nb = h.shape[1]
    h = jnp.transpose(h, (1, 0, 2))
    return (h[..., :LANES].reshape(nb, S5_GROUPS, S5_STATE), h[..., LANES:].reshape(nb, S5_GROUPS, S5_STATE))


def _split_bf16(a):
    hi = a.astype(BF16)
    return hi, (a - hi.astype(F32)).astype(BF16)


def _bdot(a, b):
    return lax.dot_general(a, b, (((2,), (1,)), ((0,), (0,))), preferred_element_type=F32)


def _bdot_nt(a, b):
    return lax.dot_general(a, b, (((2,), (2,)), ((0,), (0,))), preferred_element_type=F32)


def _bdot_split(a, b):
    ah, al = _split_bf16(a)
    bh, bl = _split_bf16(b)
    return _bdot(ah, bh) + _bdot(ah, bl) + _bdot(al, bh)


def _unit_lower_inverse(nmat):
    lc = nmat.shape[-1]
    ri = lax.broadcasted_iota(jnp.int32, nmat.shape, 1)
    ci = lax.broadcasted_iota(jnp.int32, nmat.shape, 2)
    base = 16
    dmat = jnp.where(ri // base == ci // base, nmat, 0.0)
    inv = jnp.where(ri == ci, 1.0, 0.0) - dmat
    pw = dmat
    for _ in range(3):
        pw = _bdot_split(pw, pw)
        inv = inv + _bdot_split(inv, pw)
    size = base
    while size < lc:
        off = jnp.where(ri // (2 * size) == ci // (2 * size), jnp.where(ri // size > ci // size, nmat, 0.0), 0.0)
        inv = inv - _bdot_split(_bdot_split(inv, off), inv)
        size *= 2
    return inv


def _gdn_local_kernel(qkv_ref, ba_ref, ctx_ref, cw_ref, p1_ref, p2_ref, ltri_ref,
                      u0_ref, w_ref, qd_ref, kd_ref, attn_ref, g_ref, cout_ref, xpad, *, lc, cp):
    c = pl.program_id(1)
    rb = lc * cp

    @pl.when(c == 0)
    def _():
        xpad[0:SUBLANES, :] = ctx_ref[0]

    xpad[SUBLANES:SUBLANES + rb, :] = qkv_ref[0]
    cw = cw_ref[...]
    conv = (cw[3:4] * xpad[8:8 + rb, :] + cw[2:3] * xpad[7:7 + rb, :]
            + cw[1:2] * xpad[6:6 + rb, :] + cw[0:1] * xpad[5:5 + rb, :])
    tail = xpad[rb:rb + SUBLANES, :]
    xpad[0:SUBLANES, :] = tail
    cout_ref[0] = tail
    a = conv * _sigmoid(conv)

    ba = ba_ref[0]
    beta_all = _sigmoid(ba)
    sp_in = ba + p2_ref[...]
    softplus = jnp.maximum(sp_in, 0.0) + jnp.log(1.0 + jnp.exp(-jnp.abs(sp_in)))
    g_all = p1_ref[...] * softplus
    g_hi = g_all.astype(BF16)
    g_r = g_all - g_hi.astype(F32)
    g_mid = g_r.astype(BF16)
    g_lo = (g_r - g_mid.astype(F32)).astype(BF16)
    lt = ltri_ref[...]
    G = _dot(lt, g_hi) + _dot(lt, g_mid) + _dot(lt, g_lo)
    g_ref[0] = G
    GT = G.T
    pairs = [(h, cc) for h in range(GDN_HEADS) for cc in range(cp)]
    qs, ks, vs, betas, gcols, grows, glasts = [], [], [], [], [], [], []
    for h in range(GDN_HEADS):
        qa = a[:, h * GDN_DK:(h + 1) * GDN_DK]
        ka = a[:, (GDN_HEADS + h) * GDN_DK:(GDN_HEADS + h + 1) * GDN_DK]
        va = a[:, (2 * GDN_HEADS + h) * GDN_DK:(2 * GDN_HEADS + h + 1) * GDN_DK]
        qa = qa * lax.rsqrt(jnp.sum(qa * qa, -1, keepdims=True) + NORM_EPS) * (GDN_DK ** -0.5)
        ka = ka * lax.rsqrt(jnp.sum(ka * ka, -1, keepdims=True) + NORM_EPS)
        for cc in range(cp):
            rows = slice(cc * lc, (cc + 1) * lc)
            qs.append(qa[rows])
            ks.append(ka[rows])
            vs.append(va[rows])
            betas.append(beta_all[rows, h:h + 1])
            gcols.append(G[rows, GDN_HEADS + h:GDN_HEADS + h + 1])
            grows.append(GT[GDN_HEADS + h:GDN_HEADS + h + 1, cc * lc:(cc + 1) * lc])
            glasts.append(GT[GDN_HEADS + h:GDN_HEADS + h + 1, (cc + 1) * lc - 1:(cc + 1) * lc])
    q3, k3, v3 = jnp.stack(qs), jnp.stack(ks), jnp.stack(vs)
    beta3, gcol3 = jnp.stack(betas), jnp.stack(gcols)
    grow3, glast3 = jnp.stack(grows), jnp.stack(glasts)
    shape3 = (len(pairs), lc, lc)
    ri = lax.broadcasted_iota(jnp.int32, shape3, 1)
    ci = lax.broadcasted_iota(jnp.int32, shape3, 2)
    incl = ri >= ci
    dec3 = jnp.where(incl, jnp.exp(jnp.where(incl, gcol3 - grow3, 0.0)), 0.0)
    eg3 = jnp.exp(gcol3)
    kb3 = k3 * beta3
    kbf3 = k3.astype(BF16)
    nmat3 = jnp.where(ri > ci, _bdot_nt(kb3.astype(BF16), kbf3) * dec3, 0.0)
    inv3 = _unit_lower_inverse(nmat3)
    sol3 = _bdot_split(inv3, jnp.concatenate([v3 * beta3, kb3 * eg3], axis=-1))
    w3 = sol3[:, :, GDN_DV:].astype(BF16)
    qd3 = (q3 * eg3).astype(BF16)
    kd3 = (k3 * jnp.exp(glast3 - gcol3)).astype(BF16)
    attn3 = (_bdot_nt(q3.astype(BF16), kbf3) * dec3).astype(BF16)
    for i, (h, cc) in enumerate(pairs):
        rows = slice(cc * lc, (cc + 1) * lc)
        cols = slice(h * GDN_DV, (h + 1) * GDN_DV)
        u0_ref[0, rows, cols] = sol3[i, :, :GDN_DV]
        w_ref[0, rows, cols] = w3[i]
        qd_ref[0, rows, cols] = qd3[i]
        kd_ref[0, rows, cols] = kd3[i]
        attn_ref[0, rows, h * lc:(h + 1) * lc] = attn3[i]


def _gdn_seq_kernel(u0_ref, w_ref, qd_ref, kd_ref, attn_ref, g_ref, z_ref, s0_ref, nw_ref,
                    y_ref, sout_ref, S, *, lc, nbb):
    c = pl.program_id(1)

    @pl.when(c == 0)
    def _():
        S[...] = s0_ref[...]

    nw = nw_ref[...]
    pairs = [(bb, h) for bb in range(nbb) for h in range(GDN_HEADS)]
    hcols = lambda h: slice(h * GDN_DV, (h + 1) * GDN_DV)
    stack = lambda f: jnp.stack([f(bb, h) for bb, h in pairs])
    dlast = jnp.exp(g_ref[:, lc - 1:lc, :])
    S3 = S[...].reshape(len(pairs), GDN_DK, GDN_DV)
    wq3 = stack(lambda bb, h: jnp.concatenate([w_ref[bb, :, hcols(h)], qd_ref[bb, :, hcols(h)]], axis=0))
    r3 = _bdot(wq3, S3.astype(BF16))
    ub3 = (stack(lambda bb, h: u0_ref[bb, :, hcols(h)]) - r3[:, :lc]).astype(BF16)
    o3 = r3[:, lc:] + _bdot(stack(lambda bb, h: attn_ref[bb, :, h * lc:(h + 1) * lc]), ub3)
    d3 = stack(lambda bb, h: dlast[bb, :, GDN_HEADS + h:GDN_HEADS + h + 1])
    kd3 = stack(lambda bb, h: kd_ref[bb, :, hcols(h)])
    kdu3 = lax.dot_general(kd3, ub3, (((1,), (1,)), ((0,), (0,))), preferred_element_type=F32)
    S[...] = (d3 * S3 + kdu3).reshape(S.shape)
    o3 = o3 * lax.rsqrt(jnp.mean(o3 * o3, -1, keepdims=True) + NORM_EPS) * nw
    for i, (bb, h) in enumerate(pairs):
        zh = z_ref[bb, :, hcols(h)]
        y_ref[bb, :, hcols(h)] = o3[i] * (zh * _sigmoid(zh))

    @pl.when(c == pl.num_programs(1) - 1)
    def _():
        sout_ref[...] = S[...]


def _gdn(qkv3, z3, ba3, ctx8, s0, cw, p1, p2, nw, lc, cp, nbb):
    nb, L, _ = qkv3.shape
    rb = lc * cp
    hd = GDN_HEADS * GDN_DV
    ltri = jnp.asarray(np.kron(np.eye(cp, dtype=np.float32), np.tril(np.ones((lc, lc), np.float32)))).astype(BF16)
    blk = lambda n: pl.BlockSpec((1, rb, n), lambda b, c: (b, c, 0))
    per_b = lambda shape: pl.BlockSpec((1,) + shape, lambda b, c: (b,) + (0,) * len(shape))
    cst = lambda shape: pl.BlockSpec(shape, lambda b, c: (0,) * len(shape))
    sds = lambda n, dt: jax.ShapeDtypeStruct((nb, L, n), dt)
    u0, w, qd, kd, attn, G, cout = pl.pallas_call(
        functools.partial(_gdn_local_kernel, lc=lc, cp=cp),
        grid=(nb, L // rb),
        in_specs=[blk(GDN_QKV), blk(LANES), per_b((SUBLANES, GDN_QKV)), cst(cw.shape), cst(p1.shape),
                  cst(p2.shape), cst(ltri.shape)],
        out_specs=[blk(hd), blk(hd), blk(hd), blk(hd), blk(GDN_HEADS * lc), blk(LANES),
                   per_b((SUBLANES, GDN_QKV))],
        out_shape=[sds(hd, F32), sds(hd, BF16), sds(hd, BF16), sds(hd, BF16), sds(GDN_HEADS * lc, BF16),
                   sds(LANES, F32), jax.ShapeDtypeStruct((nb, SUBLANES, GDN_QKV), F32)],
        scratch_shapes=[pltpu.VMEM((rb + SUBLANES, GDN_QKV), F32)],
        compiler_params=_params(("parallel", "arbitrary")),
        name="gdn_local",
    )(qkv3, ba3, ctx8, cw, p1, p2, ltri)
    sblk = lambda n: pl.BlockSpec((nbb, lc, n), lambda b, c: (b, c, 0))
    state = pl.BlockSpec((nbb, GDN_HEADS, GDN_DK, GDN_DV), lambda b, c: (b, 0, 0, 0))
    y, s_new = pl.pallas_call(
        functools.partial(_gdn_seq_kernel, lc=lc, nbb=nbb),
        grid=(nb // nbb, L // lc),
        in_specs=[sblk(hd), sblk(hd), sblk(hd), sblk(hd), sblk(GDN_HEADS * lc), sblk(LANES), sblk(hd),
                  state, cst(nw.shape)],
        out_specs=[sblk(hd), state],
        out_shape=[sds(hd, F32), jax.ShapeDtypeStruct(s0.shape, F32)],
        scratch_shapes=[pltpu.VMEM((nbb, GDN_HEADS, GDN_DK, GDN_DV), F32)],
        compiler_params=_params(("parallel", "arbitrary")),
        name="gdn_seq",
    )(u0, w, qd, kd, attn, G, z3, s0, nw)
    return y, s_new, cout


def _ret_kernel(q_ref, k_ref, v_ref, g_ref, r0_ref, dec_ref, qs_ref, ks_ref, cd_ref, o_ref, rout_ref, R,
                *, nbb):
    c = pl.program_id(1)

    @pl.when(c == 0)
    def _():
        R[...] = r0_ref[...]

    pairs = [(bb, h) for bb in range(nbb) for h in range(RET_HEADS)]
    stack = lambda f: jnp.stack([f(bb, h) for bb, h in pairs])
    kcols = lambda h: slice(h * RET_DK, (h + 1) * RET_DK)
    vcols = lambda h: slice(h * RET_DV, (h + 1) * RET_DV)
    q3 = stack(lambda bb, h: q_ref[bb, :, kcols(h)])
    k3 = stack(lambda bb, h: k_ref[bb, :, kcols(h)])
    v3 = stack(lambda bb, h: v_ref[bb, :, vcols(h)]).astype(BF16)
    dec3 = stack(lambda bb, h: dec_ref[h])
    qs3 = stack(lambda bb, h: qs_ref[h])
    ks3 = stack(lambda bb, h: ks_ref[h])
    cd3 = stack(lambda bb, h: cd_ref[h])
    R3 = R[...].reshape(len(pairs), RET_DK, RET_DV)
    s3 = _bdot_nt(q3.astype(BF16), k3.astype(BF16)) * dec3
    o3 = _bdot(s3.astype(BF16), v3) + _bdot((q3 * qs3).astype(BF16), R3.astype(BF16))
    kv3 = lax.dot_general((k3 * ks3).astype(BF16), v3, (((1,), (1,)), ((0,), (0,))), preferred_element_type=F32)
    R[...] = (cd3 * R3 + kv3).reshape(R.shape)
    mu = jnp.mean(o3, -1, keepdims=True)
    d3 = o3 - mu
    var = jnp.mean(d3 * d3, -1, keepdims=True)
    on3 = d3 * lax.rsqrt(var + LN_EPS)
    for i, (bb, h) in enumerate(pairs):
        gt = g_ref[bb, :, vcols(h)]
        o_ref[bb, :, vcols(h)] = gt * _sigmoid(gt) * on3[i]

    @pl.when(c == pl.num_programs(1) - 1)
    def _():
        rout_ref[...] = R[...]


def _retention(proj3, r0, lc, nbb):
    nb, L, _ = proj3.shape
    log_g = np.log(1.0 - 2.0 ** (-5.0 - np.arange(RET_HEADS, dtype=np.float64)))
    idx = np.arange(lc, dtype=np.float64)
    dec = np.exp(log_g[:, None, None] * np.abs(idx[:, None] - idx[None, :])).astype(np.float32)
    qs = np.exp(log_g[:, None] * (idx + 1.0)).astype(np.float32)[..., None]
    ks = np.exp(log_g[:, None] * (lc - 1.0 - idx)).astype(np.float32)[..., None]
    cdec = np.exp(log_g * lc).astype(np.float32)[:, None, None]
    nqk = RET_HEADS * RET_DK
    nv = RET_HEADS * RET_DV
    cst = lambda shape: pl.BlockSpec(shape, lambda b, c: (0,) * len(shape))
    state = pl.BlockSpec((nbb, RET_HEADS, RET_DK, RET_DV), lambda b, c: (b, 0, 0, 0))
    return pl.pallas_call(
        functools.partial(_ret_kernel, nbb=nbb),
        grid=(nb // nbb, L // lc),
        in_specs=[pl.BlockSpec((nbb, lc, nqk), lambda b, c: (b, c, 0)),
                  pl.BlockSpec((nbb, lc, nqk), lambda b, c: (b, c, 1)),
                  pl.BlockSpec((nbb, lc, nv), lambda b, c: (b, c, 1)),
                  pl.BlockSpec((nbb, lc, nv), lambda b, c: (b, c, 2)),
                  state, cst(dec.shape), cst(qs.shape), cst(ks.shape), cst(cdec.shape)],
        out_specs=[pl.BlockSpec((nbb, lc, nv), lambda b, c: (b, c, 0)), state],
        out_shape=[jax.ShapeDtypeStruct((nb, L, nv), F32), jax.ShapeDtypeStruct(r0.shape, F32)],
        scratch_shapes=[pltpu.VMEM((nbb, RET_HEADS, RET_DK, RET_DV), F32)],
        compiler_params=_params(("parallel", "arbitrary")),
        name="retention",
    )(proj3, proj3, proj3, proj3, r0, jnp.asarray(dec), jnp.asarray(qs), jnp.asarray(ks), jnp.asarray(cdec))


def _router_kernel(x_ref, rw_ref, rb_ref, lst_ref, ri_ref, gt_ref, cnt_ref, run):
    i = pl.program_id(0)

    @pl.when(i == 0)
    def _():
        run[...] = jnp.zeros_like(run)

    tm = x_ref.shape[0]
    lane = lax.broadcasted_iota(jnp.int32, (tm, LANES), 1)
    lane_f = lane.astype(F32)
    logits = _dot_hi(x_ref[...], rw_ref[...]) + rb_ref[...]
    logits = jnp.where(lane < N_EXPERTS, logits, -jnp.inf)
    vals, hots = [], []
    for _ in range(TOP_K):
        m = jnp.max(logits, -1, keepdims=True)
        first = jnp.min(jnp.where(logits == m, lane_f, float(LANES)), -1, keepdims=True)
        hot = lane_f == first
        vals.append(m)
        hots.append(hot)
        logits = jnp.where(hot, -jnp.inf, logits)
    es = [jnp.exp(v - vals[0]) for v in vals]
    den = es[0] + es[1] + es[2] + es[3]
    multi = jnp.zeros((tm, LANES), F32)
    for hot in hots:
        multi = multi + hot.astype(F32)
    before = _dot(lst_ref[...], multi.astype(BF16)) + run[...]
    ri = jnp.zeros((tm, LANES), F32)
    gt = jnp.zeros((tm, LANES), F32)
    for kk in range(TOP_K):
        idx = jnp.sum(jnp.where(hots[kk], lane_f, 0.0), -1, keepdims=True)
        rank = jnp.sum(jnp.where(hots[kk], before, 0.0), -1, keepdims=True)
        ri = jnp.where(lane == kk, idx, ri)
        ri = jnp.where(lane == TOP_K + kk, rank, ri)
        gt = jnp.where(lane == kk, es[kk] / den, gt)
    ri_ref[...] = ri.astype(jnp.int32)
    gt_ref[...] = gt
    run[...] = run[...] + jnp.sum(multi, 0, keepdims=True)
    cnt_ref[...] = run[...]


def _router(x, rw, rb, tm):
    t = x.shape[0]
    lst = jnp.asarray(np.tril(np.ones((tm, tm), np.float32), -1)).astype(BF16)
    row = lambda n: pl.BlockSpec((tm, n), lambda i: (i, 0))
    return pl.pallas_call(
        _router_kernel,
        grid=(t // tm,),
        in_specs=[row(D_MODEL), _full(rw.shape), _full(rb.shape), _full(lst.shape)],
        out_specs=[row(LANES), row(LANES), _full((1, LANES))],
        out_shape=[jax.ShapeDtypeStruct((t, LANES), jnp.int32), jax.ShapeDtypeStruct((t, LANES), F32),
                   jax.ShapeDtypeStruct((1, LANES), F32)],
        scratch_shapes=[pltpu.VMEM((1, LANES), F32)],
        compiler_params=_params(("arbitrary",)),
        name="router",
    )(x, rw, rb, lst)


def _dispatch_kernel(dest_ref, x_ref, xs_in, xs_out, sem):
    del xs_in
    tm = x_ref.shape[0]

    def body(r, carry):
        for kk in range(TOP_K):
            d = dest_ref[TOP_K * r + kk]
            pltpu.make_async_copy(x_ref.at[pl.ds(r, 1)], xs_out.at[pl.ds(d, 1)], sem).start()
        return carry

    lax.fori_loop(0, tm, body, 0)
    for _ in range(TOP_K):
        pltpu.make_async_copy(x_ref, xs_out.at[pl.ds(0, tm)], sem).wait()


def _dispatch(dest_flat, x, n_pad, tm):
    t = x.shape[0]
    xs0 = jnp.zeros((n_pad, D_MODEL), F32)
    return pl.pallas_call(
        _dispatch_kernel,
        grid=(t // tm,),
        in_specs=[pl.BlockSpec((tm * TOP_K,), lambda i: (i,), memory_space=pltpu.SMEM),
                  pl.BlockSpec((tm, D_MODEL), lambda i: (i, 0)),
                  pl.BlockSpec(memory_space=pl.ANY)],
        out_specs=pl.BlockSpec(memory_space=pl.ANY),
        out_shape=jax.ShapeDtypeStruct((n_pad, D_MODEL), F32),
        scratch_shapes=[pltpu.SemaphoreType.DMA],
        input_output_aliases={2: 0},
        compiler_params=_params(("arbitrary",)),
        name="moe_dispatch",
    )(dest_flat, x, xs0)


def _expert_kernel(be_ref, nu_ref, xs_ref, w1_ref, b1_ref, w2_ref, b2_ref, ys_ref, w1b, w2b):
    i = pl.program_id(0)

    @pl.when((i == 0) | (be_ref[i] != be_ref[jnp.maximum(i - 1, 0)]))
    def _():
        w1b[...] = w1_ref[0, 0].astype(BF16)
        w2b[...] = w2_ref[0, 0].astype(BF16)

    @pl.when(i < nu_ref[0])
    def _():
        h = _dot(xs_ref[...].astype(BF16), w1b[...]) + b1_ref[0]
        glu = jnp.minimum(h[:, :D_FF], SWIGLU_LIMIT)
        lin = jnp.clip(h[:, D_FF:], -SWIGLU_LIMIT, SWIGLU_LIMIT)
        act = glu * _sigmoid(SWIGLU_ALPHA * glu) * (lin + 1.0)
        ys_ref[...] = _dot(act.astype(BF16), w2b[...]) + b2_ref[0]

    @pl.when(i >= nu_ref[0])
    def _():
        ys_ref[...] = jnp.zeros_like(ys_ref)


def _experts(blk_e, n_used, xs, w1, b1, w2, b2, layer, blk):
    n_pad = xs.shape[0]
    n_blk = n_pad // blk
    used = lambda i, nu: jnp.maximum(jnp.minimum(i, nu[0] - 1), 0)
    grid_spec = pltpu.PrefetchScalarGridSpec(
        num_scalar_prefetch=2,
        grid=(n_blk,),
        in_specs=[pl.BlockSpec((blk, D_MODEL), lambda i, be, nu: (used(i, nu), 0)),
                  pl.BlockSpec((1, 1, D_MODEL, 2 * D_FF), lambda i, be, nu: (layer, be[i], 0, 0)),
                  pl.BlockSpec((1, 1, 2 * D_FF), lambda i, be, nu: (be[i], 0, 0)),
                  pl.BlockSpec((1, 1, D_FF, D_MODEL), lambda i, be, nu: (layer, be[i], 0, 0)),
                  pl.BlockSpec((1, 1, D_MODEL), lambda i, be, nu: (be[i], 0, 0))],
        out_specs=pl.BlockSpec((blk, D_MODEL), lambda i, be, nu: (i, 0)),
        scratch_shapes=[pltpu.VMEM((D_MODEL, 2 * D_FF), BF16), pltpu.VMEM((D_FF, D_MODEL), BF16)],
    )
    return pl.pallas_call(
        _expert_kernel,
        grid_spec=grid_spec,
        out_shape=jax.ShapeDtypeStruct((n_pad, D_MODEL), F32),
        compiler_params=pltpu.CompilerParams(dimension_semantics=("arbitrary",),
                                             vmem_limit_bytes=EXPERT_VMEM_LIMIT),
        name="moe_experts",
    )(blk_e, n_used, xs, w1, b1, w2, b2)


def _combine_kernel(dest_ref, x_ref, gt_ref, p_ref, ys_hbm, g_ref, b_ref, plew_ref, gatew_ref, o_ref,
                    buf, sem):
    tm = x_ref.shape[0]

    def body(r, carry):
        for kk in range(TOP_K):
            d = dest_ref[TOP_K * r + kk]
            pltpu.make_async_copy(ys_hbm.at[pl.ds(d, 1)], buf.at[kk, pl.ds(r, 1)], sem).start()
        return carry

    lax.fori_loop(0, tm, body, 0)
    for kk in range(TOP_K):
        pltpu.make_async_copy(ys_hbm.at[pl.ds(0, tm)], buf.at[kk], sem).wait()
    gt = gt_ref[...]
    y = gt[:, 0:1] * buf[0]
    for kk in range(1, TOP_K):
        y = y + gt[:, kk:kk + 1] * buf[kk]
    x2 = _layer_norm(DEEPNORM_ALPHA * x_ref[...] + y, g_ref[...], b_ref[...])
    pp = _dot(p_ref[...].astype(BF16), plew_ref[...])
    gg = _sigmoid(_dot(x2.astype(BF16), gatew_ref[...]))
    o_ref[...] = x2 + pp * gg


def _combine(dest_flat, x, gt, p, ys, g, b, plew, gatew, tm):
    t = x.shape[0]
    row = lambda n: pl.BlockSpec((tm, n), lambda i: (i, 0))
    return pl.pallas_call(
        _combine_kernel,
        grid=(t // tm,),
        in_specs=[pl.BlockSpec((tm * TOP_K,), lambda i: (i,), memory_space=pltpu.SMEM),
                  row(D_MODEL), row(LANES), row(PLE_DIM), pl.BlockSpec(memory_space=pl.ANY),
                  _full((1, D_MODEL)), _full((1, D_MODEL)), _full(plew.shape), _full(gatew.shape)],
        out_specs=row(D_MODEL),
        out_shape=jax.ShapeDtypeStruct((t, D_MODEL), F32),
        scratch_shapes=[pltpu.VMEM((TOP_K, tm, D_MODEL), F32), pltpu.SemaphoreType.DMA],
        compiler_params=_params(("arbitrary",)),
        name="moe_combine",
    )(dest_flat, x, gt, p, ys, g, b, plew, gatew)


def _moe_ple(x, p, rw, rb, w1, b1, w2, b2, layer, g, b, plew, gatew, tm, blk):
    t = x.shape[0]
    ri, gt, cnt = _router(x, rw, rb, tm)
    idx, rank = ri[:, :TOP_K], ri[:, TOP_K:2 * TOP_K]
    counts = cnt[0, :N_EXPERTS].astype(jnp.int32)
    padded = (counts + blk - 1) // blk * blk
    pend = jnp.cumsum(padded)
    pstart = pend - padded
    dest = (pstart[idx] + rank).reshape(-1).astype(jnp.int32)
    n_blk = -(-(t * TOP_K + N_EXPERTS * (blk - 1)) // blk)
    n_used = (pend[-1] // blk).astype(jnp.int32).reshape(1)
    blk_start = jnp.arange(n_blk, dtype=jnp.int32) * blk
    blk_e = jnp.minimum(jnp.sum((pend[None, :] <= blk_start[:, None]).astype(jnp.int32), axis=1), N_EXPERTS - 1)
    xs = _dispatch(dest, x, n_blk * blk, tm)
    ys = _experts(blk_e, n_used, xs, w1, b1, w2, b2, layer, blk)
    return _combine(dest, x, gt, p, ys, g, b, plew, gatew, tm)


_ROT_PERM = np.concatenate([np.arange(0, RET_DK, 2), np.arange(1, RET_DK, 2)])
_ROT_INV = np.argsort(_ROT_PERM)


def _lane_row(vals, offset):
    row = jnp.zeros((1, LANES), F32)
    return row.at[0, offset:offset + vals.shape[0]].set(vals)


def _run_group(x3, p, s5_re, s5_im, gdn_s, conv_s, ret_s, pos0, W, tm, blk):
    nb, L, _ = x3.shape
    t = nb * L
    lc = L if L <= CHUNK else CHUNK
    x = x3.reshape(t, D_MODEL).astype(F32)

    u, qkv, z, ba = _proj_even(x, W['wu'], W['wqkv'], W['wz'], W['wba'], tm)
    yA3, h_new = _s5(u.reshape(nb, L, S5_WIDTH), _s5_state_in(s5_re[0].astype(F32), s5_im[0].astype(F32)),
                     W['bmat'], W['cmat'], W['acoef'], W['dskip'], W['wglu'], W['bglu'], lc)
    new_re, new_im = _s5_state_out(h_new)
    ctx8 = jnp.concatenate([jnp.zeros((nb, SUBLANES - (GDN_CONV - 1), GDN_QKV), F32), conv_s[0].astype(F32)], axis=1)
    yB3, new_gdn, cout = _gdn(qkv.reshape(nb, L, GDN_QKV), z.reshape(nb, L, -1), ba.reshape(nb, L, LANES),
                              ctx8, gdn_s[0].astype(F32), W['convw'], W['p1'], W['p2'], W['normw'], lc,
                              cp=min(4, L // lc), nbb=4)
    new_conv = cout[:, SUBLANES - (GDN_CONV - 1):, :]
    x = _outproj_ln([yA3.reshape(t, -1), yB3.reshape(t, -1)], [W['wout_a'], W['wout_b']], x,
                    W['ln1_g'][0], W['ln1_b'][0], tm)
    x = _moe_ple(x, p[0].reshape(t, PLE_DIM), W['rw'][0], W['rb'][0], W['w1'], W['b1'][0], W['w2'],
                 W['b2'][0], 0, W['ln2_g'][0], W['ln2_b'][0], W['plew'][0], W['gatew'][0], tm, blk)

    pos = pos0 + jnp.arange(L, dtype=F32)
    freq = 1.0 / (ROPE_BASE ** jnp.linspace(0.0, 1.0, RET_DK // 2, dtype=F32))
    ang = pos[:, None] * freq[None]
    cos = jnp.broadcast_to(jnp.cos(ang)[None], (nb, L, RET_DK // 2)).reshape(t, RET_DK // 2)
    sin = jnp.broadcast_to(jnp.sin(ang)[None], (nb, L, RET_DK // 2)).reshape(t, RET_DK // 2)
    proj = _proj_odd(x, W['win_odd'], cos, sin, tm)
    r0 = ret_s[0].astype(F32)[:, :, _ROT_PERM, :]
    o3, r_new = _retention(proj.reshape(nb, L, -1), r0, lc, nbb=2)
    new_ret = r_new[:, :, _ROT_INV, :]
    x = _outproj_ln([o3.reshape(t, -1)], [W['wout_odd']], x, W['ln1_g'][1], W['ln1_b'][1], tm)
    x = _moe_ple(x, p[1].reshape(t, PLE_DIM), W['rw'][1], W['rb'][1], W['w1'], W['b1'][1], W['w2'],
                 W['b2'][1], 1, W['ln2_g'][1], W['ln2_b'][1], W['plew'][1], W['gatew'][1], tm, blk)
    return (x.reshape(nb, L, D_MODEL), new_re[None], new_im[None], new_gdn[None], new_conv[None], new_ret[None])


def kernel(x_prompt, x_sample, state_s5_re, state_s5_im, state_gdn, state_gdn_conv, state_ret, p_prompt, p_sample, w_in_even, s5_a_re, s5_a_im, s5_log_dt, s5_b_re, s5_b_im, s5_c_re, s5_c_im, s5_d, s5_w_glu, s5_b_glu, gdn_conv_w, gdn_a_log, gdn_dt_bias, gdn_norm_w, w_out_even, w_in_odd, w_out_odd, ln1_g, ln1_b, ln2_g, ln2_b, router_w, router_b, moe_w1, moe_b1, moe_w2, moe_b2, ple_w, ple_gate_w):
    o1 = S5_WIDTH
    o2 = o1 + GDN_QKV
    o3 = o2 + GDN_HEADS * GDN_DV
    win = w_in_even[0]
    bmat, cmat, acoef = _s5_weights(s5_a_re[0], s5_a_im[0], s5_log_dt[0], s5_b_re[0], s5_b_im[0],
                                    s5_c_re[0], s5_c_im[0])
    wodd = w_in_odd[0]
    nk = RET_HEADS * RET_DK
    perm_cols = lambda w: w.reshape(D_MODEL, RET_HEADS, RET_DK)[:, :, _ROT_PERM].reshape(D_MODEL, nk)
    W = dict(
        wu=win[:, :o1].astype(BF16), wqkv=win[:, o1:o2].astype(BF16), wz=win[:, o2:o3].astype(BF16),
        wba=jnp.pad(win[:, o3:], ((0, 0), (0, LANES - 2 * GDN_HEADS))).astype(BF16),
        bmat=bmat, cmat=cmat, acoef=acoef, dskip=s5_d[0][None], wglu=s5_w_glu[0].astype(BF16),
        bglu=s5_b_glu[0][None], convw=gdn_conv_w[0],
        p1=_lane_row(-jnp.exp(gdn_a_log[0]), GDN_HEADS), p2=_lane_row(gdn_dt_bias[0], GDN_HEADS),
        normw=gdn_norm_w[0][None],
        wout_a=w_out_even[0][:S5_WIDTH].astype(BF16), wout_b=w_out_even[0][S5_WIDTH:].astype(BF16),
        win_odd=jnp.concatenate([perm_cols(wodd[:, :nk]), perm_cols(wodd[:, nk:2 * nk]), wodd[:, 2 * nk:]],
                                axis=1).astype(BF16),
        wout_odd=w_out_odd[0].astype(BF16),
        ln1_g=ln1_g[:, None], ln1_b=ln1_b[:, None], ln2_g=ln2_g[:, None], ln2_b=ln2_b[:, None],
        rw=jnp.pad(router_w, ((0, 0), (0, 0), (0, LANES - N_EXPERTS))),
        rb=jnp.pad(router_b, ((0, 0), (0, LANES - N_EXPERTS)))[:, None],
        w1=moe_w1, b1=moe_b1[:, :, None], w2=moe_w2, b2=moe_b2[:, :, None],
        plew=ple_w.astype(BF16), gatew=ple_gate_w.astype(BF16),
    )
    bp = x_prompt.shape[0]
    zeros = lambda *s: jnp.zeros(s, F32)
    outs_p = _run_group(x_prompt, p_prompt, zeros(1, bp, S5_GROUPS, S5_STATE), zeros(1, bp, S5_GROUPS, S5_STATE),
                        zeros(1, bp, GDN_HEADS, GDN_DK, GDN_DV), zeros(1, bp, GDN_CONV - 1, GDN_QKV),
                        zeros(1, bp, RET_HEADS, RET_DK, RET_DV), 0.0, W, tm=512, blk=512)
    outs_s = _run_group(x_sample, p_sample, state_s5_re, state_s5_im, state_gdn, state_gdn_conv, state_ret,
                        float(PAST_LEN), W, tm=128, blk=128)
    dp = x_prompt.dtype
    y_p, p_re, p_im, p_gdn, p_conv, p_ret = outs_p
    y_s, s_re, s_im, s_gdn, s_conv, s_ret = outs_s
    return (y_p.astype(dp), y_s.astype(x_sample.dtype),
            p_re.astype(dp), p_im.astype(dp), p_gdn.astype(dp), p_conv.astype(dp), p_ret.astype(dp),
            s_re.astype(state_s5_re.dtype), s_im.astype(state_s5_im.dtype), s_gdn.astype(state_gdn.dtype),
            s_conv.astype(state_gdn_conv.dtype), s_ret.astype(state_ret.dtype))
```

```python
import functools
import math

import jax
import jax.numpy as jnp
import numpy as np
from jax import lax
from jax.experimental import pallas as pl
from jax.experimental.pallas import tpu as pltpu

F32 = jnp.float32
BF16 = jnp.bfloat16
HIGHEST = lax.Precision.HIGHEST

D_MODEL = 1024
CHUNK = 64
S5_WIDTH = 512
S5_GROUP = 16
S5_GROUPS = 32
S5_STATE = 64
S5_TILES = 16
GDN_HEADS = 4
GDN_DK = 128
GDN_DV = 128
GDN_CONV = 4
GDN_QKV = 1536
RET_HEADS = 4
RET_DK = 256
RET_DV = 512
ROPE_BASE = 10000.0
N_EXPERTS = 32
TOP_K = 4
D_FF = 1024
SWIGLU_LIMIT = 7.0
SWIGLU_ALPHA = 1.702
PLE_DIM = 256
DEPTH = 2
PAST_LEN = 1024
DEEPNORM_ALPHA = (2 * DEPTH) ** 0.25
LN_EPS = 1e-5
NORM_EPS = 1e-6

LANES = 128
SUBLANES = 8
VMEM_LIMIT = 48 * 1024 * 1024

def _params(sem):
    return pltpu.CompilerParams(dimension_semantics=sem, vmem_limit_bytes=VMEM_LIMIT)


def _dot(a, b):
    return jnp.dot(a, b, preferred_element_type=F32)


def _dot_hi(a, b):
    return jnp.dot(a, b, preferred_element_type=F32, precision=HIGHEST)


def _dot_nt(a, b, precision=None):
    return lax.dot_general(a, b, (((1,), (1,)), ((), ())), preferred_element_type=F32,
                           precision=precision)


def _dot_tn(a, b):
    return lax.dot_general(a, b, (((0,), (0,)), ((), ())), preferred_element_type=F32)


def _sigmoid(x):
    return 1.0 / (1.0 + jnp.exp(-x))


def _full(shape):
    nd = len(shape)
    return pl.BlockSpec(shape, lambda *_: (0,) * nd)


def _proj_even_kernel(x_ref, wu_ref, wqkv_ref, wz_ref, wba_ref, u_ref, qkv_ref, z_ref, ba_ref):
    xb = x_ref[...].astype(BF16)
    u_ref[...] = _dot(xb, wu_ref[...])
    qkv_ref[...] = _dot(xb, wqkv_ref[...])
    z_ref[...] = _dot(xb, wz_ref[...])
    ba_ref[...] = _dot(xb, wba_ref[...])


def _proj_even(x, wu, wqkv, wz, wba, tm):
    t = x.shape[0]
    row = lambda n: pl.BlockSpec((tm, n), lambda i: (i, 0))
    return pl.pallas_call(
        _proj_even_kernel,
        grid=(t // tm,),
        in_specs=[row(D_MODEL), _full(wu.shape), _full(wqkv.shape), _full(wz.shape), _full(wba.shape)],
        out_specs=[row(S5_WIDTH), row(GDN_QKV), row(GDN_HEADS * GDN_DV), row(LANES)],
        out_shape=[jax.ShapeDtypeStruct((t, S5_WIDTH), F32), jax.ShapeDtypeStruct((t, GDN_QKV), F32),
                   jax.ShapeDtypeStruct((t, GDN_HEADS * GDN_DV), F32), jax.ShapeDtypeStruct((t, LANES), F32)],
        compiler_params=_params(("parallel",)),
        name="proj_even",
    )(x, wu, wqkv, wz, wba)


def _proj_odd_kernel(x_ref, w_ref, cos_ref, sin_ref, o_ref):
    j = pl.program_id(0)
    acc = _dot(x_ref[...].astype(BF16), w_ref[...])

    @pl.when(j == 0)
    def _():
        cos, sin = cos_ref[...], sin_ref[...]
        half = RET_DK // 2
        for h in range(2 * RET_HEADS):
            x0 = acc[:, h * RET_DK:h * RET_DK + half]
            x1 = acc[:, h * RET_DK + half:(h + 1) * RET_DK]
            scale = 1.0 if h < RET_HEADS else RET_DK ** -0.5
            o_ref[:, h * RET_DK:h * RET_DK + half] = (x0 * cos - x1 * sin) * scale
            o_ref[:, h * RET_DK + half:(h + 1) * RET_DK] = (x0 * sin + x1 * cos) * scale

    @pl.when(j != 0)
    def _():
        o_ref[...] = acc


def _proj_odd(x, w, cos, sin, tm):
    t = x.shape[0]
    nblk = w.shape[1] // 2048
    return pl.pallas_call(
        _proj_odd_kernel,
        grid=(nblk, t // tm),
        in_specs=[pl.BlockSpec((tm, D_MODEL), lambda j, i: (i, 0)),
                  pl.BlockSpec((D_MODEL, 2048), lambda j, i: (0, j)),
                  pl.BlockSpec((tm, LANES), lambda j, i: (i, 0)),
                  pl.BlockSpec((tm, LANES), lambda j, i: (i, 0))],
        out_specs=pl.BlockSpec((tm, 2048), lambda j, i: (i, j)),
        out_shape=jax.ShapeDtypeStruct((t, w.shape[1]), F32),
        compiler_params=_params(("parallel", "parallel")),
        name="proj_odd",
    )(x, w, cos, sin)


def _layer_norm(r, g, b):
    mu = jnp.mean(r, -1, keepdims=True)
    d = r - mu
    var = jnp.mean(d * d, -1, keepdims=True)
    return d * lax.rsqrt(var + LN_EPS) * g + b


def _outproj_ln_kernel(*refs, n_in):
    a_refs = refs[:n_in]
    w_refs = refs[n_in:2 * n_in]
    x_ref, g_ref, b_ref, o_ref = refs[2 * n_in:]
    acc = _dot(a_refs[0][...].astype(BF16), w_refs[0][...])
    for a_ref, w_ref in zip(a_refs[1:], w_refs[1:]):
        acc = acc + _dot(a_ref[...].astype(BF16), w_ref[...])
    o_ref[...] = _layer_norm(DEEPNORM_ALPHA * x_ref[...] + acc, g_ref[...], b_ref[...])


def _outproj_ln(acts, ws, x, g, b, tm):
    t = x.shape[0]
    row = lambda n: pl.BlockSpec((tm, n), lambda i: (i, 0))
    return pl.pallas_call(
        functools.partial(_outproj_ln_kernel, n_in=len(acts)),
        grid=(t // tm,),
        in_specs=[row(a.shape[1]) for a in acts] + [_full(w.shape) for w in ws]
                 + [row(D_MODEL), _full((1, D_MODEL)), _full((1, D_MODEL))],
        out_specs=row(D_MODEL),
        out_shape=jax.ShapeDtypeStruct((t, D_MODEL), F32),
        compiler_params=_params(("parallel",)),
        name="outproj_ln",
    )(*acts, *ws, x, g, b)


def _s5_kernel(u_ref, h0_ref, bmat_ref, cmat_ref, acoef_ref, dskip_ref, wglu_ref, bglu_ref,
               y_ref, hout_ref, upad, sre, sim, yacc, hst, *, nb, lt, pitch):
    tb = pl.program_id(0)

    @pl.when(tb == 0)
    def _():
        hst[...] = h0_ref[...]
        upad[...] = jnp.zeros_like(upad)

    for b in range(nb):
        upad[b * pitch:b * pitch + lt, :] = u_ref[b]
    yacc[...] = jnp.zeros_like(yacc)
    halves = nb // SUBLANES

    def tile_body(j, carry):
        q = j // 4
        ub = upad[:, pl.ds(pl.multiple_of(q * LANES, LANES), LANES)].astype(BF16)
        bu = _dot(ub, bmat_ref[j])
        sre[...] = bu[:, :LANES]
        sim[...] = bu[:, LANES:]
        ac = acoef_ref[j]
        ar, ai = ac[:, :LANES], ac[:, LANES:]
        hs = [(hst[j, hf * SUBLANES:(hf + 1) * SUBLANES, 0:LANES],
               hst[j, hf * SUBLANES:(hf + 1) * SUBLANES, LANES:2 * LANES]) for hf in range(halves)]
        for t in range(lt):
            for hf in range(halves):
                hr, hi = hs[hf]
                rows = pl.ds(hf * SUBLANES * pitch + t, SUBLANES, stride=pitch)
                nhr = ar * hr - ai * hi + sre[rows, :]
                nhi = ar * hi + ai * hr + sim[rows, :]
                sre[rows, :] = nhr
                sim[rows, :] = nhi
                hs[hf] = (nhr, nhi)
        for hf in range(halves):
            hst[j, hf * SUBLANES:(hf + 1) * SUBLANES, 0:LANES] = hs[hf][0]
            hst[j, hf * SUBLANES:(hf + 1) * SUBLANES, LANES:2 * LANES] = hs[hf][1]
        st = jnp.concatenate([sre[...], sim[...]], axis=-1).astype(BF16)
        yacc[q] += _dot(st, cmat_ref[j])
        return carry

    lax.fori_loop(0, S5_TILES, tile_body, 0)
    y = jnp.concatenate([yacc[0], yacc[1], yacc[2], yacc[3]], axis=-1) + dskip_ref[...] * upad[...]
    y = jax.nn.gelu(y)
    y = y * _sigmoid(_dot(y.astype(BF16), wglu_ref[...]) + bglu_ref[...])
    for b in range(nb):
        y_ref[b] = y[b * pitch:b * pitch + lt]
    hout_ref[...] = hst[...]


def _s5(u3, h0, bmat, cmat, acoef, dskip, wglu, bglu, lt):
    nb, L, _ = u3.shape
    pitch = lt + SUBLANES
    rows = nb * pitch
    return pl.pallas_call(
        functools.partial(_s5_kernel, nb=nb, lt=lt, pitch=pitch),
        grid=(L // lt,),
        in_specs=[pl.BlockSpec((nb, lt, S5_WIDTH), lambda i: (0, i, 0)),
                  _full(h0.shape), _full(bmat.shape), _full(cmat.shape), _full(acoef.shape),
                  _full(dskip.shape), _full(wglu.shape), _full(bglu.shape)],
        out_specs=[pl.BlockSpec((nb, lt, S5_WIDTH), lambda i: (0, i, 0)), _full(h0.shape)],
        out_shape=[jax.ShapeDtypeStruct((nb, L, S5_WIDTH), F32), jax.ShapeDtypeStruct(h0.shape, F32)],
        scratch_shapes=[pltpu.VMEM((rows, S5_WIDTH), F32), pltpu.VMEM((rows, LANES), F32),
                        pltpu.VMEM((rows, LANES), F32), pltpu.VMEM((4, rows, LANES), F32),
                        pltpu.VMEM(h0.shape, F32)],
        compiler_params=_params(("arbitrary",)),
        name="s5_scan",
    )(u3, h0, bmat, cmat, acoef, dskip, wglu, bglu)


def _s5_weights(a_re, a_im, log_dt, b_re, b_im, c_re, c_im):
    dt = jnp.exp(log_dt)[:, None]
    lr, li = a_re * dt, a_im * dt
    mag = jnp.exp(lr)
    ab_re, ab_im = mag * jnp.cos(li), mag * jnp.sin(li)
    den = a_re * a_re + a_im * a_im
    cf_re = ((ab_re - 1.0) * a_re + ab_im * a_im) / den
    cf_im = (ab_im * a_re - (ab_re - 1.0) * a_im) / den
    bb_re = cf_re[..., None] * b_re - cf_im[..., None] * b_im
    bb_im = cf_re[..., None] * b_im + cf_im[..., None] * b_re
    jj = np.arange(S5_TILES)[:, None, None]
    lg = np.arange(8)[None, :, None]
    gi = np.arange(2)[None, None, :]
    sel = jnp.asarray((lg == 2 * (jj % 4) + gi).astype(np.float32))
    tiles = lambda w: w.reshape(S5_TILES, 2, *w.shape[1:])
    bt = lambda w: jnp.einsum('jlg,jgpn->jlngp', sel, tiles(w)).reshape(S5_TILES, LANES, LANES)
    bmat = jnp.concatenate([bt(bb_re), bt(bb_im)], axis=-1)
    ct = lambda w: jnp.einsum('jlg,jgnp->jgpln', sel, tiles(w)).reshape(S5_TILES, LANES, LANES)
    cmat = jnp.concatenate([ct(c_re), -ct(c_im)], axis=1)
    acoef = jnp.concatenate([ab_re.reshape(S5_TILES, LANES), ab_im.reshape(S5_TILES, LANES)], axis=-1)
    acoef = jnp.broadcast_to(acoef[:, None, :], (S5_TILES, SUBLANES, 2 * LANES))
    return bmat.astype(BF16), cmat.astype(BF16), acoef


def _s5_state_in(h_re, h_im):
    nb = h_re.shape[0]
    h = jnp.concatenate([h_re.reshape(nb, S5_TILES, LANES), h_im.reshape(nb, S5_TILES, LANES)], axis=-1)
    return jnp.transpose(h, (1, 0, 2))


def _s5_state_out(h):
    nb = h.shape[1]
    h = jnp.transpose(h, (1, 0, 2))
    return (h[..., :LANES].reshape(nb, S5_GROUPS, S5_STATE), h[..., LANES:].reshape(nb, S5_GROUPS, S5_STATE))


def _split_bf16(a):
    hi = a.astype(BF16)
    return hi, (a - hi.astype(F32)).astype(BF16)


def _bdot(a, b):
    return lax.dot_general(a, b, (((2,), (1,)), ((0,), (0,))), preferred_element_type=F32)


def _bdot_nt(a, b):
    return lax.dot_general(a, b, (((2,), (2,)), ((0,), (0,))), preferred_element_type=F32)


def _bdot_split(a, b):
    ah, al = _split_bf16(a)
    bh, bl = _split_bf16(b)
    return _bdot(ah, bh) + _bdot(ah, bl) + _bdot(al, bh)


def _unit_lower_inverse(nmat):
    lc = nmat.shape[-1]
    ri = lax.broadcasted_iota(jnp.int32, nmat.shape, 1)
    ci = lax.broadcasted_iota(jnp.int32, nmat.shape, 2)
    base = 16
    dmat = jnp.where(ri // base == ci // base, nmat, 0.0)
    inv = jnp.where(ri == ci, 1.0, 0.0) - dmat
    pw = dmat
    for _ in range(3):
        pw = _bdot_split(pw, pw)
        inv = inv + _bdot_split(inv, pw)
    size = base
    while size < lc:
        off = jnp.where(ri // (2 * size) == ci // (2 * size), jnp.where(ri // size > ci // size, nmat, 0.0), 0.0)
        inv = inv - _bdot_split(_bdot_split(inv, off), inv)
        size *= 2
    return inv


def _gdn_local_kernel(qkv_ref, ba_ref, ctx_ref, cw_ref, p1_ref, p2_ref, ltri_ref,
                      u0_ref, w_ref, qd_ref, kd_ref, attn_ref, g_ref, cout_ref, xpad, *, lc, cp):
    c = pl.program_id(1)
    rb = lc * cp

    @pl.when(c == 0)
    def _():
        xpad[0:SUBLANES, :] = ctx_ref[0]

    xpad[SUBLANES:SUBLANES + rb, :] = qkv_ref[0]
    cw = cw_ref[...]
    conv = (cw[3:4] * xpad[8:8 + rb, :] + cw[2:3] * xpad[7:7 + rb, :]
            + cw[1:2] * xpad[6:6 + rb, :] + cw[0:1] * xpad[5:5 + rb, :])
    tail = xpad[rb:rb + SUBLANES, :]
    xpad[0:SUBLANES, :] = tail
    cout_ref[0] = tail
    a = conv * _sigmoid(conv)

    ba = ba_ref[0]
    beta_all = _sigmoid(ba)
    sp_in = ba + p2_ref[...]
    softplus = jnp.maximum(sp_in, 0.0) + jnp.log(1.0 + jnp.exp(-jnp.abs(sp_in)))
    g_all = p1_ref[...] * softplus
    g_hi = g_all.astype(BF16)
    g_r = g_all - g_hi.astype(F32)
    g_mid = g_r.astype(BF16)
    g_lo = (g_r - g_mid.astype(F32)).astype(BF16)
    lt = ltri_ref[...]
    G = _dot(lt, g_hi) + _dot(lt, g_mid) + _dot(lt, g_lo)
    g_ref[0] = G
    GT = G.T
    pairs = [(h, cc) for h in range(GDN_HEADS) for cc in range(cp)]
    qs, ks, vs, betas, gcols, grows, glasts = [], [], [], [], [], [], []
    for h in range(GDN_HEADS):
        qa = a[:, h * GDN_DK:(h + 1) * GDN_DK]
        ka = a[:, (GDN_HEADS + h) * GDN_DK:(GDN_HEADS + h + 1) * GDN_DK]
        va = a[:, (2 * GDN_HEADS + h) * GDN_DK:(2 * GDN_HEADS + h + 1) * GDN_DK]
        qa = qa * lax.rsqrt(jnp.sum(qa * qa, -1, keepdims=True) + NORM_EPS) * (GDN_DK ** -0.5)
        ka = ka * lax.rsqrt(jnp.sum(ka * ka, -1, keepdims=True) + NORM_EPS)
        for cc in range(cp):
            rows = slice(cc * lc, (cc + 1) * lc)
            qs.append(qa[rows])
            ks.append(ka[rows])
            vs.append(va[rows])
            betas.append(beta_all[rows, h:h + 1])
            gcols.append(G[rows, GDN_HEADS + h:GDN_HEADS + h + 1])
            grows.append(GT[GDN_HEADS + h:GDN_HEADS + h + 1, cc * lc:(cc + 1) * lc])
            glasts.append(GT[GDN_HEADS + h:GDN_HEADS + h + 1, (cc + 1) * lc - 1:(cc + 1) * lc])
    q3, k3, v3 = jnp.stack(qs), jnp.stack(ks), jnp.stack(vs)
    beta3, gcol3 = jnp.stack(betas), jnp.stack(gcols)
    grow3, glast3 = jnp.stack(grows), jnp.stack(glasts)
    shape3 = (len(pairs), lc, lc)
    ri = lax.broadcasted_iota(jnp.int32, shape3, 1)
    ci = lax.broadcasted_iota(jnp.int32, shape3, 2)
    incl = ri >= ci
    dec3 = jnp.where(incl, jnp.exp(jnp.where(incl, gcol3 - grow3, 0.0)), 0.0)
    eg3 = jnp.exp(gcol3)
    kb3 = k3 * beta3
    kbf3 = k3.astype(BF16)
    nmat3 = jnp.where(ri > ci, _bdot_nt(kb3.astype(BF16), kbf3) * dec3, 0.0)
    inv3 = _unit_lower_inverse(nmat3)
    sol3 = _bdot_split(inv3, jnp.concatenate([v3 * beta3, kb3 * eg3], axis=-1))
    w3 = sol3[:, :, GDN_DV:].astype(BF16)
    qd3 = (q3 * eg3).astype(BF16)
    kd3 = (k3 * jnp.exp(glast3 - gcol3)).astype(BF16)
    attn3 = (_bdot_nt(q3.astype(BF16), kbf3) * dec3).astype(BF16)
    for i, (h, cc) in enumerate(pairs):
        rows = slice(cc * lc, (cc + 1) * lc)
        cols = slice(h * GDN_DV, (h + 1) * GDN_DV)
        u0_ref[0, rows, cols] = sol3[i, :, :GDN_DV]
        w_ref[0, rows, cols] = w3[i]
        qd_ref[0, rows, cols] = qd3[i]
        kd_ref[0, rows, cols] = kd3[i]
        attn_ref[0, rows, h * lc:(h + 1) * lc] = attn3[i]


def _gdn_seq_kernel(u0_ref, w_ref, qd_ref, kd_ref, attn_ref, g_ref, z_ref, s0_ref, nw_ref,
                    y_ref, sout_ref, S, *, lc, nbb):
    c = pl.program_id(1)

    @pl.when(c == 0)
    def _():
        S[...] = s0_ref[...]

    nw = nw_ref[...]
    pairs = [(bb, h) for bb in range(nbb) for h in range(GDN_HEADS)]
    hcols = lambda h: slice(h * GDN_DV, (h + 1) * GDN_DV)
    stack = lambda f: jnp.stack([f(bb, h) for bb, h in pairs])
    dlast = jnp.exp(g_ref[:, lc - 1:lc, :])
    S3 = S[...].reshape(len(pairs), GDN_DK, GDN_DV)
    wq3 = stack(lambda bb, h: jnp.concatenate([w_ref[bb, :, hcols(h)], qd_ref[bb, :, hcols(h)]], axis=0))
    r3 = _bdot(wq3, S3.astype(BF16))
    ub3 = (stack(lambda bb, h: u0_ref[bb, :, hcols(h)]) - r3[:, :lc]).astype(BF16)
    o3 = r3[:, lc:] + _bdot(stack(lambda bb, h: attn_ref[bb, :, h * lc:(h + 1) * lc]), ub3)
    d3 = stack(lambda bb, h: dlast[bb, :, GDN_HEADS + h:GDN_HEADS + h + 1])
    kd3 = stack(lambda bb, h: kd_ref[bb, :, hcols(h)])
    kdu3 = lax.dot_general(kd3, ub3, (((1,), (1,)), ((0,), (0,))), preferred_element_type=F32)
    S[...] = (d3 * S3 + kdu3).reshape(S.shape)
    o3 = o3 * lax.rsqrt(jnp.mean(o3 * o3, -1, keepdims=True) + NORM_EPS) * nw
    for i, (bb, h) in enumerate(pairs):
        zh = z_ref[bb, :, hcols(h)]
        y_ref[bb, :, hcols(h)] = o3[i] * (zh * _sigmoid(zh))

    @pl.when(c == pl.num_programs(1) - 1)
    def _():
        sout_ref[...] = S[...]


def _gdn(qkv3, z3, ba3, ctx8, s0, cw, p1, p2, nw, lc, cp, nbb):
    nb, L, _ = qkv3.shape
    rb = lc * cp
    hd = GDN_HEADS * GDN_DV
    ltri = jnp.asarray(np.kron(np.eye(cp, dtype=np.float32), np.tril(np.ones((lc, lc), np.float32)))).astype(BF16)
    blk = lambda n: pl.BlockSpec((1, rb, n), lambda b, c: (b, c, 0))
    per_b = lambda shape: pl.BlockSpec((1,) + shape, lambda b, c: (b,) + (0,) * len(shape))
    cst = lambda shape: pl.BlockSpec(shape, lambda b, c: (0,) * len(shape))
    sds = lambda n, dt: jax.ShapeDtypeStruct((nb, L, n), dt)
    u0, w, qd, kd, attn, G, cout = pl.pallas_call(
        functools.partial(_gdn_local_kernel, lc=lc, cp=cp),
        grid=(nb, L // rb),
        in_specs=[blk(GDN_QKV), blk(LANES), per_b((SUBLANES, GDN_QKV)), cst(cw.shape), cst(p1.shape),
                  cst(p2.shape), cst(ltri.shape)],
        out_specs=[blk(hd), blk(hd), blk(hd), blk(hd), blk(GDN_HEADS * lc), blk(LANES),
                   per_b((SUBLANES, GDN_QKV))],
        out_shape=[sds(hd, F32), sds(hd, BF16), sds(hd, BF16), sds(hd, BF16), sds(GDN_HEADS * lc, BF16),
                   sds(LANES, F32), jax.ShapeDtypeStruct((nb, SUBLANES, GDN_QKV), F32)],
        scratch_shapes=[pltpu.VMEM((rb + SUBLANES, GDN_QKV), F32)],
        compiler_params=_params(("parallel", "arbitrary")),
        name="gdn_local",
    )(qkv3, ba3, ctx8, cw, p1, p2, ltri)
    sblk = lambda n: pl.BlockSpec((nbb, lc, n), lambda b, c: (b, c, 0))
    state = pl.BlockSpec((nbb, GDN_HEADS, GDN_DK, GDN_DV), lambda b, c: (b, 0, 0, 0))
    y, s_new = pl.pallas_call(
        functools.partial(_gdn_seq_kernel, lc=lc, nbb=nbb),
        grid=(nb // nbb, L // lc),
        in_specs=[sblk(hd), sblk(hd), sblk(hd), sblk(hd), sblk(GDN_HEADS * lc), sblk(LANES), sblk(hd),
                  state, cst(nw.shape)],
        out_specs=[sblk(hd), state],
        out_shape=[sds(hd, F32), jax.ShapeDtypeStruct(s0.shape, F32)],
        scratch_shapes=[pltpu.VMEM((nbb, GDN_HEADS, GDN_DK, GDN_DV), F32)],
        compiler_params=_params(("parallel", "arbitrary")),
        name="gdn_seq",
    )(u0, w, qd, kd, attn, G, z3, s0, nw)
    return y, s_new, cout


def _ret_kernel(q_ref, k_ref, v_ref, g_ref, r0_ref, dec_ref, qs_ref, ks_ref, cd_ref, o_ref, rout_ref, R,
                *, nbb):
    c = pl.program_id(1)

    @pl.when(c == 0)
    def _():
        R[...] = r0_ref[...]

    pairs = [(bb, h) for bb in range(nbb) for h in range(RET_HEADS)]
    stack = lambda f: jnp.stack([f(bb, h) for bb, h in pairs])
    kcols = lambda h: slice(h * RET_DK, (h + 1) * RET_DK)
    vcols = lambda h: slice(h * RET_DV, (h + 1) * RET_DV)
    q3 = stack(lambda bb, h: q_ref[bb, :, kcols(h)])
    k3 = stack(lambda bb, h: k_ref[bb, :, kcols(h)])
    v3 = stack(lambda bb, h: v_ref[bb, :, vcols(h)]).astype(BF16)
    dec3 = stack(lambda bb, h: dec_ref[h])
    qs3 = stack(lambda bb, h: qs_ref[h])
    ks3 = stack(lambda bb, h: ks_ref[h])
    cd3 = stack(lambda bb, h: cd_ref[h])
    R3 = R[...].reshape(len(pairs), RET_DK, RET_DV)
    s3 = _bdot_nt(q3.astype(BF16), k3.astype(BF16)) * dec3
    o3 = _bdot(s3.astype(BF16), v3) + _bdot((q3 * qs3).astype(BF16), R3.astype(BF16))
    kv3 = lax.dot_general((k3 * ks3).astype(BF16), v3, (((1,), (1,)), ((0,), (0,))), preferred_element_type=F32)
    R[...] = (cd3 * R3 + kv3).reshape(R.shape)
    mu = jnp.mean(o3, -1, keepdims=True)
    d3 = o3 - mu
    var = jnp.mean(d3 * d3, -1, keepdims=True)
    on3 = d3 * lax.rsqrt(var + LN_EPS)
    for i, (bb, h) in enumerate(pairs):
        gt = g_ref[bb, :, vcols(h)]
        o_ref[bb, :, vcols(h)] = gt * _sigmoid(gt) * on3[i]

    @pl.when(c == pl.num_programs(1) - 1)
    def _():
        rout_ref[...] = R[...]


def _retention(proj3, r0, lc, nbb):
    nb, L, _ = proj3.shape
    log_g = np.log(1.0 - 2.0 ** (-5.0 - np.arange(RET_HEADS, dtype=np.float64)))
    idx = np.arange(lc, dtype=np.float64)
    dec = np.exp(log_g[:, None, None] * np.abs(idx[:, None] - idx[None, :])).astype(np.float32)
    qs = np.exp(log_g[:, None] * (idx + 1.0)).astype(np.float32)[..., None]
    ks = np.exp(log_g[:, None] * (lc - 1.0 - idx)).astype(np.float32)[..., None]
    cdec = np.exp(log_g * lc).astype(np.float32)[:, None, None]
    nqk = RET_HEADS * RET_DK
    nv = RET_HEADS * RET_DV
    cst = lambda shape: pl.BlockSpec(shape, lambda b, c: (0,) * len(shape))
    state = pl.BlockSpec((nbb, RET_HEADS, RET_DK, RET_DV), lambda b, c: (b, 0, 0, 0))
    return pl.pallas_call(
        functools.partial(_ret_kernel, nbb=nbb),
        grid=(nb // nbb, L // lc),
        in_specs=[pl.BlockSpec((nbb, lc, nqk), lambda b, c: (b, c, 0)),
                  pl.BlockSpec((nbb, lc, nqk), lambda b, c: (b, c, 1)),
                  pl.BlockSpec((nbb, lc, nv), lambda b, c: (b, c, 1)),
                  pl.BlockSpec((nbb, lc, nv), lambda b, c: (b, c, 2)),
                  state, cst(dec.shape), cst(qs.shape), cst(ks.shape), cst(cdec.shape)],
        out_specs=[pl.BlockSpec((nbb, lc, nv), lambda b, c: (b, c, 0)), state],
        out_shape=[jax.ShapeDtypeStruct((nb, L, nv), F32), jax.ShapeDtypeStruct(r0.shape, F32)],
        scratch_shapes=[pltpu.VMEM((nbb, RET_HEADS, RET_DK, RET_DV), F32)],
        compiler_params=_params(("parallel", "arbitrary")),
        name="retention",
    )(proj3, proj3, proj3, proj3, r0, jnp.asarray(dec), jnp.asarray(qs), jnp.asarray(ks), jnp.asarray(cdec))


RUN_ALIGN = SUBLANES
RUN_PIECE = 64
MOE_VMEM_LIMIT = 58 * 1024 * 1024


def _moe_params(sem):
    return pltpu.CompilerParams(dimension_semantics=sem, vmem_limit_bytes=MOE_VMEM_LIMIT)


def _tile_cap(tm):
    rows = TOP_K * tm + N_EXPERTS * (RUN_ALIGN - 1)
    return -(-rows // LANES) * LANES


def _route_kernel(x_ref, rw_ref, rb_ref, lst_ref, ust_ref, pos_ref, gt_ref, cnt_ref, xs_ref):
    tm = x_ref.shape[0]
    cap = xs_ref.shape[1]
    x = x_ref[...]
    lane = lax.broadcasted_iota(jnp.int32, (tm, LANES), 1)
    lane_f = lane.astype(F32)
    xh, xl = _split_bf16(x)
    wh, wl = _split_bf16(rw_ref[...])
    logits = _dot(xh, wh) + _dot(xh, wl) + _dot(xl, wh) + rb_ref[...]
    logits = jnp.where(lane < N_EXPERTS, logits, -jnp.inf)
    vals, hots = [], []
    for _ in range(TOP_K):
        m = jnp.max(logits, -1, keepdims=True)
        first = jnp.min(jnp.where(logits == m, lane_f, float(LANES)), -1, keepdims=True)
        hot = lane_f == first
        vals.append(m)
        hots.append(hot)
        logits = jnp.where(hot, -jnp.inf, logits)
    es = [jnp.exp(v - vals[0]) for v in vals]
    den = es[0] + es[1] + es[2] + es[3]
    multi = jnp.zeros((tm, LANES), F32)
    for hot in hots:
        multi = multi + hot.astype(F32)
    counts = jnp.sum(multi, 0, keepdims=True)
    units = jnp.floor((counts + (RUN_ALIGN - 1)) * (1.0 / RUN_ALIGN))
    offs = _dot(jnp.broadcast_to(units, (SUBLANES, LANES)).astype(BF16), ust_ref[...])[0:1] * float(RUN_ALIGN)
    before = _dot(lst_ref[...], multi.astype(BF16))
    slot = offs + before
    pos = jnp.zeros((tm, LANES), F32)
    gt = jnp.zeros((tm, LANES), F32)
    for kk in range(TOP_K):
        pos = jnp.where(lane == kk, jnp.sum(jnp.where(hots[kk], slot, 0.0), -1, keepdims=True), pos)
        gt = jnp.where(lane == kk, es[kk] / den, gt)
    pos = pos.astype(jnp.int32)
    pos_ref[...] = pos
    gt_ref[...] = gt
    cnt_ref[0] = counts
    pos_t = pos.T
    row = lax.broadcasted_iota(jnp.int32, (cap, tm), 0)
    sel = jnp.zeros((cap, tm), F32)
    for kk in range(TOP_K):
        sel = sel + jnp.where(row == pos_t[kk:kk + 1, :], 1.0, 0.0)
    xs_ref[0] = _dot(sel.astype(BF16), xh)


def _route(x, rw, rb, tm):
    t = x.shape[0]
    nt = t // tm
    cap = _tile_cap(tm)
    lst = jnp.asarray(np.tril(np.ones((tm, tm), np.float32), -1)).astype(BF16)
    ust = jnp.asarray(np.triu(np.ones((LANES, LANES), np.float32), 1)).astype(BF16)
    row = lambda n: pl.BlockSpec((tm, n), lambda i: (i, 0))
    return pl.pallas_call(
        _route_kernel,
        grid=(nt,),
        in_specs=[row(D_MODEL), _full(rw.shape), _full(rb.shape), _full(lst.shape), _full(ust.shape)],
        out_specs=[row(LANES), row(LANES), pl.BlockSpec((1, 1, LANES), lambda i: (i, 0, 0)),
                   pl.BlockSpec((1, cap, D_MODEL), lambda i: (i, 0, 0))],
        out_shape=[jax.ShapeDtypeStruct((t, LANES), jnp.int32), jax.ShapeDtypeStruct((t, LANES), F32),
                   jax.ShapeDtypeStruct((nt, 1, LANES), F32), jax.ShapeDtypeStruct((nt, cap, D_MODEL), F32)],
        compiler_params=_moe_params(("parallel",)),
        name="moe_route",
    )(x, rw, rb, lst, ust)


def _expert_kernel(be_ref, nu_ref, ilo_ref, rows_ref, gs_ref, n8_ref, lo_ref,
                   xs_hbm, w1_ref, b1_ref, w2_ref, b2_ref, ys_hbm,
                   xbuf, ybuf, w1b, w2b, in_sem, out_sem, *, blk, nt):
    j = pl.program_id(0)
    nu = nu_ref[0]

    def for_each_run(jb, fn):
        e = be_ref[jb]
        base = jb * blk

        def cond(i):
            return (i < nt) & (gs_ref[e * nt + jnp.minimum(i, nt - 1)] < base + blk)

        def body(i):
            g0 = gs_ref[e * nt + i]
            first = jnp.maximum(g0, base)
            last = jnp.minimum(g0 + n8_ref[e * nt + i], base + blk)
            fn(i, lo_ref[e * nt + i] + (first - g0), first - base, last - first)
            return i + 1

        lax.while_loop(cond, body, ilo_ref[jb])

    def pieces(length, fn):
        nbig = length // RUN_PIECE

        def big(q, carry):
            fn(q * RUN_PIECE, RUN_PIECE)
            return carry

        lax.fori_loop(0, nbig, big, 0)
        size = RUN_PIECE // 2
        while size >= RUN_ALIGN:
            @pl.when((length & size) != 0)
            def _(size=size):
                fn(nbig * RUN_PIECE + (length & (RUN_PIECE - 2 * size)), size)
            size //= 2

    def aligned(v, size):
        return pl.ds(pl.multiple_of(v, RUN_ALIGN), size)

    def copy_in(jb, slot):
        def run(i, src, dst, length):
            def piece(off, size):
                pltpu.make_async_copy(xs_hbm.at[i, aligned(src + off, size)],
                                      xbuf.at[slot, aligned(dst + off, size)], in_sem.at[slot]).start()
            pieces(length, piece)
        for_each_run(jb, run)

    def copy_out(jb, slot):
        def run(i, src, dst, length):
            def piece(off, size):
                pltpu.make_async_copy(ybuf.at[slot, aligned(dst + off, size)],
                                      ys_hbm.at[i, aligned(src + off, size)], out_sem.at[slot]).start()
            pieces(length, piece)
        for_each_run(jb, run)

    def wait_rows(sem, nrows):
        def piece(off, size):
            del off
            pltpu.make_async_copy(xs_hbm.at[0, pl.ds(0, size)], xbuf.at[0, pl.ds(0, size)], sem).wait()
        pieces(nrows, piece)

    @pl.when(j < nu)
    def _():
        slot = j % 2

        @pl.when(j == 0)
        def _():
            xbuf[...] = jnp.zeros_like(xbuf)
            copy_in(0, 0)

        wait_rows(in_sem.at[slot], rows_ref[j])

        @pl.when(j + 1 < nu)
        def _():
            copy_in(j + 1, 1 - slot)

        @pl.when(j >= 2)
        def _():
            wait_rows(out_sem.at[slot], rows_ref[jnp.maximum(j - 2, 0)])

        @pl.when((j == 0) | (be_ref[j] != be_ref[jnp.maximum(j - 1, 0)]))
        def _():
            w1b[...] = w1_ref[0, 0].astype(BF16)
            w2b[...] = w2_ref[0, 0].astype(BF16)

        h = _dot(xbuf[slot].astype(BF16), w1b[...]) + b1_ref[0]
        glu = jnp.minimum(h[:, :D_FF], SWIGLU_LIMIT)
        lin = jnp.clip(h[:, D_FF:], -SWIGLU_LIMIT, SWIGLU_LIMIT)
        act = glu * _sigmoid(SWIGLU_ALPHA * glu) * (lin + 1.0)
        ybuf[slot] = _dot(act.astype(BF16), w2b[...]) + b2_ref[0]
        copy_out(j, slot)

        @pl.when(j == nu - 1)
        def _():
            wait_rows(out_sem.at[slot], rows_ref[j])

            @pl.when(j >= 1)
            def _():
                wait_rows(out_sem.at[1 - slot], rows_ref[jnp.maximum(j - 1, 0)])


def _experts(tables, xs, w1, b1, w2, b2, layer, blk):
    nt = xs.shape[0]
    n_blk = tables[0].shape[0]
    wspec = lambda shape: pl.BlockSpec((1, 1) + shape, lambda j, be, *_: (layer, be[j], 0, 0))
    bspec = lambda n: pl.BlockSpec((1, 1, n), lambda j, be, *_: (be[j], 0, 0))
    grid_spec = pltpu.PrefetchScalarGridSpec(
        num_scalar_prefetch=len(tables),
        grid=(n_blk,),
        in_specs=[pl.BlockSpec(memory_space=pl.ANY), wspec((D_MODEL, 2 * D_FF)), bspec(2 * D_FF),
                  wspec((D_FF, D_MODEL)), bspec(D_MODEL)],
        out_specs=pl.BlockSpec(memory_space=pl.ANY),
        scratch_shapes=[pltpu.VMEM((2, blk, D_MODEL), F32), pltpu.VMEM((2, blk, D_MODEL), F32),
                        pltpu.VMEM((D_MODEL, 2 * D_FF), BF16), pltpu.VMEM((D_FF, D_MODEL), BF16),
                        pltpu.SemaphoreType.DMA((2,)), pltpu.SemaphoreType.DMA((2,))],
    )
    return pl.pallas_call(
        functools.partial(_expert_kernel, blk=blk, nt=nt),
        grid_spec=grid_spec,
        out_shape=jax.ShapeDtypeStruct(xs.shape, F32),
        input_output_aliases={len(tables): 0},
        compiler_params=_moe_params(("arbitrary",)),
        name="moe_experts",
    )(*tables, xs, w1, b1, w2, b2)


def _combine_kernel(ys_ref, pos_ref, gt_ref, x_ref, p_ref, g_ref, b_ref, plew_ref, gatew_ref, o_ref):
    tm = x_ref.shape[0]
    cap = ys_ref.shape[1]
    pos = pos_ref[...]
    gt = gt_ref[...]
    col = lax.broadcasted_iota(jnp.int32, (tm, cap), 1)
    sel = jnp.zeros((tm, cap), F32)
    for kk in range(TOP_K):
        sel = sel + jnp.where(col == pos[:, kk:kk + 1], gt[:, kk:kk + 1], 0.0)
    y = _dot(sel.astype(BF16), ys_ref[0].astype(BF16))
    x2 = _layer_norm(DEEPNORM_ALPHA * x_ref[...] + y, g_ref[...], b_ref[...])
    pp = _dot(p_ref[...].astype(BF16), plew_ref[...])
    gg = _sigmoid(_dot(x2.astype(BF16), gatew_ref[...]))
    o_ref[...] = x2 + pp * gg


def _combine(ys, pos, gt, x, p, g, b, plew, gatew, tm):
    t = x.shape[0]
    cap = ys.shape[1]
    row = lambda n: pl.BlockSpec((tm, n), lambda i: (i, 0))
    return pl.pallas_call(
        _combine_kernel,
        grid=(t // tm,),
        in_specs=[pl.BlockSpec((1, cap, D_MODEL), lambda i: (i, 0, 0)), row(LANES), row(LANES),
                  row(D_MODEL), row(PLE_DIM), _full((1, D_MODEL)), _full((1, D_MODEL)),
                  _full(plew.shape), _full(gatew.shape)],
        out_specs=row(D_MODEL),
        out_shape=jax.ShapeDtypeStruct((t, D_MODEL), F32),
        compiler_params=_moe_params(("parallel",)),
        name="moe_combine",
    )(ys, pos, gt, x, p, g, b, plew, gatew)


def _moe_tables(cnt, t, tm, blk):
    nt = t // tm
    n = cnt[:, 0, :N_EXPERTS].astype(jnp.int32)
    n8 = (n + RUN_ALIGN - 1) // RUN_ALIGN * RUN_ALIGN
    lo = jnp.cumsum(n8, axis=1) - n8
    rows_e = jnp.sum(n8, axis=0)
    padded = (rows_e + blk - 1) // blk * blk
    pend = jnp.cumsum(padded)
    pstart = pend - padded
    gstart = pstart[None, :] + jnp.cumsum(n8, axis=0) - n8
    n_blk = -(-(TOP_K * t + nt * N_EXPERTS * (RUN_ALIGN - 1) + N_EXPERTS * (blk - 1)) // blk)
    n_used = (pend[-1] // blk).astype(jnp.int32).reshape(1)
    blk_start = jnp.arange(n_blk, dtype=jnp.int32) * blk
    blk_e = jnp.minimum(jnp.sum((pend[None, :] <= blk_start[:, None]).astype(jnp.int32), axis=1), N_EXPERTS - 1)
    run_end = (gstart + n8)[:, blk_e]
    ilo = jnp.sum((run_end <= blk_start[None, :]).astype(jnp.int32), axis=0)
    rows_b = jnp.clip((pstart + rows_e)[blk_e] - blk_start, 0, blk)
    flat = lambda a: a.T.reshape(-1).astype(jnp.int32)
    i32 = lambda a: a.astype(jnp.int32)
    return (i32(blk_e), n_used, i32(ilo), i32(rows_b), flat(gstart), flat(n8), flat(lo))


def _moe_ple(x, p, rw, rb, w1, b1, w2, b2, layer, g, b, plew, gatew, tm, blk):
    t = x.shape[0]
    pos, gt, cnt, xs = _route(x, rw, rb, tm)
    ys = _experts(_moe_tables(cnt, t, tm, blk), xs, w1, b1, w2, b2, layer, blk)
    return _combine(ys, pos, gt, x, p, g, b, plew, gatew, tm)


_ROT_PERM = np.concatenate([np.arange(0, RET_DK, 2), np.arange(1, RET_DK, 2)])
_ROT_INV = np.argsort(_ROT_PERM)


def _lane_row(vals, offset):
    row = jnp.zeros((1, LANES), F32)
    return row.at[0, offset:offset + vals.shape[0]].set(vals)


def _run_group(x3, p, s5_re, s5_im, gdn_s, conv_s, ret_s, pos0, W, tm, blk):
    nb, L, _ = x3.shape
    t = nb * L
    lc = L if L <= CHUNK else CHUNK
    x = x3.reshape(t, D_MODEL).astype(F32)

    u, qkv, z, ba = _proj_even(x, W['wu'], W['wqkv'], W['wz'], W['wba'], tm)
    yA3, h_new = _s5(u.reshape(nb, L, S5_WIDTH), _s5_state_in(s5_re[0].astype(F32), s5_im[0].astype(F32)),
                     W['bmat'], W['cmat'], W['acoef'], W['dskip'], W['wglu'], W['bglu'], lc)
    new_re, new_im = _s5_state_out(h_new)
    ctx8 = jnp.concatenate([jnp.zeros((nb, SUBLANES - (GDN_CONV - 1), GDN_QKV), F32), conv_s[0].astype(F32)], axis=1)
    yB3, new_gdn, cout = _gdn(qkv.reshape(nb, L, GDN_QKV), z.reshape(nb, L, -1), ba.reshape(nb, L, LANES),
                              ctx8, gdn_s[0].astype(F32), W['convw'], W['p1'], W['p2'], W['normw'], lc,
                              cp=min(4, L // lc), nbb=4)
    new_conv = cout[:, SUBLANES - (GDN_CONV - 1):, :]
    x = _outproj_ln([yA3.reshape(t, -1), yB3.reshape(t, -1)], [W['wout_a'], W['wout_b']], x,
                    W['ln1_g'][0], W['ln1_b'][0], tm)
    x = _moe_ple(x, p[0].reshape(t, PLE_DIM), W['rw'][0], W['rb'][0], W['w1'], W['b1'][0], W['w2'],
                 W['b2'][0], 0, W['ln2_g'][0], W['ln2_b'][0], W['plew'][0], W['gatew'][0], tm, blk)

    pos = pos0 + jnp.arange(L, dtype=F32)
    freq = 1.0 / (ROPE_BASE ** jnp.linspace(0.0, 1.0, RET_DK // 2, dtype=F32))
    ang = pos[:, None] * freq[None]
    cos = jnp.broadcast_to(jnp.cos(ang)[None], (nb, L, RET_DK // 2)).reshape(t, RET_DK // 2)
    sin = jnp.broadcast_to(jnp.sin(ang)[None], (nb, L, RET_DK // 2)).reshape(t, RET_DK // 2)
    proj = _proj_odd(x, W['win_odd'], cos, sin, tm)
    r0 = ret_s[0].astype(F32)[:, :, _ROT_PERM, :]
    o3, r_new = _retention(proj.reshape(nb, L, -1), r0, lc, nbb=2)
    new_ret = r_new[:, :, _ROT_INV, :]
    x = _outproj_ln([o3.reshape(t, -1)], [W['wout_odd']], x, W['ln1_g'][1], W['ln1_b'][1], tm)
    x = _moe_ple(x, p[1].reshape(t, PLE_DIM), W['rw'][1], W['rb'][1], W['w1'], W['b1'][1], W['w2'],
                 W['b2'][1], 1, W['ln2_g'][1], W['ln2_b'][1], W['plew'][1], W['gatew'][1], tm, blk)
    return (x.reshape(nb, L, D_MODEL), new_re[None], new_im[None], new_gdn[None], new_conv[None], new_ret[None])


def kernel(x_prompt, x_sample, state_s5_re, state_s5_im, state_gdn, state_gdn_conv, state_ret, p_prompt, p_sample, w_in_even, s5_a_re, s5_a_im, s5_log_dt, s5_b_re, s5_b_im, s5_c_re, s5_c_im, s5_d, s5_w_glu, s5_b_glu, gdn_conv_w, gdn_a_log, gdn_dt_bias, gdn_norm_w, w_out_even, w_in_odd, w_out_odd, ln1_g, ln1_b, ln2_g, ln2_b, router_w, router_b, moe_w1, moe_b1, moe_w2, moe_b2, ple_w, ple_gate_w):
    o1 = S5_WIDTH
    o2 = o1 + GDN_QKV
    o3 = o2 + GDN_HEADS * GDN_DV
    win = w_in_even[0]
    bmat, cmat, acoef = _s5_weights(s5_a_re[0], s5_a_im[0], s5_log_dt[0], s5_b_re[0], s5_b_im[0],
                                    s5_c_re[0], s5_c_im[0])
    wodd = w_in_odd[0]
    nk = RET_HEADS * RET_DK
    perm_cols = lambda w: w.reshape(D_MODEL, RET_HEADS, RET_DK)[:, :, _ROT_PERM].reshape(D_MODEL, nk)
    W = dict(
        wu=win[:, :o1].astype(BF16), wqkv=win[:, o1:o2].astype(BF16), wz=win[:, o2:o3].astype(BF16),
        wba=jnp.pad(win[:, o3:], ((0, 0), (0, LANES - 2 * GDN_HEADS))).astype(BF16),
        bmat=bmat, cmat=cmat, acoef=acoef, dskip=s5_d[0][None], wglu=s5_w_glu[0].astype(BF16),
        bglu=s5_b_glu[0][None], convw=gdn_conv_w[0],
        p1=_lane_row(-jnp.exp(gdn_a_log[0]), GDN_HEADS), p2=_lane_row(gdn_dt_bias[0], GDN_HEADS),
        normw=gdn_norm_w[0][None],
        wout_a=w_out_even[0][:S5_WIDTH].astype(BF16), wout_b=w_out_even[0][S5_WIDTH:].astype(BF16),
        win_odd=jnp.concatenate([perm_cols(wodd[:, :nk]), perm_cols(wodd[:, nk:2 * nk]), wodd[:, 2 * nk:]],
                                axis=1).astype(BF16),
        wout_odd=w_out_odd[0].astype(BF16),
        ln1_g=ln1_g[:, None], ln1_b=ln1_b[:, None], ln2_g=ln2_g[:, None], ln2_b=ln2_b[:, None],
        rw=jnp.pad(router_w, ((0, 0), (0, 0), (0, LANES - N_EXPERTS))),
        rb=jnp.pad(router_b, ((0, 0), (0, LANES - N_EXPERTS)))[:, None],
        w1=moe_w1, b1=moe_b1[:, :, None], w2=moe_w2, b2=moe_b2[:, :, None],
        plew=ple_w.astype(BF16), gatew=ple_gate_w.astype(BF16),
    )
    bp = x_prompt.shape[0]
    zeros = lambda *s: jnp.zeros(s, F32)
    outs_p = _run_group(x_prompt, p_prompt, zeros(1, bp, S5_GROUPS, S5_STATE), zeros(1, bp, S5_GROUPS, S5_STATE),
                        zeros(1, bp, GDN_HEADS, GDN_DK, GDN_DV), zeros(1, bp, GDN_CONV - 1, GDN_QKV),
                        zeros(1, bp, RET_HEADS, RET_DK, RET_DV), 0.0, W, tm=512, blk=512)
    outs_s = _run_group(x_sample, p_sample, state_s5_re, state_s5_im, state_gdn, state_gdn_conv, state_ret,
                        float(PAST_LEN), W, tm=128, blk=128)
    dp = x_prompt.dtype
    y_p, p_re, p_im, p_gdn, p_conv, p_ret = outs_p
    y_s, s_re, s_im, s_gdn, s_conv, s_ret = outs_s
    return (y_p.astype(dp), y_s.astype(x_sample.dtype),
            p_re.astype(dp), p_im.astype(dp), p_gdn.astype(dp), p_conv.astype(dp), p_ret.astype(dp),
            s_re.astype(state_s5_re.dtype), s_im.astype(state_s5_im.dtype), s_gdn.astype(state_gdn.dtype),
            s_conv.astype(state_gdn_conv.dtype), s_ret.astype(state_ret.dtype))
```

```python
import functools
import math

import jax
import jax.numpy as jnp
import numpy as np
from jax import lax
from jax.experimental import pallas as pl
from jax.experimental.pallas import tpu as pltpu

F32 = jnp.float32
BF16 = jnp.bfloat16
HIGHEST = lax.Precision.HIGHEST

D_MODEL = 1024
CHUNK = 64
S5_WIDTH = 512
S5_GROUP = 16
S5_GROUPS = 32
S5_STATE = 64
S5_TILES = 16
GDN_HEADS = 4
GDN_DK = 128
GDN_DV = 128
GDN_CONV = 4
GDN_QKV = 1536
RET_HEADS = 4
RET_DK = 256
RET_DV = 512
ROPE_BASE = 10000.0
N_EXPERTS = 32
TOP_K = 4
D_FF = 1024
SWIGLU_LIMIT = 7.0
SWIGLU_ALPHA = 1.702
PLE_DIM = 256
DEPTH = 2
PAST_LEN = 1024
DEEPNORM_ALPHA = (2 * DEPTH) ** 0.25
LN_EPS = 1e-5
NORM_EPS = 1e-6

LANES = 128
SUBLANES = 8
VMEM_LIMIT = 48 * 1024 * 1024

def _params(sem):
    return pltpu.CompilerParams(dimension_semantics=sem, vmem_limit_bytes=VMEM_LIMIT)


def _dot(a, b):
    return jnp.dot(a, b, preferred_element_type=F32)


def _dot_hi(a, b):
    return jnp.dot(a, b, preferred_element_type=F32, precision=HIGHEST)


def _dot_nt(a, b, precision=None):
    return lax.dot_general(a, b, (((1,), (1,)), ((), ())), preferred_element_type=F32,
                           precision=precision)


def _dot_tn(a, b):
    return lax.dot_general(a, b, (((0,), (0,)), ((), ())), preferred_element_type=F32)


def _sigmoid(x):
    return 1.0 / (1.0 + jnp.exp(-x))


def _full(shape):
    nd = len(shape)
    return pl.BlockSpec(shape, lambda *_: (0,) * nd)


def _proj_even_kernel(x_ref, wu_ref, wqkv_ref, wz_ref, wba_ref, u_ref, qkv_ref, z_ref, ba_ref):
    xb = x_ref[...].astype(BF16)
    u_ref[...] = _dot(xb, wu_ref[...])
    qkv_ref[...] = _dot(xb, wqkv_ref[...])
    z_ref[...] = _dot(xb, wz_ref[...])
    ba_ref[...] = _dot(xb, wba_ref[...])


def _proj_even(x, wu, wqkv, wz, wba, tm):
    t = x.shape[0]
    row = lambda n: pl.BlockSpec((tm, n), lambda i: (i, 0))
    return pl.pallas_call(
        _proj_even_kernel,
        grid=(t // tm,),
        in_specs=[row(D_MODEL), _full(wu.shape), _full(wqkv.shape), _full(wz.shape), _full(wba.shape)],
        out_specs=[row(S5_WIDTH), row(GDN_QKV), row(GDN_HEADS * GDN_DV), row(LANES)],
        out_shape=[jax.ShapeDtypeStruct((t, S5_WIDTH), F32), jax.ShapeDtypeStruct((t, GDN_QKV), F32),
                   jax.ShapeDtypeStruct((t, GDN_HEADS * GDN_DV), F32), jax.ShapeDtypeStruct((t, LANES), F32)],
        compiler_params=_params(("parallel",)),
        name="proj_even",
    )(x, wu, wqkv, wz, wba)


def _proj_odd_kernel(x_ref, w_ref, cos_ref, sin_ref, o_ref):
    j = pl.program_id(0)
    acc = _dot(x_ref[...].astype(BF16), w_ref[...])

    @pl.when(j == 0)
    def _():
        cos, sin = cos_ref[...], sin_ref[...]
        half = RET_DK // 2
        for h in range(2 * RET_HEADS):
            x0 = acc[:, h * RET_DK:h * RET_DK + half]
            x1 = acc[:, h * RET_DK + half:(h + 1) * RET_DK]
            scale = 1.0 if h < RET_HEADS else RET_DK ** -0.5
            o_ref[:, h * RET_DK:h * RET_DK + half] = ((x0 * cos - x1 * sin) * scale).astype(o_ref.dtype)
            o_ref[:, h * RET_DK + half:(h + 1) * RET_DK] = ((x0 * sin + x1 * cos) * scale).astype(o_ref.dtype)

    @pl.when(j != 0)
    def _():
        o_ref[...] = acc.astype(o_ref.dtype)


def _proj_odd(x, w, cos, sin, tm):
    t = x.shape[0]
    nblk = w.shape[1] // 2048
    return pl.pallas_call(
        _proj_odd_kernel,
        grid=(nblk, t // tm),
        in_specs=[pl.BlockSpec((tm, D_MODEL), lambda j, i: (i, 0)),
                  pl.BlockSpec((D_MODEL, 2048), lambda j, i: (0, j)),
                  pl.BlockSpec((tm, LANES), lambda j, i: (i, 0)),
                  pl.BlockSpec((tm, LANES), lambda j, i: (i, 0))],
        out_specs=pl.BlockSpec((tm, 2048), lambda j, i: (i, j)),
        out_shape=jax.ShapeDtypeStruct((t, w.shape[1]), BF16),
        compiler_params=_params(("parallel", "parallel")),
        name="proj_odd",
    )(x, w, cos, sin)


def _layer_norm(r, g, b):
    mu = jnp.mean(r, -1, keepdims=True)
    d = r - mu
    var = jnp.mean(d * d, -1, keepdims=True)
    return d * lax.rsqrt(var + LN_EPS) * g + b


def _outproj_ln_kernel(*refs, n_in):
    a_refs = refs[:n_in]
    w_refs = refs[n_in:2 * n_in]
    x_ref, g_ref, b_ref, o_ref = refs[2 * n_in:]
    acc = _dot(a_refs[0][...], w_refs[0][...])
    for a_ref, w_ref in zip(a_refs[1:], w_refs[1:]):
        acc = acc + _dot(a_ref[...], w_ref[...])
    o_ref[...] = _layer_norm(DEEPNORM_ALPHA * x_ref[...] + acc, g_ref[...], b_ref[...])


def _outproj_ln(acts, ws, x, g, b, tm):
    t = x.shape[0]
    row = lambda n: pl.BlockSpec((tm, n), lambda i: (i, 0))
    return pl.pallas_call(
        functools.partial(_outproj_ln_kernel, n_in=len(acts)),
        grid=(t // tm,),
        in_specs=[row(a.shape[1]) for a in acts] + [_full(w.shape) for w in ws]
                 + [row(D_MODEL), _full((1, D_MODEL)), _full((1, D_MODEL))],
        out_specs=row(D_MODEL),
        out_shape=jax.ShapeDtypeStruct((t, D_MODEL), F32),
        compiler_params=_params(("parallel",)),
        name="outproj_ln",
    )(*acts, *ws, x, g, b)


def _s5_kernel(u_ref, h0_ref, bmat_ref, cmat_ref, acoef_ref, dskip_ref, wglu_ref, bglu_ref,
               y_ref, hout_ref, upad, sre, sim, yacc, hst, *, nb, lt, pitch):
    tb = pl.program_id(0)

    @pl.when(tb == 0)
    def _():
        hst[...] = h0_ref[...]
        upad[...] = jnp.zeros_like(upad)

    for b in range(nb):
        upad[b * pitch:b * pitch + lt, :] = u_ref[b]
    yacc[...] = jnp.zeros_like(yacc)
    halves = nb // SUBLANES

    def tile_body(j, carry):
        q = j // 4
        ub = upad[:, pl.ds(pl.multiple_of(q * LANES, LANES), LANES)].astype(BF16)
        bu = _dot(ub, bmat_ref[j])
        sre[...] = bu[:, :LANES]
        sim[...] = bu[:, LANES:]
        ac = acoef_ref[j]
        ar, ai = ac[:, :LANES], ac[:, LANES:]
        hs = [(hst[j, hf * SUBLANES:(hf + 1) * SUBLANES, 0:LANES],
               hst[j, hf * SUBLANES:(hf + 1) * SUBLANES, LANES:2 * LANES]) for hf in range(halves)]
        for t in range(lt):
            for hf in range(halves):
                hr, hi = hs[hf]
                rows = pl.ds(hf * SUBLANES * pitch + t, SUBLANES, stride=pitch)
                nhr = ar * hr - ai * hi + sre[rows, :]
                nhi = ar * hi + ai * hr + sim[rows, :]
                sre[rows, :] = nhr
                sim[rows, :] = nhi
                hs[hf] = (nhr, nhi)
        for hf in range(halves):
            hst[j, hf * SUBLANES:(hf + 1) * SUBLANES, 0:LANES] = hs[hf][0]
            hst[j, hf * SUBLANES:(hf + 1) * SUBLANES, LANES:2 * LANES] = hs[hf][1]
        st = jnp.concatenate([sre[...], sim[...]], axis=-1).astype(BF16)
        yacc[q] += _dot(st, cmat_ref[j])
        return carry

    lax.fori_loop(0, S5_TILES, tile_body, 0)
    y = jnp.concatenate([yacc[0], yacc[1], yacc[2], yacc[3]], axis=-1) + dskip_ref[...] * upad[...]
    y = jax.nn.gelu(y)
    y = y * _sigmoid(_dot(y.astype(BF16), wglu_ref[...]) + bglu_ref[...])
    for b in range(nb):
        y_ref[b] = y[b * pitch:b * pitch + lt].astype(y_ref.dtype)
    hout_ref[...] = hst[...]


def _s5(u3, h0, bmat, cmat, acoef, dskip, wglu, bglu, lt):
    nb, L, _ = u3.shape
    pitch = lt + SUBLANES
    rows = nb * pitch
    return pl.pallas_call(
        functools.partial(_s5_kernel, nb=nb, lt=lt, pitch=pitch),
        grid=(L // lt,),
        in_specs=[pl.BlockSpec((nb, lt, S5_WIDTH), lambda i: (0, i, 0)),
                  _full(h0.shape), _full(bmat.shape), _full(cmat.shape), _full(acoef.shape),
                  _full(dskip.shape), _full(wglu.shape), _full(bglu.shape)],
        out_specs=[pl.BlockSpec((nb, lt, S5_WIDTH), lambda i: (0, i, 0)), _full(h0.shape)],
        out_shape=[jax.ShapeDtypeStruct((nb, L, S5_WIDTH), BF16), jax.ShapeDtypeStruct(h0.shape, F32)],
        scratch_shapes=[pltpu.VMEM((rows, S5_WIDTH), F32), pltpu.VMEM((rows, LANES), F32),
                        pltpu.VMEM((rows, LANES), F32), pltpu.VMEM((4, rows, LANES), F32),
                        pltpu.VMEM(h0.shape, F32)],
        compiler_params=_params(("arbitrary",)),
        name="s5_scan",
    )(u3, h0, bmat, cmat, acoef, dskip, wglu, bglu)


def _s5_weights(a_re, a_im, log_dt, b_re, b_im, c_re, c_im):
    dt = jnp.exp(log_dt)[:, None]
    lr, li = a_re * dt, a_im * dt
    mag = jnp.exp(lr)
    ab_re, ab_im = mag * jnp.cos(li), mag * jnp.sin(li)
    den = a_re * a_re + a_im * a_im
    cf_re = ((ab_re - 1.0) * a_re + ab_im * a_im) / den
    cf_im = (ab_im * a_re - (ab_re - 1.0) * a_im) / den
    bb_re = cf_re[..., None] * b_re - cf_im[..., None] * b_im
    bb_im = cf_re[..., None] * b_im + cf_im[..., None] * b_re
    jj = np.arange(S5_TILES)[:, None, None]
    lg = np.arange(8)[None, :, None]
    gi = np.arange(2)[None, None, :]
    sel = jnp.asarray((lg == 2 * (jj % 4) + gi).astype(np.float32))
    tiles = lambda w: w.reshape(S5_TILES, 2, *w.shape[1:])
    bt = lambda w: jnp.einsum('jlg,jgpn->jlngp', sel, tiles(w)).reshape(S5_TILES, LANES, LANES)
    bmat = jnp.concatenate([bt(bb_re), bt(bb_im)], axis=-1)
    ct = lambda w: jnp.einsum('jlg,jgnp->jgpln', sel, tiles(w)).reshape(S5_TILES, LANES, LANES)
    cmat = jnp.concatenate([ct(c_re), -ct(c_im)], axis=1)
    acoef = jnp.concatenate([ab_re.reshape(S5_TILES, LANES), ab_im.reshape(S5_TILES, LANES)], axis=-1)
    acoef = jnp.broadcast_to(acoef[:, None, :], (S5_TILES, SUBLANES, 2 * LANES))
    return bmat.astype(BF16), cmat.astype(BF16), acoef


def _s5_state_in(h_re, h_im):
    nb = h_re.shape[0]
    h = jnp.concatenate([h_re.reshape(nb, S5_TILES, LANES), h_im.reshape(nb, S5_TILES, LANES)], axis=-1)
    return jnp.transpose(h, (1, 0, 2))


def _s5_state_out(h):
    nb = h.shape[1]
    h = jnp.transpose(h, (1, 0, 2))
    return (h[..., :LANES].reshape(nb, S5_GROUPS, S5_STATE), h[..., LANES:].reshape(nb, S5_GROUPS, S5_STATE))


def _split_bf16(a):
    hi = a.astype(BF16)
    return hi, (a - hi.astype(F32)).astype(BF16)


def _bdot(a, b):
    return lax.dot_general(a, b, (((2,), (1,)), ((0,), (0,))), preferred_element_type=F32)


def _bdot_nt(a, b):
    return lax.dot_general(a, b, (((2,), (2,)), ((0,), (0,))), preferred_element_type=F32)


def _bdot_split(a, b):
    ah, al = _split_bf16(a)
    bh, bl = _split_bf16(b)
    return _bdot(ah, bh) + _bdot(ah, bl) + _bdot(al, bh)


def _unit_lower_inverse(nmat):
    lc = nmat.shape[-1]
    ri = lax.broadcasted_iota(jnp.int32, nmat.shape, 1)
    ci = lax.broadcasted_iota(jnp.int32, nmat.shape, 2)
    base = 16
    dmat = jnp.where(ri // base == ci // base, nmat, 0.0)
    inv = jnp.where(ri == ci, 1.0, 0.0) - dmat
    pw = dmat
    for _ in range(3):
        pw = _bdot_split(pw, pw)
        inv = inv + _bdot_split(inv, pw)
    size = base
    while size < lc:
        off = jnp.where(ri // (2 * size) == ci // (2 * size), jnp.where(ri // size > ci // size, nmat, 0.0), 0.0)
        inv = inv - _bdot_split(_bdot_split(inv, off), inv)
        size *= 2
    return inv


def _gdn_local_kernel(qkv_ref, ba_ref, ctx_ref, cw_ref, p1_ref, p2_ref, ltri_ref,
                      u0_ref, w_ref, qd_ref, kd_ref, attn_ref, g_ref, cout_ref, xpad, *, lc, cp):
    c = pl.program_id(1)
    rb = lc * cp

    @pl.when(c == 0)
    def _():
        xpad[0:SUBLANES, :] = ctx_ref[0]

    xpad[SUBLANES:SUBLANES + rb, :] = qkv_ref[0]
    cw = cw_ref[...]
    conv = (cw[3:4] * xpad[8:8 + rb, :] + cw[2:3] * xpad[7:7 + rb, :]
            + cw[1:2] * xpad[6:6 + rb, :] + cw[0:1] * xpad[5:5 + rb, :])
    tail = xpad[rb:rb + SUBLANES, :]
    xpad[0:SUBLANES, :] = tail
    cout_ref[0] = tail
    a = conv * _sigmoid(conv)

    ba = ba_ref[0]
    beta_all = _sigmoid(ba)
    sp_in = ba + p2_ref[...]
    softplus = jnp.maximum(sp_in, 0.0) + jnp.log(1.0 + jnp.exp(-jnp.abs(sp_in)))
    g_all = p1_ref[...] * softplus
    g_hi = g_all.astype(BF16)
    g_r = g_all - g_hi.astype(F32)
    g_mid = g_r.astype(BF16)
    g_lo = (g_r - g_mid.astype(F32)).astype(BF16)
    lt = ltri_ref[...]
    G = _dot(lt, g_hi) + _dot(lt, g_mid) + _dot(lt, g_lo)
    g_ref[0] = G
    GT = G.T
    pairs = [(h, cc) for h in range(GDN_HEADS) for cc in range(cp)]
    qs, ks, vs, betas, gcols, grows, glasts = [], [], [], [], [], [], []
    for h in range(GDN_HEADS):
        qa = a[:, h * GDN_DK:(h + 1) * GDN_DK]
        ka = a[:, (GDN_HEADS + h) * GDN_DK:(GDN_HEADS + h + 1) * GDN_DK]
        va = a[:, (2 * GDN_HEADS + h) * GDN_DK:(2 * GDN_HEADS + h + 1) * GDN_DK]
        qa = qa * lax.rsqrt(jnp.sum(qa * qa, -1, keepdims=True) + NORM_EPS) * (GDN_DK ** -0.5)
        ka = ka * lax.rsqrt(jnp.sum(ka * ka, -1, keepdims=True) + NORM_EPS)
        for cc in range(cp):
            rows = slice(cc * lc, (cc + 1) * lc)
            qs.append(qa[rows])
            ks.append(ka[rows])
            vs.append(va[rows])
            betas.append(beta_all[rows, h:h + 1])
            gcols.append(G[rows, GDN_HEADS + h:GDN_HEADS + h + 1])
            grows.append(GT[GDN_HEADS + h:GDN_HEADS + h + 1, cc * lc:(cc + 1) * lc])
            glasts.append(GT[GDN_HEADS + h:GDN_HEADS + h + 1, (cc + 1) * lc - 1:(cc + 1) * lc])
    q3, k3, v3 = jnp.stack(qs), jnp.stack(ks), jnp.stack(vs)
    beta3, gcol3 = jnp.stack(betas), jnp.stack(gcols)
    grow3, glast3 = jnp.stack(grows), jnp.stack(glasts)
    shape3 = (len(pairs), lc, lc)
    ri = lax.broadcasted_iota(jnp.int32, shape3, 1)
    ci = lax.broadcasted_iota(jnp.int32, shape3, 2)
    incl = ri >= ci
    dec3 = jnp.where(incl, jnp.exp(jnp.where(incl, gcol3 - grow3, 0.0)), 0.0)
    eg3 = jnp.exp(gcol3)
    kb3 = k3 * beta3
    kbf3 = k3.astype(BF16)
    nmat3 = jnp.where(ri > ci, _bdot_nt(kb3.astype(BF16), kbf3) * dec3, 0.0)
    inv3 = _unit_lower_inverse(nmat3)
    sol3 = _bdot_split(inv3, jnp.concatenate([v3 * beta3, kb3 * eg3], axis=-1))
    w3 = sol3[:, :, GDN_DV:].astype(BF16)
    qd3 = (q3 * eg3).astype(BF16)
    kd3 = (k3 * jnp.exp(glast3 - gcol3)).astype(BF16)
    attn3 = (_bdot_nt(q3.astype(BF16), kbf3) * dec3).astype(BF16)
    for i, (h, cc) in enumerate(pairs):
        rows = slice(cc * lc, (cc + 1) * lc)
        cols = slice(h * GDN_DV, (h + 1) * GDN_DV)
        u0_ref[0, rows, cols] = sol3[i, :, :GDN_DV]
        w_ref[0, rows, cols] = w3[i]
        qd_ref[0, rows, cols] = qd3[i]
        kd_ref[0, rows, cols] = kd3[i]
        attn_ref[0, rows, h * lc:(h + 1) * lc] = attn3[i]


def _gdn_seq_kernel(u0_ref, w_ref, qd_ref, kd_ref, attn_ref, g_ref, z_ref, s0_ref, nw_ref,
                    y_ref, sout_ref, S, *, lc, nbb):
    c = pl.program_id(1)

    @pl.when(c == 0)
    def _():
        S[...] = s0_ref[...]

    nw = nw_ref[...]
    pairs = [(bb, h) for bb in range(nbb) for h in range(GDN_HEADS)]
    hcols = lambda h: slice(h * GDN_DV, (h + 1) * GDN_DV)
    stack = lambda f: jnp.stack([f(bb, h) for bb, h in pairs])
    dlast = jnp.exp(g_ref[:, lc - 1:lc, :])
    S3 = S[...].reshape(len(pairs), GDN_DK, GDN_DV)
    wq3 = stack(lambda bb, h: jnp.concatenate([w_ref[bb, :, hcols(h)], qd_ref[bb, :, hcols(h)]], axis=0))
    r3 = _bdot(wq3, S3.astype(BF16))
    ub3 = (stack(lambda bb, h: u0_ref[bb, :, hcols(h)]) - r3[:, :lc]).astype(BF16)
    o3 = r3[:, lc:] + _bdot(stack(lambda bb, h: attn_ref[bb, :, h * lc:(h + 1) * lc]), ub3)
    d3 = stack(lambda bb, h: dlast[bb, :, GDN_HEADS + h:GDN_HEADS + h + 1])
    kd3 = stack(lambda bb, h: kd_ref[bb, :, hcols(h)])
    kdu3 = lax.dot_general(kd3, ub3, (((1,), (1,)), ((0,), (0,))), preferred_element_type=F32)
    S[...] = (d3 * S3 + kdu3).reshape(S.shape)
    o3 = o3 * lax.rsqrt(jnp.mean(o3 * o3, -1, keepdims=True) + NORM_EPS) * nw
    for i, (bb, h) in enumerate(pairs):
        zh = z_ref[bb, :, hcols(h)]
        y_ref[bb, :, hcols(h)] = (o3[i] * (zh * _sigmoid(zh))).astype(y_ref.dtype)

    @pl.when(c == pl.num_programs(1) - 1)
    def _():
        sout_ref[...] = S[...]


def _gdn(qkv3, z3, ba3, ctx8, s0, cw, p1, p2, nw, lc, cp, nbb):
    nb, L, _ = qkv3.shape
    rb = lc * cp
    hd = GDN_HEADS * GDN_DV
    ltri = jnp.asarray(np.kron(np.eye(cp, dtype=np.float32), np.tril(np.ones((lc, lc), np.float32)))).astype(BF16)
    blk = lambda n: pl.BlockSpec((1, rb, n), lambda b, c: (b, c, 0))
    per_b = lambda shape: pl.BlockSpec((1,) + shape, lambda b, c: (b,) + (0,) * len(shape))
    cst = lambda shape: pl.BlockSpec(shape, lambda b, c: (0,) * len(shape))
    sds = lambda n, dt: jax.ShapeDtypeStruct((nb, L, n), dt)
    u0, w, qd, kd, attn, G, cout = pl.pallas_call(
        functools.partial(_gdn_local_kernel, lc=lc, cp=cp),
        grid=(nb, L // rb),
        in_specs=[blk(GDN_QKV), blk(LANES), per_b((SUBLANES, GDN_QKV)), cst(cw.shape), cst(p1.shape),
                  cst(p2.shape), cst(ltri.shape)],
        out_specs=[blk(hd), blk(hd), blk(hd), blk(hd), blk(GDN_HEADS * lc), blk(LANES),
                   per_b((SUBLANES, GDN_QKV))],
        out_shape=[sds(hd, F32), sds(hd, BF16), sds(hd, BF16), sds(hd, BF16), sds(GDN_HEADS * lc, BF16),
                   sds(LANES, F32), jax.ShapeDtypeStruct((nb, SUBLANES, GDN_QKV), F32)],
        scratch_shapes=[pltpu.VMEM((rb + SUBLANES, GDN_QKV), F32)],
        compiler_params=_params(("parallel", "arbitrary")),
        name="gdn_local",
    )(qkv3, ba3, ctx8, cw, p1, p2, ltri)
    sblk = lambda n: pl.BlockSpec((nbb, lc, n), lambda b, c: (b, c, 0))
    state = pl.BlockSpec((nbb, GDN_HEADS, GDN_DK, GDN_DV), lambda b, c: (b, 0, 0, 0))
    y, s_new = pl.pallas_call(
        functools.partial(_gdn_seq_kernel, lc=lc, nbb=nbb),
        grid=(nb // nbb, L // lc),
        in_specs=[sblk(hd), sblk(hd), sblk(hd), sblk(hd), sblk(GDN_HEADS * lc), sblk(LANES), sblk(hd),
                  state, cst(nw.shape)],
        out_specs=[sblk(hd), state],
        out_shape=[sds(hd, BF16), jax.ShapeDtypeStruct(s0.shape, F32)],
        scratch_shapes=[pltpu.VMEM((nbb, GDN_HEADS, GDN_DK, GDN_DV), F32)],
        compiler_params=_params(("parallel", "arbitrary")),
        name="gdn_seq",
    )(u0, w, qd, kd, attn, G, z3, s0, nw)
    return y, s_new, cout


def _ret_kernel(q_ref, k_ref, v_ref, g_ref, r0_ref, dec_ref, qs_ref, ks_ref, cd_ref, o_ref, rout_ref, R,
                *, nbb):
    c = pl.program_id(1)

    @pl.when(c == 0)
    def _():
        R[...] = r0_ref[...]

    pairs = [(bb, h) for bb in range(nbb) for h in range(RET_HEADS)]
    stack = lambda f: jnp.stack([f(bb, h) for bb, h in pairs])
    kcols = lambda h: slice(h * RET_DK, (h + 1) * RET_DK)
    vcols = lambda h: slice(h * RET_DV, (h + 1) * RET_DV)
    q3 = stack(lambda bb, h: q_ref[bb, :, kcols(h)])
    k3 = stack(lambda bb, h: k_ref[bb, :, kcols(h)])
    v3 = stack(lambda bb, h: v_ref[bb, :, vcols(h)])
    dec3 = stack(lambda bb, h: dec_ref[h])
    qs3 = stack(lambda bb, h: qs_ref[h])
    ks3 = stack(lambda bb, h: ks_ref[h])
    cd3 = stack(lambda bb, h: cd_ref[h])
    R3 = R[...].reshape(len(pairs), RET_DK, RET_DV)
    s3 = _bdot_nt(q3, k3) * dec3
    o3 = _bdot(s3.astype(BF16), v3) + _bdot(q3, R3.astype(BF16)) * qs3
    kv3 = lax.dot_general((k3.astype(F32) * ks3).astype(BF16), v3, (((1,), (1,)), ((0,), (0,))),
                          preferred_element_type=F32)
    R[...] = (cd3 * R3 + kv3).reshape(R.shape)
    mu = jnp.mean(o3, -1, keepdims=True)
    d3 = o3 - mu
    var = jnp.mean(d3 * d3, -1, keepdims=True)
    on3 = d3 * lax.rsqrt(var + LN_EPS)
    for i, (bb, h) in enumerate(pairs):
        gt = g_ref[bb, :, vcols(h)].astype(F32)
        o_ref[bb, :, vcols(h)] = (gt * _sigmoid(gt) * on3[i]).astype(o_ref.dtype)

    @pl.when(c == pl.num_programs(1) - 1)
    def _():
        rout_ref[...] = R[...]


def _retention(proj3, r0, lc, nbb):
    nb, L, _ = proj3.shape
    log_g = np.log(1.0 - 2.0 ** (-5.0 - np.arange(RET_HEADS, dtype=np.float64)))
    idx = np.arange(lc, dtype=np.float64)
    dec = np.exp(log_g[:, None, None] * np.abs(idx[:, None] - idx[None, :])).astype(np.float32)
    qs = np.exp(log_g[:, None] * (idx + 1.0)).astype(np.float32)[..., None]
    ks = np.exp(log_g[:, None] * (lc - 1.0 - idx)).astype(np.float32)[..., None]
    cdec = np.exp(log_g * lc).astype(np.float32)[:, None, None]
    nqk = RET_HEADS * RET_DK
    nv = RET_HEADS * RET_DV
    cst = lambda shape: pl.BlockSpec(shape, lambda b, c: (0,) * len(shape))
    state = pl.BlockSpec((nbb, RET_HEADS, RET_DK, RET_DV), lambda b, c: (b, 0, 0, 0))
    return pl.pallas_call(
        functools.partial(_ret_kernel, nbb=nbb),
        grid=(nb // nbb, L // lc),
        in_specs=[pl.BlockSpec((nbb, lc, nqk), lambda b, c: (b, c, 0)),
                  pl.BlockSpec((nbb, lc, nqk), lambda b, c: (b, c, 1)),
                  pl.BlockSpec((nbb, lc, nv), lambda b, c: (b, c, 1)),
                  pl.BlockSpec((nbb, lc, nv), lambda b, c: (b, c, 2)),
                  state, cst(dec.shape), cst(qs.shape), cst(ks.shape), cst(cdec.shape)],
        out_specs=[pl.BlockSpec((nbb, lc, nv), lambda b, c: (b, c, 0)), state],
        out_shape=[jax.ShapeDtypeStruct((nb, L, nv), BF16), jax.ShapeDtypeStruct(r0.shape, F32)],
        scratch_shapes=[pltpu.VMEM((nbb, RET_HEADS, RET_DK, RET_DV), F32)],
        compiler_params=_params(("parallel", "arbitrary")),
        name="retention",
    )(proj3, proj3, proj3, proj3, r0, jnp.asarray(dec), jnp.asarray(qs), jnp.asarray(ks), jnp.asarray(cdec))


RUN_ALIGN = SUBLANES
RUN_PIECE = 64
MOE_VMEM_LIMIT = 58 * 1024 * 1024


def _moe_params(sem):
    return pltpu.CompilerParams(dimension_semantics=sem, vmem_limit_bytes=MOE_VMEM_LIMIT)


def _tile_cap(tm):
    rows = TOP_K * tm + N_EXPERTS * (RUN_ALIGN - 1)
    return -(-rows // LANES) * LANES


def _route_kernel(x_ref, rw_ref, rb_ref, lst_ref, ust_ref, pos_ref, gt_ref, cnt_ref, xs_ref):
    tm = x_ref.shape[0]
    cap = xs_ref.shape[1]
    x = x_ref[...]
    lane = lax.broadcasted_iota(jnp.int32, (tm, LANES), 1)
    lane_f = lane.astype(F32)
    xh, xl = _split_bf16(x)
    wh, wl = _split_bf16(rw_ref[...])
    logits = _dot(xh, wh) + _dot(xh, wl) + _dot(xl, wh) + rb_ref[...]
    logits = jnp.where(lane < N_EXPERTS, logits, -jnp.inf)
    vals, hots = [], []
    for _ in range(TOP_K):
        m = jnp.max(logits, -1, keepdims=True)
        first = jnp.min(jnp.where(logits == m, lane_f, float(LANES)), -1, keepdims=True)
        hot = lane_f == first
        vals.append(m)
        hots.append(hot)
        logits = jnp.where(hot, -jnp.inf, logits)
    es = [jnp.exp(v - vals[0]) for v in vals]
    den = es[0] + es[1] + es[2] + es[3]
    multi = jnp.zeros((tm, LANES), F32)
    for hot in hots:
        multi = multi + hot.astype(F32)
    counts = jnp.sum(multi, 0, keepdims=True)
    units = jnp.floor((counts + (RUN_ALIGN - 1)) * (1.0 / RUN_ALIGN))
    offs = _dot(jnp.broadcast_to(units, (SUBLANES, LANES)).astype(BF16), ust_ref[...])[0:1] * float(RUN_ALIGN)
    before = _dot(lst_ref[...], multi.astype(BF16))
    slot = offs + before
    pos = jnp.zeros((tm, LANES), F32)
    gt = jnp.zeros((tm, LANES), F32)
    for kk in range(TOP_K):
        pos = jnp.where(lane == kk, jnp.sum(jnp.where(hots[kk], slot, 0.0), -1, keepdims=True), pos)
        gt = jnp.where(lane == kk, es[kk] / den, gt)
    pos = pos.astype(jnp.int32)
    pos_ref[...] = pos
    gt_ref[...] = gt
    cnt_ref[0] = counts
    pos_t = pos.T
    row = lax.broadcasted_iota(jnp.int32, (cap, tm), 0)
    sel = jnp.zeros((cap, tm), F32)
    for kk in range(TOP_K):
        sel = sel + jnp.where(row == pos_t[kk:kk + 1, :], 1.0, 0.0)
    xs_ref[0] = _dot(sel.astype(BF16), xh)


def _route(x, rw, rb, tm):
    t = x.shape[0]
    nt = t // tm
    cap = _tile_cap(tm)
    lst = jnp.asarray(np.tril(np.ones((tm, tm), np.float32), -1)).astype(BF16)
    ust = jnp.asarray(np.triu(np.ones((LANES, LANES), np.float32), 1)).astype(BF16)
    row = lambda n: pl.BlockSpec((tm, n), lambda i: (i, 0))
    return pl.pallas_call(
        _route_kernel,
        grid=(nt,),
        in_specs=[row(D_MODEL), _full(rw.shape), _full(rb.shape), _full(lst.shape), _full(ust.shape)],
        out_specs=[row(LANES), row(LANES), pl.BlockSpec((1, 1, LANES), lambda i: (i, 0, 0)),
                   pl.BlockSpec((1, cap, D_MODEL), lambda i: (i, 0, 0))],
        out_shape=[jax.ShapeDtypeStruct((t, LANES), jnp.int32), jax.ShapeDtypeStruct((t, LANES), F32),
                   jax.ShapeDtypeStruct((nt, 1, LANES), F32), jax.ShapeDtypeStruct((nt, cap, D_MODEL), F32)],
        compiler_params=_moe_params(("parallel",)),
        name="moe_route",
    )(x, rw, rb, lst, ust)


def _expert_kernel(be_ref, nu_ref, ilo_ref, rows_ref, gs_ref, n8_ref, lo_ref, *refs, blk, tiles):
    ng = len(tiles)
    xs_hbms = refs[:ng]
    w1_ref, b1_ref, w2_ref, b2_ref = refs[ng:ng + 4]
    ys_hbms = refs[ng + 4:2 * ng + 4]
    xbuf, ybuf, w1b, w2b, in_sem, out_sem = refs[2 * ng + 4:]
    nt = sum(tiles)
    firsts = [sum(tiles[:g]) for g in range(ng)]
    j = pl.program_id(0)
    nu = nu_ref[0]

    def per_group(i, fn):
        for g in range(ng):
            @pl.when((i >= firsts[g]) & (i < firsts[g] + tiles[g]))
            def _(g=g):
                fn(g, i - firsts[g])

    def for_each_run(jb, fn):
        e = be_ref[jb]
        base = jb * blk

        def cond(i):
            return (i < nt) & (gs_ref[e * nt + jnp.minimum(i, nt - 1)] < base + blk)

        def body(i):
            g0 = gs_ref[e * nt + i]
            first = jnp.maximum(g0, base)
            last = jnp.minimum(g0 + n8_ref[e * nt + i], base + blk)
            fn(i, lo_ref[e * nt + i] + (first - g0), first - base, last - first)
            return i + 1

        lax.while_loop(cond, body, ilo_ref[jb])

    def pieces(length, fn):
        nbig = length // RUN_PIECE

        def big(q, carry):
            fn(q * RUN_PIECE, RUN_PIECE)
            return carry

        lax.fori_loop(0, nbig, big, 0)
        size = RUN_PIECE // 2
        while size >= RUN_ALIGN:
            @pl.when((length & size) != 0)
            def _(size=size):
                fn(nbig * RUN_PIECE + (length & (RUN_PIECE - 2 * size)), size)
            size //= 2

    def aligned(v, size):
        return pl.ds(pl.multiple_of(v, RUN_ALIGN), size)

    def copy_in(jb, slot):
        def run(i, src, dst, length):
            def group(g, ig):
                def piece(off, size):
                    pltpu.make_async_copy(xs_hbms[g].at[ig, aligned(src + off, size)],
                                          xbuf.at[slot, aligned(dst + off, size)], in_sem.at[slot]).start()
                pieces(length, piece)
            per_group(i, group)
        for_each_run(jb, run)

    def copy_out(jb, slot):
        def run(i, src, dst, length):
            def group(g, ig):
                def piece(off, size):
                    pltpu.make_async_copy(ybuf.at[slot, aligned(dst + off, size)],
                                          ys_hbms[g].at[ig, aligned(src + off, size)], out_sem.at[slot]).start()
                pieces(length, piece)
            per_group(i, group)
        for_each_run(jb, run)

    def wait_rows(sem, nrows):
        size = blk
        while size >= RUN_ALIGN:
            @pl.when((nrows & size) != 0)
            def _(size=size):
                pltpu.make_async_copy(xbuf.at[1, pl.ds(0, size)], xbuf.at[0, pl.ds(0, size)], sem).wait()
            size //= 2

    @pl.when(j < nu)
    def _():
        slot = j % 2

        @pl.when(j == 0)
        def _():
            xbuf[...] = jnp.zeros_like(xbuf)
            copy_in(0, 0)

        wait_rows(in_sem.at[slot], rows_ref[j])

        @pl.when(j + 1 < nu)
        def _():
            copy_in(j + 1, 1 - slot)

        @pl.when(j >= 2)
        def _():
            wait_rows(out_sem.at[slot], rows_ref[jnp.maximum(j - 2, 0)])

        @pl.when((j == 0) | (be_ref[j] != be_ref[jnp.maximum(j - 1, 0)]))
        def _():
            w1b[...] = w1_ref[0, 0].astype(BF16)
            w2b[...] = w2_ref[0, 0].astype(BF16)

        h = _dot(xbuf[slot].astype(BF16), w1b[...]) + b1_ref[0]
        glu = jnp.minimum(h[:, :D_FF], SWIGLU_LIMIT)
        lin = jnp.clip(h[:, D_FF:], -SWIGLU_LIMIT, SWIGLU_LIMIT)
        act = glu * _sigmoid(SWIGLU_ALPHA * glu) * (lin + 1.0)
        ybuf[slot] = _dot(act.astype(BF16), w2b[...]) + b2_ref[0]
        copy_out(j, slot)

        @pl.when(j == nu - 1)
        def _():
            wait_rows(out_sem.at[slot], rows_ref[j])

            @pl.when(j >= 1)
            def _():
                wait_rows(out_sem.at[1 - slot], rows_ref[jnp.maximum(j - 1, 0)])


def _experts(tables, xs_list, w1, b1, w2, b2, layer, blk):
    ng = len(xs_list)
    n_blk = tables[0].shape[0]
    wspec = lambda shape: pl.BlockSpec((1, 1) + shape, lambda j, be, *_: (layer, be[j], 0, 0))
    bspec = lambda n: pl.BlockSpec((1, 1, n), lambda j, be, *_: (be[j], 0, 0))
    hbm = pl.BlockSpec(memory_space=pl.ANY)
    grid_spec = pltpu.PrefetchScalarGridSpec(
        num_scalar_prefetch=len(tables),
        grid=(n_blk,),
        in_specs=[hbm] * ng + [wspec((D_MODEL, 2 * D_FF)), bspec(2 * D_FF), wspec((D_FF, D_MODEL)), bspec(D_MODEL)],
        out_specs=[hbm] * ng,
        scratch_shapes=[pltpu.VMEM((2, blk, D_MODEL), F32), pltpu.VMEM((2, blk, D_MODEL), F32),
                        pltpu.VMEM((D_MODEL, 2 * D_FF), BF16), pltpu.VMEM((D_FF, D_MODEL), BF16),
                        pltpu.SemaphoreType.DMA((2,)), pltpu.SemaphoreType.DMA((2,))],
    )
    return pl.pallas_call(
        functools.partial(_expert_kernel, blk=blk, tiles=tuple(xs.shape[0] for xs in xs_list)),
        grid_spec=grid_spec,
        out_shape=[jax.ShapeDtypeStruct(xs.shape, F32) for xs in xs_list],
        input_output_aliases={len(tables) + g: g for g in range(ng)},
        compiler_params=_moe_params(("arbitrary",)),
        name="moe_experts",
    )(*tables, *xs_list, w1, b1, w2, b2)


def _combine_kernel(ys_ref, pos_ref, gt_ref, x_ref, p_ref, g_ref, b_ref, plew_ref, gatew_ref, o_ref):
    tm = x_ref.shape[0]
    cap = ys_ref.shape[1]
    pos = pos_ref[...]
    gt = gt_ref[...]
    col = lax.broadcasted_iota(jnp.int32, (tm, cap), 1)
    sel = jnp.zeros((tm, cap), F32)
    for kk in range(TOP_K):
        sel = sel + jnp.where(col == pos[:, kk:kk + 1], gt[:, kk:kk + 1], 0.0)
    y = _dot(sel.astype(BF16), ys_ref[0].astype(BF16))
    x2 = _layer_norm(DEEPNORM_ALPHA * x_ref[...] + y, g_ref[...], b_ref[...])
    pp = _dot(p_ref[...].astype(BF16), plew_ref[...])
    gg = _sigmoid(_dot(x2.astype(BF16), gatew_ref[...]))
    o_ref[...] = x2 + pp * gg


def _combine(ys, pos, gt, x, p, g, b, plew, gatew, tm):
    t = x.shape[0]
    cap = ys.shape[1]
    row = lambda n: pl.BlockSpec((tm, n), lambda i: (i, 0))
    return pl.pallas_call(
        _combine_kernel,
        grid=(t // tm,),
        in_specs=[pl.BlockSpec((1, cap, D_MODEL), lambda i: (i, 0, 0)), row(LANES), row(LANES),
                  row(D_MODEL), row(PLE_DIM), _full((1, D_MODEL)), _full((1, D_MODEL)),
                  _full(plew.shape), _full(gatew.shape)],
        out_specs=row(D_MODEL),
        out_shape=jax.ShapeDtypeStruct((t, D_MODEL), F32),
        compiler_params=_moe_params(("parallel",)),
        name="moe_combine",
    )(ys, pos, gt, x, p, g, b, plew, gatew)


def _moe_tables(cnt, t, blk):
    nt = cnt.shape[0]
    n = cnt[:, 0, :N_EXPERTS].astype(jnp.int32)
    n8 = (n + RUN_ALIGN - 1) // RUN_ALIGN * RUN_ALIGN
    lo = jnp.cumsum(n8, axis=1) - n8
    rows_e = jnp.sum(n8, axis=0)
    padded = (rows_e + blk - 1) // blk * blk
    pend = jnp.cumsum(padded)
    pstart = pend - padded
    gstart = pstart[None, :] + jnp.cumsum(n8, axis=0) - n8
    n_blk = -(-(TOP_K * t + nt * N_EXPERTS * (RUN_ALIGN - 1) + N_EXPERTS * (blk - 1)) // blk)
    n_used = (pend[-1] // blk).astype(jnp.int32).reshape(1)
    blk_start = jnp.arange(n_blk, dtype=jnp.int32) * blk
    blk_e = jnp.minimum(jnp.sum((pend[None, :] <= blk_start[:, None]).astype(jnp.int32), axis=1), N_EXPERTS - 1)
    run_end = (gstart + n8)[:, blk_e]
    ilo = jnp.sum((run_end <= blk_start[None, :]).astype(jnp.int32), axis=0)
    rows_b = jnp.clip((pstart + rows_e)[blk_e] - blk_start, 0, blk)
    flat = lambda a: a.T.reshape(-1).astype(jnp.int32)
    i32 = lambda a: a.astype(jnp.int32)
    return (i32(blk_e), n_used, i32(ilo), i32(rows_b), flat(gstart), flat(n8), flat(lo))


def _moe_ple(xs_in, ps, rw, rb, w1, b1, w2, b2, layer, g, b, plew, gatew, tms, blk):
    routed = [_route(x, rw, rb, tm) for x, tm in zip(xs_in, tms)]
    cnt = jnp.concatenate([r[2] for r in routed], axis=0)
    tables = _moe_tables(cnt, sum(x.shape[0] for x in xs_in), blk)
    ys = _experts(tables, [r[3] for r in routed], w1, b1, w2, b2, layer, blk)
    return [_combine(y, r[0], r[1], x, p, g, b, plew, gatew, tm)
            for y, r, x, p, tm in zip(ys, routed, xs_in, ps, tms)]


_ROT_PERM = np.concatenate([np.arange(0, RET_DK, 2), np.arange(1, RET_DK, 2)])
_ROT_INV = np.argsort(_ROT_PERM)


def _lane_row(vals, offset):
    row = jnp.zeros((1, LANES), F32)
    return row.at[0, offset:offset + vals.shape[0]].set(vals)


def _even_mixer(x, nb, L, s5_re, s5_im, gdn_s, conv_s, W, tm):
    t = nb * L
    lc = L if L <= CHUNK else CHUNK
    u, qkv, z, ba = _proj_even(x, W['wu'], W['wqkv'], W['wz'], W['wba'], tm)
    yA3, h_new = _s5(u.reshape(nb, L, S5_WIDTH), _s5_state_in(s5_re[0].astype(F32), s5_im[0].astype(F32)),
                     W['bmat'], W['cmat'], W['acoef'], W['dskip'], W['wglu'], W['bglu'], lc)
    new_re, new_im = _s5_state_out(h_new)
    ctx8 = jnp.concatenate([jnp.zeros((nb, SUBLANES - (GDN_CONV - 1), GDN_QKV), F32), conv_s[0].astype(F32)], axis=1)
    yB3, new_gdn, cout = _gdn(qkv.reshape(nb, L, GDN_QKV), z.reshape(nb, L, -1), ba.reshape(nb, L, LANES),
                              ctx8, gdn_s[0].astype(F32), W['convw'], W['p1'], W['p2'], W['normw'], lc,
                              cp=min(4, L // lc), nbb=4)
    new_conv = cout[:, SUBLANES - (GDN_CONV - 1):, :]
    x = _outproj_ln([yA3.reshape(t, -1), yB3.reshape(t, -1)], [W['wout_a'], W['wout_b']], x,
                    W['ln1_g'][0], W['ln1_b'][0], tm)
    return x, new_re[None], new_im[None], new_gdn[None], new_conv[None]


def _odd_mixer(x, nb, L, ret_s, pos0, W, tm):
    t = nb * L
    lc = L if L <= CHUNK else CHUNK
    pos = pos0 + jnp.arange(L, dtype=F32)
    freq = 1.0 / (ROPE_BASE ** jnp.linspace(0.0, 1.0, RET_DK // 2, dtype=F32))
    ang = pos[:, None] * freq[None]
    cos = jnp.broadcast_to(jnp.cos(ang)[None], (nb, L, RET_DK // 2)).reshape(t, RET_DK // 2)
    sin = jnp.broadcast_to(jnp.sin(ang)[None], (nb, L, RET_DK // 2)).reshape(t, RET_DK // 2)
    proj = _proj_odd(x, W['win_odd'], cos, sin, tm)
    r0 = ret_s[0].astype(F32)[:, :, _ROT_PERM, :]
    o3, r_new = _retention(proj.reshape(nb, L, -1), r0, lc, nbb=2)
    new_ret = r_new[:, :, _ROT_INV, :]
    x = _outproj_ln([o3.reshape(t, -1)], [W['wout_odd']], x, W['ln1_g'][1], W['ln1_b'][1], tm)
    return x, new_ret[None]


def kernel(x_prompt, x_sample, state_s5_re, state_s5_im, state_gdn, state_gdn_conv, state_ret, p_prompt, p_sample, w_in_even, s5_a_re, s5_a_im, s5_log_dt, s5_b_re, s5_b_im, s5_c_re, s5_c_im, s5_d, s5_w_glu, s5_b_glu, gdn_conv_w, gdn_a_log, gdn_dt_bias, gdn_norm_w, w_out_even, w_in_odd, w_out_odd, ln1_g, ln1_b, ln2_g, ln2_b, router_w, router_b, moe_w1, moe_b1, moe_w2, moe_b2, ple_w, ple_gate_w):
    o1 = S5_WIDTH
    o2 = o1 + GDN_QKV
    o3 = o2 + GDN_HEADS * GDN_DV
    win = w_in_even[0]
    bmat, cmat, acoef = _s5_weights(s5_a_re[0], s5_a_im[0], s5_log_dt[0], s5_b_re[0], s5_b_im[0],
                                    s5_c_re[0], s5_c_im[0])
    wodd = w_in_odd[0]
    nk = RET_HEADS * RET_DK
    perm_cols = lambda w: w.reshape(D_MODEL, RET_HEADS, RET_DK)[:, :, _ROT_PERM].reshape(D_MODEL, nk)
    W = dict(
        wu=win[:, :o1].astype(BF16), wqkv=win[:, o1:o2].astype(BF16), wz=win[:, o2:o3].astype(BF16),
        wba=jnp.pad(win[:, o3:], ((0, 0), (0, LANES - 2 * GDN_HEADS))).astype(BF16),
        bmat=bmat, cmat=cmat, acoef=acoef, dskip=s5_d[0][None], wglu=s5_w_glu[0].astype(BF16),
        bglu=s5_b_glu[0][None], convw=gdn_conv_w[0],
        p1=_lane_row(-jnp.exp(gdn_a_log[0]), GDN_HEADS), p2=_lane_row(gdn_dt_bias[0], GDN_HEADS),
        normw=gdn_norm_w[0][None],
        wout_a=w_out_even[0][:S5_WIDTH].astype(BF16), wout_b=w_out_even[0][S5_WIDTH:].astype(BF16),
        win_odd=jnp.concatenate([perm_cols(wodd[:, :nk]), perm_cols(wodd[:, nk:2 * nk]), wodd[:, 2 * nk:]],
                                axis=1).astype(BF16),
        wout_odd=w_out_odd[0].astype(BF16),
        ln1_g=ln1_g[:, None], ln1_b=ln1_b[:, None], ln2_g=ln2_g[:, None], ln2_b=ln2_b[:, None],
        rw=jnp.pad(router_w, ((0, 0), (0, 0), (0, LANES - N_EXPERTS))),
        rb=jnp.pad(router_b, ((0, 0), (0, LANES - N_EXPERTS)))[:, None],
        w1=moe_w1, b1=moe_b1[:, :, None], w2=moe_w2, b2=moe_b2[:, :, None],
        plew=ple_w.astype(BF16), gatew=ple_gate_w.astype(BF16),
    )
    bp, lp, _ = x_prompt.shape
    bs, ls, _ = x_sample.shape
    zeros = lambda *s: jnp.zeros(s, F32)
    shapes = [(bp, lp), (bs, ls)]
    tms = [512, 128]
    ps = [p_prompt, p_sample]
    xs = [x_prompt.reshape(bp * lp, D_MODEL).astype(F32), x_sample.reshape(bs * ls, D_MODEL).astype(F32)]
    even_states = [(zeros(1, bp, S5_GROUPS, S5_STATE), zeros(1, bp, S5_GROUPS, S5_STATE),
                    zeros(1, bp, GDN_HEADS, GDN_DK, GDN_DV), zeros(1, bp, GDN_CONV - 1, GDN_QKV)),
                   (state_s5_re, state_s5_im, state_gdn, state_gdn_conv)]
    ret_states = [zeros(1, bp, RET_HEADS, RET_DK, RET_DV), state_ret]
    pos0s = [0.0, float(PAST_LEN)]

    def moe(xs, layer):
        return _moe_ple(xs, [p[layer].reshape(-1, PLE_DIM) for p in ps], W['rw'][layer], W['rb'][layer],
                        W['w1'], W['b1'][layer], W['w2'], W['b2'][layer], layer, W['ln2_g'][layer],
                        W['ln2_b'][layer], W['plew'][layer], W['gatew'][layer], tms, blk=512)

    even = [_even_mixer(x, nb, L, *st, W, tm) for x, (nb, L), st, tm in zip(xs, shapes, even_states, tms)]
    xs = moe([e[0] for e in even], 0)
    odd = [_odd_mixer(x, nb, L, st, pos0, W, tm)
           for x, (nb, L), st, pos0, tm in zip(xs, shapes, ret_states, pos0s, tms)]
    xs = moe([o[0] for o in odd], 1)
    dp = x_prompt.dtype
    y_p, y_s = xs[0].reshape(bp, lp, D_MODEL), xs[1].reshape(bs, ls, D_MODEL)
    (_, p_re, p_im, p_gdn, p_conv), (_, s_re, s_im, s_gdn, s_conv) = even
    p_ret, s_ret = odd[0][1], odd[1][1]
    return (y_p.astype(dp), y_s.astype(x_sample.dtype),
            p_re.astype(dp), p_im.astype(dp), p_gdn.astype(dp), p_conv.astype(dp), p_ret.astype(dp),
            s_re.astype(state_s5_re.dtype), s_im.astype(state_s5_im.dtype), s_gdn.astype(state_gdn.dtype),
            s_conv.astype(state_gdn_conv.dtype), s_ret.astype(state_ret.dtype))
```

```python
import functools
import math

import jax
import jax.numpy as jnp
import numpy as np
from jax import lax
from jax.experimental import pallas as pl
from jax.experimental.pallas import tpu as pltpu

F32 = jnp.float32
BF16 = jnp.bfloat16
HIGHEST = lax.Precision.HIGHEST

D_MODEL = 1024
CHUNK = 64
S5_WIDTH = 512
S5_GROUP = 16
S5_GROUPS = 32
S5_STATE = 64
S5_TILES = 16
GDN_HEADS = 4
GDN_DK = 128
GDN_DV = 128
GDN_CONV = 4
GDN_QKV = 1536
RET_HEADS = 4
RET_DK = 256
RET_DV = 512
ROPE_BASE = 10000.0
N_EXPERTS = 32
TOP_K = 4
D_FF = 1024
SWIGLU_LIMIT = 7.0
SWIGLU_ALPHA = 1.702
PLE_DIM = 256
DEPTH = 2
PAST_LEN = 1024
DEEPNORM_ALPHA = (2 * DEPTH) ** 0.25
LN_EPS = 1e-5
NORM_EPS = 1e-6

LANES = 128
SUBLANES = 8
VMEM_LIMIT = 48 * 1024 * 1024

def _params(sem):
    return pltpu.CompilerParams(dimension_semantics=sem, vmem_limit_bytes=VMEM_LIMIT)


def _dot(a, b):
    return jnp.dot(a, b, preferred_element_type=F32)


def _dot_hi(a, b):
    return jnp.dot(a, b, preferred_element_type=F32, precision=HIGHEST)


def _dot_nt(a, b, precision=None):
    return lax.dot_general(a, b, (((1,), (1,)), ((), ())), preferred_element_type=F32,
                           precision=precision)


def _dot_tn(a, b):
    return lax.dot_general(a, b, (((0,), (0,)), ((), ())), preferred_element_type=F32)


def _sigmoid(x):
    return 1.0 / (1.0 + jnp.exp(-x))


def _full(shape):
    nd = len(shape)
    return pl.BlockSpec(shape, lambda *_: (0,) * nd)


def _proj_even_kernel(x_ref, wu_ref, wqkv_ref, wz_ref, wba_ref, u_ref, qkv_ref, z_ref, ba_ref):
    xb = x_ref[...].astype(BF16)
    u_ref[...] = _dot(xb, wu_ref[...])
    qkv_ref[...] = _dot(xb, wqkv_ref[...])
    z_ref[...] = _dot(xb, wz_ref[...])
    ba_ref[...] = _dot(xb, wba_ref[...])


def _proj_even(x, wu, wqkv, wz, wba, tm):
    t = x.shape[0]
    row = lambda n: pl.BlockSpec((tm, n), lambda i: (i, 0))
    return pl.pallas_call(
        _proj_even_kernel,
        grid=(t // tm,),
        in_specs=[row(D_MODEL), _full(wu.shape), _full(wqkv.shape), _full(wz.shape), _full(wba.shape)],
        out_specs=[row(S5_WIDTH), row(GDN_QKV), row(GDN_HEADS * GDN_DV), row(LANES)],
        out_shape=[jax.ShapeDtypeStruct((t, S5_WIDTH), F32), jax.ShapeDtypeStruct((t, GDN_QKV), F32),
                   jax.ShapeDtypeStruct((t, GDN_HEADS * GDN_DV), F32), jax.ShapeDtypeStruct((t, LANES), F32)],
        compiler_params=_params(("parallel",)),
        name="proj_even",
    )(x, wu, wqkv, wz, wba)


def _proj_odd_kernel(x_ref, w_ref, cos_ref, sin_ref, o_ref):
    j = pl.program_id(0)
    acc = _dot(x_ref[...].astype(BF16), w_ref[...])

    @pl.when(j == 0)
    def _():
        cos, sin = cos_ref[...], sin_ref[...]
        half = RET_DK // 2
        for h in range(2 * RET_HEADS):
            x0 = acc[:, h * RET_DK:h * RET_DK + half]
            x1 = acc[:, h * RET_DK + half:(h + 1) * RET_DK]
            scale = 1.0 if h < RET_HEADS else RET_DK ** -0.5
            o_ref[:, h * RET_DK:h * RET_DK + half] = ((x0 * cos - x1 * sin) * scale).astype(o_ref.dtype)
            o_ref[:, h * RET_DK + half:(h + 1) * RET_DK] = ((x0 * sin + x1 * cos) * scale).astype(o_ref.dtype)

    @pl.when(j != 0)
    def _():
        o_ref[...] = acc.astype(o_ref.dtype)


def _proj_odd(x, w, cos, sin, tm):
    t = x.shape[0]
    nblk = w.shape[1] // 2048
    return pl.pallas_call(
        _proj_odd_kernel,
        grid=(nblk, t // tm),
        in_specs=[pl.BlockSpec((tm, D_MODEL), lambda j, i: (i, 0)),
                  pl.BlockSpec((D_MODEL, 2048), lambda j, i: (0, j)),
                  pl.BlockSpec((tm, LANES), lambda j, i: (i, 0)),
                  pl.BlockSpec((tm, LANES), lambda j, i: (i, 0))],
        out_specs=pl.BlockSpec((tm, 2048), lambda j, i: (i, j)),
        out_shape=jax.ShapeDtypeStruct((t, w.shape[1]), BF16),
        compiler_params=_params(("parallel", "parallel")),
        name="proj_odd",
    )(x, w, cos, sin)


def _layer_norm(r, g, b):
    mu = jnp.mean(r, -1, keepdims=True)
    d = r - mu
    var = jnp.mean(d * d, -1, keepdims=True)
    return d * lax.rsqrt(var + LN_EPS) * g + b


def _outproj_ln_kernel(*refs, n_in):
    a_refs = refs[:n_in]
    w_refs = refs[n_in:2 * n_in]
    x_ref, g_ref, b_ref, o_ref = refs[2 * n_in:]
    acc = _dot(a_refs[0][...], w_refs[0][...])
    for a_ref, w_ref in zip(a_refs[1:], w_refs[1:]):
        acc = acc + _dot(a_ref[...], w_ref[...])
    o_ref[...] = _layer_norm(DEEPNORM_ALPHA * x_ref[...] + acc, g_ref[...], b_ref[...])


def _outproj_ln(acts, ws, x, g, b, tm):
    t = x.shape[0]
    row = lambda n: pl.BlockSpec((tm, n), lambda i: (i, 0))
    return pl.pallas_call(
        functools.partial(_outproj_ln_kernel, n_in=len(acts)),
        grid=(t // tm,),
        in_specs=[row(a.shape[1]) for a in acts] + [_full(w.shape) for w in ws]
                 + [row(D_MODEL), _full((1, D_MODEL)), _full((1, D_MODEL))],
        out_specs=row(D_MODEL),
        out_shape=jax.ShapeDtypeStruct((t, D_MODEL), F32),
        compiler_params=_params(("parallel",)),
        name="outproj_ln",
    )(*acts, *ws, x, g, b)


S5_LANE_BLOCKS = S5_WIDTH // LANES
S5_TILES_PER_BLOCK = S5_TILES // S5_LANE_BLOCKS


def _s5_kernel(u_ref, h0_ref, bmat_ref, cmat_ref, acoef_ref, dskip_ref, wglu_ref, bglu_ref,
               y_ref, hout_ref, utm, sre, sim, ytm, hst, *, nb, lt):
    tb = pl.program_id(0)

    @pl.when(tb == 0)
    def _():
        hst[...] = h0_ref[...]

    for b in range(nb):
        for q in range(S5_LANE_BLOCKS):
            utm[q, pl.ds(b, lt, stride=nb), :] = u_ref[b, :, q * LANES:(q + 1) * LANES]

    def block_body(q, carry):
        bu = _dot(utm[q].astype(BF16), bmat_ref[q])
        for g in range(S5_TILES_PER_BLOCK):
            sre[g] = bu[:, (2 * g) * LANES:(2 * g + 1) * LANES]
            sim[g] = bu[:, (2 * g + 1) * LANES:(2 * g + 2) * LANES]
        js = [q * S5_TILES_PER_BLOCK + g for g in range(S5_TILES_PER_BLOCK)]
        acs = [acoef_ref[j] for j in js]
        hs = [(hst[j, :, 0:LANES], hst[j, :, LANES:2 * LANES]) for j in js]
        for t in range(lt):
            rows = slice(t * nb, (t + 1) * nb)
            for g in range(S5_TILES_PER_BLOCK):
                ar, ai = acs[g][:, :LANES], acs[g][:, LANES:]
                hr, hi = hs[g]
                nhr = ar * hr - ai * hi + sre[g, rows, :]
                nhi = ar * hi + ai * hr + sim[g, rows, :]
                sre[g, rows, :] = nhr
                sim[g, rows, :] = nhi
                hs[g] = (nhr, nhi)
        for g, j in enumerate(js):
            hst[j, :, 0:LANES] = hs[g][0]
            hst[j, :, LANES:2 * LANES] = hs[g][1]
        st = jnp.concatenate([part for g in range(S5_TILES_PER_BLOCK) for part in (sre[g], sim[g])], axis=-1)
        ytm[q] = _dot(st.astype(BF16), cmat_ref[q])
        return carry

    lax.fori_loop(0, S5_LANE_BLOCKS, block_body, 0)
    y = (jnp.concatenate([ytm[q] for q in range(S5_LANE_BLOCKS)], axis=-1)
         + dskip_ref[...] * jnp.concatenate([utm[q] for q in range(S5_LANE_BLOCKS)], axis=-1))
    y = jax.nn.gelu(y)
    y = y * _sigmoid(_dot(y.astype(BF16), wglu_ref[...]) + bglu_ref[...])
    for q in range(S5_LANE_BLOCKS):
        ytm[q] = y[:, q * LANES:(q + 1) * LANES]
    for b in range(nb):
        for q in range(S5_LANE_BLOCKS):
            y_ref[b, :, q * LANES:(q + 1) * LANES] = ytm[q, pl.ds(b, lt, stride=nb), :].astype(y_ref.dtype)
    hout_ref[...] = hst[...]


def _s5(u3, h0, bmat, cmat, acoef, dskip, wglu, bglu, lt):
    nb, L, _ = u3.shape
    rows = nb * lt
    tpb = S5_TILES_PER_BLOCK
    bmat = bmat.reshape(S5_LANE_BLOCKS, tpb, LANES, 2 * LANES).transpose(0, 2, 1, 3).reshape(
        S5_LANE_BLOCKS, LANES, tpb * 2 * LANES)
    cmat = cmat.reshape(S5_LANE_BLOCKS, tpb * 2 * LANES, LANES)
    acoef = jnp.broadcast_to(acoef[:, :1], (S5_TILES, nb, 2 * LANES))
    return pl.pallas_call(
        functools.partial(_s5_kernel, nb=nb, lt=lt),
        grid=(L // lt,),
        in_specs=[pl.BlockSpec((nb, lt, S5_WIDTH), lambda i: (0, i, 0)),
                  _full(h0.shape), _full(bmat.shape), _full(cmat.shape), _full(acoef.shape),
                  _full(dskip.shape), _full(wglu.shape), _full(bglu.shape)],
        out_specs=[pl.BlockSpec((nb, lt, S5_WIDTH), lambda i: (0, i, 0)), _full(h0.shape)],
        out_shape=[jax.ShapeDtypeStruct((nb, L, S5_WIDTH), BF16), jax.ShapeDtypeStruct(h0.shape, F32)],
        scratch_shapes=[pltpu.VMEM((S5_LANE_BLOCKS, rows, LANES), F32), pltpu.VMEM((tpb, rows, LANES), F32),
                        pltpu.VMEM((tpb, rows, LANES), F32), pltpu.VMEM((S5_LANE_BLOCKS, rows, LANES), F32),
                        pltpu.VMEM(h0.shape, F32)],
        compiler_params=_params(("arbitrary",)),
        name="s5_scan",
    )(u3, h0, bmat, cmat, acoef, dskip, wglu, bglu)


def _s5_weights(a_re, a_im, log_dt, b_re, b_im, c_re, c_im):
    dt = jnp.exp(log_dt)[:, None]
    lr, li = a_re * dt, a_im * dt
    mag = jnp.exp(lr)
    ab_re, ab_im = mag * jnp.cos(li), mag * jnp.sin(li)
    den = a_re * a_re + a_im * a_im
    cf_re = ((ab_re - 1.0) * a_re + ab_im * a_im) / den
    cf_im = (ab_im * a_re - (ab_re - 1.0) * a_im) / den
    bb_re = cf_re[..., None] * b_re - cf_im[..., None] * b_im
    bb_im = cf_re[..., None] * b_im + cf_im[..., None] * b_re
    jj = np.arange(S5_TILES)[:, None, None]
    lg = np.arange(8)[None, :, None]
    gi = np.arange(2)[None, None, :]
    sel = jnp.asarray((lg == 2 * (jj % 4) + gi).astype(np.float32))
    tiles = lambda w: w.reshape(S5_TILES, 2, *w.shape[1:])
    bt = lambda w: jnp.einsum('jlg,jgpn->jlngp', sel, tiles(w)).reshape(S5_TILES, LANES, LANES)
    bmat = jnp.concatenate([bt(bb_re), bt(bb_im)], axis=-1)
    ct = lambda w: jnp.einsum('jlg,jgnp->jgpln', sel, tiles(w)).reshape(S5_TILES, LANES, LANES)
    cmat = jnp.concatenate([ct(c_re), -ct(c_im)], axis=1)
    acoef = jnp.concatenate([ab_re.reshape(S5_TILES, LANES), ab_im.reshape(S5_TILES, LANES)], axis=-1)
    acoef = jnp.broadcast_to(acoef[:, None, :], (S5_TILES, SUBLANES, 2 * LANES))
    return bmat.astype(BF16), cmat.astype(BF16), acoef


def _s5_state_in(h_re, h_im):
    nb = h_re.shape[0]
    h = jnp.concatenate([h_re.reshape(nb, S5_TILES, LANES), h_im.reshape(nb, S5_TILES, LANES)], axis=-1)
    return jnp.transpose(h, (1, 0, 2))


def _s5_state_out(h):
    nb = h.shape[1]
    h = jnp.transpose(h, (1, 0, 2))
    return (h[..., :LANES].reshape(nb, S5_GROUPS, S5_STATE), h[..., LANES:].reshape(nb, S5_GROUPS, S5_STATE))


def _split_bf16(a):
    hi = a.astype(BF16)
    return hi, (a - hi.astype(F32)).astype(BF16)


def _bdot(a, b):
    return lax.dot_general(a, b, (((2,), (1,)), ((0,), (0,))), preferred_element_type=F32)


def _bdot_nt(a, b):
    return lax.dot_general(a, b, (((2,), (2,)), ((0,), (0,))), preferred_element_type=F32)


def _bdot_split(a, b):
    ah, al = _split_bf16(a)
    bh, bl = _split_bf16(b)
    return _bdot(ah, bh) + _bdot(ah, bl) + _bdot(al, bh)


def _unit_lower_inverse(nmat):
    lc = nmat.shape[-1]
    ri = lax.broadcasted_iota(jnp.int32, nmat.shape, 1)
    ci = lax.broadcasted_iota(jnp.int32, nmat.shape, 2)
    base = 16
    dmat = jnp.where(ri // base == ci // base, nmat, 0.0)
    inv = jnp.where(ri == ci, 1.0, 0.0) - dmat
    pw = dmat
    for _ in range(3):
        pw = _bdot_split(pw, pw)
        inv = inv + _bdot_split(inv, pw)
    size = base
    while size < lc:
        off = jnp.where(ri // (2 * size) == ci // (2 * size), jnp.where(ri // size > ci // size, nmat, 0.0), 0.0)
        inv = inv - _bdot_split(_bdot_split(inv, off), inv)
        size *= 2
    return inv


def _gdn_local_kernel(qkv_ref, ba_ref, ctx_ref, cw_ref, p1_ref, p2_ref, ltri_ref,
                      u0_ref, w_ref, qd_ref, kd_ref, attn_ref, g_ref, cout_ref, xpad, *, lc, cp):
    c = pl.program_id(1)
    rb = lc * cp

    @pl.when(c == 0)
    def _():
        xpad[0:SUBLANES, :] = ctx_ref[0]

    xpad[SUBLANES:SUBLANES + rb, :] = qkv_ref[0]
    cw = cw_ref[...]
    conv = (cw[3:4] * xpad[8:8 + rb, :] + cw[2:3] * xpad[7:7 + rb, :]
            + cw[1:2] * xpad[6:6 + rb, :] + cw[0:1] * xpad[5:5 + rb, :])
    tail = xpad[rb:rb + SUBLANES, :]
    xpad[0:SUBLANES, :] = tail
    cout_ref[0] = tail
    a = conv * _sigmoid(conv)

    ba = ba_ref[0]
    beta_all = _sigmoid(ba)
    sp_in = ba + p2_ref[...]
    softplus = jnp.maximum(sp_in, 0.0) + jnp.log(1.0 + jnp.exp(-jnp.abs(sp_in)))
    g_all = p1_ref[...] * softplus
    g_hi = g_all.astype(BF16)
    g_r = g_all - g_hi.astype(F32)
    g_mid = g_r.astype(BF16)
    g_lo = (g_r - g_mid.astype(F32)).astype(BF16)
    lt = ltri_ref[...]
    G = _dot(lt, g_hi) + _dot(lt, g_mid) + _dot(lt, g_lo)
    g_ref[0] = G
    GT = G.T
    pairs = [(h, cc) for h in range(GDN_HEADS) for cc in range(cp)]
    qs, ks, vs, betas, gcols, grows, glasts = [], [], [], [], [], [], []
    for h in range(GDN_HEADS):
        qa = a[:, h * GDN_DK:(h + 1) * GDN_DK]
        ka = a[:, (GDN_HEADS + h) * GDN_DK:(GDN_HEADS + h + 1) * GDN_DK]
        va = a[:, (2 * GDN_HEADS + h) * GDN_DK:(2 * GDN_HEADS + h + 1) * GDN_DK]
        qa = qa * lax.rsqrt(jnp.sum(qa * qa, -1, keepdims=True) + NORM_EPS) * (GDN_DK ** -0.5)
        ka = ka * lax.rsqrt(jnp.sum(ka * ka, -1, keepdims=True) + NORM_EPS)
        for cc in range(cp):
            rows = slice(cc * lc, (cc + 1) * lc)
            qs.append(qa[rows])
            ks.append(ka[rows])
            vs.append(va[rows])
            betas.append(beta_all[rows, h:h + 1])
            gcols.append(G[rows, GDN_HEADS + h:GDN_HEADS + h + 1])
            grows.append(GT[GDN_HEADS + h:GDN_HEADS + h + 1, cc * lc:(cc + 1) * lc])
            glasts.append(GT[GDN_HEADS + h:GDN_HEADS + h + 1, (cc + 1) * lc - 1:(cc + 1) * lc])
    q3, k3, v3 = jnp.stack(qs), jnp.stack(ks), jnp.stack(vs)
    beta3, gcol3 = jnp.stack(betas), jnp.stack(gcols)
    grow3, glast3 = jnp.stack(grows), jnp.stack(glasts)
    shape3 = (len(pairs), lc, lc)
    ri = lax.broadcasted_iota(jnp.int32, shape3, 1)
    ci = lax.broadcasted_iota(jnp.int32, shape3, 2)
    incl = ri >= ci
    dec3 = jnp.where(incl, jnp.exp(jnp.where(incl, gcol3 - grow3, 0.0)), 0.0)
    eg3 = jnp.exp(gcol3)
    kb3 = k3 * beta3
    kbf3 = k3.astype(BF16)
    nmat3 = jnp.where(ri > ci, _bdot_nt(kb3.astype(BF16), kbf3) * dec3, 0.0)
    inv3 = _unit_lower_inverse(nmat3)
    sol3 = _bdot_split(inv3, jnp.concatenate([v3 * beta3, kb3 * eg3], axis=-1))
    w3 = sol3[:, :, GDN_DV:].astype(BF16)
    qd3 = (q3 * eg3).astype(BF16)
    kd3 = (k3 * jnp.exp(glast3 - gcol3)).astype(BF16)
    attn3 = (_bdot_nt(q3.astype(BF16), kbf3) * dec3).astype(BF16)
    for i, (h, cc) in enumerate(pairs):
        rows = slice(cc * lc, (cc + 1) * lc)
        cols = slice(h * GDN_DV, (h + 1) * GDN_DV)
        u0_ref[0, rows, cols] = sol3[i, :, :GDN_DV]
        w_ref[0, rows, cols] = w3[i]
        qd_ref[0, rows, cols] = qd3[i]
        kd_ref[0, rows, cols] = kd3[i]
        attn_ref[0, rows, h * lc:(h + 1) * lc] = attn3[i]


def _gdn_seq_kernel(u0_ref, w_ref, qd_ref, kd_ref, attn_ref, g_ref, z_ref, s0_ref, nw_ref,
                    y_ref, sout_ref, S, *, lc, nbb):
    c = pl.program_id(1)

    @pl.when(c == 0)
    def _():
        S[...] = s0_ref[...]

    nw = nw_ref[...]
    pairs = [(bb, h) for bb in range(nbb) for h in range(GDN_HEADS)]
    hcols = lambda h: slice(h * GDN_DV, (h + 1) * GDN_DV)
    stack = lambda f: jnp.stack([f(bb, h) for bb, h in pairs])
    dlast = jnp.exp(g_ref[:, lc - 1:lc, :])
    S3 = S[...].reshape(len(pairs), GDN_DK, GDN_DV)
    wq3 = stack(lambda bb, h: jnp.concatenate([w_ref[bb, :, hcols(h)], qd_ref[bb, :, hcols(h)]], axis=0))
    r3 = _bdot(wq3, S3.astype(BF16))
    ub3 = (stack(lambda bb, h: u0_ref[bb, :, hcols(h)]) - r3[:, :lc]).astype(BF16)
    o3 = r3[:, lc:] + _bdot(stack(lambda bb, h: attn_ref[bb, :, h * lc:(h + 1) * lc]), ub3)
    d3 = stack(lambda bb, h: dlast[bb, :, GDN_HEADS + h:GDN_HEADS + h + 1])
    kd3 = stack(lambda bb, h: kd_ref[bb, :, hcols(h)])
    kdu3 = lax.dot_general(kd3, ub3, (((1,), (1,)), ((0,), (0,))), preferred_element_type=F32)
    S[...] = (d3 * S3 + kdu3).reshape(S.shape)
    o3 = o3 * lax.rsqrt(jnp.mean(o3 * o3, -1, keepdims=True) + NORM_EPS) * nw
    for i, (bb, h) in enumerate(pairs):
        zh = z_ref[bb, :, hcols(h)]
        y_ref[bb, :, hcols(h)] = (o3[i] * (zh * _sigmoid(zh))).astype(y_ref.dtype)

    @pl.when(c == pl.num_programs(1) - 1)
    def _():
        sout_ref[...] = S[...]


def _gdn(qkv3, z3, ba3, ctx8, s0, cw, p1, p2, nw, lc, cp, nbb):
    nb, L, _ = qkv3.shape
    rb = lc * cp
    hd = GDN_HEADS * GDN_DV
    ltri = jnp.asarray(np.kron(np.eye(cp, dtype=np.float32), np.tril(np.ones((lc, lc), np.float32)))).astype(BF16)
    blk = lambda n: pl.BlockSpec((1, rb, n), lambda b, c: (b, c, 0))
    per_b = lambda shape: pl.BlockSpec((1,) + shape, lambda b, c: (b,) + (0,) * len(shape))
    cst = lambda shape: pl.BlockSpec(shape, lambda b, c: (0,) * len(shape))
    sds = lambda n, dt: jax.ShapeDtypeStruct((nb, L, n), dt)
    u0, w, qd, kd, attn, G, cout = pl.pallas_call(
        functools.partial(_gdn_local_kernel, lc=lc, cp=cp),
        grid=(nb, L // rb),
        in_specs=[blk(GDN_QKV), blk(LANES), per_b((SUBLANES, GDN_QKV)), cst(cw.shape), cst(p1.shape),
                  cst(p2.shape), cst(ltri.shape)],
        out_specs=[blk(hd), blk(hd), blk(hd), blk(hd), blk(GDN_HEADS * lc), blk(LANES),
                   per_b((SUBLANES, GDN_QKV))],
        out_shape=[sds(hd, F32), sds(hd, BF16), sds(hd, BF16), sds(hd, BF16), sds(GDN_HEADS * lc, BF16),
                   sds(LANES, F32), jax.ShapeDtypeStruct((nb, SUBLANES, GDN_QKV), F32)],
        scratch_shapes=[pltpu.VMEM((rb + SUBLANES, GDN_QKV), F32)],
        compiler_params=_params(("parallel", "arbitrary")),
        name="gdn_local",
    )(qkv3, ba3, ctx8, cw, p1, p2, ltri)
    sblk = lambda n: pl.BlockSpec((nbb, lc, n), lambda b, c: (b, c, 0))
    state = pl.BlockSpec((nbb, GDN_HEADS, GDN_DK, GDN_DV), lambda b, c: (b, 0, 0, 0))
    y, s_new = pl.pallas_call(
        functools.partial(_gdn_seq_kernel, lc=lc, nbb=nbb),
        grid=(nb // nbb, L // lc),
        in_specs=[sblk(hd), sblk(hd), sblk(hd), sblk(hd), sblk(GDN_HEADS * lc), sblk(LANES), sblk(hd),
                  state, cst(nw.shape)],
        out_specs=[sblk(hd), state],
        out_shape=[sds(hd, BF16), jax.ShapeDtypeStruct(s0.shape, F32)],
        scratch_shapes=[pltpu.VMEM((nbb, GDN_HEADS, GDN_DK, GDN_DV), F32)],
        compiler_params=_params(("parallel", "arbitrary")),
        name="gdn_seq",
    )(u0, w, qd, kd, attn, G, z3, s0, nw)
    return y, s_new, cout


def _ret_kernel(q_ref, k_ref, v_ref, g_ref, r0_ref, dec_ref, qs_ref, ks_ref, cd_ref, o_ref, rout_ref, R,
                *, nbb):
    c = pl.program_id(1)

    @pl.when(c == 0)
    def _():
        R[...] = r0_ref[...]

    pairs = [(bb, h) for bb in range(nbb) for h in range(RET_HEADS)]
    stack = lambda f: jnp.stack([f(bb, h) for bb, h in pairs])
    kcols = lambda h: slice(h * RET_DK, (h + 1) * RET_DK)
    vcols = lambda h: slice(h * RET_DV, (h + 1) * RET_DV)
    q3 = stack(lambda bb, h: q_ref[bb, :, kcols(h)])
    k3 = stack(lambda bb, h: k_ref[bb, :, kcols(h)])
    v3 = stack(lambda bb, h: v_ref[bb, :, vcols(h)])
    dec3 = stack(lambda bb, h: dec_ref[h])
    qs3 = stack(lambda bb, h: qs_ref[h])
    ks3 = stack(lambda bb, h: ks_ref[h])
    cd3 = stack(lambda bb, h: cd_ref[h])
    R3 = R[...].reshape(len(pairs), RET_DK, RET_DV)
    s3 = _bdot_nt(q3, k3) * dec3
    o3 = _bdot(s3.astype(BF16), v3) + _bdot(q3, R3.astype(BF16)) * qs3
    kv3 = lax.dot_general((k3.astype(F32) * ks3).astype(BF16), v3, (((1,), (1,)), ((0,), (0,))),
                          preferred_element_type=F32)
    R[...] = (cd3 * R3 + kv3).reshape(R.shape)
    mu = jnp.mean(o3, -1, keepdims=True)
    d3 = o3 - mu
    var = jnp.mean(d3 * d3, -1, keepdims=True)
    on3 = d3 * lax.rsqrt(var + LN_EPS)
    for i, (bb, h) in enumerate(pairs):
        gt = g_ref[bb, :, vcols(h)].astype(F32)
        o_ref[bb, :, vcols(h)] = (gt * _sigmoid(gt) * on3[i]).astype(o_ref.dtype)

    @pl.when(c == pl.num_programs(1) - 1)
    def _():
        rout_ref[...] = R[...]


def _retention(proj3, r0, lc, nbb):
    nb, L, _ = proj3.shape
    log_g = np.log(1.0 - 2.0 ** (-5.0 - np.arange(RET_HEADS, dtype=np.float64)))
    idx = np.arange(lc, dtype=np.float64)
    dec = np.exp(log_g[:, None, None] * np.abs(idx[:, None] - idx[None, :])).astype(np.float32)
    qs = np.exp(log_g[:, None] * (idx + 1.0)).astype(np.float32)[..., None]
    ks = np.exp(log_g[:, None] * (lc - 1.0 - idx)).astype(np.float32)[..., None]
    cdec = np.exp(log_g * lc).astype(np.float32)[:, None, None]
    nqk = RET_HEADS * RET_DK
    nv = RET_HEADS * RET_DV
    cst = lambda shape: pl.BlockSpec(shape, lambda b, c: (0,) * len(shape))
    state = pl.BlockSpec((nbb, RET_HEADS, RET_DK, RET_DV), lambda b, c: (b, 0, 0, 0))
    return pl.pallas_call(
        functools.partial(_ret_kernel, nbb=nbb),
        grid=(nb // nbb, L // lc),
        in_specs=[pl.BlockSpec((nbb, lc, nqk), lambda b, c: (b, c, 0)),
                  pl.BlockSpec((nbb, lc, nqk), lambda b, c: (b, c, 1)),
                  pl.BlockSpec((nbb, lc, nv), lambda b, c: (b, c, 1)),
                  pl.BlockSpec((nbb, lc, nv), lambda b, c: (b, c, 2)),
                  state, cst(dec.shape), cst(qs.shape), cst(ks.shape), cst(cdec.shape)],
        out_specs=[pl.BlockSpec((nbb, lc, nv), lambda b, c: (b, c, 0)), state],
        out_shape=[jax.ShapeDtypeStruct((nb, L, nv), BF16), jax.ShapeDtypeStruct(r0.shape, F32)],
        scratch_shapes=[pltpu.VMEM((nbb, RET_HEADS, RET_DK, RET_DV), F32)],
        compiler_params=_params(("parallel", "arbitrary")),
        name="retention",
    )(proj3, proj3, proj3, proj3, r0, jnp.asarray(dec), jnp.asarray(qs), jnp.asarray(ks), jnp.asarray(cdec))


RUN_ALIGN = SUBLANES
RUN_PIECE = 64
MOE_VMEM_LIMIT = 58 * 1024 * 1024


def _moe_params(sem):
    return pltpu.CompilerParams(dimension_semantics=sem, vmem_limit_bytes=MOE_VMEM_LIMIT)


def _tile_cap(tm):
    rows = TOP_K * tm + N_EXPERTS * (RUN_ALIGN - 1)
    return -(-rows // LANES) * LANES


def _route_kernel(x_ref, rw_ref, rb_ref, lst_ref, ust_ref, pos_ref, gt_ref, cnt_ref, xs_ref):
    tm = x_ref.shape[0]
    cap = xs_ref.shape[1]
    x = x_ref[...]
    lane = lax.broadcasted_iota(jnp.int32, (tm, LANES), 1)
    lane_f = lane.astype(F32)
    xh, xl = _split_bf16(x)
    wh, wl = _split_bf16(rw_ref[...])
    logits = _dot(xh, wh) + _dot(xh, wl) + _dot(xl, wh) + rb_ref[...]
    logits = jnp.where(lane < N_EXPERTS, logits, -jnp.inf)
    vals, hots = [], []
    for _ in range(TOP_K):
        m = jnp.max(logits, -1, keepdims=True)
        first = jnp.min(jnp.where(logits == m, lane_f, float(LANES)), -1, keepdims=True)
        hot = lane_f == first
        vals.append(m)
        hots.append(hot)
        logits = jnp.where(hot, -jnp.inf, logits)
    es = [jnp.exp(v - vals[0]) for v in vals]
    den = es[0] + es[1] + es[2] + es[3]
    multi = jnp.zeros((tm, LANES), F32)
    for hot in hots:
        multi = multi + hot.astype(F32)
    counts = jnp.sum(multi, 0, keepdims=True)
    units = jnp.floor((counts + (RUN_ALIGN - 1)) * (1.0 / RUN_ALIGN))
    offs = _dot(jnp.broadcast_to(units, (SUBLANES, LANES)).astype(BF16), ust_ref[...])[0:1] * float(RUN_ALIGN)
    before = _dot(lst_ref[...], multi.astype(BF16))
    slot = offs + before
    pos = jnp.zeros((tm, LANES), F32)
    gt = jnp.zeros((tm, LANES), F32)
    for kk in range(TOP_K):
        pos = jnp.where(lane == kk, jnp.sum(jnp.where(hots[kk], slot, 0.0), -1, keepdims=True), pos)
        gt = jnp.where(lane == kk, es[kk] / den, gt)
    pos = pos.astype(jnp.int32)
    pos_ref[...] = pos
    gt_ref[...] = gt
    cnt_ref[0] = counts
    pos_t = pos.T
    row = lax.broadcasted_iota(jnp.int32, (cap, tm), 0)
    sel = jnp.zeros((cap, tm), F32)
    for kk in range(TOP_K):
        sel = sel + jnp.where(row == pos_t[kk:kk + 1, :], 1.0, 0.0)
    xs_ref[0] = _dot(sel.astype(BF16), xh)


def _route(x, rw, rb, tm):
    t = x.shape[0]
    nt = t // tm
    cap = _tile_cap(tm)
    lst = jnp.asarray(np.tril(np.ones((tm, tm), np.float32), -1)).astype(BF16)
    ust = jnp.asarray(np.triu(np.ones((LANES, LANES), np.float32), 1)).astype(BF16)
    row = lambda n: pl.BlockSpec((tm, n), lambda i: (i, 0))
    return pl.pallas_call(
        _route_kernel,
        grid=(nt,),
        in_specs=[row(D_MODEL), _full(rw.shape), _full(rb.shape), _full(lst.shape), _full(ust.shape)],
        out_specs=[row(LANES), row(LANES), pl.BlockSpec((1, 1, LANES), lambda i: (i, 0, 0)),
                   pl.BlockSpec((1, cap, D_MODEL), lambda i: (i, 0, 0))],
        out_shape=[jax.ShapeDtypeStruct((t, LANES), jnp.int32), jax.ShapeDtypeStruct((t, LANES), F32),
                   jax.ShapeDtypeStruct((nt, 1, LANES), F32), jax.ShapeDtypeStruct((nt, cap, D_MODEL), F32)],
        compiler_params=_moe_params(("parallel",)),
        name="moe_route",
    )(x, rw, rb, lst, ust)


def _expert_kernel(be_ref, nu_ref, ilo_ref, rows_ref, gs_ref, n8_ref, lo_ref, *refs, blk, tiles):
    ng = len(tiles)
    xs_hbms = refs[:ng]
    w1_ref, b1_ref, w2_ref, b2_ref = refs[ng:ng + 4]
    ys_hbms = refs[ng + 4:2 * ng + 4]
    xbuf, ybuf, w1b, w2b, in_sem, out_sem = refs[2 * ng + 4:]
    nt = sum(tiles)
    firsts = [sum(tiles[:g]) for g in range(ng)]
    j = pl.program_id(0)
    nu = nu_ref[0]

    def per_group(i, fn):
        for g in range(ng):
            @pl.when((i >= firsts[g]) & (i < firsts[g] + tiles[g]))
            def _(g=g):
                fn(g, i - firsts[g])

    def for_each_run(jb, fn):
        e = be_ref[jb]
        base = jb * blk

        def cond(i):
            return (i < nt) & (gs_ref[e * nt + jnp.minimum(i, nt - 1)] < base + blk)

        def body(i):
            g0 = gs_ref[e * nt + i]
            first = jnp.maximum(g0, base)
            last = jnp.minimum(g0 + n8_ref[e * nt + i], base + blk)
            fn(i, lo_ref[e * nt + i] + (first - g0), first - base, last - first)
            return i + 1

        lax.while_loop(cond, body, ilo_ref[jb])

    def pieces(length, fn):
        nbig = length // RUN_PIECE

        def big(q, carry):
            fn(q * RUN_PIECE, RUN_PIECE)
            return carry

        lax.fori_loop(0, nbig, big, 0)
        size = RUN_PIECE // 2
        while size >= RUN_ALIGN:
            @pl.when((length & size) != 0)
            def _(size=size):
                fn(nbig * RUN_PIECE + (length & (RUN_PIECE - 2 * size)), size)
            size //= 2

    def aligned(v, size):
        return pl.ds(pl.multiple_of(v, RUN_ALIGN), size)

    def copy_in(jb, slot):
        def run(i, src, dst, length):
            def group(g, ig):
                def piece(off, size):
                    pltpu.make_async_copy(xs_hbms[g].at[ig, aligned(src + off, size)],
                                          xbuf.at[slot, aligned(dst + off, size)], in_sem.at[slot]).start()
                pieces(length, piece)
            per_group(i, group)
        for_each_run(jb, run)

    def copy_out(jb, slot):
        def run(i, src, dst, length):
            def group(g, ig):
                def piece(off, size):
                    pltpu.make_async_copy(ybuf.at[slot, aligned(dst + off, size)],
                                          ys_hbms[g].at[ig, aligned(src + off, size)], out_sem.at[slot]).start()
                pieces(length, piece)
            per_group(i, group)
        for_each_run(jb, run)

    def wait_rows(sem, nrows):
        size = blk
        while size >= RUN_ALIGN:
            @pl.when((nrows & size) != 0)
            def _(size=size):
                pltpu.make_async_copy(xbuf.at[1, pl.ds(0, size)], xbuf.at[0, pl.ds(0, size)], sem).wait()
            size //= 2

    @pl.when(j < nu)
    def _():
        slot = j % 2

        @pl.when(j == 0)
        def _():
            xbuf[...] = jnp.zeros_like(xbuf)
            copy_in(0, 0)

        wait_rows(in_sem.at[slot], rows_ref[j])

        @pl.when(j + 1 < nu)
        def _():
            copy_in(j + 1, 1 - slot)

        @pl.when(j >= 2)
        def _():
            wait_rows(out_sem.at[slot], rows_ref[jnp.maximum(j - 2, 0)])

        @pl.when((j == 0) | (be_ref[j] != be_ref[jnp.maximum(j - 1, 0)]))
        def _():
            w1b[...] = w1_ref[0, 0].astype(BF16)
            w2b[...] = w2_ref[0, 0].astype(BF16)

        h = _dot(xbuf[slot].astype(BF16), w1b[...]) + b1_ref[0]
        glu = jnp.minimum(h[:, :D_FF], SWIGLU_LIMIT)
        lin = jnp.clip(h[:, D_FF:], -SWIGLU_LIMIT, SWIGLU_LIMIT)
        act = glu * _sigmoid(SWIGLU_ALPHA * glu) * (lin + 1.0)
        ybuf[slot] = _dot(act.astype(BF16), w2b[...]) + b2_ref[0]
        copy_out(j, slot)

        @pl.when(j == nu - 1)
        def _():
            wait_rows(out_sem.at[slot], rows_ref[j])

            @pl.when(j >= 1)
            def _():
                wait_rows(out_sem.at[1 - slot], rows_ref[jnp.maximum(j - 1, 0)])


def _experts(tables, xs_list, w1, b1, w2, b2, layer, blk):
    ng = len(xs_list)
    n_blk = tables[0].shape[0]
    wspec = lambda shape: pl.BlockSpec((1, 1) + shape, lambda j, be, *_: (layer, be[j], 0, 0))
    bspec = lambda n: pl.BlockSpec((1, 1, n), lambda j, be, *_: (be[j], 0, 0))
    hbm = pl.BlockSpec(memory_space=pl.ANY)
    grid_spec = pltpu.PrefetchScalarGridSpec(
        num_scalar_prefetch=len(tables),
        grid=(n_blk,),
        in_specs=[hbm] * ng + [wspec((D_MODEL, 2 * D_FF)), bspec(2 * D_FF), wspec((D_FF, D_MODEL)), bspec(D_MODEL)],
        out_specs=[hbm] * ng,
        scratch_shapes=[pltpu.VMEM((2, blk, D_MODEL), F32), pltpu.VMEM((2, blk, D_MODEL), F32),
                        pltpu.VMEM((D_MODEL, 2 * D_FF), BF16), pltpu.VMEM((D_FF, D_MODEL), BF16),
                        pltpu.SemaphoreType.DMA((2,)), pltpu.SemaphoreType.DMA((2,))],
    )
    return pl.pallas_call(
        functools.partial(_expert_kernel, blk=blk, tiles=tuple(xs.shape[0] for xs in xs_list)),
        grid_spec=grid_spec,
        out_shape=[jax.ShapeDtypeStruct(xs.shape, F32) for xs in xs_list],
        input_output_aliases={len(tables) + g: g for g in range(ng)},
        compiler_params=_moe_params(("arbitrary",)),
        name="moe_experts",
    )(*tables, *xs_list, w1, b1, w2, b2)


def _combine_kernel(ys_ref, pos_ref, gt_ref, x_ref, p_ref, g_ref, b_ref, plew_ref, gatew_ref, o_ref):
    tm = x_ref.shape[0]
    cap = ys_ref.shape[1]
    pos = pos_ref[...]
    gt = gt_ref[...]
    col = lax.broadcasted_iota(jnp.int32, (tm, cap), 1)
    sel = jnp.zeros((tm, cap), F32)
    for kk in range(TOP_K):
        sel = sel + jnp.where(col == pos[:, kk:kk + 1], gt[:, kk:kk + 1], 0.0)
    y = _dot(sel.astype(BF16), ys_ref[0].astype(BF16))
    x2 = _layer_norm(DEEPNORM_ALPHA * x_ref[...] + y, g_ref[...], b_ref[...])
    pp = _dot(p_ref[...].astype(BF16), plew_ref[...])
    gg = _sigmoid(_dot(x2.astype(BF16), gatew_ref[...]))
    o_ref[...] = x2 + pp * gg


def _combine(ys, pos, gt, x, p, g, b, plew, gatew, tm):
    t = x.shape[0]
    cap = ys.shape[1]
    row = lambda n: pl.BlockSpec((tm, n), lambda i: (i, 0))
    return pl.pallas_call(
        _combine_kernel,
        grid=(t // tm,),
        in_specs=[pl.BlockSpec((1, cap, D_MODEL), lambda i: (i, 0, 0)), row(LANES), row(LANES),
                  row(D_MODEL), row(PLE_DIM), _full((1, D_MODEL)), _full((1, D_MODEL)),
                  _full(plew.shape), _full(gatew.shape)],
        out_specs=row(D_MODEL),
        out_shape=jax.ShapeDtypeStruct((t, D_MODEL), F32),
        compiler_params=_moe_params(("parallel",)),
        name="moe_combine",
    )(ys, pos, gt, x, p, g, b, plew, gatew)


def _moe_tables(cnt, t, blk):
    nt = cnt.shape[0]
    n = cnt[:, 0, :N_EXPERTS].astype(jnp.int32)
    n8 = (n + RUN_ALIGN - 1) // RUN_ALIGN * RUN_ALIGN
    lo = jnp.cumsum(n8, axis=1) - n8
    rows_e = jnp.sum(n8, axis=0)
    padded = (rows_e + blk - 1) // blk * blk
    pend = jnp.cumsum(padded)
    pstart = pend - padded
    gstart = pstart[None, :] + jnp.cumsum(n8, axis=0) - n8
    n_blk = -(-(TOP_K * t + nt * N_EXPERTS * (RUN_ALIGN - 1) + N_EXPERTS * (blk - 1)) // blk)
    n_used = (pend[-1] // blk).astype(jnp.int32).reshape(1)
    blk_start = jnp.arange(n_blk, dtype=jnp.int32) * blk
    blk_e = jnp.minimum(jnp.sum((pend[None, :] <= blk_start[:, None]).astype(jnp.int32), axis=1), N_EXPERTS - 1)
    run_end = (gstart + n8)[:, blk_e]
    ilo = jnp.sum((run_end <= blk_start[None, :]).astype(jnp.int32), axis=0)
    rows_b = jnp.clip((pstart + rows_e)[blk_e] - blk_start, 0, blk)
    flat = lambda a: a.T.reshape(-1).astype(jnp.int32)
    i32 = lambda a: a.astype(jnp.int32)
    return (i32(blk_e), n_used, i32(ilo), i32(rows_b), flat(gstart), flat(n8), flat(lo))


def _moe_ple(xs_in, ps, rw, rb, w1, b1, w2, b2, layer, g, b, plew, gatew, tms, blk):
    routed = [_route(x, rw, rb, tm) for x, tm in zip(xs_in, tms)]
    cnt = jnp.concatenate([r[2] for r in routed], axis=0)
    tables = _moe_tables(cnt, sum(x.shape[0] for x in xs_in), blk)
    ys = _experts(tables, [r[3] for r in routed], w1, b1, w2, b2, layer, blk)
    return [_combine(y, r[0], r[1], x, p, g, b, plew, gatew, tm)
            for y, r, x, p, tm in zip(ys, routed, xs_in, ps, tms)]


_ROT_PERM = np.concatenate([np.arange(0, RET_DK, 2), np.arange(1, RET_DK, 2)])
_ROT_INV = np.argsort(_ROT_PERM)


def _lane_row(vals, offset):
    row = jnp.zeros((1, LANES), F32)
    return row.at[0, offset:offset + vals.shape[0]].set(vals)


def _even_mixer(x, nb, L, s5_re, s5_im, gdn_s, conv_s, W, tm):
    t = nb * L
    lc = L if L <= CHUNK else CHUNK
    u, qkv, z, ba = _proj_even(x, W['wu'], W['wqkv'], W['wz'], W['wba'], tm)
    yA3, h_new = _s5(u.reshape(nb, L, S5_WIDTH), _s5_state_in(s5_re[0].astype(F32), s5_im[0].astype(F32)),
                     W['bmat'], W['cmat'], W['acoef'], W['dskip'], W['wglu'], W['bglu'], lc)
    new_re, new_im = _s5_state_out(h_new)
    ctx8 = jnp.concatenate([jnp.zeros((nb, SUBLANES - (GDN_CONV - 1), GDN_QKV), F32), conv_s[0].astype(F32)], axis=1)
    yB3, new_gdn, cout = _gdn(qkv.reshape(nb, L, GDN_QKV), z.reshape(nb, L, -1), ba.reshape(nb, L, LANES),
                              ctx8, gdn_s[0].astype(F32), W['convw'], W['p1'], W['p2'], W['normw'], lc,
                              cp=min(4, L // lc), nbb=4)
    new_conv = cout[:, SUBLANES - (GDN_CONV - 1):, :]
    x = _outproj_ln([yA3.reshape(t, -1), yB3.reshape(t, -1)], [W['wout_a'], W['wout_b']], x,
                    W['ln1_g'][0], W['ln1_b'][0], tm)
    return x, new_re[None], new_im[None], new_gdn[None], new_conv[None]


def _odd_mixer(x, nb, L, ret_s, pos0, W, tm):
    t = nb * L
    lc = L if L <= CHUNK else CHUNK
    pos = pos0 + jnp.arange(L, dtype=F32)
    freq = 1.0 / (ROPE_BASE ** jnp.linspace(0.0, 1.0, RET_DK // 2, dtype=F32))
    ang = pos[:, None] * freq[None]
    cos = jnp.broadcast_to(jnp.cos(ang)[None], (nb, L, RET_DK // 2)).reshape(t, RET_DK // 2)
    sin = jnp.broadcast_to(jnp.sin(ang)[None], (nb, L, RET_DK // 2)).reshape(t, RET_DK // 2)
    proj = _proj_odd(x, W['win_odd'], cos, sin, tm)
    r0 = ret_s[0].astype(F32)[:, :, _ROT_PERM, :]
    o3, r_new = _retention(proj.reshape(nb, L, -1), r0, lc, nbb=2)
    new_ret = r_new[:, :, _ROT_INV, :]
    x = _outproj_ln([o3.reshape(t, -1)], [W['wout_odd']], x, W['ln1_g'][1], W['ln1_b'][1], tm)
    return x, new_ret[None]


def kernel(x_prompt, x_sample, state_s5_re, state_s5_im, state_gdn, state_gdn_conv, state_ret, p_prompt, p_sample, w_in_even, s5_a_re, s5_a_im, s5_log_dt, s5_b_re, s5_b_im, s5_c_re, s5_c_im, s5_d, s5_w_glu, s5_b_glu, gdn_conv_w, gdn_a_log, gdn_dt_bias, gdn_norm_w, w_out_even, w_in_odd, w_out_odd, ln1_g, ln1_b, ln2_g, ln2_b, router_w, router_b, moe_w1, moe_b1, moe_w2, moe_b2, ple_w, ple_gate_w):
    o1 = S5_WIDTH
    o2 = o1 + GDN_QKV
    o3 = o2 + GDN_HEADS * GDN_DV
    win = w_in_even[0]
    bmat, cmat, acoef = _s5_weights(s5_a_re[0], s5_a_im[0], s5_log_dt[0], s5_b_re[0], s5_b_im[0],
                                    s5_c_re[0], s5_c_im[0])
    wodd = w_in_odd[0]
    nk = RET_HEADS * RET_DK
    perm_cols = lambda w: w.reshape(D_MODEL, RET_HEADS, RET_DK)[:, :, _ROT_PERM].reshape(D_MODEL, nk)
    W = dict(
        wu=win[:, :o1].astype(BF16), wqkv=win[:, o1:o2].astype(BF16), wz=win[:, o2:o3].astype(BF16),
        wba=jnp.pad(win[:, o3:], ((0, 0), (0, LANES - 2 * GDN_HEADS))).astype(BF16),
        bmat=bmat, cmat=cmat, acoef=acoef, dskip=s5_d[0][None], wglu=s5_w_glu[0].astype(BF16),
        bglu=s5_b_glu[0][None], convw=gdn_conv_w[0],
        p1=_lane_row(-jnp.exp(gdn_a_log[0]), GDN_HEADS), p2=_lane_row(gdn_dt_bias[0], GDN_HEADS),
        normw=gdn_norm_w[0][None],
        wout_a=w_out_even[0][:S5_WIDTH].astype(BF16), wout_b=w_out_even[0][S5_WIDTH:].astype(BF16),
        win_odd=jnp.concatenate([perm_cols(wodd[:, :nk]), perm_cols(wodd[:, nk:2 * nk]), wodd[:, 2 * nk:]],
                                axis=1).astype(BF16),
        wout_odd=w_out_odd[0].astype(BF16),
        ln1_g=ln1_g[:, None], ln1_b=ln1_b[:, None], ln2_g=ln2_g[:, None], ln2_b=ln2_b[:, None],
        rw=jnp.pad(router_w, ((0, 0), (0, 0), (0, LANES - N_EXPERTS))),
        rb=jnp.pad(router_b, ((0, 0), (0, LANES - N_EXPERTS)))[:, None],
        w1=moe_w1, b1=moe_b1[:, :, None], w2=moe_w2, b2=moe_b2[:, :, None],
        plew=ple_w.astype(BF16), gatew=ple_gate_w.astype(BF16),
    )
    bp, lp, _ = x_prompt.shape
    bs, ls, _ = x_sample.shape
    zeros = lambda *s: jnp.zeros(s, F32)
    shapes = [(bp, lp), (bs, ls)]
    tms = [512, 128]
    ps = [p_prompt, p_sample]
    xs = [x_prompt.reshape(bp * lp, D_MODEL).astype(F32), x_sample.reshape(bs * ls, D_MODEL).astype(F32)]
    even_states = [(zeros(1, bp, S5_GROUPS, S5_STATE), zeros(1, bp, S5_GROUPS, S5_STATE),
                    zeros(1, bp, GDN_HEADS, GDN_DK, GDN_DV), zeros(1, bp, GDN_CONV - 1, GDN_QKV)),
                   (state_s5_re, state_s5_im, state_gdn, state_gdn_conv)]
    ret_states = [zeros(1, bp, RET_HEADS, RET_DK, RET_DV), state_ret]
    pos0s = [0.0, float(PAST_LEN)]

    def moe(xs, layer):
        return _moe_ple(xs, [p[layer].reshape(-1, PLE_DIM) for p in ps], W['rw'][layer], W['rb'][layer],
                        W['w1'], W['b1'][layer], W['w2'], W['b2'][layer], layer, W['ln2_g'][layer],
                        W['ln2_b'][layer], W['plew'][layer], W['gatew'][layer], tms, blk=512)

    even = [_even_mixer(x, nb, L, *st, W, tm) for x, (nb, L), st, tm in zip(xs, shapes, even_states, tms)]
    xs = moe([e[0] for e in even], 0)
    odd = [_odd_mixer(x, nb, L, st, pos0, W, tm)
           for x, (nb, L), st, pos0, tm in zip(xs, shapes, ret_states, pos0s, tms)]
    xs = moe([o[0] for o in odd], 1)
    dp = x_prompt.dtype
    y_p, y_s = xs[0].reshape(bp, lp, D_MODEL), xs[1].reshape(bs, ls, D_MODEL)
    (_, p_re, p_im, p_gdn, p_conv), (_, s_re, s_im, s_gdn, s_conv) = even
    p_ret, s_ret = odd[0][1], odd[1][1]
    return (y_p.astype(dp), y_s.astype(x_sample.dtype),
            p_re.astype(dp), p_im.astype(dp), p_gdn.astype(dp), p_conv.astype(dp), p_ret.astype(dp),
            s_re.astype(state_s5_re.dtype), s_im.astype(state_s5_im.dtype), s_gdn.astype(state_gdn.dtype),
            s_conv.astype(state_gdn_conv.dtype), s_ret.astype(state_ret.dtype))
```

```python
import functools
import math

import jax
import jax.numpy as jnp
import numpy as np
from jax import lax
from jax.experimental import pallas as pl
from jax.experimental.pallas import tpu as pltpu

F32 = jnp.float32
BF16 = jnp.bfloat16
HIGHEST = lax.Precision.HIGHEST

D_MODEL = 1024
CHUNK = 64
S5_WIDTH = 512
S5_GROUP = 16
S5_GROUPS = 32
S5_STATE = 64
S5_TILES = 16
GDN_HEADS = 4
GDN_DK = 128
GDN_DV = 128
GDN_CONV = 4
GDN_QKV = 1536
RET_HEADS = 4
RET_DK = 256
RET_DV = 512
ROPE_BASE = 10000.0
N_EXPERTS = 32
TOP_K = 4
D_FF = 1024
SWIGLU_LIMIT = 7.0
SWIGLU_ALPHA = 1.702
PLE_DIM = 256
DEPTH = 2
PAST_LEN = 1024
DEEPNORM_ALPHA = (2 * DEPTH) ** 0.25
LN_EPS = 1e-5
NORM_EPS = 1e-6

LANES = 128
SUBLANES = 8
VMEM_LIMIT = 48 * 1024 * 1024

def _params(sem):
    return pltpu.CompilerParams(dimension_semantics=sem, vmem_limit_bytes=VMEM_LIMIT)


def _dot(a, b):
    return jnp.dot(a, b, preferred_element_type=F32)


def _dot_hi(a, b):
    return jnp.dot(a, b, preferred_element_type=F32, precision=HIGHEST)


def _dot_nt(a, b, precision=None):
    return lax.dot_general(a, b, (((1,), (1,)), ((), ())), preferred_element_type=F32,
                           precision=precision)


def _dot_tn(a, b):
    return lax.dot_general(a, b, (((0,), (0,)), ((), ())), preferred_element_type=F32)


def _sigmoid(x):
    return 1.0 / (1.0 + jnp.exp(-x))


def _full(shape):
    nd = len(shape)
    return pl.BlockSpec(shape, lambda *_: (0,) * nd)


def _proj_even_kernel(x_ref, wu_ref, wqkv_ref, wz_ref, wba_ref, u_ref, qkv_ref, z_ref, ba_ref):
    xb = x_ref[...].astype(BF16)
    u_ref[...] = _dot(xb, wu_ref[...])
    qkv_ref[...] = _dot(xb, wqkv_ref[...])
    z_ref[...] = _dot(xb, wz_ref[...])
    ba_ref[...] = _dot(xb, wba_ref[...])


def _proj_even(x, wu, wqkv, wz, wba, tm):
    t = x.shape[0]
    row = lambda n: pl.BlockSpec((tm, n), lambda i: (i, 0))
    return pl.pallas_call(
        _proj_even_kernel,
        grid=(t // tm,),
        in_specs=[row(D_MODEL), _full(wu.shape), _full(wqkv.shape), _full(wz.shape), _full(wba.shape)],
        out_specs=[row(S5_WIDTH), row(GDN_QKV), row(GDN_HEADS * GDN_DV), row(LANES)],
        out_shape=[jax.ShapeDtypeStruct((t, S5_WIDTH), F32), jax.ShapeDtypeStruct((t, GDN_QKV), F32),
                   jax.ShapeDtypeStruct((t, GDN_HEADS * GDN_DV), F32), jax.ShapeDtypeStruct((t, LANES), F32)],
        compiler_params=_params(("parallel",)),
        name="proj_even",
    )(x, wu, wqkv, wz, wba)


def _proj_odd_kernel(x_ref, w_ref, cos_ref, sin_ref, o_ref):
    j = pl.program_id(0)
    acc = _dot(x_ref[...].astype(BF16), w_ref[...])

    @pl.when(j == 0)
    def _():
        cos, sin = cos_ref[...], sin_ref[...]
        half = RET_DK // 2
        for h in range(2 * RET_HEADS):
            x0 = acc[:, h * RET_DK:h * RET_DK + half]
            x1 = acc[:, h * RET_DK + half:(h + 1) * RET_DK]
            scale = 1.0 if h < RET_HEADS else RET_DK ** -0.5
            o_ref[:, h * RET_DK:h * RET_DK + half] = ((x0 * cos - x1 * sin) * scale).astype(o_ref.dtype)
            o_ref[:, h * RET_DK + half:(h + 1) * RET_DK] = ((x0 * sin + x1 * cos) * scale).astype(o_ref.dtype)

    @pl.when(j != 0)
    def _():
        o_ref[...] = acc.astype(o_ref.dtype)


def _proj_odd(x, w, cos, sin, tm):
    t = x.shape[0]
    nblk = w.shape[1] // 2048
    return pl.pallas_call(
        _proj_odd_kernel,
        grid=(nblk, t // tm),
        in_specs=[pl.BlockSpec((tm, D_MODEL), lambda j, i: (i, 0)),
                  pl.BlockSpec((D_MODEL, 2048), lambda j, i: (0, j)),
                  pl.BlockSpec((tm, LANES), lambda j, i: (i, 0)),
                  pl.BlockSpec((tm, LANES), lambda j, i: (i, 0))],
        out_specs=pl.BlockSpec((tm, 2048), lambda j, i: (i, j)),
        out_shape=jax.ShapeDtypeStruct((t, w.shape[1]), BF16),
        compiler_params=_params(("parallel", "parallel")),
        name="proj_odd",
    )(x, w, cos, sin)


def _layer_norm(r, g, b):
    mu = jnp.mean(r, -1, keepdims=True)
    d = r - mu
    var = jnp.mean(d * d, -1, keepdims=True)
    return d * lax.rsqrt(var + LN_EPS) * g + b


def _outproj_ln_kernel(*refs, n_in):
    a_refs = refs[:n_in]
    w_refs = refs[n_in:2 * n_in]
    x_ref, g_ref, b_ref, o_ref = refs[2 * n_in:]
    acc = _dot(a_refs[0][...], w_refs[0][...])
    for a_ref, w_ref in zip(a_refs[1:], w_refs[1:]):
        acc = acc + _dot(a_ref[...], w_ref[...])
    o_ref[...] = _layer_norm(DEEPNORM_ALPHA * x_ref[...] + acc, g_ref[...], b_ref[...])


def _outproj_ln(acts, ws, x, g, b, tm):
    t = x.shape[0]
    row = lambda n: pl.BlockSpec((tm, n), lambda i: (i, 0))
    return pl.pallas_call(
        functools.partial(_outproj_ln_kernel, n_in=len(acts)),
        grid=(t // tm,),
        in_specs=[row(a.shape[1]) for a in acts] + [_full(w.shape) for w in ws]
                 + [row(D_MODEL), _full((1, D_MODEL)), _full((1, D_MODEL))],
        out_specs=row(D_MODEL),
        out_shape=jax.ShapeDtypeStruct((t, D_MODEL), F32),
        compiler_params=_params(("parallel",)),
        name="outproj_ln",
    )(*acts, *ws, x, g, b)


S5_LANE_BLOCKS = S5_WIDTH // LANES
S5_TILES_PER_BLOCK = S5_TILES // S5_LANE_BLOCKS


def _s5_kernel(u_ref, h0_ref, bmat_ref, cmat_ref, acoef_ref, dskip_ref, wglu_ref, bglu_ref,
               y_ref, hout_ref, utm, sre, sim, ytm, hst, *, nb, lt):
    tb = pl.program_id(0)

    @pl.when(tb == 0)
    def _():
        hst[...] = h0_ref[...]

    for b in range(nb):
        for q in range(S5_LANE_BLOCKS):
            utm[q, pl.ds(b, lt, stride=nb), :] = u_ref[b, :, q * LANES:(q + 1) * LANES]

    def block_body(q, carry):
        bu = _dot(utm[q].astype(BF16), bmat_ref[q])
        for g in range(S5_TILES_PER_BLOCK):
            sre[g] = bu[:, (2 * g) * LANES:(2 * g + 1) * LANES]
            sim[g] = bu[:, (2 * g + 1) * LANES:(2 * g + 2) * LANES]
        js = [q * S5_TILES_PER_BLOCK + g for g in range(S5_TILES_PER_BLOCK)]
        acs = [acoef_ref[j] for j in js]
        hs = [(hst[j, :, 0:LANES], hst[j, :, LANES:2 * LANES]) for j in js]
        for t in range(lt):
            rows = slice(t * nb, (t + 1) * nb)
            for g in range(S5_TILES_PER_BLOCK):
                ar, ai = acs[g][:, :LANES], acs[g][:, LANES:]
                hr, hi = hs[g]
                nhr = ar * hr - ai * hi + sre[g, rows, :]
                nhi = ar * hi + ai * hr + sim[g, rows, :]
                sre[g, rows, :] = nhr
                sim[g, rows, :] = nhi
                hs[g] = (nhr, nhi)
        for g, j in enumerate(js):
            hst[j, :, 0:LANES] = hs[g][0]
            hst[j, :, LANES:2 * LANES] = hs[g][1]
        st = jnp.concatenate([part for g in range(S5_TILES_PER_BLOCK) for part in (sre[g], sim[g])], axis=-1)
        ytm[q] = _dot(st.astype(BF16), cmat_ref[q])
        return carry

    lax.fori_loop(0, S5_LANE_BLOCKS, block_body, 0)
    y = (jnp.concatenate([ytm[q] for q in range(S5_LANE_BLOCKS)], axis=-1)
         + dskip_ref[...] * jnp.concatenate([utm[q] for q in range(S5_LANE_BLOCKS)], axis=-1))
    y = jax.nn.gelu(y)
    y = y * _sigmoid(_dot(y.astype(BF16), wglu_ref[...]) + bglu_ref[...])
    for q in range(S5_LANE_BLOCKS):
        ytm[q] = y[:, q * LANES:(q + 1) * LANES]
    for b in range(nb):
        for q in range(S5_LANE_BLOCKS):
            y_ref[b, :, q * LANES:(q + 1) * LANES] = ytm[q, pl.ds(b, lt, stride=nb), :].astype(y_ref.dtype)
    hout_ref[...] = hst[...]


def _s5(u3, h0, bmat, cmat, acoef, dskip, wglu, bglu, lt):
    nb, L, _ = u3.shape
    rows = nb * lt
    tpb = S5_TILES_PER_BLOCK
    bmat = bmat.reshape(S5_LANE_BLOCKS, tpb, LANES, 2 * LANES).transpose(0, 2, 1, 3).reshape(
        S5_LANE_BLOCKS, LANES, tpb * 2 * LANES)
    cmat = cmat.reshape(S5_LANE_BLOCKS, tpb * 2 * LANES, LANES)
    acoef = jnp.broadcast_to(acoef[:, :1], (S5_TILES, nb, 2 * LANES))
    return pl.pallas_call(
        functools.partial(_s5_kernel, nb=nb, lt=lt),
        grid=(L // lt,),
        in_specs=[pl.BlockSpec((nb, lt, S5_WIDTH), lambda i: (0, i, 0)),
                  _full(h0.shape), _full(bmat.shape), _full(cmat.shape), _full(acoef.shape),
                  _full(dskip.shape), _full(wglu.shape), _full(bglu.shape)],
        out_specs=[pl.BlockSpec((nb, lt, S5_WIDTH), lambda i: (0, i, 0)), _full(h0.shape)],
        out_shape=[jax.ShapeDtypeStruct((nb, L, S5_WIDTH), BF16), jax.ShapeDtypeStruct(h0.shape, F32)],
        scratch_shapes=[pltpu.VMEM((S5_LANE_BLOCKS, rows, LANES), F32), pltpu.VMEM((tpb, rows, LANES), F32),
                        pltpu.VMEM((tpb, rows, LANES), F32), pltpu.VMEM((S5_LANE_BLOCKS, rows, LANES), F32),
                        pltpu.VMEM(h0.shape, F32)],
        compiler_params=_params(("arbitrary",)),
        name="s5_scan",
    )(u3, h0, bmat, cmat, acoef, dskip, wglu, bglu)


def _s5_weights(a_re, a_im, log_dt, b_re, b_im, c_re, c_im):
    dt = jnp.exp(log_dt)[:, None]
    lr, li = a_re * dt, a_im * dt
    mag = jnp.exp(lr)
    ab_re, ab_im = mag * jnp.cos(li), mag * jnp.sin(li)
    den = a_re * a_re + a_im * a_im
    cf_re = ((ab_re - 1.0) * a_re + ab_im * a_im) / den
    cf_im = (ab_im * a_re - (ab_re - 1.0) * a_im) / den
    bb_re = cf_re[..., None] * b_re - cf_im[..., None] * b_im
    bb_im = cf_re[..., None] * b_im + cf_im[..., None] * b_re
    jj = np.arange(S5_TILES)[:, None, None]
    lg = np.arange(8)[None, :, None]
    gi = np.arange(2)[None, None, :]
    sel = jnp.asarray((lg == 2 * (jj % 4) + gi).astype(np.float32))
    tiles = lambda w: w.reshape(S5_TILES, 2, *w.shape[1:])
    bt = lambda w: jnp.einsum('jlg,jgpn->jlngp', sel, tiles(w)).reshape(S5_TILES, LANES, LANES)
    bmat = jnp.concatenate([bt(bb_re), bt(bb_im)], axis=-1)
    ct = lambda w: jnp.einsum('jlg,jgnp->jgpln', sel, tiles(w)).reshape(S5_TILES, LANES, LANES)
    cmat = jnp.concatenate([ct(c_re), -ct(c_im)], axis=1)
    acoef = jnp.concatenate([ab_re.reshape(S5_TILES, LANES), ab_im.reshape(S5_TILES, LANES)], axis=-1)
    acoef = jnp.broadcast_to(acoef[:, None, :], (S5_TILES, SUBLANES, 2 * LANES))
    return bmat.astype(BF16), cmat.astype(BF16), acoef


def _s5_state_in(h_re, h_im):
    nb = h_re.shape[0]
    h = jnp.concatenate([h_re.reshape(nb, S5_TILES, LANES), h_im.reshape(nb, S5_TILES, LANES)], axis=-1)
    return jnp.transpose(h, (1, 0, 2))


def _s5_state_out(h):
    nb = h.shape[1]
    h = jnp.transpose(h, (1, 0, 2))
    return (h[..., :LANES].reshape(nb, S5_GROUPS, S5_STATE), h[..., LANES:].reshape(nb, S5_GROUPS, S5_STATE))


def _split_bf16(a):
    hi = a.astype(BF16)
    return hi, (a - hi.astype(F32)).astype(BF16)


def _bdot(a, b):
    return lax.dot_general(a, b, (((2,), (1,)), ((0,), (0,))), preferred_element_type=F32)


def _bdot_nt(a, b):
    return lax.dot_general(a, b, (((2,), (2,)), ((0,), (0,))), preferred_element_type=F32)


def _bdot_split(a, b):
    ah, al = _split_bf16(a)
    bh, bl = _split_bf16(b)
    return _bdot(ah, bh) + _bdot(ah, bl) + _bdot(al, bh)


def _unit_lower_inverse(nmat):
    lc = nmat.shape[-1]
    ri = lax.broadcasted_iota(jnp.int32, nmat.shape, 1)
    ci = lax.broadcasted_iota(jnp.int32, nmat.shape, 2)
    base = 16
    dmat = jnp.where(ri // base == ci // base, nmat, 0.0)
    inv = jnp.where(ri == ci, 1.0, 0.0) - dmat
    pw = dmat
    for _ in range(3):
        pw = _bdot_split(pw, pw)
        inv = inv + _bdot_split(inv, pw)
    size = base
    while size < lc:
        off = jnp.where(ri // (2 * size) == ci // (2 * size), jnp.where(ri // size > ci // size, nmat, 0.0), 0.0)
        inv = inv - _bdot_split(_bdot_split(inv, off), inv)
        size *= 2
    return inv


def _gdn_local_kernel(qkv_ref, ba_ref, ctx_ref, cw_ref, p1_ref, p2_ref, ltri_ref,
                      u0_ref, w_ref, qd_ref, kd_ref, attn_ref, g_ref, cout_ref, xpad, *, lc, cp):
    c = pl.program_id(1)
    rb = lc * cp

    @pl.when(c == 0)
    def _():
        xpad[0:SUBLANES, :] = ctx_ref[0]

    xpad[SUBLANES:SUBLANES + rb, :] = qkv_ref[0]
    cw = cw_ref[...]
    conv = (cw[3:4] * xpad[8:8 + rb, :] + cw[2:3] * xpad[7:7 + rb, :]
            + cw[1:2] * xpad[6:6 + rb, :] + cw[0:1] * xpad[5:5 + rb, :])
    tail = xpad[rb:rb + SUBLANES, :]
    xpad[0:SUBLANES, :] = tail
    cout_ref[0] = tail
    a = conv * _sigmoid(conv)

    ba = ba_ref[0]
    beta_all = _sigmoid(ba)
    sp_in = ba + p2_ref[...]
    softplus = jnp.maximum(sp_in, 0.0) + jnp.log(1.0 + jnp.exp(-jnp.abs(sp_in)))
    g_all = p1_ref[...] * softplus
    g_hi = g_all.astype(BF16)
    g_r = g_all - g_hi.astype(F32)
    g_mid = g_r.astype(BF16)
    g_lo = (g_r - g_mid.astype(F32)).astype(BF16)
    lt = ltri_ref[...]
    G = _dot(lt, g_hi) + _dot(lt, g_mid) + _dot(lt, g_lo)
    g_ref[0] = G
    GT = G.T
    pairs = [(h, cc) for h in range(GDN_HEADS) for cc in range(cp)]
    qs, ks, vs, betas, gcols, grows, glasts = [], [], [], [], [], [], []
    for h in range(GDN_HEADS):
        qa = a[:, h * GDN_DK:(h + 1) * GDN_DK]
        ka = a[:, (GDN_HEADS + h) * GDN_DK:(GDN_HEADS + h + 1) * GDN_DK]
        va = a[:, (2 * GDN_HEADS + h) * GDN_DK:(2 * GDN_HEADS + h + 1) * GDN_DK]
        qa = qa * lax.rsqrt(jnp.sum(qa * qa, -1, keepdims=True) + NORM_EPS) * (GDN_DK ** -0.5)
        ka = ka * lax.rsqrt(jnp.sum(ka * ka, -1, keepdims=True) + NORM_EPS)
        for cc in range(cp):
            rows = slice(cc * lc, (cc + 1) * lc)
            qs.append(qa[rows])
            ks.append(ka[rows])
            vs.append(va[rows])
            betas.append(beta_all[rows, h:h + 1])
            gcols.append(G[rows, GDN_HEADS + h:GDN_HEADS + h + 1])
            grows.append(GT[GDN_HEADS + h:GDN_HEADS + h + 1, cc * lc:(cc + 1) * lc])
            glasts.append(GT[GDN_HEADS + h:GDN_HEADS + h + 1, (cc + 1) * lc - 1:(cc + 1) * lc])
    q3, k3, v3 = jnp.stack(qs), jnp.stack(ks), jnp.stack(vs)
    beta3, gcol3 = jnp.stack(betas), jnp.stack(gcols)
    grow3, glast3 = jnp.stack(grows), jnp.stack(glasts)
    shape3 = (len(pairs), lc, lc)
    ri = lax.broadcasted_iota(jnp.int32, shape3, 1)
    ci = lax.broadcasted_iota(jnp.int32, shape3, 2)
    incl = ri >= ci
    dec3 = jnp.where(incl, jnp.exp(jnp.where(incl, gcol3 - grow3, 0.0)), 0.0)
    eg3 = jnp.exp(gcol3)
    kb3 = k3 * beta3
    kbf3 = k3.astype(BF16)
    nmat3 = jnp.where(ri > ci, _bdot_nt(kb3.astype(BF16), kbf3) * dec3, 0.0)
    inv3 = _unit_lower_inverse(nmat3)
    sol3 = _bdot_split(inv3, jnp.concatenate([v3 * beta3, kb3 * eg3], axis=-1))
    w3 = sol3[:, :, GDN_DV:].astype(BF16)
    qd3 = (q3 * eg3).astype(BF16)
    kd3 = (k3 * jnp.exp(glast3 - gcol3)).astype(BF16)
    attn3 = (_bdot_nt(q3.astype(BF16), kbf3) * dec3).astype(BF16)
    for i, (h, cc) in enumerate(pairs):
        rows = slice(cc * lc, (cc + 1) * lc)
        cols = slice(h * GDN_DV, (h + 1) * GDN_DV)
        u0_ref[0, rows, cols] = sol3[i, :, :GDN_DV]
        w_ref[0, rows, cols] = w3[i]
        qd_ref[0, rows, cols] = qd3[i]
        kd_ref[0, rows, cols] = kd3[i]
        attn_ref[0, rows, h * lc:(h + 1) * lc] = attn3[i]


def _gdn_seq_kernel(u0_ref, w_ref, qd_ref, kd_ref, attn_ref, g_ref, z_ref, s0_ref, nw_ref,
                    y_ref, sout_ref, S, *, lc, nbb):
    c = pl.program_id(1)

    @pl.when(c == 0)
    def _():
        S[...] = s0_ref[...]

    nw = nw_ref[...]
    pairs = [(bb, h) for bb in range(nbb) for h in range(GDN_HEADS)]
    hcols = lambda h: slice(h * GDN_DV, (h + 1) * GDN_DV)
    stack = lambda f: jnp.stack([f(bb, h) for bb, h in pairs])
    dlast = jnp.exp(g_ref[:, lc - 1:lc, :])
    S3 = S[...].reshape(len(pairs), GDN_DK, GDN_DV)
    wq3 = stack(lambda bb, h: jnp.concatenate([w_ref[bb, :, hcols(h)], qd_ref[bb, :, hcols(h)]], axis=0))
    r3 = _bdot(wq3, S3.astype(BF16))
    ub3 = (stack(lambda bb, h: u0_ref[bb, :, hcols(h)]) - r3[:, :lc]).astype(BF16)
    o3 = r3[:, lc:] + _bdot(stack(lambda bb, h: attn_ref[bb, :, h * lc:(h + 1) * lc]), ub3)
    d3 = stack(lambda bb, h: dlast[bb, :, GDN_HEADS + h:GDN_HEADS + h + 1])
    kd3 = stack(lambda bb, h: kd_ref[bb, :, hcols(h)])
    kdu3 = lax.dot_general(kd3, ub3, (((1,), (1,)), ((0,), (0,))), preferred_element_type=F32)
    S[...] = (d3 * S3 + kdu3).reshape(S.shape)
    o3 = o3 * lax.rsqrt(jnp.mean(o3 * o3, -1, keepdims=True) + NORM_EPS) * nw
    for i, (bb, h) in enumerate(pairs):
        zh = z_ref[bb, :, hcols(h)]
        y_ref[bb, :, hcols(h)] = (o3[i] * (zh * _sigmoid(zh))).astype(y_ref.dtype)

    @pl.when(c == pl.num_programs(1) - 1)
    def _():
        sout_ref[...] = S[...]


def _gdn(qkv3, z3, ba3, ctx8, s0, cw, p1, p2, nw, lc, cp, nbb):
    nb, L, _ = qkv3.shape
    rb = lc * cp
    hd = GDN_HEADS * GDN_DV
    ltri = jnp.asarray(np.kron(np.eye(cp, dtype=np.float32), np.tril(np.ones((lc, lc), np.float32)))).astype(BF16)
    blk = lambda n: pl.BlockSpec((1, rb, n), lambda b, c: (b, c, 0))
    per_b = lambda shape: pl.BlockSpec((1,) + shape, lambda b, c: (b,) + (0,) * len(shape))
    cst = lambda shape: pl.BlockSpec(shape, lambda b, c: (0,) * len(shape))
    sds = lambda n, dt: jax.ShapeDtypeStruct((nb, L, n), dt)
    u0, w, qd, kd, attn, G, cout = pl.pallas_call(
        functools.partial(_gdn_local_kernel, lc=lc, cp=cp),
        grid=(nb, L // rb),
        in_specs=[blk(GDN_QKV), blk(LANES), per_b((SUBLANES, GDN_QKV)), cst(cw.shape), cst(p1.shape),
                  cst(p2.shape), cst(ltri.shape)],
        out_specs=[blk(hd), blk(hd), blk(hd), blk(hd), blk(GDN_HEADS * lc), blk(LANES),
                   per_b((SUBLANES, GDN_QKV))],
        out_shape=[sds(hd, F32), sds(hd, BF16), sds(hd, BF16), sds(hd, BF16), sds(GDN_HEADS * lc, BF16),
                   sds(LANES, F32), jax.ShapeDtypeStruct((nb, SUBLANES, GDN_QKV), F32)],
        scratch_shapes=[pltpu.VMEM((rb + SUBLANES, GDN_QKV), F32)],
        compiler_params=_params(("parallel", "arbitrary")),
        name="gdn_local",
    )(qkv3, ba3, ctx8, cw, p1, p2, ltri)
    sblk = lambda n: pl.BlockSpec((nbb, lc, n), lambda b, c: (b, c, 0))
    state = pl.BlockSpec((nbb, GDN_HEADS, GDN_DK, GDN_DV), lambda b, c: (b, 0, 0, 0))
    y, s_new = pl.pallas_call(
        functools.partial(_gdn_seq_kernel, lc=lc, nbb=nbb),
        grid=(nb // nbb, L // lc),
        in_specs=[sblk(hd), sblk(hd), sblk(hd), sblk(hd), sblk(GDN_HEADS * lc), sblk(LANES), sblk(hd),
                  state, cst(nw.shape)],
        out_specs=[sblk(hd), state],
        out_shape=[sds(hd, BF16), jax.ShapeDtypeStruct(s0.shape, F32)],
        scratch_shapes=[pltpu.VMEM((nbb, GDN_HEADS, GDN_DK, GDN_DV), F32)],
        compiler_params=_params(("parallel", "arbitrary")),
        name="gdn_seq",
    )(u0, w, qd, kd, attn, G, z3, s0, nw)
    return y, s_new, cout


def _ret_kernel(q_ref, k_ref, v_ref, g_ref, r0_ref, dec_ref, qs_ref, ks_ref, cd_ref, o_ref, rout_ref, R,
                *, nbb):
    c = pl.program_id(1)

    @pl.when(c == 0)
    def _():
        R[...] = r0_ref[...]

    pairs = [(bb, h) for bb in range(nbb) for h in range(RET_HEADS)]
    stack = lambda f: jnp.stack([f(bb, h) for bb, h in pairs])
    kcols = lambda h: slice(h * RET_DK, (h + 1) * RET_DK)
    vcols = lambda h: slice(h * RET_DV, (h + 1) * RET_DV)
    q3 = stack(lambda bb, h: q_ref[bb, :, kcols(h)])
    k3 = stack(lambda bb, h: k_ref[bb, :, kcols(h)])
    v3 = stack(lambda bb, h: v_ref[bb, :, vcols(h)])
    dec3 = stack(lambda bb, h: dec_ref[h])
    qs3 = stack(lambda bb, h: qs_ref[h])
    ks3 = stack(lambda bb, h: ks_ref[h])
    cd3 = stack(lambda bb, h: cd_ref[h])
    R3 = R[...].reshape(len(pairs), RET_DK, RET_DV)
    s3 = _bdot_nt(q3, k3) * dec3
    o3 = _bdot(s3.astype(BF16), v3) + _bdot(q3, R3.astype(BF16)) * qs3
    kv3 = lax.dot_general((k3.astype(F32) * ks3).astype(BF16), v3, (((1,), (1,)), ((0,), (0,))),
                          preferred_element_type=F32)
    R[...] = (cd3 * R3 + kv3).reshape(R.shape)
    mu = jnp.mean(o3, -1, keepdims=True)
    d3 = o3 - mu
    var = jnp.mean(d3 * d3, -1, keepdims=True)
    on3 = d3 * lax.rsqrt(var + LN_EPS)
    for i, (bb, h) in enumerate(pairs):
        gt = g_ref[bb, :, vcols(h)].astype(F32)
        o_ref[bb, :, vcols(h)] = (gt * _sigmoid(gt) * on3[i]).astype(o_ref.dtype)

    @pl.when(c == pl.num_programs(1) - 1)
    def _():
        rout_ref[...] = R[...]


def _retention(proj3, r0, lc, nbb):
    nb, L, _ = proj3.shape
    log_g = np.log(1.0 - 2.0 ** (-5.0 - np.arange(RET_HEADS, dtype=np.float64)))
    idx = np.arange(lc, dtype=np.float64)
    dec = np.exp(log_g[:, None, None] * np.abs(idx[:, None] - idx[None, :])).astype(np.float32)
    qs = np.exp(log_g[:, None] * (idx + 1.0)).astype(np.float32)[..., None]
    ks = np.exp(log_g[:, None] * (lc - 1.0 - idx)).astype(np.float32)[..., None]
    cdec = np.exp(log_g * lc).astype(np.float32)[:, None, None]
    nqk = RET_HEADS * RET_DK
    nv = RET_HEADS * RET_DV
    cst = lambda shape: pl.BlockSpec(shape, lambda b, c: (0,) * len(shape))
    state = pl.BlockSpec((nbb, RET_HEADS, RET_DK, RET_DV), lambda b, c: (b, 0, 0, 0))
    return pl.pallas_call(
        functools.partial(_ret_kernel, nbb=nbb),
        grid=(nb // nbb, L // lc),
        in_specs=[pl.BlockSpec((nbb, lc, nqk), lambda b, c: (b, c, 0)),
                  pl.BlockSpec((nbb, lc, nqk), lambda b, c: (b, c, 1)),
                  pl.BlockSpec((nbb, lc, nv), lambda b, c: (b, c, 1)),
                  pl.BlockSpec((nbb, lc, nv), lambda b, c: (b, c, 2)),
                  state, cst(dec.shape), cst(qs.shape), cst(ks.shape), cst(cdec.shape)],
        out_specs=[pl.BlockSpec((nbb, lc, nv), lambda b, c: (b, c, 0)), state],
        out_shape=[jax.ShapeDtypeStruct((nb, L, nv), BF16), jax.ShapeDtypeStruct(r0.shape, F32)],
        scratch_shapes=[pltpu.VMEM((nbb, RET_HEADS, RET_DK, RET_DV), F32)],
        compiler_params=_params(("parallel", "arbitrary")),
        name="retention",
    )(proj3, proj3, proj3, proj3, r0, jnp.asarray(dec), jnp.asarray(qs), jnp.asarray(ks), jnp.asarray(cdec))


RUN_ALIGN = SUBLANES
RUN_PIECE = 64
MOE_VMEM_LIMIT = 58 * 1024 * 1024


def _moe_params(sem):
    return pltpu.CompilerParams(dimension_semantics=sem, vmem_limit_bytes=MOE_VMEM_LIMIT)


def _tile_cap(tm):
    rows = TOP_K * tm + N_EXPERTS * (RUN_ALIGN - 1)
    return -(-rows // LANES) * LANES


def _route_kernel(x_ref, rw_ref, rb_ref, lst_ref, ust_ref, pos_ref, gt_ref, cnt_ref, xs_ref):
    tm = x_ref.shape[0]
    cap = xs_ref.shape[1]
    x = x_ref[...]
    lane = lax.broadcasted_iota(jnp.int32, (tm, LANES), 1)
    lane_f = lane.astype(F32)
    xh, xl = _split_bf16(x)
    wh, wl = _split_bf16(rw_ref[...])
    logits = _dot(xh, wh) + _dot(xh, wl) + _dot(xl, wh) + rb_ref[...]
    logits = jnp.where(lane < N_EXPERTS, logits, -jnp.inf)
    vals, hots = [], []
    for _ in range(TOP_K):
        m = jnp.max(logits, -1, keepdims=True)
        first = jnp.min(jnp.where(logits == m, lane_f, float(LANES)), -1, keepdims=True)
        hot = lane_f == first
        vals.append(m)
        hots.append(hot)
        logits = jnp.where(hot, -jnp.inf, logits)
    es = [jnp.exp(v - vals[0]) for v in vals]
    den = es[0] + es[1] + es[2] + es[3]
    multi = jnp.zeros((tm, LANES), F32)
    for hot in hots:
        multi = multi + hot.astype(F32)
    counts = jnp.sum(multi, 0, keepdims=True)
    units = jnp.floor((counts + (RUN_ALIGN - 1)) * (1.0 / RUN_ALIGN))
    offs = _dot(jnp.broadcast_to(units, (SUBLANES, LANES)).astype(BF16), ust_ref[...])[0:1] * float(RUN_ALIGN)
    before = _dot(lst_ref[...], multi.astype(BF16))
    slot = offs + before
    pos = jnp.zeros((tm, LANES), F32)
    gt = jnp.zeros((tm, LANES), F32)
    for kk in range(TOP_K):
        pos = jnp.where(lane == kk, jnp.sum(jnp.where(hots[kk], slot, 0.0), -1, keepdims=True), pos)
        gt = jnp.where(lane == kk, es[kk] / den, gt)
    pos = pos.astype(jnp.int32)
    pos_ref[...] = pos
    gt_ref[...] = gt
    cnt_ref[0] = counts
    pos_t = pos.T[0:2 * SUBLANES].astype(jnp.int16)
    row = lax.broadcasted_iota(jnp.int16, (cap, tm), 0)
    sel = jnp.zeros((cap, tm), BF16)
    for kk in range(TOP_K):
        sel = sel + jnp.where(row == pos_t[kk:kk + 1, :], jnp.ones((), BF16), jnp.zeros((), BF16))
    xs_ref[0] = _dot(sel, xh)


def _route(x, rw, rb, tm):
    t = x.shape[0]
    nt = t // tm
    cap = _tile_cap(tm)
    lst = jnp.asarray(np.tril(np.ones((tm, tm), np.float32), -1)).astype(BF16)
    ust = jnp.asarray(np.triu(np.ones((LANES, LANES), np.float32), 1)).astype(BF16)
    row = lambda n: pl.BlockSpec((tm, n), lambda i: (i, 0))
    return pl.pallas_call(
        _route_kernel,
        grid=(nt,),
        in_specs=[row(D_MODEL), _full(rw.shape), _full(rb.shape), _full(lst.shape), _full(ust.shape)],
        out_specs=[row(LANES), row(LANES), pl.BlockSpec((1, 1, LANES), lambda i: (i, 0, 0)),
                   pl.BlockSpec((1, cap, D_MODEL), lambda i: (i, 0, 0))],
        out_shape=[jax.ShapeDtypeStruct((t, LANES), jnp.int32), jax.ShapeDtypeStruct((t, LANES), F32),
                   jax.ShapeDtypeStruct((nt, 1, LANES), F32), jax.ShapeDtypeStruct((nt, cap, D_MODEL), F32)],
        compiler_params=_moe_params(("parallel",)),
        name="moe_route",
    )(x, rw, rb, lst, ust)


def _expert_kernel(be_ref, nu_ref, ilo_ref, rows_ref, gs_ref, n8_ref, lo_ref, *refs, blk, tiles):
    ng = len(tiles)
    xs_hbms = refs[:ng]
    w1_ref, b1_ref, w2_ref, b2_ref = refs[ng:ng + 4]
    ys_hbms = refs[ng + 4:2 * ng + 4]
    xbuf, ybuf, w1b, w2b, in_sem, out_sem = refs[2 * ng + 4:]
    nt = sum(tiles)
    firsts = [sum(tiles[:g]) for g in range(ng)]
    j = pl.program_id(0)
    nu = nu_ref[0]

    def for_each_run(jb, fn):
        e = be_ref[jb]
        base = jb * blk
        for g in range(ng):
            end = firsts[g] + tiles[g]

            def cond(i, end=end):
                return (i < end) & (gs_ref[e * nt + jnp.minimum(i, nt - 1)] < base + blk)

            def body(i, g=g):
                g0 = gs_ref[e * nt + i]
                first = jnp.maximum(g0, base)
                last = jnp.minimum(g0 + n8_ref[e * nt + i], base + blk)
                fn(g, i - firsts[g], lo_ref[e * nt + i] + (first - g0), first - base, last - first)
                return i + 1

            lax.while_loop(cond, body, jnp.clip(ilo_ref[jb], firsts[g], end))

    def pieces(length, fn):
        def digits(sizes):
            for size in sizes:
                @pl.when((length & size) != 0)
                def _(size=size):
                    fn(length & ~(2 * size - 1), size)

        sizes = [blk >> k for k in range(blk.bit_length()) if blk >> k >= RUN_ALIGN]

        @pl.when(length > 2 * RUN_PIECE - 1)
        def _():
            digits([z for z in sizes if z > RUN_PIECE])

        digits([z for z in sizes if z <= RUN_PIECE])

    def aligned(v, size):
        return pl.ds(pl.multiple_of(v, RUN_ALIGN), size)

    def copy_in(jb, slot):
        def run(g, ig, src, dst, length):
            def piece(off, size):
                pltpu.make_async_copy(xs_hbms[g].at[ig, aligned(src + off, size)],
                                      xbuf.at[slot, aligned(dst + off, size)], in_sem.at[slot]).start()
            pieces(length, piece)
        for_each_run(jb, run)

    def copy_out(jb, slot):
        def run(g, ig, src, dst, length):
            def piece(off, size):
                pltpu.make_async_copy(ybuf.at[slot, aligned(dst + off, size)],
                                      ys_hbms[g].at[ig, aligned(src + off, size)], out_sem.at[slot]).start()
            pieces(length, piece)
        for_each_run(jb, run)

    def wait_rows(sem, nrows):
        size = blk
        while size >= RUN_ALIGN:
            @pl.when((nrows & size) != 0)
            def _(size=size):
                pltpu.make_async_copy(xbuf.at[1, pl.ds(0, size)], xbuf.at[0, pl.ds(0, size)], sem).wait()
            size //= 2

    @pl.when(j < nu)
    def _():
        slot = j % 2

        @pl.when(j == 0)
        def _():
            xbuf[...] = jnp.zeros_like(xbuf)
            copy_in(0, 0)

        wait_rows(in_sem.at[slot], rows_ref[j])

        @pl.when(j + 1 < nu)
        def _():
            copy_in(j + 1, 1 - slot)

        @pl.when(j >= 2)
        def _():
            wait_rows(out_sem.at[slot], rows_ref[jnp.maximum(j - 2, 0)])

        @pl.when((j == 0) | (be_ref[j] != be_ref[jnp.maximum(j - 1, 0)]))
        def _():
            w1b[...] = w1_ref[0, 0].astype(BF16)
            w2b[...] = w2_ref[0, 0].astype(BF16)

        h = _dot(xbuf[slot].astype(BF16), w1b[...]) + b1_ref[0]
        glu = jnp.minimum(h[:, :D_FF], SWIGLU_LIMIT)
        lin = jnp.clip(h[:, D_FF:], -SWIGLU_LIMIT, SWIGLU_LIMIT)
        act = glu * _sigmoid(SWIGLU_ALPHA * glu) * (lin + 1.0)
        ybuf[slot] = _dot(act.astype(BF16), w2b[...]) + b2_ref[0]
        copy_out(j, slot)

        @pl.when(j == nu - 1)
        def _():
            wait_rows(out_sem.at[slot], rows_ref[j])

            @pl.when(j >= 1)
            def _():
                wait_rows(out_sem.at[1 - slot], rows_ref[jnp.maximum(j - 1, 0)])


def _experts(tables, xs_list, w1, b1, w2, b2, layer, blk):
    ng = len(xs_list)
    n_blk = tables[0].shape[0]
    wspec = lambda shape: pl.BlockSpec((1, 1) + shape, lambda j, be, *_: (layer, be[j], 0, 0))
    bspec = lambda n: pl.BlockSpec((1, 1, n), lambda j, be, *_: (be[j], 0, 0))
    hbm = pl.BlockSpec(memory_space=pl.ANY)
    grid_spec = pltpu.PrefetchScalarGridSpec(
        num_scalar_prefetch=len(tables),
        grid=(n_blk,),
        in_specs=[hbm] * ng + [wspec((D_MODEL, 2 * D_FF)), bspec(2 * D_FF), wspec((D_FF, D_MODEL)), bspec(D_MODEL)],
        out_specs=[hbm] * ng,
        scratch_shapes=[pltpu.VMEM((2, blk, D_MODEL), F32), pltpu.VMEM((2, blk, D_MODEL), F32),
                        pltpu.VMEM((D_MODEL, 2 * D_FF), BF16), pltpu.VMEM((D_FF, D_MODEL), BF16),
                        pltpu.SemaphoreType.DMA((2,)), pltpu.SemaphoreType.DMA((2,))],
    )
    return pl.pallas_call(
        functools.partial(_expert_kernel, blk=blk, tiles=tuple(xs.shape[0] for xs in xs_list)),
        grid_spec=grid_spec,
        out_shape=[jax.ShapeDtypeStruct(xs.shape, F32) for xs in xs_list],
        input_output_aliases={len(tables) + g: g for g in range(ng)},
        compiler_params=_moe_params(("arbitrary",)),
        name="moe_experts",
    )(*tables, *xs_list, w1, b1, w2, b2)


def _combine_kernel(ys_ref, pos_ref, gt_ref, x_ref, p_ref, g_ref, b_ref, plew_ref, gatew_ref, o_ref):
    tm = x_ref.shape[0]
    cap = ys_ref.shape[1]
    pos = pos_ref[...].astype(jnp.int16)
    gt = gt_ref[...].astype(BF16)
    col = lax.broadcasted_iota(jnp.int16, (tm, cap), 1)
    sel = jnp.zeros((tm, cap), BF16)
    for kk in range(TOP_K):
        sel = sel + jnp.where(col == pos[:, kk:kk + 1], gt[:, kk:kk + 1], jnp.zeros((), BF16))
    y = _dot(sel, ys_ref[0].astype(BF16))
    x2 = _layer_norm(DEEPNORM_ALPHA * x_ref[...] + y, g_ref[...], b_ref[...])
    pp = _dot(p_ref[...].astype(BF16), plew_ref[...])
    gg = _sigmoid(_dot(x2.astype(BF16), gatew_ref[...]))
    o_ref[...] = x2 + pp * gg


def _combine(ys, pos, gt, x, p, g, b, plew, gatew, tm):
    t = x.shape[0]
    cap = ys.shape[1]
    row = lambda n: pl.BlockSpec((tm, n), lambda i: (i, 0))
    return pl.pallas_call(
        _combine_kernel,
        grid=(t // tm,),
        in_specs=[pl.BlockSpec((1, cap, D_MODEL), lambda i: (i, 0, 0)), row(LANES), row(LANES),
                  row(D_MODEL), row(PLE_DIM), _full((1, D_MODEL)), _full((1, D_MODEL)),
                  _full(plew.shape), _full(gatew.shape)],
        out_specs=row(D_MODEL),
        out_shape=jax.ShapeDtypeStruct((t, D_MODEL), F32),
        compiler_params=_moe_params(("parallel",)),
        name="moe_combine",
    )(ys, pos, gt, x, p, g, b, plew, gatew)


def _moe_tables(cnt, t, blk):
    nt = cnt.shape[0]
    n = cnt[:, 0, :N_EXPERTS].astype(jnp.int32)
    n8 = (n + RUN_ALIGN - 1) // RUN_ALIGN * RUN_ALIGN
    lo = jnp.cumsum(n8, axis=1) - n8
    rows_e = jnp.sum(n8, axis=0)
    padded = (rows_e + blk - 1) // blk * blk
    pend = jnp.cumsum(padded)
    pstart = pend - padded
    gstart = pstart[None, :] + jnp.cumsum(n8, axis=0) - n8
    n_blk = -(-(TOP_K * t + nt * N_EXPERTS * (RUN_ALIGN - 1) + N_EXPERTS * (blk - 1)) // blk)
    n_used = (pend[-1] // blk).astype(jnp.int32).reshape(1)
    blk_start = jnp.arange(n_blk, dtype=jnp.int32) * blk
    blk_e = jnp.minimum(jnp.sum((pend[None, :] <= blk_start[:, None]).astype(jnp.int32), axis=1), N_EXPERTS - 1)
    run_end = (gstart + n8)[:, blk_e]
    ilo = jnp.sum((run_end <= blk_start[None, :]).astype(jnp.int32), axis=0)
    rows_b = jnp.clip((pstart + rows_e)[blk_e] - blk_start, 0, blk)
    flat = lambda a: a.T.reshape(-1).astype(jnp.int32)
    i32 = lambda a: a.astype(jnp.int32)
    return (i32(blk_e), n_used, i32(ilo), i32(rows_b), flat(gstart), flat(n8), flat(lo))


def _moe_ple(xs_in, ps, rw, rb, w1, b1, w2, b2, layer, g, b, plew, gatew, tms, blk):
    routed = [_route(x, rw, rb, tm) for x, tm in zip(xs_in, tms)]
    cnt = jnp.concatenate([r[2] for r in routed], axis=0)
    tables = _moe_tables(cnt, sum(x.shape[0] for x in xs_in), blk)
    ys = _experts(tables, [r[3] for r in routed], w1, b1, w2, b2, layer, blk)
    return [_combine(y, r[0], r[1], x, p, g, b, plew, gatew, tm)
            for y, r, x, p, tm in zip(ys, routed, xs_in, ps, tms)]


_ROT_PERM = np.concatenate([np.arange(0, RET_DK, 2), np.arange(1, RET_DK, 2)])
_ROT_INV = np.argsort(_ROT_PERM)


def _lane_row(vals, offset):
    row = jnp.zeros((1, LANES), F32)
    return row.at[0, offset:offset + vals.shape[0]].set(vals)


def _even_mixer(x, nb, L, s5_re, s5_im, gdn_s, conv_s, W, tm):
    t = nb * L
    lc = L if L <= CHUNK else CHUNK
    u, qkv, z, ba = _proj_even(x, W['wu'], W['wqkv'], W['wz'], W['wba'], tm)
    yA3, h_new = _s5(u.reshape(nb, L, S5_WIDTH), _s5_state_in(s5_re[0].astype(F32), s5_im[0].astype(F32)),
                     W['bmat'], W['cmat'], W['acoef'], W['dskip'], W['wglu'], W['bglu'], lc)
    new_re, new_im = _s5_state_out(h_new)
    ctx8 = jnp.concatenate([jnp.zeros((nb, SUBLANES - (GDN_CONV - 1), GDN_QKV), F32), conv_s[0].astype(F32)], axis=1)
    yB3, new_gdn, cout = _gdn(qkv.reshape(nb, L, GDN_QKV), z.reshape(nb, L, -1), ba.reshape(nb, L, LANES),
                              ctx8, gdn_s[0].astype(F32), W['convw'], W['p1'], W['p2'], W['normw'], lc,
                              cp=min(4, L // lc), nbb=4)
    new_conv = cout[:, SUBLANES - (GDN_CONV - 1):, :]
    x = _outproj_ln([yA3.reshape(t, -1), yB3.reshape(t, -1)], [W['wout_a'], W['wout_b']], x,
                    W['ln1_g'][0], W['ln1_b'][0], tm)
    return x, new_re[None], new_im[None], new_gdn[None], new_conv[None]


def _odd_mixer(x, nb, L, ret_s, pos0, W, tm):
    t = nb * L
    lc = L if L <= CHUNK else CHUNK
    pos = pos0 + jnp.arange(L, dtype=F32)
    freq = 1.0 / (ROPE_BASE ** jnp.linspace(0.0, 1.0, RET_DK // 2, dtype=F32))
    ang = pos[:, None] * freq[None]
    cos = jnp.broadcast_to(jnp.cos(ang)[None], (nb, L, RET_DK // 2)).reshape(t, RET_DK // 2)
    sin = jnp.broadcast_to(jnp.sin(ang)[None], (nb, L, RET_DK // 2)).reshape(t, RET_DK // 2)
    proj = _proj_odd(x, W['win_odd'], cos, sin, tm)
    r0 = ret_s[0].astype(F32)[:, :, _ROT_PERM, :]
    o3, r_new = _retention(proj.reshape(nb, L, -1), r0, lc, nbb=2)
    new_ret = r_new[:, :, _ROT_INV, :]
    x = _outproj_ln([o3.reshape(t, -1)], [W['wout_odd']], x, W['ln1_g'][1], W['ln1_b'][1], tm)
    return x, new_ret[None]


def kernel(x_prompt, x_sample, state_s5_re, state_s5_im, state_gdn, state_gdn_conv, state_ret, p_prompt, p_sample, w_in_even, s5_a_re, s5_a_im, s5_log_dt, s5_b_re, s5_b_im, s5_c_re, s5_c_im, s5_d, s5_w_glu, s5_b_glu, gdn_conv_w, gdn_a_log, gdn_dt_bias, gdn_norm_w, w_out_even, w_in_odd, w_out_odd, ln1_g, ln1_b, ln2_g, ln2_b, router_w, router_b, moe_w1, moe_b1, moe_w2, moe_b2, ple_w, ple_gate_w):
    o1 = S5_WIDTH
    o2 = o1 + GDN_QKV
    o3 = o2 + GDN_HEADS * GDN_DV
    win = w_in_even[0]
    bmat, cmat, acoef = _s5_weights(s5_a_re[0], s5_a_im[0], s5_log_dt[0], s5_b_re[0], s5_b_im[0],
                                    s5_c_re[0], s5_c_im[0])
    wodd = w_in_odd[0]
    nk = RET_HEADS * RET_DK
    perm_cols = lambda w: w.reshape(D_MODEL, RET_HEADS, RET_DK)[:, :, _ROT_PERM].reshape(D_MODEL, nk)
    W = dict(
        wu=win[:, :o1].astype(BF16), wqkv=win[:, o1:o2].astype(BF16), wz=win[:, o2:o3].astype(BF16),
        wba=jnp.pad(win[:, o3:], ((0, 0), (0, LANES - 2 * GDN_HEADS))).astype(BF16),
        bmat=bmat, cmat=cmat, acoef=acoef, dskip=s5_d[0][None], wglu=s5_w_glu[0].astype(BF16),
        bglu=s5_b_glu[0][None], convw=gdn_conv_w[0],
        p1=_lane_row(-jnp.exp(gdn_a_log[0]), GDN_HEADS), p2=_lane_row(gdn_dt_bias[0], GDN_HEADS),
        normw=gdn_norm_w[0][None],
        wout_a=w_out_even[0][:S5_WIDTH].astype(BF16), wout_b=w_out_even[0][S5_WIDTH:].astype(BF16),
        win_odd=jnp.concatenate([perm_cols(wodd[:, :nk]), perm_cols(wodd[:, nk:2 * nk]), wodd[:, 2 * nk:]],
                                axis=1).astype(BF16),
        wout_odd=w_out_odd[0].astype(BF16),
        ln1_g=ln1_g[:, None], ln1_b=ln1_b[:, None], ln2_g=ln2_g[:, None], ln2_b=ln2_b[:, None],
        rw=jnp.pad(router_w, ((0, 0), (0, 0), (0, LANES - N_EXPERTS))),
        rb=jnp.pad(router_b, ((0, 0), (0, LANES - N_EXPERTS)))[:, None],
        w1=moe_w1, b1=moe_b1[:, :, None], w2=moe_w2, b2=moe_b2[:, :, None],
        plew=ple_w.astype(BF16), gatew=ple_gate_w.astype(BF16),
    )
    bp, lp, _ = x_prompt.shape
    bs, ls, _ = x_sample.shape
    zeros = lambda *s: jnp.zeros(s, F32)
    shapes = [(bp, lp), (bs, ls)]
    tms = [512, 128]
    ps = [p_prompt, p_sample]
    xs = [x_prompt.reshape(bp * lp, D_MODEL).astype(F32), x_sample.reshape(bs * ls, D_MODEL).astype(F32)]
    even_states = [(zeros(1, bp, S5_GROUPS, S5_STATE), zeros(1, bp, S5_GROUPS, S5_STATE),
                    zeros(1, bp, GDN_HEADS, GDN_DK, GDN_DV), zeros(1, bp, GDN_CONV - 1, GDN_QKV)),
                   (state_s5_re, state_s5_im, state_gdn, state_gdn_conv)]
    ret_states = [zeros(1, bp, RET_HEADS, RET_DK, RET_DV), state_ret]
    pos0s = [0.0, float(PAST_LEN)]

    def moe(xs, layer):
        return _moe_ple(xs, [p[layer].reshape(-1, PLE_DIM) for p in ps], W['rw'][layer], W['rb'][layer],
                        W['w1'], W['b1'][layer], W['w2'], W['b2'][layer], layer, W['ln2_g'][layer],
                        W['ln2_b'][layer], W['plew'][layer], W['gatew'][layer], tms, blk=512)

    even = [_even_mixer(x, nb, L, *st, W, tm) for x, (nb, L), st, tm in zip(xs, shapes, even_states, tms)]
    xs = moe([e[0] for e in even], 0)
    odd = [_odd_mixer(x, nb, L, st, pos0, W, tm)
           for x, (nb, L), st, pos0, tm in zip(xs, shapes, ret_states, pos0s, tms)]
    xs = moe([o[0] for o in odd], 1)
    dp = x_prompt.dtype
    y_p, y_s = xs[0].reshape(bp, lp, D_MODEL), xs[1].reshape(bs, ls, D_MODEL)
    (_, p_re, p_im, p_gdn, p_conv), (_, s_re, s_im, s_gdn, s_conv) = even
    p_ret, s_ret = odd[0][1], odd[1][1]
    return (y_p.astype(dp), y_s.astype(x_sample.dtype),
            p_re.astype(dp), p_im.astype(dp), p_gdn.astype(dp), p_conv.astype(dp), p_ret.astype(dp),
            s_re.astype(state_s5_re.dtype), s_im.astype(state_s5_im.dtype), s_gdn.astype(state_gdn.dtype),
            s_conv.astype(state_gdn_conv.dtype), s_ret.astype(state_ret.dtype))
```

```python
import functools
import math

import jax
import jax.numpy as jnp
import numpy as np
from jax import lax
from jax.experimental import pallas as pl
from jax.experimental.pallas import tpu as pltpu

F32 = jnp.float32
BF16 = jnp.bfloat16
HIGHEST = lax.Precision.HIGHEST

D_MODEL = 1024
CHUNK = 64
S5_WIDTH = 512
S5_GROUP = 16
S5_GROUPS = 32
S5_STATE = 64
S5_TILES = 16
GDN_HEADS = 4
GDN_DK = 128
GDN_DV = 128
GDN_CONV = 4
GDN_QKV = 1536
RET_HEADS = 4
RET_DK = 256
RET_DV = 512
ROPE_BASE = 10000.0
N_EXPERTS = 32
TOP_K = 4
D_FF = 1024
SWIGLU_LIMIT = 7.0
SWIGLU_ALPHA = 1.702
PLE_DIM = 256
DEPTH = 2
PAST_LEN = 1024
DEEPNORM_ALPHA = (2 * DEPTH) ** 0.25
LN_EPS = 1e-5
NORM_EPS = 1e-6

LANES = 128
SUBLANES = 8
VMEM_LIMIT = 48 * 1024 * 1024

def _params(sem):
    return pltpu.CompilerParams(dimension_semantics=sem, vmem_limit_bytes=VMEM_LIMIT)


def _dot(a, b):
    return jnp.dot(a, b, preferred_element_type=F32)


def _dot_hi(a, b):
    return jnp.dot(a, b, preferred_element_type=F32, precision=HIGHEST)


def _dot_nt(a, b, precision=None):
    return lax.dot_general(a, b, (((1,), (1,)), ((), ())), preferred_element_type=F32,
                           precision=precision)


def _dot_tn(a, b):
    return lax.dot_general(a, b, (((0,), (0,)), ((), ())), preferred_element_type=F32)


def _sigmoid(x):
    return 1.0 / (1.0 + jnp.exp(-x))


def _full(shape):
    nd = len(shape)
    return pl.BlockSpec(shape, lambda *_: (0,) * nd)


def _proj_even_kernel(x_ref, wu_ref, wqkv_ref, wz_ref, wba_ref, u_ref, qkv_ref, z_ref, ba_ref):
    xb = x_ref[...].astype(BF16)
    u_ref[...] = _dot(xb, wu_ref[...])
    qkv_ref[...] = _dot(xb, wqkv_ref[...])
    z_ref[...] = _dot(xb, wz_ref[...])
    ba_ref[...] = _dot(xb, wba_ref[...])


def _proj_even(x, wu, wqkv, wz, wba, tm):
    t = x.shape[0]
    row = lambda n: pl.BlockSpec((tm, n), lambda i: (i, 0))
    return pl.pallas_call(
        _proj_even_kernel,
        grid=(t // tm,),
        in_specs=[row(D_MODEL), _full(wu.shape), _full(wqkv.shape), _full(wz.shape), _full(wba.shape)],
        out_specs=[row(S5_WIDTH), row(GDN_QKV), row(GDN_HEADS * GDN_DV), row(LANES)],
        out_shape=[jax.ShapeDtypeStruct((t, S5_WIDTH), F32), jax.ShapeDtypeStruct((t, GDN_QKV), F32),
                   jax.ShapeDtypeStruct((t, GDN_HEADS * GDN_DV), F32), jax.ShapeDtypeStruct((t, LANES), F32)],
        compiler_params=_params(("parallel",)),
        name="proj_even",
    )(x, wu, wqkv, wz, wba)


def _proj_odd_kernel(x_ref, w_ref, cos_ref, sin_ref, o_ref):
    j = pl.program_id(0)
    acc = _dot(x_ref[...].astype(BF16), w_ref[...])

    @pl.when(j == 0)
    def _():
        cos, sin = cos_ref[...], sin_ref[...]
        half = RET_DK // 2
        for h in range(2 * RET_HEADS):
            x0 = acc[:, h * RET_DK:h * RET_DK + half]
            x1 = acc[:, h * RET_DK + half:(h + 1) * RET_DK]
            scale = 1.0 if h < RET_HEADS else RET_DK ** -0.5
            o_ref[:, h * RET_DK:h * RET_DK + half] = ((x0 * cos - x1 * sin) * scale).astype(o_ref.dtype)
            o_ref[:, h * RET_DK + half:(h + 1) * RET_DK] = ((x0 * sin + x1 * cos) * scale).astype(o_ref.dtype)

    @pl.when(j != 0)
    def _():
        o_ref[...] = acc.astype(o_ref.dtype)


def _proj_odd(x, w, cos, sin, tm):
    t = x.shape[0]
    nblk = w.shape[1] // 2048
    period = cos.shape[0] // tm
    return pl.pallas_call(
        _proj_odd_kernel,
        grid=(nblk, t // tm),
        in_specs=[pl.BlockSpec((tm, D_MODEL), lambda j, i: (i, 0)),
                  pl.BlockSpec((D_MODEL, 2048), lambda j, i: (0, j)),
                  pl.BlockSpec((tm, LANES), lambda j, i: (i % period, 0)),
                  pl.BlockSpec((tm, LANES), lambda j, i: (i % period, 0))],
        out_specs=pl.BlockSpec((tm, 2048), lambda j, i: (i, j)),
        out_shape=jax.ShapeDtypeStruct((t, w.shape[1]), BF16),
        compiler_params=_params(("parallel", "parallel")),
        name="proj_odd",
    )(x, w, cos, sin)


def _layer_norm(r, g, b):
    mu = jnp.mean(r, -1, keepdims=True)
    d = r - mu
    var = jnp.mean(d * d, -1, keepdims=True)
    return d * lax.rsqrt(var + LN_EPS) * g + b


def _outproj_ln_kernel(*refs, n_in):
    a_refs = refs[:n_in]
    w_refs = refs[n_in:2 * n_in]
    x_ref, g_ref, b_ref, o_ref = refs[2 * n_in:]
    acc = _dot(a_refs[0][...], w_refs[0][...])
    for a_ref, w_ref in zip(a_refs[1:], w_refs[1:]):
        acc = acc + _dot(a_ref[...], w_ref[...])
    o_ref[...] = _layer_norm(DEEPNORM_ALPHA * x_ref[...] + acc, g_ref[...], b_ref[...])


def _outproj_ln(acts, ws, x, g, b, tm):
    t = x.shape[0]
    row = lambda n: pl.BlockSpec((tm, n), lambda i: (i, 0))
    return pl.pallas_call(
        functools.partial(_outproj_ln_kernel, n_in=len(acts)),
        grid=(t // tm,),
        in_specs=[row(a.shape[1]) for a in acts] + [_full(w.shape) for w in ws]
                 + [row(D_MODEL), _full((1, D_MODEL)), _full((1, D_MODEL))],
        out_specs=row(D_MODEL),
        out_shape=jax.ShapeDtypeStruct((t, D_MODEL), F32),
        compiler_params=_params(("parallel",)),
        name="outproj_ln",
    )(*acts, *ws, x, g, b)


S5_LANE_BLOCKS = S5_WIDTH // LANES
S5_TILES_PER_BLOCK = S5_TILES // S5_LANE_BLOCKS


def _s5_kernel(u_ref, h0_ref, bmat_ref, cmat_ref, acoef_ref, dskip_ref, wglu_ref, bglu_ref,
               y_ref, hout_ref, utm, sre, sim, ytm, hst, *, nb, lt):
    tb = pl.program_id(0)

    @pl.when(tb == 0)
    def _():
        hst[...] = h0_ref[...]

    for b in range(nb):
        for q in range(S5_LANE_BLOCKS):
            utm[q, pl.ds(b, lt, stride=nb), :] = u_ref[b, :, q * LANES:(q + 1) * LANES]

    def block_body(q, carry):
        bu = _dot(utm[q].astype(BF16), bmat_ref[q])
        for g in range(S5_TILES_PER_BLOCK):
            sre[g] = bu[:, (2 * g) * LANES:(2 * g + 1) * LANES]
            sim[g] = bu[:, (2 * g + 1) * LANES:(2 * g + 2) * LANES]
        js = [q * S5_TILES_PER_BLOCK + g for g in range(S5_TILES_PER_BLOCK)]
        acs = [acoef_ref[j] for j in js]
        hs = [(hst[j, :, 0:LANES], hst[j, :, LANES:2 * LANES]) for j in js]
        for t in range(lt):
            rows = slice(t * nb, (t + 1) * nb)
            for g in range(S5_TILES_PER_BLOCK):
                ar, ai = acs[g][:, :LANES], acs[g][:, LANES:]
                hr, hi = hs[g]
                nhr = ar * hr - ai * hi + sre[g, rows, :]
                nhi = ar * hi + ai * hr + sim[g, rows, :]
                sre[g, rows, :] = nhr
                sim[g, rows, :] = nhi
                hs[g] = (nhr, nhi)
        for g, j in enumerate(js):
            hst[j, :, 0:LANES] = hs[g][0]
            hst[j, :, LANES:2 * LANES] = hs[g][1]
        st = jnp.concatenate([part for g in range(S5_TILES_PER_BLOCK) for part in (sre[g], sim[g])], axis=-1)
        ytm[q] = _dot(st.astype(BF16), cmat_ref[q])
        return carry

    lax.fori_loop(0, S5_LANE_BLOCKS, block_body, 0)
    y = (jnp.concatenate([ytm[q] for q in range(S5_LANE_BLOCKS)], axis=-1)
         + dskip_ref[...] * jnp.concatenate([utm[q] for q in range(S5_LANE_BLOCKS)], axis=-1))
    y = jax.nn.gelu(y)
    y = y * _sigmoid(_dot(y.astype(BF16), wglu_ref[...]) + bglu_ref[...])
    for q in range(S5_LANE_BLOCKS):
        ytm[q] = y[:, q * LANES:(q + 1) * LANES]
    for b in range(nb):
        for q in range(S5_LANE_BLOCKS):
            y_ref[b, :, q * LANES:(q + 1) * LANES] = ytm[q, pl.ds(b, lt, stride=nb), :].astype(y_ref.dtype)
    hout_ref[...] = hst[...]


def _s5(u3, h0, bmat, cmat, acoef, dskip, wglu, bglu, lt):
    nb, L, _ = u3.shape
    rows = nb * lt
    tpb = S5_TILES_PER_BLOCK
    bmat = bmat.reshape(S5_LANE_BLOCKS, tpb, LANES, 2 * LANES).transpose(0, 2, 1, 3).reshape(
        S5_LANE_BLOCKS, LANES, tpb * 2 * LANES)
    cmat = cmat.reshape(S5_LANE_BLOCKS, tpb * 2 * LANES, LANES)
    acoef = jnp.broadcast_to(acoef[:, :1], (S5_TILES, nb, 2 * LANES))
    return pl.pallas_call(
        functools.partial(_s5_kernel, nb=nb, lt=lt),
        grid=(L // lt,),
        in_specs=[pl.BlockSpec((nb, lt, S5_WIDTH), lambda i: (0, i, 0)),
                  _full(h0.shape), _full(bmat.shape), _full(cmat.shape), _full(acoef.shape),
                  _full(dskip.shape), _full(wglu.shape), _full(bglu.shape)],
        out_specs=[pl.BlockSpec((nb, lt, S5_WIDTH), lambda i: (0, i, 0)), _full(h0.shape)],
        out_shape=[jax.ShapeDtypeStruct((nb, L, S5_WIDTH), BF16), jax.ShapeDtypeStruct(h0.shape, F32)],
        scratch_shapes=[pltpu.VMEM((S5_LANE_BLOCKS, rows, LANES), F32), pltpu.VMEM((tpb, rows, LANES), F32),
                        pltpu.VMEM((tpb, rows, LANES), F32), pltpu.VMEM((S5_LANE_BLOCKS, rows, LANES), F32),
                        pltpu.VMEM(h0.shape, F32)],
        compiler_params=_params(("arbitrary",)),
        name="s5_scan",
    )(u3, h0, bmat, cmat, acoef, dskip, wglu, bglu)


def _s5_weights(a_re, a_im, log_dt, b_re, b_im, c_re, c_im):
    dt = jnp.exp(log_dt)[:, None]
    lr, li = a_re * dt, a_im * dt
    mag = jnp.exp(lr)
    ab_re, ab_im = mag * jnp.cos(li), mag * jnp.sin(li)
    den = a_re * a_re + a_im * a_im
    cf_re = ((ab_re - 1.0) * a_re + ab_im * a_im) / den
    cf_im = (ab_im * a_re - (ab_re - 1.0) * a_im) / den
    bb_re = cf_re[..., None] * b_re - cf_im[..., None] * b_im
    bb_im = cf_re[..., None] * b_im + cf_im[..., None] * b_re
    jj = np.arange(S5_TILES)[:, None, None]
    lg = np.arange(8)[None, :, None]
    gi = np.arange(2)[None, None, :]
    sel = jnp.asarray((lg == 2 * (jj % 4) + gi).astype(np.float32))
    tiles = lambda w: w.reshape(S5_TILES, 2, *w.shape[1:])
    bt = lambda w: jnp.einsum('jlg,jgpn->jlngp', sel, tiles(w)).reshape(S5_TILES, LANES, LANES)
    bmat = jnp.concatenate([bt(bb_re), bt(bb_im)], axis=-1)
    ct = lambda w: jnp.einsum('jlg,jgnp->jgpln', sel, tiles(w)).reshape(S5_TILES, LANES, LANES)
    cmat = jnp.concatenate([ct(c_re), -ct(c_im)], axis=1)
    acoef = jnp.concatenate([ab_re.reshape(S5_TILES, LANES), ab_im.reshape(S5_TILES, LANES)], axis=-1)
    acoef = jnp.broadcast_to(acoef[:, None, :], (S5_TILES, SUBLANES, 2 * LANES))
    return bmat.astype(BF16), cmat.astype(BF16), acoef


def _s5_state_in(h_re, h_im):
    nb = h_re.shape[0]
    h = jnp.concatenate([h_re.reshape(nb, S5_TILES, LANES), h_im.reshape(nb, S5_TILES, LANES)], axis=-1)
    return jnp.transpose(h, (1, 0, 2))


def _s5_state_out(h):
    nb = h.shape[1]
    h = jnp.transpose(h, (1, 0, 2))
    return (h[..., :LANES].reshape(nb, S5_GROUPS, S5_STATE), h[..., LANES:].reshape(nb, S5_GROUPS, S5_STATE))


def _split_bf16(a):
    hi = a.astype(BF16)
    return hi, (a - hi.astype(F32)).astype(BF16)


def _bdot(a, b):
    return lax.dot_general(a, b, (((2,), (1,)), ((0,), (0,))), preferred_element_type=F32)


def _bdot_nt(a, b):
    return lax.dot_general(a, b, (((2,), (2,)), ((0,), (0,))), preferred_element_type=F32)


def _bdot_bf16(a, b):
    return _bdot(a.astype(BF16), b.astype(BF16))


def _unit_lower_inverse(nmat):
    lc = nmat.shape[-1]
    ri = lax.broadcasted_iota(jnp.int32, nmat.shape, 1)
    ci = lax.broadcasted_iota(jnp.int32, nmat.shape, 2)
    base = 16
    dmat = jnp.where(ri // base == ci // base, nmat, 0.0)
    inv = jnp.where(ri == ci, 1.0, 0.0) - dmat
    pw = dmat
    for _ in range(3):
        pw = _bdot_bf16(pw, pw)
        inv = inv + _bdot_bf16(inv, pw)
    size = base
    while size < lc:
        off = jnp.where(ri // (2 * size) == ci // (2 * size), jnp.where(ri // size > ci // size, nmat, 0.0), 0.0)
        inv = inv - _bdot_bf16(_bdot_bf16(inv, off), inv)
        size *= 2
    return inv


def _gdn_local_kernel(qkv_ref, ba_ref, ctx_ref, cw_ref, p1_ref, p2_ref, ltri_ref,
                      u0_ref, w_ref, qd_ref, kd_ref, attn_ref, g_ref, cout_ref, xpad, *, lc, cp):
    c = pl.program_id(1)
    rb = lc * cp

    @pl.when(c == 0)
    def _():
        xpad[0:SUBLANES, :] = ctx_ref[0]

    xpad[SUBLANES:SUBLANES + rb, :] = qkv_ref[0]
    cw = cw_ref[...]
    conv = (cw[3:4] * xpad[8:8 + rb, :] + cw[2:3] * xpad[7:7 + rb, :]
            + cw[1:2] * xpad[6:6 + rb, :] + cw[0:1] * xpad[5:5 + rb, :])
    tail = xpad[rb:rb + SUBLANES, :]
    xpad[0:SUBLANES, :] = tail
    cout_ref[0] = tail
    a = conv * _sigmoid(conv)

    ba = ba_ref[0]
    beta_all = _sigmoid(ba)
    sp_in = ba + p2_ref[...]
    softplus = jnp.maximum(sp_in, 0.0) + jnp.log(1.0 + jnp.exp(-jnp.abs(sp_in)))
    g_all = p1_ref[...] * softplus
    g_hi = g_all.astype(BF16)
    g_r = g_all - g_hi.astype(F32)
    g_mid = g_r.astype(BF16)
    g_lo = (g_r - g_mid.astype(F32)).astype(BF16)
    lt = ltri_ref[...]
    G = _dot(lt, g_hi) + _dot(lt, g_mid) + _dot(lt, g_lo)
    g_ref[0] = G
    GT = G.T
    pairs = [(h, cc) for h in range(GDN_HEADS) for cc in range(cp)]
    qs, ks, vs, betas, gcols, grows, glasts = [], [], [], [], [], [], []
    for h in range(GDN_HEADS):
        qa = a[:, h * GDN_DK:(h + 1) * GDN_DK]
        ka = a[:, (GDN_HEADS + h) * GDN_DK:(GDN_HEADS + h + 1) * GDN_DK]
        va = a[:, (2 * GDN_HEADS + h) * GDN_DK:(2 * GDN_HEADS + h + 1) * GDN_DK]
        qa = qa * lax.rsqrt(jnp.sum(qa * qa, -1, keepdims=True) + NORM_EPS) * (GDN_DK ** -0.5)
        ka = ka * lax.rsqrt(jnp.sum(ka * ka, -1, keepdims=True) + NORM_EPS)
        for cc in range(cp):
            rows = slice(cc * lc, (cc + 1) * lc)
            qs.append(qa[rows])
            ks.append(ka[rows])
            vs.append(va[rows])
            betas.append(beta_all[rows, h:h + 1])
            gcols.append(G[rows, GDN_HEADS + h:GDN_HEADS + h + 1])
            grows.append(GT[GDN_HEADS + h:GDN_HEADS + h + 1, cc * lc:(cc + 1) * lc])
            glasts.append(GT[GDN_HEADS + h:GDN_HEADS + h + 1, (cc + 1) * lc - 1:(cc + 1) * lc])
    q3, k3, v3 = jnp.stack(qs), jnp.stack(ks), jnp.stack(vs)
    beta3, gcol3 = jnp.stack(betas), jnp.stack(gcols)
    grow3, glast3 = jnp.stack(grows), jnp.stack(glasts)
    shape3 = (len(pairs), lc, lc)
    ri = lax.broadcasted_iota(jnp.int32, shape3, 1)
    ci = lax.broadcasted_iota(jnp.int32, shape3, 2)
    incl = ri >= ci
    dec3 = jnp.where(incl, jnp.exp(jnp.where(incl, gcol3 - grow3, 0.0)), 0.0)
    eg3 = jnp.exp(gcol3)
    kb3 = k3 * beta3
    kbf3 = k3.astype(BF16)
    nmat3 = jnp.where(ri > ci, _bdot_nt(kb3.astype(BF16), kbf3) * dec3, 0.0)
    inv3 = _unit_lower_inverse(nmat3)
    sol3 = _bdot_bf16(inv3, jnp.concatenate([v3 * beta3, kb3 * eg3], axis=-1))
    w3 = sol3[:, :, GDN_DV:].astype(BF16)
    qd3 = (q3 * eg3).astype(BF16)
    kd3 = (k3 * jnp.exp(glast3 - gcol3)).astype(BF16)
    attn3 = (_bdot_nt(q3.astype(BF16), kbf3) * dec3).astype(BF16)
    for i, (h, cc) in enumerate(pairs):
        rows = slice(cc * lc, (cc + 1) * lc)
        cols = slice(h * GDN_DV, (h + 1) * GDN_DV)
        u0_ref[0, rows, cols] = sol3[i, :, :GDN_DV]
        w_ref[0, rows, cols] = w3[i]
        qd_ref[0, rows, cols] = qd3[i]
        kd_ref[0, rows, cols] = kd3[i]
        attn_ref[0, rows, h * lc:(h + 1) * lc] = attn3[i]


def _gdn_seq_kernel(u0_ref, w_ref, qd_ref, kd_ref, attn_ref, g_ref, z_ref, s0_ref, nw_ref,
                    y_ref, sout_ref, S, *, lc, nbb):
    c = pl.program_id(1)

    @pl.when(c == 0)
    def _():
        S[...] = s0_ref[...]

    nw = nw_ref[...]
    pairs = [(bb, h) for bb in range(nbb) for h in range(GDN_HEADS)]
    hcols = lambda h: slice(h * GDN_DV, (h + 1) * GDN_DV)
    stack = lambda f: jnp.stack([f(bb, h) for bb, h in pairs])
    dlast = jnp.exp(g_ref[:, lc - 1:lc, :])
    S3 = S[...].reshape(len(pairs), GDN_DK, GDN_DV)
    wq3 = stack(lambda bb, h: jnp.concatenate([w_ref[bb, :, hcols(h)], qd_ref[bb, :, hcols(h)]], axis=0))
    r3 = _bdot(wq3, S3.astype(BF16))
    ub3 = (stack(lambda bb, h: u0_ref[bb, :, hcols(h)]) - r3[:, :lc]).astype(BF16)
    o3 = r3[:, lc:] + _bdot(stack(lambda bb, h: attn_ref[bb, :, h * lc:(h + 1) * lc]), ub3)
    d3 = stack(lambda bb, h: dlast[bb, :, GDN_HEADS + h:GDN_HEADS + h + 1])
    kd3 = stack(lambda bb, h: kd_ref[bb, :, hcols(h)])
    kdu3 = lax.dot_general(kd3, ub3, (((1,), (1,)), ((0,), (0,))), preferred_element_type=F32)
    S[...] = (d3 * S3 + kdu3).reshape(S.shape)
    o3 = o3 * lax.rsqrt(jnp.mean(o3 * o3, -1, keepdims=True) + NORM_EPS) * nw
    for i, (bb, h) in enumerate(pairs):
        zh = z_ref[bb, :, hcols(h)]
        y_ref[bb, :, hcols(h)] = (o3[i] * (zh * _sigmoid(zh))).astype(y_ref.dtype)

    @pl.when(c == pl.num_programs(1) - 1)
    def _():
        sout_ref[...] = S[...]


def _gdn(qkv3, z3, ba3, ctx8, s0, cw, p1, p2, nw, lc, cp, nbb):
    nb, L, _ = qkv3.shape
    rb = lc * cp
    hd = GDN_HEADS * GDN_DV
    ltri = jnp.asarray(np.kron(np.eye(cp, dtype=np.float32), np.tril(np.ones((lc, lc), np.float32)))).astype(BF16)
    blk = lambda n: pl.BlockSpec((1, rb, n), lambda b, c: (b, c, 0))
    per_b = lambda shape: pl.BlockSpec((1,) + shape, lambda b, c: (b,) + (0,) * len(shape))
    cst = lambda shape: pl.BlockSpec(shape, lambda b, c: (0,) * len(shape))
    sds = lambda n, dt: jax.ShapeDtypeStruct((nb, L, n), dt)
    u0, w, qd, kd, attn, G, cout = pl.pallas_call(
        functools.partial(_gdn_local_kernel, lc=lc, cp=cp),
        grid=(nb, L // rb),
        in_specs=[blk(GDN_QKV), blk(LANES), per_b((SUBLANES, GDN_QKV)), cst(cw.shape), cst(p1.shape),
                  cst(p2.shape), cst(ltri.shape)],
        out_specs=[blk(hd), blk(hd), blk(hd), blk(hd), blk(GDN_HEADS * lc), blk(LANES),
                   per_b((SUBLANES, GDN_QKV))],
        out_shape=[sds(hd, F32), sds(hd, BF16), sds(hd, BF16), sds(hd, BF16), sds(GDN_HEADS * lc, BF16),
                   sds(LANES, F32), jax.ShapeDtypeStruct((nb, SUBLANES, GDN_QKV), F32)],
        scratch_shapes=[pltpu.VMEM((rb + SUBLANES, GDN_QKV), F32)],
        compiler_params=_params(("parallel", "arbitrary")),
        name="gdn_local",
    )(qkv3, ba3, ctx8, cw, p1, p2, ltri)
    sblk = lambda n: pl.BlockSpec((nbb, lc, n), lambda b, c: (b, c, 0))
    state = pl.BlockSpec((nbb, GDN_HEADS, GDN_DK, GDN_DV), lambda b, c: (b, 0, 0, 0))
    y, s_new = pl.pallas_call(
        functools.partial(_gdn_seq_kernel, lc=lc, nbb=nbb),
        grid=(nb // nbb, L // lc),
        in_specs=[sblk(hd), sblk(hd), sblk(hd), sblk(hd), sblk(GDN_HEADS * lc), sblk(LANES), sblk(hd),
                  state, cst(nw.shape)],
        out_specs=[sblk(hd), state],
        out_shape=[sds(hd, BF16), jax.ShapeDtypeStruct(s0.shape, F32)],
        scratch_shapes=[pltpu.VMEM((nbb, GDN_HEADS, GDN_DK, GDN_DV), F32)],
        compiler_params=_params(("parallel", "arbitrary")),
        name="gdn_seq",
    )(u0, w, qd, kd, attn, G, z3, s0, nw)
    return y, s_new, cout


def _ret_kernel(q_ref, k_ref, v_ref, g_ref, r0_ref, dec_ref, qs_ref, ks_ref, cd_ref, o_ref, rout_ref, R,
                *, nbb):
    c = pl.program_id(1)

    @pl.when(c == 0)
    def _():
        R[...] = r0_ref[...]

    pairs = [(bb, h) for bb in range(nbb) for h in range(RET_HEADS)]
    stack = lambda f: jnp.stack([f(bb, h) for bb, h in pairs])
    kcols = lambda h: slice(h * RET_DK, (h + 1) * RET_DK)
    vcols = lambda h: slice(h * RET_DV, (h + 1) * RET_DV)
    q3 = stack(lambda bb, h: q_ref[bb, :, kcols(h)])
    k3 = stack(lambda bb, h: k_ref[bb, :, kcols(h)])
    v3 = stack(lambda bb, h: v_ref[bb, :, vcols(h)])
    dec3 = stack(lambda bb, h: dec_ref[h])
    qs3 = stack(lambda bb, h: qs_ref[h])
    ks3 = stack(lambda bb, h: ks_ref[h])
    cd3 = stack(lambda bb, h: cd_ref[h])
    R3 = R[...].reshape(len(pairs), RET_DK, RET_DV)
    s3 = _bdot_nt(q3, k3) * dec3
    o3 = _bdot(s3.astype(BF16), v3) + _bdot(q3, R3.astype(BF16)) * qs3
    kv3 = lax.dot_general((k3.astype(F32) * ks3).astype(BF16), v3, (((1,), (1,)), ((0,), (0,))),
                          preferred_element_type=F32)
    R[...] = (cd3 * R3 + kv3).reshape(R.shape)
    mu = jnp.mean(o3, -1, keepdims=True)
    d3 = o3 - mu
    var = jnp.mean(d3 * d3, -1, keepdims=True)
    on3 = d3 * lax.rsqrt(var + LN_EPS)
    for i, (bb, h) in enumerate(pairs):
        gt = g_ref[bb, :, vcols(h)].astype(F32)
        o_ref[bb, :, vcols(h)] = (gt * _sigmoid(gt) * on3[i]).astype(o_ref.dtype)

    @pl.when(c == pl.num_programs(1) - 1)
    def _():
        rout_ref[...] = R[...]


def _retention(proj3, r0, lc, nbb):
    nb, L, _ = proj3.shape
    log_g = np.log(1.0 - 2.0 ** (-5.0 - np.arange(RET_HEADS, dtype=np.float64)))
    idx = np.arange(lc, dtype=np.float64)
    dec = np.exp(log_g[:, None, None] * np.abs(idx[:, None] - idx[None, :])).astype(np.float32)
    qs = np.exp(log_g[:, None] * (idx + 1.0)).astype(np.float32)[..., None]
    ks = np.exp(log_g[:, None] * (lc - 1.0 - idx)).astype(np.float32)[..., None]
    cdec = np.exp(log_g * lc).astype(np.float32)[:, None, None]
    nqk = RET_HEADS * RET_DK
    nv = RET_HEADS * RET_DV
    cst = lambda shape: pl.BlockSpec(shape, lambda b, c: (0,) * len(shape))
    state = pl.BlockSpec((nbb, RET_HEADS, RET_DK, RET_DV), lambda b, c: (b, 0, 0, 0))
    return pl.pallas_call(
        functools.partial(_ret_kernel, nbb=nbb),
        grid=(nb // nbb, L // lc),
        in_specs=[pl.BlockSpec((nbb, lc, nqk), lambda b, c: (b, c, 0)),
                  pl.BlockSpec((nbb, lc, nqk), lambda b, c: (b, c, 1)),
                  pl.BlockSpec((nbb, lc, nv), lambda b, c: (b, c, 1)),
                  pl.BlockSpec((nbb, lc, nv), lambda b, c: (b, c, 2)),
                  state, cst(dec.shape), cst(qs.shape), cst(ks.shape), cst(cdec.shape)],
        out_specs=[pl.BlockSpec((nbb, lc, nv), lambda b, c: (b, c, 0)), state],
        out_shape=[jax.ShapeDtypeStruct((nb, L, nv), BF16), jax.ShapeDtypeStruct(r0.shape, F32)],
        scratch_shapes=[pltpu.VMEM((nbb, RET_HEADS, RET_DK, RET_DV), F32)],
        compiler_params=_params(("parallel", "arbitrary")),
        name="retention",
    )(proj3, proj3, proj3, proj3, r0, jnp.asarray(dec), jnp.asarray(qs), jnp.asarray(ks), jnp.asarray(cdec))


RUN_ALIGN = SUBLANES
RUN_PIECE = 64
MOE_VMEM_LIMIT = 58 * 1024 * 1024


def _moe_params(sem):
    return pltpu.CompilerParams(dimension_semantics=sem, vmem_limit_bytes=MOE_VMEM_LIMIT)


def _tile_cap(tm):
    rows = TOP_K * tm + N_EXPERTS * (RUN_ALIGN - 1)
    return -(-rows // LANES) * LANES


def _route_kernel(x_ref, rw_ref, rb_ref, lst_ref, ust_ref, pos_ref, gt_ref, cnt_ref, xs_ref):
    tm = x_ref.shape[0]
    cap = xs_ref.shape[1]
    x = x_ref[...]
    lane = lax.broadcasted_iota(jnp.int32, (tm, LANES), 1)
    lane_f = lane.astype(F32)
    xh, xl = _split_bf16(x)
    wh, wl = _split_bf16(rw_ref[...])
    logits = _dot(xh, wh) + _dot(xh, wl) + _dot(xl, wh) + rb_ref[...]
    logits = jnp.where(lane < N_EXPERTS, logits, -jnp.inf)
    vals, hots = [], []
    for _ in range(TOP_K):
        m = jnp.max(logits, -1, keepdims=True)
        first = jnp.min(jnp.where(logits == m, lane_f, float(LANES)), -1, keepdims=True)
        hot = lane_f == first
        vals.append(m)
        hots.append(hot)
        logits = jnp.where(hot, -jnp.inf, logits)
    es = [jnp.exp(v - vals[0]) for v in vals]
    den = es[0] + es[1] + es[2] + es[3]
    multi = jnp.zeros((tm, LANES), F32)
    for hot in hots:
        multi = multi + hot.astype(F32)
    counts = jnp.sum(multi, 0, keepdims=True)
    units = jnp.floor((counts + (RUN_ALIGN - 1)) * (1.0 / RUN_ALIGN))
    offs = _dot(jnp.broadcast_to(units, (SUBLANES, LANES)).astype(BF16), ust_ref[...])[0:1] * float(RUN_ALIGN)
    before = _dot(lst_ref[...], multi.astype(BF16))
    slot = offs + before
    pos = jnp.zeros((tm, LANES), F32)
    gt = jnp.zeros((tm, LANES), F32)
    for kk in range(TOP_K):
        pos = jnp.where(lane == kk, jnp.sum(jnp.where(hots[kk], slot, 0.0), -1, keepdims=True), pos)
        gt = jnp.where(lane == kk, es[kk] / den, gt)
    pos = pos.astype(jnp.int32)
    pos_ref[...] = pos
    gt_ref[...] = gt
    cnt_ref[0] = counts
    pos_t = pos.T[0:2 * SUBLANES].astype(jnp.int16)
    row = lax.broadcasted_iota(jnp.int16, (cap, tm), 0)
    sel = jnp.zeros((cap, tm), BF16)
    for kk in range(TOP_K):
        sel = sel + jnp.where(row == pos_t[kk:kk + 1, :], jnp.ones((), BF16), jnp.zeros((), BF16))
    xs_ref[0] = _dot(sel, xh)


def _route(x, rw, rb, tm):
    t = x.shape[0]
    nt = t // tm
    cap = _tile_cap(tm)
    lst = jnp.asarray(np.tril(np.ones((tm, tm), np.float32), -1)).astype(BF16)
    ust = jnp.asarray(np.triu(np.ones((LANES, LANES), np.float32), 1)).astype(BF16)
    row = lambda n: pl.BlockSpec((tm, n), lambda i: (i, 0))
    return pl.pallas_call(
        _route_kernel,
        grid=(nt,),
        in_specs=[row(D_MODEL), _full(rw.shape), _full(rb.shape), _full(lst.shape), _full(ust.shape)],
        out_specs=[row(LANES), row(LANES), pl.BlockSpec((1, 1, LANES), lambda i: (i, 0, 0)),
                   pl.BlockSpec((1, cap, D_MODEL), lambda i: (i, 0, 0))],
        out_shape=[jax.ShapeDtypeStruct((t, LANES), jnp.int32), jax.ShapeDtypeStruct((t, LANES), F32),
                   jax.ShapeDtypeStruct((nt, 1, LANES), F32), jax.ShapeDtypeStruct((nt, cap, D_MODEL), F32)],
        compiler_params=_moe_params(("parallel",)),
        name="moe_route",
    )(x, rw, rb, lst, ust)


def _expert_kernel(be_ref, nu_ref, ilo_ref, rows_ref, gs_ref, n8_ref, lo_ref, *refs, blk, tiles):
    ng = len(tiles)
    xs_hbms = refs[:ng]
    w1_ref, b1_ref, w2_ref, b2_ref = refs[ng:ng + 4]
    ys_hbms = refs[ng + 4:2 * ng + 4]
    xbuf, ybuf, w1b, w2b, in_sem, out_sem = refs[2 * ng + 4:]
    nt = sum(tiles)
    firsts = [sum(tiles[:g]) for g in range(ng)]
    j = pl.program_id(0)
    nu = nu_ref[0]

    def for_each_run(jb, fn):
        e = be_ref[jb]
        base = jb * blk
        for g in range(ng):
            end = firsts[g] + tiles[g]

            def cond(i, end=end):
                return (i < end) & (gs_ref[e * nt + jnp.minimum(i, nt - 1)] < base + blk)

            def body(i, g=g):
                g0 = gs_ref[e * nt + i]
                first = jnp.maximum(g0, base)
                last = jnp.minimum(g0 + n8_ref[e * nt + i], base + blk)
                fn(g, i - firsts[g], lo_ref[e * nt + i] + (first - g0), first - base, last - first)
                return i + 1

            lax.while_loop(cond, body, jnp.clip(ilo_ref[jb], firsts[g], end))

    def pieces(length, fn):
        def digits(sizes):
            for size in sizes:
                @pl.when((length & size) != 0)
                def _(size=size):
                    fn(length & ~(2 * size - 1), size)

        sizes = [blk >> k for k in range(blk.bit_length()) if blk >> k >= RUN_ALIGN]

        @pl.when(length > 2 * RUN_PIECE - 1)
        def _():
            digits([z for z in sizes if z > RUN_PIECE])

        digits([z for z in sizes if z <= RUN_PIECE])

    def aligned(v, size):
        return pl.ds(pl.multiple_of(v, RUN_ALIGN), size)

    def copy_in(jb, slot):
        def run(g, ig, src, dst, length):
            def piece(off, size):
                pltpu.make_async_copy(xs_hbms[g].at[ig, aligned(src + off, size)],
                                      xbuf.at[slot, aligned(dst + off, size)], in_sem.at[slot]).start()
            pieces(length, piece)
        for_each_run(jb, run)

    def copy_out(jb, slot):
        def run(g, ig, src, dst, length):
            def piece(off, size):
                pltpu.make_async_copy(ybuf.at[slot, aligned(dst + off, size)],
                                      ys_hbms[g].at[ig, aligned(src + off, size)], out_sem.at[slot]).start()
            pieces(length, piece)
        for_each_run(jb, run)

    def wait_rows(sem, nrows):
        size = blk
        while size >= RUN_ALIGN:
            @pl.when((nrows & size) != 0)
            def _(size=size):
                pltpu.make_async_copy(xbuf.at[1, pl.ds(0, size)], xbuf.at[0, pl.ds(0, size)], sem).wait()
            size //= 2

    @pl.when(j < nu)
    def _():
        slot = j % 2

        @pl.when(j == 0)
        def _():
            xbuf[...] = jnp.zeros_like(xbuf)
            copy_in(0, 0)

        wait_rows(in_sem.at[slot], rows_ref[j])

        @pl.when(j + 1 < nu)
        def _():
            copy_in(j + 1, 1 - slot)

        @pl.when(j >= 2)
        def _():
            wait_rows(out_sem.at[slot], rows_ref[jnp.maximum(j - 2, 0)])

        @pl.when((j == 0) | (be_ref[j] != be_ref[jnp.maximum(j - 1, 0)]))
        def _():
            w1b[...] = w1_ref[0, 0].astype(BF16)
            w2b[...] = w2_ref[0, 0].astype(BF16)

        h = _dot(xbuf[slot].astype(BF16), w1b[...]) + b1_ref[0]
        glu = jnp.minimum(h[:, :D_FF], SWIGLU_LIMIT)
        lin = jnp.clip(h[:, D_FF:], -SWIGLU_LIMIT, SWIGLU_LIMIT)
        act = glu * _sigmoid(SWIGLU_ALPHA * glu) * (lin + 1.0)
        ybuf[slot] = _dot(act.astype(BF16), w2b[...]) + b2_ref[0]
        copy_out(j, slot)

        @pl.when(j == nu - 1)
        def _():
            wait_rows(out_sem.at[slot], rows_ref[j])

            @pl.when(j >= 1)
            def _():
                wait_rows(out_sem.at[1 - slot], rows_ref[jnp.maximum(j - 1, 0)])


def _experts(tables, xs_list, w1, b1, w2, b2, layer, blk):
    ng = len(xs_list)
    n_blk = tables[0].shape[0]
    wspec = lambda shape: pl.BlockSpec((1, 1) + shape, lambda j, be, *_: (layer, be[j], 0, 0))
    bspec = lambda n: pl.BlockSpec((1, 1, n), lambda j, be, *_: (be[j], 0, 0))
    hbm = pl.BlockSpec(memory_space=pl.ANY)
    grid_spec = pltpu.PrefetchScalarGridSpec(
        num_scalar_prefetch=len(tables),
        grid=(n_blk,),
        in_specs=[hbm] * ng + [wspec((D_MODEL, 2 * D_FF)), bspec(2 * D_FF), wspec((D_FF, D_MODEL)), bspec(D_MODEL)],
        out_specs=[hbm] * ng,
        scratch_shapes=[pltpu.VMEM((2, blk, D_MODEL), F32), pltpu.VMEM((2, blk, D_MODEL), F32),
                        pltpu.VMEM((D_MODEL, 2 * D_FF), BF16), pltpu.VMEM((D_FF, D_MODEL), BF16),
                        pltpu.SemaphoreType.DMA((2,)), pltpu.SemaphoreType.DMA((2,))],
    )
    return pl.pallas_call(
        functools.partial(_expert_kernel, blk=blk, tiles=tuple(xs.shape[0] for xs in xs_list)),
        grid_spec=grid_spec,
        out_shape=[jax.ShapeDtypeStruct(xs.shape, F32) for xs in xs_list],
        input_output_aliases={len(tables) + g: g for g in range(ng)},
        compiler_params=_moe_params(("arbitrary",)),
        name="moe_experts",
    )(*tables, *xs_list, w1, b1, w2, b2)


def _combine_kernel(ys_ref, pos_ref, gt_ref, x_ref, p_ref, g_ref, b_ref, plew_ref, gatew_ref, o_ref):
    tm = x_ref.shape[0]
    cap = ys_ref.shape[1]
    pos = pos_ref[...].astype(jnp.int16)
    gt = gt_ref[...].astype(BF16)
    col = lax.broadcasted_iota(jnp.int16, (tm, cap), 1)
    sel = jnp.zeros((tm, cap), BF16)
    for kk in range(TOP_K):
        sel = sel + jnp.where(col == pos[:, kk:kk + 1], gt[:, kk:kk + 1], jnp.zeros((), BF16))
    y = _dot(sel, ys_ref[0].astype(BF16))
    x2 = _layer_norm(DEEPNORM_ALPHA * x_ref[...] + y, g_ref[...], b_ref[...])
    pp = _dot(p_ref[...].astype(BF16), plew_ref[...])
    gg = _sigmoid(_dot(x2.astype(BF16), gatew_ref[...]))
    o_ref[...] = x2 + pp * gg


def _combine(ys, pos, gt, x, p, g, b, plew, gatew, tm):
    t = x.shape[0]
    cap = ys.shape[1]
    row = lambda n: pl.BlockSpec((tm, n), lambda i: (i, 0))
    return pl.pallas_call(
        _combine_kernel,
        grid=(t // tm,),
        in_specs=[pl.BlockSpec((1, cap, D_MODEL), lambda i: (i, 0, 0)), row(LANES), row(LANES),
                  row(D_MODEL), row(PLE_DIM), _full((1, D_MODEL)), _full((1, D_MODEL)),
                  _full(plew.shape), _full(gatew.shape)],
        out_specs=row(D_MODEL),
        out_shape=jax.ShapeDtypeStruct((t, D_MODEL), F32),
        compiler_params=_moe_params(("parallel",)),
        name="moe_combine",
    )(ys, pos, gt, x, p, g, b, plew, gatew)


def _moe_tables(cnt, t, blk):
    nt = cnt.shape[0]
    n = cnt[:, 0, :N_EXPERTS].astype(jnp.int32)
    n8 = (n + RUN_ALIGN - 1) // RUN_ALIGN * RUN_ALIGN
    lo = jnp.cumsum(n8, axis=1) - n8
    rows_e = jnp.sum(n8, axis=0)
    padded = (rows_e + blk - 1) // blk * blk
    pend = jnp.cumsum(padded)
    pstart = pend - padded
    gstart = pstart[None, :] + jnp.cumsum(n8, axis=0) - n8
    n_blk = -(-(TOP_K * t + nt * N_EXPERTS * (RUN_ALIGN - 1) + N_EXPERTS * (blk - 1)) // blk)
    n_used = (pend[-1] // blk).astype(jnp.int32).reshape(1)
    blk_start = jnp.arange(n_blk, dtype=jnp.int32) * blk
    blk_e = jnp.minimum(jnp.sum((pend[None, :] <= blk_start[:, None]).astype(jnp.int32), axis=1), N_EXPERTS - 1)
    run_end = (gstart + n8)[:, blk_e]
    ilo = jnp.sum((run_end <= blk_start[None, :]).astype(jnp.int32), axis=0)
    rows_b = jnp.clip((pstart + rows_e)[blk_e] - blk_start, 0, blk)
    flat = lambda a: a.T.reshape(-1).astype(jnp.int32)
    i32 = lambda a: a.astype(jnp.int32)
    return (i32(blk_e), n_used, i32(ilo), i32(rows_b), flat(gstart), flat(n8), flat(lo))


def _moe_ple(xs_in, ps, rw, rb, w1, b1, w2, b2, layer, g, b, plew, gatew, tms, blk):
    routed = [_route(x, rw, rb, tm) for x, tm in zip(xs_in, tms)]
    cnt = jnp.concatenate([r[2] for r in routed], axis=0)
    tables = _moe_tables(cnt, sum(x.shape[0] for x in xs_in), blk)
    ys = _experts(tables, [r[3] for r in routed], w1, b1, w2, b2, layer, blk)
    return [_combine(y, r[0], r[1], x, p, g, b, plew, gatew, tm)
            for y, r, x, p, tm in zip(ys, routed, xs_in, ps, tms)]


_ROT_PERM = np.concatenate([np.arange(0, RET_DK, 2), np.arange(1, RET_DK, 2)])
_ROT_INV = np.argsort(_ROT_PERM)


def _lane_row(vals, offset):
    row = jnp.zeros((1, LANES), F32)
    return row.at[0, offset:offset + vals.shape[0]].set(vals)


def _even_mixer(x, nb, L, s5_re, s5_im, gdn_s, conv_s, W, tm):
    t = nb * L
    lc = L if L <= CHUNK else CHUNK
    u, qkv, z, ba = _proj_even(x, W['wu'], W['wqkv'], W['wz'], W['wba'], tm)
    yA3, h_new = _s5(u.reshape(nb, L, S5_WIDTH), _s5_state_in(s5_re[0].astype(F32), s5_im[0].astype(F32)),
                     W['bmat'], W['cmat'], W['acoef'], W['dskip'], W['wglu'], W['bglu'], lc)
    new_re, new_im = _s5_state_out(h_new)
    ctx8 = jnp.concatenate([jnp.zeros((nb, SUBLANES - (GDN_CONV - 1), GDN_QKV), F32), conv_s[0].astype(F32)], axis=1)
    yB3, new_gdn, cout = _gdn(qkv.reshape(nb, L, GDN_QKV), z.reshape(nb, L, -1), ba.reshape(nb, L, LANES),
                              ctx8, gdn_s[0].astype(F32), W['convw'], W['p1'], W['p2'], W['normw'], lc,
                              cp=min(4, L // lc), nbb=4)
    new_conv = cout[:, SUBLANES - (GDN_CONV - 1):, :]
    x = _outproj_ln([yA3.reshape(t, -1), yB3.reshape(t, -1)], [W['wout_a'], W['wout_b']], x,
                    W['ln1_g'][0], W['ln1_b'][0], tm)
    return x, new_re[None], new_im[None], new_gdn[None], new_conv[None]


def _odd_mixer(x, nb, L, ret_s, pos0, W, tm):
    t = nb * L
    lc = L if L <= CHUNK else CHUNK
    pos = pos0 + jnp.arange(L, dtype=F32)
    freq = 1.0 / (ROPE_BASE ** jnp.linspace(0.0, 1.0, RET_DK // 2, dtype=F32))
    ang = pos[:, None] * freq[None]
    reps = max(1, tm // L)
    cos = jnp.tile(jnp.cos(ang), (reps, 1))
    sin = jnp.tile(jnp.sin(ang), (reps, 1))
    proj = _proj_odd(x, W['win_odd'], cos, sin, tm)
    r0 = ret_s[0].astype(F32)[:, :, _ROT_PERM, :]
    o3, r_new = _retention(proj.reshape(nb, L, -1), r0, lc, nbb=2)
    new_ret = r_new[:, :, _ROT_INV, :]
    x = _outproj_ln([o3.reshape(t, -1)], [W['wout_odd']], x, W['ln1_g'][1], W['ln1_b'][1], tm)
    return x, new_ret[None]


def kernel(x_prompt, x_sample, state_s5_re, state_s5_im, state_gdn, state_gdn_conv, state_ret, p_prompt, p_sample, w_in_even, s5_a_re, s5_a_im, s5_log_dt, s5_b_re, s5_b_im, s5_c_re, s5_c_im, s5_d, s5_w_glu, s5_b_glu, gdn_conv_w, gdn_a_log, gdn_dt_bias, gdn_norm_w, w_out_even, w_in_odd, w_out_odd, ln1_g, ln1_b, ln2_g, ln2_b, router_w, router_b, moe_w1, moe_b1, moe_w2, moe_b2, ple_w, ple_gate_w):
    o1 = S5_WIDTH
    o2 = o1 + GDN_QKV
    o3 = o2 + GDN_HEADS * GDN_DV
    win = w_in_even[0]
    bmat, cmat, acoef = _s5_weights(s5_a_re[0], s5_a_im[0], s5_log_dt[0], s5_b_re[0], s5_b_im[0],
                                    s5_c_re[0], s5_c_im[0])
    wodd = w_in_odd[0]
    nk = RET_HEADS * RET_DK
    perm_cols = lambda w: w.reshape(D_MODEL, RET_HEADS, RET_DK)[:, :, _ROT_PERM].reshape(D_MODEL, nk)
    W = dict(
        wu=win[:, :o1].astype(BF16), wqkv=win[:, o1:o2].astype(BF16), wz=win[:, o2:o3].astype(BF16),
        wba=jnp.pad(win[:, o3:], ((0, 0), (0, LANES - 2 * GDN_HEADS))).astype(BF16),
        bmat=bmat, cmat=cmat, acoef=acoef, dskip=s5_d[0][None], wglu=s5_w_glu[0].astype(BF16),
        bglu=s5_b_glu[0][None], convw=gdn_conv_w[0],
        p1=_lane_row(-jnp.exp(gdn_a_log[0]), GDN_HEADS), p2=_lane_row(gdn_dt_bias[0], GDN_HEADS),
        normw=gdn_norm_w[0][None],
        wout_a=w_out_even[0][:S5_WIDTH].astype(BF16), wout_b=w_out_even[0][S5_WIDTH:].astype(BF16),
        win_odd=jnp.concatenate([perm_cols(wodd[:, :nk]), perm_cols(wodd[:, nk:2 * nk]), wodd[:, 2 * nk:]],
                                axis=1).astype(BF16),
        wout_odd=w_out_odd[0].astype(BF16),
        ln1_g=ln1_g[:, None], ln1_b=ln1_b[:, None], ln2_g=ln2_g[:, None], ln2_b=ln2_b[:, None],
        rw=jnp.pad(router_w, ((0, 0), (0, 0), (0, LANES - N_EXPERTS))),
        rb=jnp.pad(router_b, ((0, 0), (0, LANES - N_EXPERTS)))[:, None],
        w1=moe_w1, b1=moe_b1[:, :, None], w2=moe_w2, b2=moe_b2[:, :, None],
        plew=ple_w.astype(BF16), gatew=ple_gate_w.astype(BF16),
    )
    bp, lp, _ = x_prompt.shape
    bs, ls, _ = x_sample.shape
    zeros = lambda *s: jnp.zeros(s, F32)
    shapes = [(bp, lp), (bs, ls)]
    tms = [512, 128]
    ps = [p_prompt, p_sample]
    xs = [x_prompt.reshape(bp * lp, D_MODEL).astype(F32), x_sample.reshape(bs * ls, D_MODEL).astype(F32)]
    even_states = [(zeros(1, bp, S5_GROUPS, S5_STATE), zeros(1, bp, S5_GROUPS, S5_STATE),
                    zeros(1, bp, GDN_HEADS, GDN_DK, GDN_DV), zeros(1, bp, GDN_CONV - 1, GDN_QKV)),
                   (state_s5_re, state_s5_im, state_gdn, state_gdn_conv)]
    ret_states = [zeros(1, bp, RET_HEADS, RET_DK, RET_DV), state_ret]
    pos0s = [0.0, float(PAST_LEN)]

    def moe(xs, layer):
        return _moe_ple(xs, [p[layer].reshape(-1, PLE_DIM) for p in ps], W['rw'][layer], W['rb'][layer],
                        W['w1'], W['b1'][layer], W['w2'], W['b2'][layer], layer, W['ln2_g'][layer],
                        W['ln2_b'][layer], W['plew'][layer], W['gatew'][layer], tms, blk=512)

    even = [_even_mixer(x, nb, L, *st, W, tm) for x, (nb, L), st, tm in zip(xs, shapes, even_states, tms)]
    xs = moe([e[0] for e in even], 0)
    odd = [_odd_mixer(x, nb, L, st, pos0, W, tm)
           for x, (nb, L), st, pos0, tm in zip(xs, shapes, ret_states, pos0s, tms)]
    xs = moe([o[0] for o in odd], 1)
    dp = x_prompt.dtype
    y_p, y_s = xs[0].reshape(bp, lp, D_MODEL), xs[1].reshape(bs, ls, D_MODEL)
    (_, p_re, p_im, p_gdn, p_conv), (_, s_re, s_im, s_gdn, s_conv) = even
    p_ret, s_ret = odd[0][1], odd[1][1]
    return (y_p.astype(dp), y_s.astype(x_sample.dtype),
            p_re.astype(dp), p_im.astype(dp), p_gdn.astype(dp), p_conv.astype(dp), p_ret.astype(dp),
            s_re.astype(state_s5_re.dtype), s_im.astype(state_s5_im.dtype), s_gdn.astype(state_gdn.dtype),
            s_conv.astype(state_gdn_conv.dtype), s_ret.astype(state_ret.dtype))
```

```python
import functools
import math

import jax
import jax.numpy as jnp
import numpy as np
from jax import lax
from jax.experimental import pallas as pl
from jax.experimental.pallas import tpu as pltpu

F32 = jnp.float32
BF16 = jnp.bfloat16
HIGHEST = lax.Precision.HIGHEST

D_MODEL = 1024
CHUNK = 64
S5_WIDTH = 512
S5_GROUP = 16
S5_GROUPS = 32
S5_STATE = 64
S5_TILES = 16
GDN_HEADS = 4
GDN_DK = 128
GDN_DV = 128
GDN_CONV = 4
GDN_QKV = 1536
RET_HEADS = 4
RET_DK = 256
RET_DV = 512
ROPE_BASE = 10000.0
N_EXPERTS = 32
TOP_K = 4
D_FF = 1024
SWIGLU_LIMIT = 7.0
SWIGLU_ALPHA = 1.702
PLE_DIM = 256
DEPTH = 2
PAST_LEN = 1024
DEEPNORM_ALPHA = (2 * DEPTH) ** 0.25
LN_EPS = 1e-5
NORM_EPS = 1e-6

LANES = 128
SUBLANES = 8
VMEM_LIMIT = 48 * 1024 * 1024

def _params(sem):
    return pltpu.CompilerParams(dimension_semantics=sem, vmem_limit_bytes=VMEM_LIMIT)


def _dot(a, b):
    return jnp.dot(a, b, preferred_element_type=F32)


def _dot_hi(a, b):
    return jnp.dot(a, b, preferred_element_type=F32, precision=HIGHEST)


def _dot_nt(a, b, precision=None):
    return lax.dot_general(a, b, (((1,), (1,)), ((), ())), preferred_element_type=F32,
                           precision=precision)


def _dot_tn(a, b):
    return lax.dot_general(a, b, (((0,), (0,)), ((), ())), preferred_element_type=F32)


def _sigmoid(x):
    return 1.0 / (1.0 + jnp.exp(-x))


def _full(shape):
    nd = len(shape)
    return pl.BlockSpec(shape, lambda *_: (0,) * nd)


def _proj_even_kernel(x_ref, wu_ref, wqkv_ref, wz_ref, wba_ref, u_ref, qkv_ref, z_ref, ba_ref):
    xb = x_ref[...].astype(BF16)
    u_ref[...] = _dot(xb, wu_ref[...])
    qkv_ref[...] = _dot(xb, wqkv_ref[...])
    z_ref[...] = _dot(xb, wz_ref[...])
    ba_ref[...] = _dot(xb, wba_ref[...])


def _proj_even(x, wu, wqkv, wz, wba, tm):
    t = x.shape[0]
    row = lambda n: pl.BlockSpec((tm, n), lambda i: (i, 0))
    return pl.pallas_call(
        _proj_even_kernel,
        grid=(t // tm,),
        in_specs=[row(D_MODEL), _full(wu.shape), _full(wqkv.shape), _full(wz.shape), _full(wba.shape)],
        out_specs=[row(S5_WIDTH), row(GDN_QKV), row(GDN_HEADS * GDN_DV), row(LANES)],
        out_shape=[jax.ShapeDtypeStruct((t, S5_WIDTH), F32), jax.ShapeDtypeStruct((t, GDN_QKV), F32),
                   jax.ShapeDtypeStruct((t, GDN_HEADS * GDN_DV), F32), jax.ShapeDtypeStruct((t, LANES), F32)],
        compiler_params=_params(("parallel",)),
        name="proj_even",
    )(x, wu, wqkv, wz, wba)


def _proj_odd_kernel(x_ref, w_ref, cos_ref, sin_ref, o_ref):
    j = pl.program_id(0)
    acc = _dot(x_ref[...].astype(BF16), w_ref[...])

    @pl.when(j == 0)
    def _():
        cos, sin = cos_ref[...], sin_ref[...]
        half = RET_DK // 2
        for h in range(2 * RET_HEADS):
            x0 = acc[:, h * RET_DK:h * RET_DK + half]
            x1 = acc[:, h * RET_DK + half:(h + 1) * RET_DK]
            scale = 1.0 if h < RET_HEADS else RET_DK ** -0.5
            o_ref[:, h * RET_DK:h * RET_DK + half] = ((x0 * cos - x1 * sin) * scale).astype(o_ref.dtype)
            o_ref[:, h * RET_DK + half:(h + 1) * RET_DK] = ((x0 * sin + x1 * cos) * scale).astype(o_ref.dtype)

    @pl.when(j != 0)
    def _():
        o_ref[...] = acc.astype(o_ref.dtype)


def _proj_odd(x, w, cos, sin, tm):
    t = x.shape[0]
    nblk = w.shape[1] // 2048
    period = cos.shape[0] // tm
    return pl.pallas_call(
        _proj_odd_kernel,
        grid=(nblk, t // tm),
        in_specs=[pl.BlockSpec((tm, D_MODEL), lambda j, i: (i, 0)),
                  pl.BlockSpec((D_MODEL, 2048), lambda j, i: (0, j)),
                  pl.BlockSpec((tm, LANES), lambda j, i: (i % period, 0)),
                  pl.BlockSpec((tm, LANES), lambda j, i: (i % period, 0))],
        out_specs=pl.BlockSpec((tm, 2048), lambda j, i: (i, j)),
        out_shape=jax.ShapeDtypeStruct((t, w.shape[1]), BF16),
        compiler_params=_params(("parallel", "parallel")),
        name="proj_odd",
    )(x, w, cos, sin)


def _layer_norm(r, g, b):
    mu = jnp.mean(r, -1, keepdims=True)
    d = r - mu
    var = jnp.mean(d * d, -1, keepdims=True)
    return d * lax.rsqrt(var + LN_EPS) * g + b


def _outproj_ln_kernel(*refs, n_in):
    a_refs = refs[:n_in]
    w_refs = refs[n_in:2 * n_in]
    x_ref, g_ref, b_ref, o_ref = refs[2 * n_in:]
    acc = _dot(a_refs[0][...], w_refs[0][...])
    for a_ref, w_ref in zip(a_refs[1:], w_refs[1:]):
        acc = acc + _dot(a_ref[...], w_ref[...])
    o_ref[...] = _layer_norm(DEEPNORM_ALPHA * x_ref[...] + acc, g_ref[...], b_ref[...])


def _outproj_ln(acts, ws, x, g, b, tm):
    t = x.shape[0]
    row = lambda n: pl.BlockSpec((tm, n), lambda i: (i, 0))
    return pl.pallas_call(
        functools.partial(_outproj_ln_kernel, n_in=len(acts)),
        grid=(t // tm,),
        in_specs=[row(a.shape[1]) for a in acts] + [_full(w.shape) for w in ws]
                 + [row(D_MODEL), _full((1, D_MODEL)), _full((1, D_MODEL))],
        out_specs=row(D_MODEL),
        out_shape=jax.ShapeDtypeStruct((t, D_MODEL), F32),
        compiler_params=_params(("parallel",)),
        name="outproj_ln",
    )(*acts, *ws, x, g, b)


S5_LANE_BLOCKS = S5_WIDTH // LANES
S5_TILES_PER_BLOCK = S5_TILES // S5_LANE_BLOCKS


def _s5_kernel(u_ref, h0_ref, bmat_ref, cmat_ref, acoef_ref, dskip_ref, wglu_ref, bglu_ref,
               y_ref, hout_ref, utm, sre, sim, ytm, hst, *, nb, lt):
    tb = pl.program_id(0)

    @pl.when(tb == 0)
    def _():
        hst[...] = h0_ref[...]

    for b in range(nb):
        for q in range(S5_LANE_BLOCKS):
            utm[q, pl.ds(b, lt, stride=nb), :] = u_ref[b, :, q * LANES:(q + 1) * LANES]

    def block_body(q, carry):
        bu = _dot(utm[q].astype(BF16), bmat_ref[q])
        for g in range(S5_TILES_PER_BLOCK):
            sre[g] = bu[:, (2 * g) * LANES:(2 * g + 1) * LANES]
            sim[g] = bu[:, (2 * g + 1) * LANES:(2 * g + 2) * LANES]
        js = [q * S5_TILES_PER_BLOCK + g for g in range(S5_TILES_PER_BLOCK)]
        acs = [acoef_ref[j] for j in js]
        hs = [(hst[j, :, 0:LANES], hst[j, :, LANES:2 * LANES]) for j in js]
        for t in range(lt):
            rows = slice(t * nb, (t + 1) * nb)
            for g in range(S5_TILES_PER_BLOCK):
                ar, ai = acs[g][:, :LANES], acs[g][:, LANES:]
                hr, hi = hs[g]
                nhr = ar * hr - ai * hi + sre[g, rows, :]
                nhi = ar * hi + ai * hr + sim[g, rows, :]
                sre[g, rows, :] = nhr
                sim[g, rows, :] = nhi
                hs[g] = (nhr, nhi)
        for g, j in enumerate(js):
            hst[j, :, 0:LANES] = hs[g][0]
            hst[j, :, LANES:2 * LANES] = hs[g][1]
        st = jnp.concatenate([part for g in range(S5_TILES_PER_BLOCK) for part in (sre[g], sim[g])], axis=-1)
        ytm[q] = _dot(st.astype(BF16), cmat_ref[q])
        return carry

    lax.fori_loop(0, S5_LANE_BLOCKS, block_body, 0)
    y = (jnp.concatenate([ytm[q] for q in range(S5_LANE_BLOCKS)], axis=-1)
         + dskip_ref[...] * jnp.concatenate([utm[q] for q in range(S5_LANE_BLOCKS)], axis=-1))
    y = jax.nn.gelu(y)
    y = y * _sigmoid(_dot(y.astype(BF16), wglu_ref[...]) + bglu_ref[...])
    for q in range(S5_LANE_BLOCKS):
        ytm[q] = y[:, q * LANES:(q + 1) * LANES]
    for b in range(nb):
        for q in range(S5_LANE_BLOCKS):
            y_ref[b, :, q * LANES:(q + 1) * LANES] = ytm[q, pl.ds(b, lt, stride=nb), :].astype(y_ref.dtype)
    hout_ref[...] = hst[...]


def _s5(u3, h0, bmat, cmat, acoef, dskip, wglu, bglu, lt):
    nb, L, _ = u3.shape
    rows = nb * lt
    tpb = S5_TILES_PER_BLOCK
    bmat = bmat.reshape(S5_LANE_BLOCKS, tpb, LANES, 2 * LANES).transpose(0, 2, 1, 3).reshape(
        S5_LANE_BLOCKS, LANES, tpb * 2 * LANES)
    cmat = cmat.reshape(S5_LANE_BLOCKS, tpb * 2 * LANES, LANES)
    acoef = jnp.broadcast_to(acoef[:, :1], (S5_TILES, nb, 2 * LANES))
    return pl.pallas_call(
        functools.partial(_s5_kernel, nb=nb, lt=lt),
        grid=(L // lt,),
        in_specs=[pl.BlockSpec((nb, lt, S5_WIDTH), lambda i: (0, i, 0)),
                  _full(h0.shape), _full(bmat.shape), _full(cmat.shape), _full(acoef.shape),
                  _full(dskip.shape), _full(wglu.shape), _full(bglu.shape)],
        out_specs=[pl.BlockSpec((nb, lt, S5_WIDTH), lambda i: (0, i, 0)), _full(h0.shape)],
        out_shape=[jax.ShapeDtypeStruct((nb, L, S5_WIDTH), BF16), jax.ShapeDtypeStruct(h0.shape, F32)],
        scratch_shapes=[pltpu.VMEM((S5_LANE_BLOCKS, rows, LANES), F32), pltpu.VMEM((tpb, rows, LANES), F32),
                        pltpu.VMEM((tpb, rows, LANES), F32), pltpu.VMEM((S5_LANE_BLOCKS, rows, LANES), F32),
                        pltpu.VMEM(h0.shape, F32)],
        compiler_params=_params(("arbitrary",)),
        name="s5_scan",
    )(u3, h0, bmat, cmat, acoef, dskip, wglu, bglu)


def _s5_weights(a_re, a_im, log_dt, b_re, b_im, c_re, c_im):
    dt = jnp.exp(log_dt)[:, None]
    lr, li = a_re * dt, a_im * dt
    mag = jnp.exp(lr)
    ab_re, ab_im = mag * jnp.cos(li), mag * jnp.sin(li)
    den = a_re * a_re + a_im * a_im
    cf_re = ((ab_re - 1.0) * a_re + ab_im * a_im) / den
    cf_im = (ab_im * a_re - (ab_re - 1.0) * a_im) / den
    bb_re = cf_re[..., None] * b_re - cf_im[..., None] * b_im
    bb_im = cf_re[..., None] * b_im + cf_im[..., None] * b_re
    jj = np.arange(S5_TILES)[:, None, None]
    lg = np.arange(8)[None, :, None]
    gi = np.arange(2)[None, None, :]
    sel = jnp.asarray((lg == 2 * (jj % 4) + gi).astype(np.float32))
    tiles = lambda w: w.reshape(S5_TILES, 2, *w.shape[1:])
    bt = lambda w: jnp.einsum('jlg,jgpn->jlngp', sel, tiles(w)).reshape(S5_TILES, LANES, LANES)
    bmat = jnp.concatenate([bt(bb_re), bt(bb_im)], axis=-1)
    ct = lambda w: jnp.einsum('jlg,jgnp->jgpln', sel, tiles(w)).reshape(S5_TILES, LANES, LANES)
    cmat = jnp.concatenate([ct(c_re), -ct(c_im)], axis=1)
    acoef = jnp.concatenate([ab_re.reshape(S5_TILES, LANES), ab_im.reshape(S5_TILES, LANES)], axis=-1)
    acoef = jnp.broadcast_to(acoef[:, None, :], (S5_TILES, SUBLANES, 2 * LANES))
    return bmat.astype(BF16), cmat.astype(BF16), acoef


def _s5_state_in(h_re, h_im):
    nb = h_re.shape[0]
    h = jnp.concatenate([h_re.reshape(nb, S5_TILES, LANES), h_im.reshape(nb, S5_TILES, LANES)], axis=-1)
    return jnp.transpose(h, (1, 0, 2))


def _s5_state_out(h):
    nb = h.shape[1]
    h = jnp.transpose(h, (1, 0, 2))
    return (h[..., :LANES].reshape(nb, S5_GROUPS, S5_STATE), h[..., LANES:].reshape(nb, S5_GROUPS, S5_STATE))


def _split_bf16(a):
    hi = a.astype(BF16)
    return hi, (a - hi.astype(F32)).astype(BF16)


def _bdot(a, b):
    return lax.dot_general(a, b, (((2,), (1,)), ((0,), (0,))), preferred_element_type=F32)


def _bdot_nt(a, b):
    return lax.dot_general(a, b, (((2,), (2,)), ((0,), (0,))), preferred_element_type=F32)


def _bdot_bf16(a, b):
    return _bdot(a.astype(BF16), b.astype(BF16))


def _unit_lower_inverse(nmat):
    lc = nmat.shape[-1]
    ri = lax.broadcasted_iota(jnp.int32, nmat.shape, 1)
    ci = lax.broadcasted_iota(jnp.int32, nmat.shape, 2)
    base = 16
    dmat = jnp.where(ri // base == ci // base, nmat, 0.0)
    inv = jnp.where(ri == ci, 1.0, 0.0) - dmat
    pw = dmat
    for _ in range(3):
        pw = _bdot_bf16(pw, pw)
        inv = inv + _bdot_bf16(inv, pw)
    size = base
    while size < lc:
        off = jnp.where(ri // (2 * size) == ci // (2 * size), jnp.where(ri // size > ci // size, nmat, 0.0), 0.0)
        inv = inv - _bdot_bf16(_bdot_bf16(inv, off), inv)
        size *= 2
    return inv


def _gdn_local_kernel(qkv_ref, ba_ref, ctx_ref, cw_ref, p1_ref, p2_ref, ltri_ref,
                      u0_ref, w_ref, qd_ref, kd_ref, attn_ref, g_ref, cout_ref, xpad, *, lc, cp):
    c = pl.program_id(1)
    rb = lc * cp

    @pl.when(c == 0)
    def _():
        xpad[0:SUBLANES, :] = ctx_ref[0]

    xpad[SUBLANES:SUBLANES + rb, :] = qkv_ref[0]
    cw = cw_ref[...]
    conv = (cw[3:4] * xpad[8:8 + rb, :] + cw[2:3] * xpad[7:7 + rb, :]
            + cw[1:2] * xpad[6:6 + rb, :] + cw[0:1] * xpad[5:5 + rb, :])
    tail = xpad[rb:rb + SUBLANES, :]
    xpad[0:SUBLANES, :] = tail
    cout_ref[0] = tail
    a = conv * _sigmoid(conv)

    ba = ba_ref[0]
    beta_all = _sigmoid(ba)
    sp_in = ba + p2_ref[...]
    softplus = jnp.maximum(sp_in, 0.0) + jnp.log(1.0 + jnp.exp(-jnp.abs(sp_in)))
    g_all = p1_ref[...] * softplus
    g_hi = g_all.astype(BF16)
    g_r = g_all - g_hi.astype(F32)
    g_mid = g_r.astype(BF16)
    g_lo = (g_r - g_mid.astype(F32)).astype(BF16)
    lt = ltri_ref[...]
    G = _dot(lt, g_hi) + _dot(lt, g_mid) + _dot(lt, g_lo)
    g_ref[0] = G
    GT = G.T
    pairs = [(h, cc) for h in range(GDN_HEADS) for cc in range(cp)]
    qs, ks, vs, betas, gcols, grows, glasts = [], [], [], [], [], [], []
    for h in range(GDN_HEADS):
        qa = a[:, h * GDN_DK:(h + 1) * GDN_DK]
        ka = a[:, (GDN_HEADS + h) * GDN_DK:(GDN_HEADS + h + 1) * GDN_DK]
        va = a[:, (2 * GDN_HEADS + h) * GDN_DK:(2 * GDN_HEADS + h + 1) * GDN_DK]
        qa = qa * lax.rsqrt(jnp.sum(qa * qa, -1, keepdims=True) + NORM_EPS) * (GDN_DK ** -0.5)
        ka = ka * lax.rsqrt(jnp.sum(ka * ka, -1, keepdims=True) + NORM_EPS)
        for cc in range(cp):
            rows = slice(cc * lc, (cc + 1) * lc)
            qs.append(qa[rows])
            ks.append(ka[rows])
            vs.append(va[rows])
            betas.append(beta_all[rows, h:h + 1])
            gcols.append(G[rows, GDN_HEADS + h:GDN_HEADS + h + 1])
            grows.append(GT[GDN_HEADS + h:GDN_HEADS + h + 1, cc * lc:(cc + 1) * lc])
            glasts.append(GT[GDN_HEADS + h:GDN_HEADS + h + 1, (cc + 1) * lc - 1:(cc + 1) * lc])
    q3, k3, v3 = jnp.stack(qs), jnp.stack(ks), jnp.stack(vs)
    beta3, gcol3 = jnp.stack(betas), jnp.stack(gcols)
    grow3, glast3 = jnp.stack(grows), jnp.stack(glasts)
    shape3 = (len(pairs), lc, lc)
    ri = lax.broadcasted_iota(jnp.int32, shape3, 1)
    ci = lax.broadcasted_iota(jnp.int32, shape3, 2)
    incl = ri >= ci
    dec3 = jnp.where(incl, jnp.exp(jnp.where(incl, gcol3 - grow3, 0.0)), 0.0)
    eg3 = jnp.exp(gcol3)
    kb3 = k3 * beta3
    kbf3 = k3.astype(BF16)
    nmat3 = jnp.where(ri > ci, _bdot_nt(kb3.astype(BF16), kbf3) * dec3, 0.0)
    inv3 = _unit_lower_inverse(nmat3)
    sol3 = _bdot_bf16(inv3, jnp.concatenate([v3 * beta3, kb3 * eg3], axis=-1))
    w3 = sol3[:, :, GDN_DV:].astype(BF16)
    qd3 = (q3 * eg3).astype(BF16)
    kd3 = (k3 * jnp.exp(glast3 - gcol3)).astype(BF16)
    attn3 = (_bdot_nt(q3.astype(BF16), kbf3) * dec3).astype(BF16)
    for i, (h, cc) in enumerate(pairs):
        rows = slice(cc * lc, (cc + 1) * lc)
        cols = slice(h * GDN_DV, (h + 1) * GDN_DV)
        u0_ref[0, rows, cols] = sol3[i, :, :GDN_DV]
        w_ref[0, rows, cols] = w3[i]
        qd_ref[0, rows, cols] = qd3[i]
        kd_ref[0, rows, cols] = kd3[i]
        attn_ref[0, rows, h * lc:(h + 1) * lc] = attn3[i]


def _gdn_seq_kernel(u0_ref, w_ref, qd_ref, kd_ref, attn_ref, g_ref, z_ref, s0_ref, nw_ref,
                    y_ref, sout_ref, S, *, lc, nbb):
    c = pl.program_id(1)

    @pl.when(c == 0)
    def _():
        S[...] = s0_ref[...]

    nw = nw_ref[...]
    pairs = [(bb, h) for bb in range(nbb) for h in range(GDN_HEADS)]
    hcols = lambda h: slice(h * GDN_DV, (h + 1) * GDN_DV)
    stack = lambda f: jnp.stack([f(bb, h) for bb, h in pairs])
    dlast = jnp.exp(g_ref[:, lc - 1:lc, :])
    S3 = S[...].reshape(len(pairs), GDN_DK, GDN_DV)
    wq3 = stack(lambda bb, h: jnp.concatenate([w_ref[bb, :, hcols(h)], qd_ref[bb, :, hcols(h)]], axis=0))
    r3 = _bdot(wq3, S3.astype(BF16))
    ub3 = (stack(lambda bb, h: u0_ref[bb, :, hcols(h)]) - r3[:, :lc]).astype(BF16)
    o3 = r3[:, lc:] + _bdot(stack(lambda bb, h: attn_ref[bb, :, h * lc:(h + 1) * lc]), ub3)
    d3 = stack(lambda bb, h: dlast[bb, :, GDN_HEADS + h:GDN_HEADS + h + 1])
    kd3 = stack(lambda bb, h: kd_ref[bb, :, hcols(h)])
    kdu3 = lax.dot_general(kd3, ub3, (((1,), (1,)), ((0,), (0,))), preferred_element_type=F32)
    S[...] = (d3 * S3 + kdu3).reshape(S.shape)
    o3 = o3 * lax.rsqrt(jnp.mean(o3 * o3, -1, keepdims=True) + NORM_EPS) * nw
    for i, (bb, h) in enumerate(pairs):
        zh = z_ref[bb, :, hcols(h)]
        y_ref[bb, :, hcols(h)] = (o3[i] * (zh * _sigmoid(zh))).astype(y_ref.dtype)

    @pl.when(c == pl.num_programs(1) - 1)
    def _():
        sout_ref[...] = S[...]


def _gdn(qkv3, z3, ba3, ctx8, s0, cw, p1, p2, nw, lc, cp, nbb):
    nb, L, _ = qkv3.shape
    rb = lc * cp
    hd = GDN_HEADS * GDN_DV
    ltri = jnp.asarray(np.kron(np.eye(cp, dtype=np.float32), np.tril(np.ones((lc, lc), np.float32)))).astype(BF16)
    blk = lambda n: pl.BlockSpec((1, rb, n), lambda b, c: (b, c, 0))
    per_b = lambda shape: pl.BlockSpec((1,) + shape, lambda b, c: (b,) + (0,) * len(shape))
    cst = lambda shape: pl.BlockSpec(shape, lambda b, c: (0,) * len(shape))
    sds = lambda n, dt: jax.ShapeDtypeStruct((nb, L, n), dt)
    u0, w, qd, kd, attn, G, cout = pl.pallas_call(
        functools.partial(_gdn_local_kernel, lc=lc, cp=cp),
        grid=(nb, L // rb),
        in_specs=[blk(GDN_QKV), blk(LANES), per_b((SUBLANES, GDN_QKV)), cst(cw.shape), cst(p1.shape),
                  cst(p2.shape), cst(ltri.shape)],
        out_specs=[blk(hd), blk(hd), blk(hd), blk(hd), blk(GDN_HEADS * lc), blk(LANES),
                   per_b((SUBLANES, GDN_QKV))],
        out_shape=[sds(hd, F32), sds(hd, BF16), sds(hd, BF16), sds(hd, BF16), sds(GDN_HEADS * lc, BF16),
                   sds(LANES, F32), jax.ShapeDtypeStruct((nb, SUBLANES, GDN_QKV), F32)],
        scratch_shapes=[pltpu.VMEM((rb + SUBLANES, GDN_QKV), F32)],
        compiler_params=_params(("parallel", "arbitrary")),
        name="gdn_local",
    )(qkv3, ba3, ctx8, cw, p1, p2, ltri)
    sblk = lambda n: pl.BlockSpec((nbb, lc, n), lambda b, c: (b, c, 0))
    state = pl.BlockSpec((nbb, GDN_HEADS, GDN_DK, GDN_DV), lambda b, c: (b, 0, 0, 0))
    y, s_new = pl.pallas_call(
        functools.partial(_gdn_seq_kernel, lc=lc, nbb=nbb),
        grid=(nb // nbb, L // lc),
        in_specs=[sblk(hd), sblk(hd), sblk(hd), sblk(hd), sblk(GDN_HEADS * lc), sblk(LANES), sblk(hd),
                  state, cst(nw.shape)],
        out_specs=[sblk(hd), state],
        out_shape=[sds(hd, BF16), jax.ShapeDtypeStruct(s0.shape, F32)],
        scratch_shapes=[pltpu.VMEM((nbb, GDN_HEADS, GDN_DK, GDN_DV), F32)],
        compiler_params=_params(("parallel", "arbitrary")),
        name="gdn_seq",
    )(u0, w, qd, kd, attn, G, z3, s0, nw)
    return y, s_new, cout


def _ret_kernel(q_ref, k_ref, v_ref, g_ref, r0_ref, dec_ref, qs_ref, ks_ref, cd_ref, o_ref, rout_ref, R,
                *, nbb):
    c = pl.program_id(1)

    @pl.when(c == 0)
    def _():
        R[...] = r0_ref[...]

    pairs = [(bb, h) for bb in range(nbb) for h in range(RET_HEADS)]
    stack = lambda f: jnp.stack([f(bb, h) for bb, h in pairs])
    kcols = lambda h: slice(h * RET_DK, (h + 1) * RET_DK)
    vcols = lambda h: slice(h * RET_DV, (h + 1) * RET_DV)
    q3 = stack(lambda bb, h: q_ref[bb, :, kcols(h)])
    k3 = stack(lambda bb, h: k_ref[bb, :, kcols(h)])
    v3 = stack(lambda bb, h: v_ref[bb, :, vcols(h)])
    dec3 = stack(lambda bb, h: dec_ref[h])
    qs3 = stack(lambda bb, h: qs_ref[h])
    ks3 = stack(lambda bb, h: ks_ref[h])
    cd3 = stack(lambda bb, h: cd_ref[h])
    R3 = R[...].reshape(len(pairs), RET_DK, RET_DV)
    s3 = _bdot_nt(q3, k3) * dec3
    o3 = _bdot(s3.astype(BF16), v3) + _bdot(q3, R3.astype(BF16)) * qs3
    kv3 = lax.dot_general((k3.astype(F32) * ks3).astype(BF16), v3, (((1,), (1,)), ((0,), (0,))),
                          preferred_element_type=F32)
    R[...] = (cd3 * R3 + kv3).reshape(R.shape)
    mu = jnp.mean(o3, -1, keepdims=True)
    d3 = o3 - mu
    var = jnp.mean(d3 * d3, -1, keepdims=True)
    on3 = d3 * lax.rsqrt(var + LN_EPS)
    for i, (bb, h) in enumerate(pairs):
        gt = g_ref[bb, :, vcols(h)].astype(F32)
        o_ref[bb, :, vcols(h)] = (gt * _sigmoid(gt) * on3[i]).astype(o_ref.dtype)

    @pl.when(c == pl.num_programs(1) - 1)
    def _():
        rout_ref[...] = R[...]


def _retention(proj3, r0, chunk, nbb, nc):
    nb, L, _ = proj3.shape
    lc = chunk * nc
    log_g = np.log(1.0 - 2.0 ** (-5.0 - np.arange(RET_HEADS, dtype=np.float64)))
    idx = np.arange(lc, dtype=np.float64)
    dist = idx[:, None] - idx[None, :]
    which = (idx // chunk)[:, None] - (idx // chunk)[None, :]
    dec = np.where(which >= 0, np.exp(log_g[:, None, None] * np.abs(dist)), 0.0).astype(np.float32)
    qs = np.exp(log_g[:, None] * (idx + 1.0)).astype(np.float32)[..., None]
    ks = np.exp(log_g[:, None] * (lc - 1.0 - idx)).astype(np.float32)[..., None]
    cdec = np.exp(log_g * lc).astype(np.float32)[:, None, None]
    nqk = RET_HEADS * RET_DK
    nv = RET_HEADS * RET_DV
    cst = lambda shape: pl.BlockSpec(shape, lambda b, c: (0,) * len(shape))
    state = pl.BlockSpec((nbb, RET_HEADS, RET_DK, RET_DV), lambda b, c: (b, 0, 0, 0))
    return pl.pallas_call(
        functools.partial(_ret_kernel, nbb=nbb),
        grid=(nb // nbb, L // lc),
        in_specs=[pl.BlockSpec((nbb, lc, nqk), lambda b, c: (b, c, 0)),
                  pl.BlockSpec((nbb, lc, nqk), lambda b, c: (b, c, 1)),
                  pl.BlockSpec((nbb, lc, nv), lambda b, c: (b, c, 1)),
                  pl.BlockSpec((nbb, lc, nv), lambda b, c: (b, c, 2)),
                  state, cst(dec.shape), cst(qs.shape), cst(ks.shape), cst(cdec.shape)],
        out_specs=[pl.BlockSpec((nbb, lc, nv), lambda b, c: (b, c, 0)), state],
        out_shape=[jax.ShapeDtypeStruct((nb, L, nv), BF16), jax.ShapeDtypeStruct(r0.shape, F32)],
        scratch_shapes=[pltpu.VMEM((nbb, RET_HEADS, RET_DK, RET_DV), F32)],
        compiler_params=_params(("parallel", "arbitrary")),
        name="retention",
    )(proj3, proj3, proj3, proj3, r0, jnp.asarray(dec), jnp.asarray(qs), jnp.asarray(ks), jnp.asarray(cdec))


RUN_ALIGN = SUBLANES
RUN_PIECE = 64
MOE_VMEM_LIMIT = 58 * 1024 * 1024


def _moe_params(sem):
    return pltpu.CompilerParams(dimension_semantics=sem, vmem_limit_bytes=MOE_VMEM_LIMIT)


def _tile_cap(tm):
    rows = TOP_K * tm + N_EXPERTS * (RUN_ALIGN - 1)
    return -(-rows // LANES) * LANES


def _route_kernel(x_ref, rw_ref, rb_ref, lst_ref, ust_ref, pos_ref, gt_ref, cnt_ref, xs_ref):
    tm = x_ref.shape[0]
    cap = xs_ref.shape[1]
    x = x_ref[...]
    lane = lax.broadcasted_iota(jnp.int32, (tm, LANES), 1)
    lane_f = lane.astype(F32)
    xh, xl = _split_bf16(x)
    wh, wl = _split_bf16(rw_ref[...])
    logits = _dot(xh, wh) + _dot(xh, wl) + _dot(xl, wh) + rb_ref[...]
    logits = jnp.where(lane < N_EXPERTS, logits, -jnp.inf)
    vals, hots = [], []
    for _ in range(TOP_K):
        m = jnp.max(logits, -1, keepdims=True)
        first = jnp.min(jnp.where(logits == m, lane_f, float(LANES)), -1, keepdims=True)
        hot = lane_f == first
        vals.append(m)
        hots.append(hot)
        logits = jnp.where(hot, -jnp.inf, logits)
    es = [jnp.exp(v - vals[0]) for v in vals]
    den = es[0] + es[1] + es[2] + es[3]
    multi = jnp.zeros((tm, LANES), F32)
    for hot in hots:
        multi = multi + hot.astype(F32)
    counts = jnp.sum(multi, 0, keepdims=True)
    units = jnp.floor((counts + (RUN_ALIGN - 1)) * (1.0 / RUN_ALIGN))
    offs = _dot(jnp.broadcast_to(units, (SUBLANES, LANES)).astype(BF16), ust_ref[...])[0:1] * float(RUN_ALIGN)
    before = _dot(lst_ref[...], multi.astype(BF16))
    slot = offs + before
    pos = jnp.zeros((tm, LANES), F32)
    gt = jnp.zeros((tm, LANES), F32)
    for kk in range(TOP_K):
        pos = jnp.where(lane == kk, jnp.sum(jnp.where(hots[kk], slot, 0.0), -1, keepdims=True), pos)
        gt = jnp.where(lane == kk, es[kk] / den, gt)
    pos = pos.astype(jnp.int32)
    pos_ref[...] = pos
    gt_ref[...] = gt
    cnt_ref[0] = counts
    pos_t = pos.T[0:2 * SUBLANES].astype(jnp.int16)
    row = lax.broadcasted_iota(jnp.int16, (cap, tm), 0)
    sel = jnp.zeros((cap, tm), BF16)
    for kk in range(TOP_K):
        sel = sel + jnp.where(row == pos_t[kk:kk + 1, :], jnp.ones((), BF16), jnp.zeros((), BF16))
    xs_ref[0] = _dot(sel, xh)


def _route(x, rw, rb, tm):
    t = x.shape[0]
    nt = t // tm
    cap = _tile_cap(tm)
    lst = jnp.asarray(np.tril(np.ones((tm, tm), np.float32), -1)).astype(BF16)
    ust = jnp.asarray(np.triu(np.ones((LANES, LANES), np.float32), 1)).astype(BF16)
    row = lambda n: pl.BlockSpec((tm, n), lambda i: (i, 0))
    return pl.pallas_call(
        _route_kernel,
        grid=(nt,),
        in_specs=[row(D_MODEL), _full(rw.shape), _full(rb.shape), _full(lst.shape), _full(ust.shape)],
        out_specs=[row(LANES), row(LANES), pl.BlockSpec((1, 1, LANES), lambda i: (i, 0, 0)),
                   pl.BlockSpec((1, cap, D_MODEL), lambda i: (i, 0, 0))],
        out_shape=[jax.ShapeDtypeStruct((t, LANES), jnp.int32), jax.ShapeDtypeStruct((t, LANES), F32),
                   jax.ShapeDtypeStruct((nt, 1, LANES), F32), jax.ShapeDtypeStruct((nt, cap, D_MODEL), F32)],
        compiler_params=_moe_params(("parallel",)),
        name="moe_route",
    )(x, rw, rb, lst, ust)


def _expert_kernel(be_ref, nu_ref, ilo_ref, rows_ref, gs_ref, n8_ref, lo_ref, *refs, blk, tiles):
    ng = len(tiles)
    xs_hbms = refs[:ng]
    w1_ref, b1_ref, w2_ref, b2_ref = refs[ng:ng + 4]
    ys_hbms = refs[ng + 4:2 * ng + 4]
    xbuf, ybuf, w1b, w2b, in_sem, out_sem = refs[2 * ng + 4:]
    nt = sum(tiles)
    firsts = [sum(tiles[:g]) for g in range(ng)]
    j = pl.program_id(0)
    nu = nu_ref[0]

    def for_each_run(jb, fn):
        e = be_ref[jb]
        base = jb * blk
        for g in range(ng):
            end = firsts[g] + tiles[g]

            def cond(i, end=end):
                return (i < end) & (gs_ref[e * nt + jnp.minimum(i, nt - 1)] < base + blk)

            def body(i, g=g):
                g0 = gs_ref[e * nt + i]
                first = jnp.maximum(g0, base)
                last = jnp.minimum(g0 + n8_ref[e * nt + i], base + blk)
                fn(g, i - firsts[g], lo_ref[e * nt + i] + (first - g0), first - base, last - first)
                return i + 1

            lax.while_loop(cond, body, jnp.clip(ilo_ref[jb], firsts[g], end))

    def pieces(length, fn):
        def digits(sizes):
            for size in sizes:
                @pl.when((length & size) != 0)
                def _(size=size):
                    fn(length & ~(2 * size - 1), size)

        sizes = [blk >> k for k in range(blk.bit_length()) if blk >> k >= RUN_ALIGN]

        @pl.when(length > 2 * RUN_PIECE - 1)
        def _():
            digits([z for z in sizes if z > RUN_PIECE])

        digits([z for z in sizes if z <= RUN_PIECE])

    def aligned(v, size):
        return pl.ds(pl.multiple_of(v, RUN_ALIGN), size)

    def copy_in(jb, slot):
        def run(g, ig, src, dst, length):
            def piece(off, size):
                pltpu.make_async_copy(xs_hbms[g].at[ig, aligned(src + off, size)],
                                      xbuf.at[slot, aligned(dst + off, size)], in_sem.at[slot]).start()
            pieces(length, piece)
        for_each_run(jb, run)

    def copy_out(jb, slot):
        def run(g, ig, src, dst, length):
            def piece(off, size):
                pltpu.make_async_copy(ybuf.at[slot, aligned(dst + off, size)],
                                      ys_hbms[g].at[ig, aligned(src + off, size)], out_sem.at[slot]).start()
            pieces(length, piece)
        for_each_run(jb, run)

    def wait_rows(sem, nrows):
        size = blk
        while size >= RUN_ALIGN:
            @pl.when((nrows & size) != 0)
            def _(size=size):
                pltpu.make_async_copy(xbuf.at[1, pl.ds(0, size)], xbuf.at[0, pl.ds(0, size)], sem).wait()
            size //= 2

    @pl.when(j < nu)
    def _():
        slot = j % 2

        @pl.when(j == 0)
        def _():
            xbuf[...] = jnp.zeros_like(xbuf)
            copy_in(0, 0)

        wait_rows(in_sem.at[slot], rows_ref[j])

        @pl.when(j + 1 < nu)
        def _():
            copy_in(j + 1, 1 - slot)

        @pl.when(j >= 2)
        def _():
            wait_rows(out_sem.at[slot], rows_ref[jnp.maximum(j - 2, 0)])

        @pl.when((j == 0) | (be_ref[j] != be_ref[jnp.maximum(j - 1, 0)]))
        def _():
            w1b[...] = w1_ref[0, 0].astype(BF16)
            w2b[...] = w2_ref[0, 0].astype(BF16)

        h = _dot(xbuf[slot].astype(BF16), w1b[...]) + b1_ref[0]
        glu = jnp.minimum(h[:, :D_FF], SWIGLU_LIMIT)
        lin = jnp.clip(h[:, D_FF:], -SWIGLU_LIMIT, SWIGLU_LIMIT)
        act = glu * _sigmoid(SWIGLU_ALPHA * glu) * (lin + 1.0)
        ybuf[slot] = _dot(act.astype(BF16), w2b[...]) + b2_ref[0]
        copy_out(j, slot)

        @pl.when(j == nu - 1)
        def _():
            wait_rows(out_sem.at[slot], rows_ref[j])

            @pl.when(j >= 1)
            def _():
                wait_rows(out_sem.at[1 - slot], rows_ref[jnp.maximum(j - 1, 0)])


def _experts(tables, xs_list, w1, b1, w2, b2, layer, blk):
    ng = len(xs_list)
    n_blk = tables[0].shape[0]
    wspec = lambda shape: pl.BlockSpec((1, 1) + shape, lambda j, be, *_: (layer, be[j], 0, 0))
    bspec = lambda n: pl.BlockSpec((1, 1, n), lambda j, be, *_: (be[j], 0, 0))
    hbm = pl.BlockSpec(memory_space=pl.ANY)
    grid_spec = pltpu.PrefetchScalarGridSpec(
        num_scalar_prefetch=len(tables),
        grid=(n_blk,),
        in_specs=[hbm] * ng + [wspec((D_MODEL, 2 * D_FF)), bspec(2 * D_FF), wspec((D_FF, D_MODEL)), bspec(D_MODEL)],
        out_specs=[hbm] * ng,
        scratch_shapes=[pltpu.VMEM((2, blk, D_MODEL), F32), pltpu.VMEM((2, blk, D_MODEL), F32),
                        pltpu.VMEM((D_MODEL, 2 * D_FF), BF16), pltpu.VMEM((D_FF, D_MODEL), BF16),
                        pltpu.SemaphoreType.DMA((2,)), pltpu.SemaphoreType.DMA((2,))],
    )
    return pl.pallas_call(
        functools.partial(_expert_kernel, blk=blk, tiles=tuple(xs.shape[0] for xs in xs_list)),
        grid_spec=grid_spec,
        out_shape=[jax.ShapeDtypeStruct(xs.shape, F32) for xs in xs_list],
        input_output_aliases={len(tables) + g: g for g in range(ng)},
        compiler_params=_moe_params(("arbitrary",)),
        name="moe_experts",
    )(*tables, *xs_list, w1, b1, w2, b2)


def _combine_kernel(ys_ref, pos_ref, gt_ref, x_ref, p_ref, g_ref, b_ref, plew_ref, gatew_ref, o_ref):
    tm = x_ref.shape[0]
    cap = ys_ref.shape[1]
    pos = pos_ref[...].astype(jnp.int16)
    gt = gt_ref[...].astype(BF16)
    col = lax.broadcasted_iota(jnp.int16, (tm, cap), 1)
    sel = jnp.zeros((tm, cap), BF16)
    for kk in range(TOP_K):
        sel = sel + jnp.where(col == pos[:, kk:kk + 1], gt[:, kk:kk + 1], jnp.zeros((), BF16))
    y = _dot(sel, ys_ref[0].astype(BF16))
    x2 = _layer_norm(DEEPNORM_ALPHA * x_ref[...] + y, g_ref[...], b_ref[...])
    pp = _dot(p_ref[...].astype(BF16), plew_ref[...])
    gg = _sigmoid(_dot(x2.astype(BF16), gatew_ref[...]))
    o_ref[...] = x2 + pp * gg


def _combine(ys, pos, gt, x, p, g, b, plew, gatew, tm):
    t = x.shape[0]
    cap = ys.shape[1]
    row = lambda n: pl.BlockSpec((tm, n), lambda i: (i, 0))
    return pl.pallas_call(
        _combine_kernel,
        grid=(t // tm,),
        in_specs=[pl.BlockSpec((1, cap, D_MODEL), lambda i: (i, 0, 0)), row(LANES), row(LANES),
                  row(D_MODEL), row(PLE_DIM), _full((1, D_MODEL)), _full((1, D_MODEL)),
                  _full(plew.shape), _full(gatew.shape)],
        out_specs=row(D_MODEL),
        out_shape=jax.ShapeDtypeStruct((t, D_MODEL), F32),
        compiler_params=_moe_params(("parallel",)),
        name="moe_combine",
    )(ys, pos, gt, x, p, g, b, plew, gatew)


def _moe_tables(cnt, t, blk):
    nt = cnt.shape[0]
    n = cnt[:, 0, :N_EXPERTS].astype(jnp.int32)
    n8 = (n + RUN_ALIGN - 1) // RUN_ALIGN * RUN_ALIGN
    lo = jnp.cumsum(n8, axis=1) - n8
    rows_e = jnp.sum(n8, axis=0)
    padded = (rows_e + blk - 1) // blk * blk
    pend = jnp.cumsum(padded)
    pstart = pend - padded
    gstart = pstart[None, :] + jnp.cumsum(n8, axis=0) - n8
    n_blk = -(-(TOP_K * t + nt * N_EXPERTS * (RUN_ALIGN - 1) + N_EXPERTS * (blk - 1)) // blk)
    n_used = (pend[-1] // blk).astype(jnp.int32).reshape(1)
    blk_start = jnp.arange(n_blk, dtype=jnp.int32) * blk
    blk_e = jnp.minimum(jnp.sum((pend[None, :] <= blk_start[:, None]).astype(jnp.int32), axis=1), N_EXPERTS - 1)
    run_end = (gstart + n8)[:, blk_e]
    ilo = jnp.sum((run_end <= blk_start[None, :]).astype(jnp.int32), axis=0)
    rows_b = jnp.clip((pstart + rows_e)[blk_e] - blk_start, 0, blk)
    flat = lambda a: a.T.reshape(-1).astype(jnp.int32)
    i32 = lambda a: a.astype(jnp.int32)
    return (i32(blk_e), n_used, i32(ilo), i32(rows_b), flat(gstart), flat(n8), flat(lo))


def _moe_ple(xs_in, ps, rw, rb, w1, b1, w2, b2, layer, g, b, plew, gatew, tms, blk):
    routed = [_route(x, rw, rb, tm) for x, tm in zip(xs_in, tms)]
    cnt = jnp.concatenate([r[2] for r in routed], axis=0)
    tables = _moe_tables(cnt, sum(x.shape[0] for x in xs_in), blk)
    ys = _experts(tables, [r[3] for r in routed], w1, b1, w2, b2, layer, blk)
    return [_combine(y, r[0], r[1], x, p, g, b, plew, gatew, tm)
            for y, r, x, p, tm in zip(ys, routed, xs_in, ps, tms)]


_ROT_PERM = np.concatenate([np.arange(0, RET_DK, 2), np.arange(1, RET_DK, 2)])
_ROT_INV = np.argsort(_ROT_PERM)


def _lane_row(vals, offset):
    row = jnp.zeros((1, LANES), F32)
    return row.at[0, offset:offset + vals.shape[0]].set(vals)


def _even_mixer(x, nb, L, s5_re, s5_im, gdn_s, conv_s, W, tm):
    t = nb * L
    lc = L if L <= CHUNK else CHUNK
    u, qkv, z, ba = _proj_even(x, W['wu'], W['wqkv'], W['wz'], W['wba'], tm)
    yA3, h_new = _s5(u.reshape(nb, L, S5_WIDTH), _s5_state_in(s5_re[0].astype(F32), s5_im[0].astype(F32)),
                     W['bmat'], W['cmat'], W['acoef'], W['dskip'], W['wglu'], W['bglu'], lc)
    new_re, new_im = _s5_state_out(h_new)
    ctx8 = jnp.concatenate([jnp.zeros((nb, SUBLANES - (GDN_CONV - 1), GDN_QKV), F32), conv_s[0].astype(F32)], axis=1)
    yB3, new_gdn, cout = _gdn(qkv.reshape(nb, L, GDN_QKV), z.reshape(nb, L, -1), ba.reshape(nb, L, LANES),
                              ctx8, gdn_s[0].astype(F32), W['convw'], W['p1'], W['p2'], W['normw'], lc,
                              cp=min(4, L // lc), nbb=4)
    new_conv = cout[:, SUBLANES - (GDN_CONV - 1):, :]
    x = _outproj_ln([yA3.reshape(t, -1), yB3.reshape(t, -1)], [W['wout_a'], W['wout_b']], x,
                    W['ln1_g'][0], W['ln1_b'][0], tm)
    return x, new_re[None], new_im[None], new_gdn[None], new_conv[None]


def _odd_mixer(x, nb, L, ret_s, pos0, W, tm):
    t = nb * L
    lc = L if L <= CHUNK else CHUNK
    pos = pos0 + jnp.arange(L, dtype=F32)
    freq = 1.0 / (ROPE_BASE ** jnp.linspace(0.0, 1.0, RET_DK // 2, dtype=F32))
    ang = pos[:, None] * freq[None]
    reps = max(1, tm // L)
    cos = jnp.tile(jnp.cos(ang), (reps, 1))
    sin = jnp.tile(jnp.sin(ang), (reps, 1))
    proj = _proj_odd(x, W['win_odd'], cos, sin, tm)
    r0 = ret_s[0].astype(F32)[:, :, _ROT_PERM, :]
    o3, r_new = _retention(proj.reshape(nb, L, -1), r0, lc, nbb=2, nc=min(4, L // lc))
    new_ret = r_new[:, :, _ROT_INV, :]
    x = _outproj_ln([o3.reshape(t, -1)], [W['wout_odd']], x, W['ln1_g'][1], W['ln1_b'][1], tm)
    return x, new_ret[None]


def kernel(x_prompt, x_sample, state_s5_re, state_s5_im, state_gdn, state_gdn_conv, state_ret, p_prompt, p_sample, w_in_even, s5_a_re, s5_a_im, s5_log_dt, s5_b_re, s5_b_im, s5_c_re, s5_c_im, s5_d, s5_w_glu, s5_b_glu, gdn_conv_w, gdn_a_log, gdn_dt_bias, gdn_norm_w, w_out_even, w_in_odd, w_out_odd, ln1_g, ln1_b, ln2_g, ln2_b, router_w, router_b, moe_w1, moe_b1, moe_w2, moe_b2, ple_w, ple_gate_w):
    o1 = S5_WIDTH
    o2 = o1 + GDN_QKV
    o3 = o2 + GDN_HEADS * GDN_DV
    win = w_in_even[0]
    bmat, cmat, acoef = _s5_weights(s5_a_re[0], s5_a_im[0], s5_log_dt[0], s5_b_re[0], s5_b_im[0],
                                    s5_c_re[0], s5_c_im[0])
    wodd = w_in_odd[0]
    nk = RET_HEADS * RET_DK
    perm_cols = lambda w: w.reshape(D_MODEL, RET_HEADS, RET_DK)[:, :, _ROT_PERM].reshape(D_MODEL, nk)
    W = dict(
        wu=win[:, :o1].astype(BF16), wqkv=win[:, o1:o2].astype(BF16), wz=win[:, o2:o3].astype(BF16),
        wba=jnp.pad(win[:, o3:], ((0, 0), (0, LANES - 2 * GDN_HEADS))).astype(BF16),
        bmat=bmat, cmat=cmat, acoef=acoef, dskip=s5_d[0][None], wglu=s5_w_glu[0].astype(BF16),
        bglu=s5_b_glu[0][None], convw=gdn_conv_w[0],
        p1=_lane_row(-jnp.exp(gdn_a_log[0]), GDN_HEADS), p2=_lane_row(gdn_dt_bias[0], GDN_HEADS),
        normw=gdn_norm_w[0][None],
        wout_a=w_out_even[0][:S5_WIDTH].astype(BF16), wout_b=w_out_even[0][S5_WIDTH:].astype(BF16),
        win_odd=jnp.concatenate([perm_cols(wodd[:, :nk]), perm_cols(wodd[:, nk:2 * nk]), wodd[:, 2 * nk:]],
                                axis=1).astype(BF16),
        wout_odd=w_out_odd[0].astype(BF16),
        ln1_g=ln1_g[:, None], ln1_b=ln1_b[:, None], ln2_g=ln2_g[:, None], ln2_b=ln2_b[:, None],
        rw=jnp.pad(router_w, ((0, 0), (0, 0), (0, LANES - N_EXPERTS))),
        rb=jnp.pad(router_b, ((0, 0), (0, LANES - N_EXPERTS)))[:, None],
        w1=moe_w1, b1=moe_b1[:, :, None], w2=moe_w2, b2=moe_b2[:, :, None],
        plew=ple_w.astype(BF16), gatew=ple_gate_w.astype(BF16),
    )
    bp, lp, _ = x_prompt.shape
    bs, ls, _ = x_sample.shape
    zeros = lambda *s: jnp.zeros(s, F32)
    shapes = [(bp, lp), (bs, ls)]
    tms = [512, 128]
    ps = [p_prompt, p_sample]
    xs = [x_prompt.reshape(bp * lp, D_MODEL).astype(F32), x_sample.reshape(bs * ls, D_MODEL).astype(F32)]
    even_states = [(zeros(1, bp, S5_GROUPS, S5_STATE), zeros(1, bp, S5_GROUPS, S5_STATE),
                    zeros(1, bp, GDN_HEADS, GDN_DK, GDN_DV), zeros(1, bp, GDN_CONV - 1, GDN_QKV)),
                   (state_s5_re, state_s5_im, state_gdn, state_gdn_conv)]
    ret_states = [zeros(1, bp, RET_HEADS, RET_DK, RET_DV), state_ret]
    pos0s = [0.0, float(PAST_LEN)]

    def moe(xs, layer):
        return _moe_ple(xs, [p[layer].reshape(-1, PLE_DIM) for p in ps], W['rw'][layer], W['rb'][layer],
                        W['w1'], W['b1'][layer], W['w2'], W['b2'][layer], layer, W['ln2_g'][layer],
                        W['ln2_b'][layer], W['plew'][layer], W['gatew'][layer], tms, blk=512)

    even = [_even_mixer(x, nb, L, *st, W, tm) for x, (nb, L), st, tm in zip(xs, shapes, even_states, tms)]
    xs = moe([e[0] for e in even], 0)
    odd = [_odd_mixer(x, nb, L, st, pos0, W, tm)
           for x, (nb, L), st, pos0, tm in zip(xs, shapes, ret_states, pos0s, tms)]
    xs = moe([o[0] for o in odd], 1)
    dp = x_prompt.dtype
    y_p, y_s = xs[0].reshape(bp, lp, D_MODEL), xs[1].reshape(bs, ls, D_MODEL)
    (_, p_re, p_im, p_gdn, p_conv), (_, s_re, s_im, s_gdn, s_conv) = even
    p_ret, s_ret = odd[0][1], odd[1][1]
    return (y_p.astype(dp), y_s.astype(x_sample.dtype),
            p_re.astype(dp), p_im.astype(dp), p_gdn.astype(dp), p_conv.astype(dp), p_ret.astype(dp),
            s_re.astype(state_s5_re.dtype), s_im.astype(state_s5_im.dtype), s_gdn.astype(state_gdn.dtype),
            s_conv.astype(state_gdn_conv.dtype), s_ret.astype(state_ret.dtype))
```

```python
import functools
import math

import jax
import jax.numpy as jnp
import numpy as np
from jax import lax
from jax.experimental import pallas as pl
from jax.experimental.pallas import tpu as pltpu

F32 = jnp.float32
BF16 = jnp.bfloat16
HIGHEST = lax.Precision.HIGHEST

D_MODEL = 1024
CHUNK = 64
S5_WIDTH = 512
S5_GROUP = 16
S5_GROUPS = 32
S5_STATE = 64
S5_TILES = 16
GDN_HEADS = 4
GDN_DK = 128
GDN_DV = 128
GDN_CONV = 4
GDN_QKV = 1536
RET_HEADS = 4
RET_DK = 256
RET_DV = 512
ROPE_BASE = 10000.0
N_EXPERTS = 32
TOP_K = 4
D_FF = 1024
SWIGLU_LIMIT = 7.0
SWIGLU_ALPHA = 1.702
PLE_DIM = 256
DEPTH = 2
PAST_LEN = 1024
DEEPNORM_ALPHA = (2 * DEPTH) ** 0.25
LN_EPS = 1e-5
NORM_EPS = 1e-6

LANES = 128
SUBLANES = 8
VMEM_LIMIT = 48 * 1024 * 1024

def _params(sem):
    return pltpu.CompilerParams(dimension_semantics=sem, vmem_limit_bytes=VMEM_LIMIT)


def _dot(a, b):
    return jnp.dot(a, b, preferred_element_type=F32)


def _dot_hi(a, b):
    return jnp.dot(a, b, preferred_element_type=F32, precision=HIGHEST)


def _dot_nt(a, b, precision=None):
    return lax.dot_general(a, b, (((1,), (1,)), ((), ())), preferred_element_type=F32,
                           precision=precision)


def _dot_tn(a, b):
    return lax.dot_general(a, b, (((0,), (0,)), ((), ())), preferred_element_type=F32)


def _sigmoid(x):
    return 1.0 / (1.0 + jnp.exp(-x))


def _full(shape):
    nd = len(shape)
    return pl.BlockSpec(shape, lambda *_: (0,) * nd)


def _proj_even_kernel(x_ref, wu_ref, wqkv_ref, wz_ref, wba_ref, u_ref, qkv_ref, z_ref, ba_ref):
    xb = x_ref[...].astype(BF16)
    u_ref[...] = _dot(xb, wu_ref[...])
    qkv_ref[...] = _dot(xb, wqkv_ref[...])
    z_ref[...] = _dot(xb, wz_ref[...])
    ba_ref[...] = _dot(xb, wba_ref[...])


def _proj_even(x, wu, wqkv, wz, wba, tm):
    t = x.shape[0]
    row = lambda n: pl.BlockSpec((tm, n), lambda i: (i, 0))
    return pl.pallas_call(
        _proj_even_kernel,
        grid=(t // tm,),
        in_specs=[row(D_MODEL), _full(wu.shape), _full(wqkv.shape), _full(wz.shape), _full(wba.shape)],
        out_specs=[row(S5_WIDTH), row(GDN_QKV), row(GDN_HEADS * GDN_DV), row(LANES)],
        out_shape=[jax.ShapeDtypeStruct((t, S5_WIDTH), F32), jax.ShapeDtypeStruct((t, GDN_QKV), F32),
                   jax.ShapeDtypeStruct((t, GDN_HEADS * GDN_DV), F32), jax.ShapeDtypeStruct((t, LANES), F32)],
        compiler_params=_params(("parallel",)),
        name="proj_even",
    )(x, wu, wqkv, wz, wba)


def _proj_odd_kernel(x_ref, w_ref, cos_ref, sin_ref, o_ref):
    j = pl.program_id(0)
    acc = _dot(x_ref[...].astype(BF16), w_ref[...])

    @pl.when(j == 0)
    def _():
        cos, sin = cos_ref[...], sin_ref[...]
        half = RET_DK // 2
        for h in range(2 * RET_HEADS):
            x0 = acc[:, h * RET_DK:h * RET_DK + half]
            x1 = acc[:, h * RET_DK + half:(h + 1) * RET_DK]
            scale = 1.0 if h < RET_HEADS else RET_DK ** -0.5
            o_ref[:, h * RET_DK:h * RET_DK + half] = ((x0 * cos - x1 * sin) * scale).astype(o_ref.dtype)
            o_ref[:, h * RET_DK + half:(h + 1) * RET_DK] = ((x0 * sin + x1 * cos) * scale).astype(o_ref.dtype)

    @pl.when(j != 0)
    def _():
        o_ref[...] = acc.astype(o_ref.dtype)


def _proj_odd(x, w, cos, sin, tm):
    t = x.shape[0]
    nblk = w.shape[1] // 2048
    period = cos.shape[0] // tm
    return pl.pallas_call(
        _proj_odd_kernel,
        grid=(nblk, t // tm),
        in_specs=[pl.BlockSpec((tm, D_MODEL), lambda j, i: (i, 0)),
                  pl.BlockSpec((D_MODEL, 2048), lambda j, i: (0, j)),
                  pl.BlockSpec((tm, LANES), lambda j, i: (i % period, 0)),
                  pl.BlockSpec((tm, LANES), lambda j, i: (i % period, 0))],
        out_specs=pl.BlockSpec((tm, 2048), lambda j, i: (i, j)),
        out_shape=jax.ShapeDtypeStruct((t, w.shape[1]), BF16),
        compiler_params=_params(("parallel", "parallel")),
        name="proj_odd",
    )(x, w, cos, sin)


def _layer_norm(r, g, b):
    mu = jnp.mean(r, -1, keepdims=True)
    d = r - mu
    var = jnp.mean(d * d, -1, keepdims=True)
    return d * lax.rsqrt(var + LN_EPS) * g + b


def _outproj_ln_kernel(*refs, n_in):
    a_refs = refs[:n_in]
    w_refs = refs[n_in:2 * n_in]
    x_ref, g_ref, b_ref, o_ref = refs[2 * n_in:]
    acc = _dot(a_refs[0][...], w_refs[0][...])
    for a_ref, w_ref in zip(a_refs[1:], w_refs[1:]):
        acc = acc + _dot(a_ref[...], w_ref[...])
    o_ref[...] = _layer_norm(DEEPNORM_ALPHA * x_ref[...] + acc, g_ref[...], b_ref[...])


def _outproj_ln(acts, ws, x, g, b, tm):
    t = x.shape[0]
    row = lambda n: pl.BlockSpec((tm, n), lambda i: (i, 0))
    return pl.pallas_call(
        functools.partial(_outproj_ln_kernel, n_in=len(acts)),
        grid=(t // tm,),
        in_specs=[row(a.shape[1]) for a in acts] + [_full(w.shape) for w in ws]
                 + [row(D_MODEL), _full((1, D_MODEL)), _full((1, D_MODEL))],
        out_specs=row(D_MODEL),
        out_shape=jax.ShapeDtypeStruct((t, D_MODEL), F32),
        compiler_params=_params(("parallel",)),
        name="outproj_ln",
    )(*acts, *ws, x, g, b)


S5_LANE_BLOCKS = S5_WIDTH // LANES
S5_TILES_PER_BLOCK = S5_TILES // S5_LANE_BLOCKS


def _s5_kernel(u_ref, h0_ref, bmat_ref, cmat_ref, acoef_ref, dskip_ref, wglu_ref, bglu_ref,
               y_ref, hout_ref, utm, sre, sim, ytm, hst, *, nb, lt):
    tb = pl.program_id(0)

    @pl.when(tb == 0)
    def _():
        hst[...] = h0_ref[...]

    for b in range(nb):
        for q in range(S5_LANE_BLOCKS):
            utm[q, pl.ds(b, lt, stride=nb), :] = u_ref[b, :, q * LANES:(q + 1) * LANES]

    def block_body(q, carry):
        bu = _dot(utm[q].astype(BF16), bmat_ref[q])
        for g in range(S5_TILES_PER_BLOCK):
            sre[g] = bu[:, (2 * g) * LANES:(2 * g + 1) * LANES]
            sim[g] = bu[:, (2 * g + 1) * LANES:(2 * g + 2) * LANES]
        js = [q * S5_TILES_PER_BLOCK + g for g in range(S5_TILES_PER_BLOCK)]
        acs = [acoef_ref[j] for j in js]
        hs = [(hst[j, :, 0:LANES], hst[j, :, LANES:2 * LANES]) for j in js]
        for t in range(lt):
            rows = slice(t * nb, (t + 1) * nb)
            for g in range(S5_TILES_PER_BLOCK):
                ar, ai = acs[g][:, :LANES], acs[g][:, LANES:]
                hr, hi = hs[g]
                nhr = ar * hr - ai * hi + sre[g, rows, :]
                nhi = ar * hi + ai * hr + sim[g, rows, :]
                sre[g, rows, :] = nhr
                sim[g, rows, :] = nhi
                hs[g] = (nhr, nhi)
        for g, j in enumerate(js):
            hst[j, :, 0:LANES] = hs[g][0]
            hst[j, :, LANES:2 * LANES] = hs[g][1]
        st = jnp.concatenate([part for g in range(S5_TILES_PER_BLOCK) for part in (sre[g], sim[g])], axis=-1)
        ytm[q] = _dot(st.astype(BF16), cmat_ref[q])
        return carry

    lax.fori_loop(0, S5_LANE_BLOCKS, block_body, 0)
    y = (jnp.concatenate([ytm[q] for q in range(S5_LANE_BLOCKS)], axis=-1)
         + dskip_ref[...] * jnp.concatenate([utm[q] for q in range(S5_LANE_BLOCKS)], axis=-1))
    y = jax.nn.gelu(y)
    y = y * _sigmoid(_dot(y.astype(BF16), wglu_ref[...]) + bglu_ref[...])
    for q in range(S5_LANE_BLOCKS):
        ytm[q] = y[:, q * LANES:(q + 1) * LANES]
    for b in range(nb):
        for q in range(S5_LANE_BLOCKS):
            y_ref[b, :, q * LANES:(q + 1) * LANES] = ytm[q, pl.ds(b, lt, stride=nb), :].astype(y_ref.dtype)
    hout_ref[...] = hst[...]


def _s5(u3, h0, bmat, cmat, acoef, dskip, wglu, bglu, lt):
    nb, L, _ = u3.shape
    rows = nb * lt
    tpb = S5_TILES_PER_BLOCK
    bmat = bmat.reshape(S5_LANE_BLOCKS, tpb, LANES, 2 * LANES).transpose(0, 2, 1, 3).reshape(
        S5_LANE_BLOCKS, LANES, tpb * 2 * LANES)
    cmat = cmat.reshape(S5_LANE_BLOCKS, tpb * 2 * LANES, LANES)
    acoef = jnp.broadcast_to(acoef[:, :1], (S5_TILES, nb, 2 * LANES))
    return pl.pallas_call(
        functools.partial(_s5_kernel, nb=nb, lt=lt),
        grid=(L // lt,),
        in_specs=[pl.BlockSpec((nb, lt, S5_WIDTH), lambda i: (0, i, 0)),
                  _full(h0.shape), _full(bmat.shape), _full(cmat.shape), _full(acoef.shape),
                  _full(dskip.shape), _full(wglu.shape), _full(bglu.shape)],
        out_specs=[pl.BlockSpec((nb, lt, S5_WIDTH), lambda i: (0, i, 0)), _full(h0.shape)],
        out_shape=[jax.ShapeDtypeStruct((nb, L, S5_WIDTH), BF16), jax.ShapeDtypeStruct(h0.shape, F32)],
        scratch_shapes=[pltpu.VMEM((S5_LANE_BLOCKS, rows, LANES), F32), pltpu.VMEM((tpb, rows, LANES), F32),
                        pltpu.VMEM((tpb, rows, LANES), F32), pltpu.VMEM((S5_LANE_BLOCKS, rows, LANES), F32),
                        pltpu.VMEM(h0.shape, F32)],
        compiler_params=_params(("arbitrary",)),
        name="s5_scan",
    )(u3, h0, bmat, cmat, acoef, dskip, wglu, bglu)


def _s5_weights(a_re, a_im, log_dt, b_re, b_im, c_re, c_im):
    dt = jnp.exp(log_dt)[:, None]
    lr, li = a_re * dt, a_im * dt
    mag = jnp.exp(lr)
    ab_re, ab_im = mag * jnp.cos(li), mag * jnp.sin(li)
    den = a_re * a_re + a_im * a_im
    cf_re = ((ab_re - 1.0) * a_re + ab_im * a_im) / den
    cf_im = (ab_im * a_re - (ab_re - 1.0) * a_im) / den
    bb_re = cf_re[..., None] * b_re - cf_im[..., None] * b_im
    bb_im = cf_re[..., None] * b_im + cf_im[..., None] * b_re
    jj = np.arange(S5_TILES)[:, None, None]
    lg = np.arange(8)[None, :, None]
    gi = np.arange(2)[None, None, :]
    sel = jnp.asarray((lg == 2 * (jj % 4) + gi).astype(np.float32))
    tiles = lambda w: w.reshape(S5_TILES, 2, *w.shape[1:])
    bt = lambda w: jnp.einsum('jlg,jgpn->jlngp', sel, tiles(w)).reshape(S5_TILES, LANES, LANES)
    bmat = jnp.concatenate([bt(bb_re), bt(bb_im)], axis=-1)
    ct = lambda w: jnp.einsum('jlg,jgnp->jgpln', sel, tiles(w)).reshape(S5_TILES, LANES, LANES)
    cmat = jnp.concatenate([ct(c_re), -ct(c_im)], axis=1)
    acoef = jnp.concatenate([ab_re.reshape(S5_TILES, LANES), ab_im.reshape(S5_TILES, LANES)], axis=-1)
    acoef = jnp.broadcast_to(acoef[:, None, :], (S5_TILES, SUBLANES, 2 * LANES))
    return bmat.astype(BF16), cmat.astype(BF16), acoef


def _s5_state_in(h_re, h_im):
    nb = h_re.shape[0]
    h = jnp.concatenate([h_re.reshape(nb, S5_TILES, LANES), h_im.reshape(nb, S5_TILES, LANES)], axis=-1)
    return jnp.transpose(h, (1, 0, 2))


def _s5_state_out(h):
    nb = h.shape[1]
    h = jnp.transpose(h, (1, 0, 2))
    return (h[..., :LANES].reshape(nb, S5_GROUPS, S5_STATE), h[..., LANES:].reshape(nb, S5_GROUPS, S5_STATE))


def _split_bf16(a):
    hi = a.astype(BF16)
    return hi, (a - hi.astype(F32)).astype(BF16)


def _bdot(a, b):
    return lax.dot_general(a, b, (((2,), (1,)), ((0,), (0,))), preferred_element_type=F32)


def _bdot_nt(a, b):
    return lax.dot_general(a, b, (((2,), (2,)), ((0,), (0,))), preferred_element_type=F32)


def _bdot_bf16(a, b):
    return _bdot(a.astype(BF16), b.astype(BF16))


def _unit_lower_inverse(nmat):
    lc = nmat.shape[-1]
    ri = lax.broadcasted_iota(jnp.int32, nmat.shape, 1)
    ci = lax.broadcasted_iota(jnp.int32, nmat.shape, 2)
    base = 16
    dmat = jnp.where(ri // base == ci // base, nmat, 0.0)
    inv = jnp.where(ri == ci, 1.0, 0.0) - dmat
    pw = dmat
    for _ in range(3):
        pw = _bdot_bf16(pw, pw)
        inv = inv + _bdot_bf16(inv, pw)
    size = base
    while size < lc:
        off = jnp.where(ri // (2 * size) == ci // (2 * size), jnp.where(ri // size > ci // size, nmat, 0.0), 0.0)
        inv = inv - _bdot_bf16(_bdot_bf16(inv, off), inv)
        size *= 2
    return inv


def _gdn_local_kernel(qkv_ref, ba_ref, ctx_ref, cw_ref, p1_ref, p2_ref, ltri_ref,
                      u0_ref, w_ref, qd_ref, kd_ref, attn_ref, g_ref, cout_ref, xpad, *, lc, cp):
    c = pl.program_id(1)
    rb = lc * cp

    @pl.when(c == 0)
    def _():
        xpad[0:SUBLANES, :] = ctx_ref[0]

    xpad[SUBLANES:SUBLANES + rb, :] = qkv_ref[0]
    cw = cw_ref[...]
    conv = (cw[3:4] * xpad[8:8 + rb, :] + cw[2:3] * xpad[7:7 + rb, :]
            + cw[1:2] * xpad[6:6 + rb, :] + cw[0:1] * xpad[5:5 + rb, :])
    tail = xpad[rb:rb + SUBLANES, :]
    xpad[0:SUBLANES, :] = tail
    cout_ref[0] = tail
    a = conv * _sigmoid(conv)

    ba = ba_ref[0]
    beta_all = _sigmoid(ba)
    sp_in = ba + p2_ref[...]
    softplus = jnp.maximum(sp_in, 0.0) + jnp.log(1.0 + jnp.exp(-jnp.abs(sp_in)))
    g_all = p1_ref[...] * softplus
    g_hi = g_all.astype(BF16)
    g_r = g_all - g_hi.astype(F32)
    g_mid = g_r.astype(BF16)
    g_lo = (g_r - g_mid.astype(F32)).astype(BF16)
    lt = ltri_ref[...]
    G = _dot(lt, g_hi) + _dot(lt, g_mid) + _dot(lt, g_lo)
    g_ref[0] = G
    GT = G.T
    pairs = [(h, cc) for h in range(GDN_HEADS) for cc in range(cp)]
    qs, ks, vs, betas, gcols, grows, glasts = [], [], [], [], [], [], []
    for h in range(GDN_HEADS):
        qa = a[:, h * GDN_DK:(h + 1) * GDN_DK]
        ka = a[:, (GDN_HEADS + h) * GDN_DK:(GDN_HEADS + h + 1) * GDN_DK]
        va = a[:, (2 * GDN_HEADS + h) * GDN_DK:(2 * GDN_HEADS + h + 1) * GDN_DK]
        qa = qa * lax.rsqrt(jnp.sum(qa * qa, -1, keepdims=True) + NORM_EPS) * (GDN_DK ** -0.5)
        ka = ka * lax.rsqrt(jnp.sum(ka * ka, -1, keepdims=True) + NORM_EPS)
        for cc in range(cp):
            rows = slice(cc * lc, (cc + 1) * lc)
            qs.append(qa[rows])
            ks.append(ka[rows])
            vs.append(va[rows])
            betas.append(beta_all[rows, h:h + 1])
            gcols.append(G[rows, GDN_HEADS + h:GDN_HEADS + h + 1])
            grows.append(GT[GDN_HEADS + h:GDN_HEADS + h + 1, cc * lc:(cc + 1) * lc])
            glasts.append(GT[GDN_HEADS + h:GDN_HEADS + h + 1, (cc + 1) * lc - 1:(cc + 1) * lc])
    q3, k3, v3 = jnp.stack(qs), jnp.stack(ks), jnp.stack(vs)
    beta3, gcol3 = jnp.stack(betas), jnp.stack(gcols)
    grow3, glast3 = jnp.stack(grows), jnp.stack(glasts)
    shape3 = (len(pairs), lc, lc)
    ri = lax.broadcasted_iota(jnp.int32, shape3, 1)
    ci = lax.broadcasted_iota(jnp.int32, shape3, 2)
    incl = ri >= ci
    dec3 = jnp.where(incl, jnp.exp(jnp.where(incl, gcol3 - grow3, 0.0)), 0.0)
    eg3 = jnp.exp(gcol3)
    kb3 = k3 * beta3
    kbf3 = k3.astype(BF16)
    nmat3 = jnp.where(ri > ci, _bdot_nt(kb3.astype(BF16), kbf3) * dec3, 0.0)
    inv3 = _unit_lower_inverse(nmat3)
    sol3 = _bdot_bf16(inv3, jnp.concatenate([v3 * beta3, kb3 * eg3], axis=-1))
    w3 = sol3[:, :, GDN_DV:].astype(BF16)
    qd3 = (q3 * eg3).astype(BF16)
    kd3 = (k3 * jnp.exp(glast3 - gcol3)).astype(BF16)
    attn3 = (_bdot_nt(q3.astype(BF16), kbf3) * dec3).astype(BF16)
    for i, (h, cc) in enumerate(pairs):
        rows = slice(cc * lc, (cc + 1) * lc)
        cols = slice(h * GDN_DV, (h + 1) * GDN_DV)
        u0_ref[0, rows, cols] = sol3[i, :, :GDN_DV]
        w_ref[0, rows, cols] = w3[i]
        qd_ref[0, rows, cols] = qd3[i]
        kd_ref[0, rows, cols] = kd3[i]
        attn_ref[0, rows, h * lc:(h + 1) * lc] = attn3[i]


def _gdn_seq_kernel(u0_ref, w_ref, qd_ref, kd_ref, attn_ref, g_ref, z_ref, s0_ref, nw_ref,
                    y_ref, sout_ref, S, *, lc, nbb):
    c = pl.program_id(1)

    @pl.when(c == 0)
    def _():
        S[...] = s0_ref[...]

    nw = nw_ref[...]
    pairs = [(bb, h) for bb in range(nbb) for h in range(GDN_HEADS)]
    hcols = lambda h: slice(h * GDN_DV, (h + 1) * GDN_DV)
    stack = lambda f: jnp.stack([f(bb, h) for bb, h in pairs])
    dlast = jnp.exp(g_ref[:, lc - 1:lc, :])
    S3 = S[...].reshape(len(pairs), GDN_DK, GDN_DV)
    wq3 = stack(lambda bb, h: jnp.concatenate([w_ref[bb, :, hcols(h)], qd_ref[bb, :, hcols(h)]], axis=0))
    r3 = _bdot(wq3, S3.astype(BF16))
    ub3 = (stack(lambda bb, h: u0_ref[bb, :, hcols(h)]) - r3[:, :lc]).astype(BF16)
    o3 = r3[:, lc:] + _bdot(stack(lambda bb, h: attn_ref[bb, :, h * lc:(h + 1) * lc]), ub3)
    d3 = stack(lambda bb, h: dlast[bb, :, GDN_HEADS + h:GDN_HEADS + h + 1])
    kd3 = stack(lambda bb, h: kd_ref[bb, :, hcols(h)])
    kdu3 = lax.dot_general(kd3, ub3, (((1,), (1,)), ((0,), (0,))), preferred_element_type=F32)
    S[...] = (d3 * S3 + kdu3).reshape(S.shape)
    o3 = o3 * lax.rsqrt(jnp.mean(o3 * o3, -1, keepdims=True) + NORM_EPS) * nw
    for i, (bb, h) in enumerate(pairs):
        zh = z_ref[bb, :, hcols(h)]
        y_ref[bb, :, hcols(h)] = (o3[i] * (zh * _sigmoid(zh))).astype(y_ref.dtype)

    @pl.when(c == pl.num_programs(1) - 1)
    def _():
        sout_ref[...] = S[...]


def _gdn(qkv3, z3, ba3, ctx8, s0, cw, p1, p2, nw, lc, cp, nbb):
    nb, L, _ = qkv3.shape
    rb = lc * cp
    hd = GDN_HEADS * GDN_DV
    ltri = jnp.asarray(np.kron(np.eye(cp, dtype=np.float32), np.tril(np.ones((lc, lc), np.float32)))).astype(BF16)
    blk = lambda n: pl.BlockSpec((1, rb, n), lambda b, c: (b, c, 0))
    per_b = lambda shape: pl.BlockSpec((1,) + shape, lambda b, c: (b,) + (0,) * len(shape))
    cst = lambda shape: pl.BlockSpec(shape, lambda b, c: (0,) * len(shape))
    sds = lambda n, dt: jax.ShapeDtypeStruct((nb, L, n), dt)
    u0, w, qd, kd, attn, G, cout = pl.pallas_call(
        functools.partial(_gdn_local_kernel, lc=lc, cp=cp),
        grid=(nb, L // rb),
        in_specs=[blk(GDN_QKV), blk(LANES), per_b((SUBLANES, GDN_QKV)), cst(cw.shape), cst(p1.shape),
                  cst(p2.shape), cst(ltri.shape)],
        out_specs=[blk(hd), blk(hd), blk(hd), blk(hd), blk(GDN_HEADS * lc), blk(LANES),
                   per_b((SUBLANES, GDN_QKV))],
        out_shape=[sds(hd, F32), sds(hd, BF16), sds(hd, BF16), sds(hd, BF16), sds(GDN_HEADS * lc, BF16),
                   sds(LANES, F32), jax.ShapeDtypeStruct((nb, SUBLANES, GDN_QKV), F32)],
        scratch_shapes=[pltpu.VMEM((rb + SUBLANES, GDN_QKV), F32)],
        compiler_params=_params(("parallel", "arbitrary")),
        name="gdn_local",
    )(qkv3, ba3, ctx8, cw, p1, p2, ltri)
    sblk = lambda n: pl.BlockSpec((nbb, lc, n), lambda b, c: (b, c, 0))
    state = pl.BlockSpec((nbb, GDN_HEADS, GDN_DK, GDN_DV), lambda b, c: (b, 0, 0, 0))
    y, s_new = pl.pallas_call(
        functools.partial(_gdn_seq_kernel, lc=lc, nbb=nbb),
        grid=(nb // nbb, L // lc),
        in_specs=[sblk(hd), sblk(hd), sblk(hd), sblk(hd), sblk(GDN_HEADS * lc), sblk(LANES), sblk(hd),
                  state, cst(nw.shape)],
        out_specs=[sblk(hd), state],
        out_shape=[sds(hd, BF16), jax.ShapeDtypeStruct(s0.shape, F32)],
        scratch_shapes=[pltpu.VMEM((nbb, GDN_HEADS, GDN_DK, GDN_DV), F32)],
        compiler_params=_params(("parallel", "arbitrary")),
        name="gdn_seq",
    )(u0, w, qd, kd, attn, G, z3, s0, nw)
    return y, s_new, cout


def _ret_kernel(q_ref, k_ref, v_ref, g_ref, r0_ref, dec_ref, qs_ref, ks_ref, cd_ref, o_ref, rout_ref, R,
                *, nbb):
    c = pl.program_id(1)

    @pl.when(c == 0)
    def _():
        R[...] = r0_ref[...]

    pairs = [(bb, h) for bb in range(nbb) for h in range(RET_HEADS)]
    stack = lambda f: jnp.stack([f(bb, h) for bb, h in pairs])
    kcols = lambda h: slice(h * RET_DK, (h + 1) * RET_DK)
    vcols = lambda h: slice(h * RET_DV, (h + 1) * RET_DV)
    q3 = stack(lambda bb, h: q_ref[bb, :, kcols(h)])
    k3 = stack(lambda bb, h: k_ref[bb, :, kcols(h)])
    v3 = stack(lambda bb, h: v_ref[bb, :, vcols(h)])
    dec3 = stack(lambda bb, h: dec_ref[h])
    qs3 = stack(lambda bb, h: qs_ref[h])
    ks3 = stack(lambda bb, h: ks_ref[h])
    cd3 = stack(lambda bb, h: cd_ref[h])
    R3 = R[...].reshape(len(pairs), RET_DK, RET_DV)
    s3 = _bdot_nt(q3, k3) * dec3
    o3 = _bdot(s3.astype(BF16), v3) + _bdot(q3, R3.astype(BF16)) * qs3
    kv3 = lax.dot_general((k3.astype(F32) * ks3).astype(BF16), v3, (((1,), (1,)), ((0,), (0,))),
                          preferred_element_type=F32)
    R[...] = (cd3 * R3 + kv3).reshape(R.shape)
    mu = jnp.mean(o3, -1, keepdims=True)
    d3 = o3 - mu
    var = jnp.mean(d3 * d3, -1, keepdims=True)
    on3 = d3 * lax.rsqrt(var + LN_EPS)
    for i, (bb, h) in enumerate(pairs):
        gt = g_ref[bb, :, vcols(h)].astype(F32)
        o_ref[bb, :, vcols(h)] = (gt * _sigmoid(gt) * on3[i]).astype(o_ref.dtype)

    @pl.when(c == pl.num_programs(1) - 1)
    def _():
        rout_ref[...] = R[...]


def _retention(proj3, r0, chunk, nbb, nc):
    nb, L, _ = proj3.shape
    lc = chunk * nc
    log_g = np.log(1.0 - 2.0 ** (-5.0 - np.arange(RET_HEADS, dtype=np.float64)))
    idx = np.arange(lc, dtype=np.float64)
    dist = idx[:, None] - idx[None, :]
    which = (idx // chunk)[:, None] - (idx // chunk)[None, :]
    dec = np.where(which >= 0, np.exp(log_g[:, None, None] * np.abs(dist)), 0.0).astype(np.float32)
    qs = np.exp(log_g[:, None] * (idx + 1.0)).astype(np.float32)[..., None]
    ks = np.exp(log_g[:, None] * (lc - 1.0 - idx)).astype(np.float32)[..., None]
    cdec = np.exp(log_g * lc).astype(np.float32)[:, None, None]
    nqk = RET_HEADS * RET_DK
    nv = RET_HEADS * RET_DV
    cst = lambda shape: pl.BlockSpec(shape, lambda b, c: (0,) * len(shape))
    state = pl.BlockSpec((nbb, RET_HEADS, RET_DK, RET_DV), lambda b, c: (b, 0, 0, 0))
    return pl.pallas_call(
        functools.partial(_ret_kernel, nbb=nbb),
        grid=(nb // nbb, L // lc),
        in_specs=[pl.BlockSpec((nbb, lc, nqk), lambda b, c: (b, c, 0)),
                  pl.BlockSpec((nbb, lc, nqk), lambda b, c: (b, c, 1)),
                  pl.BlockSpec((nbb, lc, nv), lambda b, c: (b, c, 1)),
                  pl.BlockSpec((nbb, lc, nv), lambda b, c: (b, c, 2)),
                  state, cst(dec.shape), cst(qs.shape), cst(ks.shape), cst(cdec.shape)],
        out_specs=[pl.BlockSpec((nbb, lc, nv), lambda b, c: (b, c, 0)), state],
        out_shape=[jax.ShapeDtypeStruct((nb, L, nv), BF16), jax.ShapeDtypeStruct(r0.shape, F32)],
        scratch_shapes=[pltpu.VMEM((nbb, RET_HEADS, RET_DK, RET_DV), F32)],
        compiler_params=_params(("parallel", "arbitrary")),
        name="retention",
    )(proj3, proj3, proj3, proj3, r0, jnp.asarray(dec), jnp.asarray(qs), jnp.asarray(ks), jnp.asarray(cdec))


RUN_ALIGN = SUBLANES
RUN_PIECE = 64
MOE_VMEM_LIMIT = 58 * 1024 * 1024


def _moe_params(sem):
    return pltpu.CompilerParams(dimension_semantics=sem, vmem_limit_bytes=MOE_VMEM_LIMIT)


def _tile_cap(tm):
    rows = TOP_K * tm + N_EXPERTS * (RUN_ALIGN - 1)
    return -(-rows // LANES) * LANES


def _route_kernel(x_ref, rw_ref, rb_ref, lst_ref, ust_ref, pos_ref, gt_ref, cnt_ref, xs_ref):
    tm = x_ref.shape[0]
    cap = xs_ref.shape[1]
    x = x_ref[...]
    lane = lax.broadcasted_iota(jnp.int32, (tm, LANES), 1)
    lane_f = lane.astype(F32)
    xh, xl = _split_bf16(x)
    wh, wl = _split_bf16(rw_ref[...])
    logits = _dot(xh, wh) + _dot(xh, wl) + _dot(xl, wh) + rb_ref[...]
    logits = jnp.where(lane < N_EXPERTS, logits, -jnp.inf)
    vals, hots = [], []
    for _ in range(TOP_K):
        m = jnp.max(logits, -1, keepdims=True)
        first = jnp.min(jnp.where(logits == m, lane_f, float(LANES)), -1, keepdims=True)
        hot = lane_f == first
        vals.append(m)
        hots.append(hot)
        logits = jnp.where(hot, -jnp.inf, logits)
    es = [jnp.exp(v - vals[0]) for v in vals]
    den = es[0] + es[1] + es[2] + es[3]
    multi = jnp.zeros((tm, LANES), F32)
    for hot in hots:
        multi = multi + hot.astype(F32)
    counts = jnp.sum(multi, 0, keepdims=True)
    units = jnp.floor((counts + (RUN_ALIGN - 1)) * (1.0 / RUN_ALIGN))
    offs = _dot(jnp.broadcast_to(units, (SUBLANES, LANES)).astype(BF16), ust_ref[...])[0:1] * float(RUN_ALIGN)
    before = _dot(lst_ref[...], multi.astype(BF16))
    slot = offs + before
    pos = jnp.zeros((tm, LANES), F32)
    gt = jnp.zeros((tm, LANES), F32)
    for kk in range(TOP_K):
        pos = jnp.where(lane == kk, jnp.sum(jnp.where(hots[kk], slot, 0.0), -1, keepdims=True), pos)
        gt = jnp.where(lane == kk, es[kk] / den, gt)
    pos = pos.astype(jnp.int32)
    pos_ref[...] = pos
    gt_ref[...] = gt
    cnt_ref[0] = counts
    pos_t = pos.T[0:2 * SUBLANES].astype(jnp.int16)
    row = lax.broadcasted_iota(jnp.int16, (cap, tm), 0)
    sel = jnp.zeros((cap, tm), BF16)
    for kk in range(TOP_K):
        sel = sel + jnp.where(row == pos_t[kk:kk + 1, :], jnp.ones((), BF16), jnp.zeros((), BF16))
    xs_ref[0] = _dot(sel, xh)


def _route(x, rw, rb, tm):
    t = x.shape[0]
    nt = t // tm
    cap = _tile_cap(tm)
    lst = jnp.asarray(np.tril(np.ones((tm, tm), np.float32), -1)).astype(BF16)
    ust = jnp.asarray(np.triu(np.ones((LANES, LANES), np.float32), 1)).astype(BF16)
    row = lambda n: pl.BlockSpec((tm, n), lambda i: (i, 0))
    return pl.pallas_call(
        _route_kernel,
        grid=(nt,),
        in_specs=[row(D_MODEL), _full(rw.shape), _full(rb.shape), _full(lst.shape), _full(ust.shape)],
        out_specs=[row(LANES), row(LANES), pl.BlockSpec((1, 1, LANES), lambda i: (i, 0, 0)),
                   pl.BlockSpec((1, cap, D_MODEL), lambda i: (i, 0, 0))],
        out_shape=[jax.ShapeDtypeStruct((t, LANES), jnp.int32), jax.ShapeDtypeStruct((t, LANES), F32),
                   jax.ShapeDtypeStruct((nt, 1, LANES), F32), jax.ShapeDtypeStruct((nt, cap, D_MODEL), F32)],
        compiler_params=_moe_params(("parallel",)),
        name="moe_route",
    )(x, rw, rb, lst, ust)


def _expert_kernel(be_ref, nu_ref, ilo_ref, rows_ref, gs_ref, n8_ref, lo_ref, *refs, blk, tiles):
    ng = len(tiles)
    xs_hbms = refs[:ng]
    w1_ref, b1_ref, w2_ref, b2_ref = refs[ng:ng + 4]
    ys_hbms = refs[ng + 4:2 * ng + 4]
    xbuf, ybuf, w1b, w2b, in_sem, out_sem = refs[2 * ng + 4:]
    nt = sum(tiles)
    firsts = [sum(tiles[:g]) for g in range(ng)]
    j = pl.program_id(0)
    nu = nu_ref[0]

    def for_each_run(jb, fn):
        e = be_ref[jb]
        base = jb * blk
        for g in range(ng):
            end = firsts[g] + tiles[g]

            def cond(i, end=end):
                return (i < end) & (gs_ref[e * nt + jnp.minimum(i, nt - 1)] < base + blk)

            def body(i, g=g):
                g0 = gs_ref[e * nt + i]
                first = jnp.maximum(g0, base)
                last = jnp.minimum(g0 + n8_ref[e * nt + i], base + blk)
                fn(g, i - firsts[g], lo_ref[e * nt + i] + (first - g0), first - base, last - first)
                return i + 1

            lax.while_loop(cond, body, jnp.clip(ilo_ref[jb], firsts[g], end))

    def pieces(length, fn):
        def digits(sizes):
            for size in sizes:
                @pl.when((length & size) != 0)
                def _(size=size):
                    fn(length & ~(2 * size - 1), size)

        sizes = [blk >> k for k in range(blk.bit_length()) if blk >> k >= RUN_ALIGN]

        @pl.when(length > 2 * RUN_PIECE - 1)
        def _():
            digits([z for z in sizes if z > RUN_PIECE])

        digits([z for z in sizes if z <= RUN_PIECE])

    def aligned(v, size):
        return pl.ds(pl.multiple_of(v, RUN_ALIGN), size)

    def copy_in(jb, slot):
        def run(g, ig, src, dst, length):
            def piece(off, size):
                pltpu.make_async_copy(xs_hbms[g].at[ig, aligned(src + off, size)],
                                      xbuf.at[slot, aligned(dst + off, size)], in_sem.at[slot]).start()
            pieces(length, piece)
        for_each_run(jb, run)

    def copy_out(jb, slot):
        def run(g, ig, src, dst, length):
            def piece(off, size):
                pltpu.make_async_copy(ybuf.at[slot, aligned(dst + off, size)],
                                      ys_hbms[g].at[ig, aligned(src + off, size)], out_sem.at[slot]).start()
            pieces(length, piece)
        for_each_run(jb, run)

    def wait_rows(sem, nrows):
        size = blk
        while size >= RUN_ALIGN:
            @pl.when((nrows & size) != 0)
            def _(size=size):
                pltpu.make_async_copy(xbuf.at[1, pl.ds(0, size)], xbuf.at[0, pl.ds(0, size)], sem).wait()
            size //= 2

    @pl.when(j < nu)
    def _():
        slot = j % 2

        @pl.when(j == 0)
        def _():
            xbuf[...] = jnp.zeros_like(xbuf)
            copy_in(0, 0)

        wait_rows(in_sem.at[slot], rows_ref[j])

        @pl.when(j + 1 < nu)
        def _():
            copy_in(j + 1, 1 - slot)

        @pl.when(j >= 2)
        def _():
            wait_rows(out_sem.at[slot], rows_ref[jnp.maximum(j - 2, 0)])

        @pl.when((j == 0) | (be_ref[j] != be_ref[jnp.maximum(j - 1, 0)]))
        def _():
            w1b[...] = w1_ref[0, 0].astype(BF16)
            w2b[...] = w2_ref[0, 0].astype(BF16)

        h = _dot(xbuf[slot].astype(BF16), w1b[...]) + b1_ref[0]
        glu = jnp.minimum(h[:, :D_FF], SWIGLU_LIMIT)
        lin = jnp.clip(h[:, D_FF:], -SWIGLU_LIMIT, SWIGLU_LIMIT)
        act = glu * _sigmoid(SWIGLU_ALPHA * glu) * (lin + 1.0)
        ybuf[slot] = _dot(act.astype(BF16), w2b[...]) + b2_ref[0]
        copy_out(j, slot)

        @pl.when(j == nu - 1)
        def _():
            wait_rows(out_sem.at[slot], rows_ref[j])

            @pl.when(j >= 1)
            def _():
                wait_rows(out_sem.at[1 - slot], rows_ref[jnp.maximum(j - 1, 0)])


def _experts(tables, xs_list, w1, b1, w2, b2, layer, blk):
    ng = len(xs_list)
    n_blk = tables[0].shape[0]
    wspec = lambda shape: pl.BlockSpec((1, 1) + shape, lambda j, be, *_: (layer, be[j], 0, 0))
    bspec = lambda n: pl.BlockSpec((1, 1, n), lambda j, be, *_: (be[j], 0, 0))
    hbm = pl.BlockSpec(memory_space=pl.ANY)
    grid_spec = pltpu.PrefetchScalarGridSpec(
        num_scalar_prefetch=len(tables),
        grid=(n_blk,),
        in_specs=[hbm] * ng + [wspec((D_MODEL, 2 * D_FF)), bspec(2 * D_FF), wspec((D_FF, D_MODEL)), bspec(D_MODEL)],
        out_specs=[hbm] * ng,
        scratch_shapes=[pltpu.VMEM((2, blk, D_MODEL), F32), pltpu.VMEM((2, blk, D_MODEL), F32),
                        pltpu.VMEM((D_MODEL, 2 * D_FF), BF16), pltpu.VMEM((D_FF, D_MODEL), BF16),
                        pltpu.SemaphoreType.DMA((2,)), pltpu.SemaphoreType.DMA((2,))],
    )
    return pl.pallas_call(
        functools.partial(_expert_kernel, blk=blk, tiles=tuple(xs.shape[0] for xs in xs_list)),
        grid_spec=grid_spec,
        out_shape=[jax.ShapeDtypeStruct(xs.shape, F32) for xs in xs_list],
        input_output_aliases={len(tables) + g: g for g in range(ng)},
        compiler_params=_moe_params(("arbitrary",)),
        name="moe_experts",
    )(*tables, *xs_list, w1, b1, w2, b2)


def _combine_kernel(ys_ref, pos_ref, gt_ref, x_ref, p_ref, g_ref, b_ref, plew_ref, gatew_ref, o_ref):
    tm = x_ref.shape[0]
    cap = ys_ref.shape[1]
    pos = pos_ref[...].astype(jnp.int16)
    gt = gt_ref[...].astype(BF16)
    col = lax.broadcasted_iota(jnp.int16, (tm, cap), 1)
    sel = jnp.zeros((tm, cap), BF16)
    for kk in range(TOP_K):
        sel = sel + jnp.where(col == pos[:, kk:kk + 1], gt[:, kk:kk + 1], jnp.zeros((), BF16))
    y = _dot(sel, ys_ref[0].astype(BF16))
    x2 = _layer_norm(DEEPNORM_ALPHA * x_ref[...] + y, g_ref[...], b_ref[...])
    pp = _dot(p_ref[...].astype(BF16), plew_ref[...])
    gg = _sigmoid(_dot(x2.astype(BF16), gatew_ref[...]))
    o_ref[...] = x2 + pp * gg


def _combine(ys, pos, gt, x, p, layer, g, b, plew, gatew, tm):
    t = x.shape[0]
    cap = ys.shape[1]
    row = lambda n: pl.BlockSpec((tm, n), lambda i: (i, 0))
    return pl.pallas_call(
        _combine_kernel,
        grid=(t // tm,),
        in_specs=[pl.BlockSpec((1, cap, D_MODEL), lambda i: (i, 0, 0)), row(LANES), row(LANES),
                  row(D_MODEL), pl.BlockSpec((None, tm, PLE_DIM), lambda i: (layer, i, 0)),
                  _full((1, D_MODEL)), _full((1, D_MODEL)),
                  _full(plew.shape), _full(gatew.shape)],
        out_specs=row(D_MODEL),
        out_shape=jax.ShapeDtypeStruct((t, D_MODEL), F32),
        compiler_params=_moe_params(("parallel",)),
        name="moe_combine",
    )(ys, pos, gt, x, p, g, b, plew, gatew)


def _moe_tables(cnt, t, blk):
    nt = cnt.shape[0]
    n = cnt[:, 0, :N_EXPERTS].astype(jnp.int32)
    n8 = (n + RUN_ALIGN - 1) // RUN_ALIGN * RUN_ALIGN
    lo = jnp.cumsum(n8, axis=1) - n8
    rows_e = jnp.sum(n8, axis=0)
    padded = (rows_e + blk - 1) // blk * blk
    pend = jnp.cumsum(padded)
    pstart = pend - padded
    gstart = pstart[None, :] + jnp.cumsum(n8, axis=0) - n8
    n_blk = -(-(TOP_K * t + nt * N_EXPERTS * (RUN_ALIGN - 1) + N_EXPERTS * (blk - 1)) // blk)
    n_used = (pend[-1] // blk).astype(jnp.int32).reshape(1)
    blk_start = jnp.arange(n_blk, dtype=jnp.int32) * blk
    blk_e = jnp.minimum(jnp.sum((pend[None, :] <= blk_start[:, None]).astype(jnp.int32), axis=1), N_EXPERTS - 1)
    run_end = (gstart + n8)[:, blk_e]
    ilo = jnp.sum((run_end <= blk_start[None, :]).astype(jnp.int32), axis=0)
    rows_b = jnp.clip((pstart + rows_e)[blk_e] - blk_start, 0, blk)
    flat = lambda a: a.T.reshape(-1).astype(jnp.int32)
    i32 = lambda a: a.astype(jnp.int32)
    return (i32(blk_e), n_used, i32(ilo), i32(rows_b), flat(gstart), flat(n8), flat(lo))


def _moe_ple(xs_in, ps, rw, rb, w1, b1, w2, b2, layer, g, b, plew, gatew, tms, blk):
    routed = [_route(x, rw, rb, tm) for x, tm in zip(xs_in, tms)]
    cnt = jnp.concatenate([r[2] for r in routed], axis=0)
    tables = _moe_tables(cnt, sum(x.shape[0] for x in xs_in), blk)
    ys = _experts(tables, [r[3] for r in routed], w1, b1, w2, b2, layer, blk)
    return [_combine(y, r[0], r[1], x, p, layer, g, b, plew, gatew, tm)
            for y, r, x, p, tm in zip(ys, routed, xs_in, ps, tms)]


_ROT_PERM = np.concatenate([np.arange(0, RET_DK, 2), np.arange(1, RET_DK, 2)])


def _lane_row(vals, offset):
    row = jnp.zeros((1, LANES), F32)
    return row.at[0, offset:offset + vals.shape[0]].set(vals)


def _even_mixer(x, nb, L, s5_re, s5_im, gdn_s, conv_s, W, tm):
    t = nb * L
    lc = L if L <= CHUNK else CHUNK
    u, qkv, z, ba = _proj_even(x, W['wu'], W['wqkv'], W['wz'], W['wba'], tm)
    yA3, h_new = _s5(u.reshape(nb, L, S5_WIDTH), _s5_state_in(s5_re[0].astype(F32), s5_im[0].astype(F32)),
                     W['bmat'], W['cmat'], W['acoef'], W['dskip'], W['wglu'], W['bglu'], lc)
    new_re, new_im = _s5_state_out(h_new)
    ctx8 = jnp.concatenate([jnp.zeros((nb, SUBLANES - (GDN_CONV - 1), GDN_QKV), F32), conv_s[0].astype(F32)], axis=1)
    yB3, new_gdn, cout = _gdn(qkv.reshape(nb, L, GDN_QKV), z.reshape(nb, L, -1), ba.reshape(nb, L, LANES),
                              ctx8, gdn_s[0].astype(F32), W['convw'], W['p1'], W['p2'], W['normw'], lc,
                              cp=min(4, L // lc), nbb=4)
    new_conv = cout[:, SUBLANES - (GDN_CONV - 1):, :]
    x = _outproj_ln([yA3.reshape(t, -1), yB3.reshape(t, -1)], [W['wout_a'], W['wout_b']], x,
                    W['ln1_g'][0], W['ln1_b'][0], tm)
    return x, new_re[None], new_im[None], new_gdn[None], new_conv[None]


def _odd_mixer(x, nb, L, ret_s, pos0, W, tm):
    t = nb * L
    lc = L if L <= CHUNK else CHUNK
    pos = pos0 + jnp.arange(L, dtype=F32)
    freq = 1.0 / (ROPE_BASE ** jnp.linspace(0.0, 1.0, RET_DK // 2, dtype=F32))
    ang = pos[:, None] * freq[None]
    reps = max(1, tm // L)
    cos = jnp.tile(jnp.cos(ang), (reps, 1))
    sin = jnp.tile(jnp.sin(ang), (reps, 1))
    proj = _proj_odd(x, W['win_odd'], cos, sin, tm)
    half = RET_DK // 2
    r0 = ret_s[0].astype(F32).reshape(nb, RET_HEADS, half, 2, RET_DV).swapaxes(2, 3).reshape(
        nb, RET_HEADS, RET_DK, RET_DV)
    o3, r_new = _retention(proj.reshape(nb, L, -1), r0, lc, nbb=2, nc=min(4, L // lc))
    new_ret = r_new.reshape(nb, RET_HEADS, 2, half, RET_DV).swapaxes(2, 3).reshape(nb, RET_HEADS, RET_DK, RET_DV)
    x = _outproj_ln([o3.reshape(t, -1)], [W['wout_odd']], x, W['ln1_g'][1], W['ln1_b'][1], tm)
    return x, new_ret[None]


def kernel(x_prompt, x_sample, state_s5_re, state_s5_im, state_gdn, state_gdn_conv, state_ret, p_prompt, p_sample, w_in_even, s5_a_re, s5_a_im, s5_log_dt, s5_b_re, s5_b_im, s5_c_re, s5_c_im, s5_d, s5_w_glu, s5_b_glu, gdn_conv_w, gdn_a_log, gdn_dt_bias, gdn_norm_w, w_out_even, w_in_odd, w_out_odd, ln1_g, ln1_b, ln2_g, ln2_b, router_w, router_b, moe_w1, moe_b1, moe_w2, moe_b2, ple_w, ple_gate_w):
    o1 = S5_WIDTH
    o2 = o1 + GDN_QKV
    o3 = o2 + GDN_HEADS * GDN_DV
    win = w_in_even[0]
    bmat, cmat, acoef = _s5_weights(s5_a_re[0], s5_a_im[0], s5_log_dt[0], s5_b_re[0], s5_b_im[0],
                                    s5_c_re[0], s5_c_im[0])
    wodd = w_in_odd[0]
    nk = RET_HEADS * RET_DK
    perm_cols = lambda w: w.reshape(D_MODEL, RET_HEADS, RET_DK)[:, :, _ROT_PERM].reshape(D_MODEL, nk)
    W = dict(
        wu=win[:, :o1].astype(BF16), wqkv=win[:, o1:o2].astype(BF16), wz=win[:, o2:o3].astype(BF16),
        wba=jnp.pad(win[:, o3:], ((0, 0), (0, LANES - 2 * GDN_HEADS))).astype(BF16),
        bmat=bmat, cmat=cmat, acoef=acoef, dskip=s5_d[0][None], wglu=s5_w_glu[0].astype(BF16),
        bglu=s5_b_glu[0][None], convw=gdn_conv_w[0],
        p1=_lane_row(-jnp.exp(gdn_a_log[0]), GDN_HEADS), p2=_lane_row(gdn_dt_bias[0], GDN_HEADS),
        normw=gdn_norm_w[0][None],
        wout_a=w_out_even[0][:S5_WIDTH].astype(BF16), wout_b=w_out_even[0][S5_WIDTH:].astype(BF16),
        win_odd=jnp.concatenate([perm_cols(wodd[:, :nk]), perm_cols(wodd[:, nk:2 * nk]), wodd[:, 2 * nk:]],
                                axis=1).astype(BF16),
        wout_odd=w_out_odd[0].astype(BF16),
        ln1_g=ln1_g[:, None], ln1_b=ln1_b[:, None], ln2_g=ln2_g[:, None], ln2_b=ln2_b[:, None],
        rw=jnp.pad(router_w, ((0, 0), (0, 0), (0, LANES - N_EXPERTS))),
        rb=jnp.pad(router_b, ((0, 0), (0, LANES - N_EXPERTS)))[:, None],
        w1=moe_w1, b1=moe_b1[:, :, None], w2=moe_w2, b2=moe_b2[:, :, None],
        plew=ple_w.astype(BF16), gatew=ple_gate_w.astype(BF16),
    )
    bp, lp, _ = x_prompt.shape
    bs, ls, _ = x_sample.shape
    zeros = lambda *s: jnp.zeros(s, F32)
    shapes = [(bp, lp), (bs, ls)]
    tms = [512, 128]
    ps = [p_prompt, p_sample]
    xs = [x_prompt.reshape(bp * lp, D_MODEL).astype(F32), x_sample.reshape(bs * ls, D_MODEL).astype(F32)]
    even_states = [(zeros(1, bp, S5_GROUPS, S5_STATE), zeros(1, bp, S5_GROUPS, S5_STATE),
                    zeros(1, bp, GDN_HEADS, GDN_DK, GDN_DV), zeros(1, bp, GDN_CONV - 1, GDN_QKV)),
                   (state_s5_re, state_s5_im, state_gdn, state_gdn_conv)]
    ret_states = [zeros(1, bp, RET_HEADS, RET_DK, RET_DV), state_ret]
    pos0s = [0.0, float(PAST_LEN)]

    def moe(xs, layer):
        return _moe_ple(xs, [p.reshape(DEPTH, -1, PLE_DIM) for p in ps], W['rw'][layer], W['rb'][layer],
                        W['w1'], W['b1'][layer], W['w2'], W['b2'][layer], layer, W['ln2_g'][layer],
                        W['ln2_b'][layer], W['plew'][layer], W['gatew'][layer], tms, blk=512)

    even = [_even_mixer(x, nb, L, *st, W, tm) for x, (nb, L), st, tm in zip(xs, shapes, even_states, tms)]
    xs = moe([e[0] for e in even], 0)
    odd = [_odd_mixer(x, nb, L, st, pos0, W, tm)
           for x, (nb, L), st, pos0, tm in zip(xs, shapes, ret_states, pos0s, tms)]
    xs = moe([o[0] for o in odd], 1)
    dp = x_prompt.dtype
    y_p, y_s = xs[0].reshape(bp, lp, D_MODEL), xs[1].reshape(bs, ls, D_MODEL)
    (_, p_re, p_im, p_gdn, p_conv), (_, s_re, s_im, s_gdn, s_conv) = even
    p_ret, s_ret = odd[0][1], odd[1][1]
    return (y_p.astype(dp), y_s.astype(x_sample.dtype),
            p_re.astype(dp), p_im.astype(dp), p_gdn.astype(dp), p_conv.astype(dp), p_ret.astype(dp),
            s_re.astype(state_s5_re.dtype), s_im.astype(state_s5_im.dtype), s_gdn.astype(state_gdn.dtype),
            s_conv.astype(state_gdn_conv.dtype), s_ret.astype(state_ret.dtype))
```

```python
import functools
import math

import jax
import jax.numpy as jnp
import numpy as np
from jax import lax
from jax.experimental import pallas as pl
from jax.experimental.pallas import tpu as pltpu

F32 = jnp.float32
BF16 = jnp.bfloat16
HIGHEST = lax.Precision.HIGHEST

D_MODEL = 1024
CHUNK = 64
S5_WIDTH = 512
S5_GROUP = 16
S5_GROUPS = 32
S5_STATE = 64
S5_TILES = 16
GDN_HEADS = 4
GDN_DK = 128
GDN_DV = 128
GDN_CONV = 4
GDN_QKV = 1536
RET_HEADS = 4
RET_DK = 256
RET_DV = 512
ROPE_BASE = 10000.0
N_EXPERTS = 32
TOP_K = 4
D_FF = 1024
SWIGLU_LIMIT = 7.0
SWIGLU_ALPHA = 1.702
PLE_DIM = 256
DEPTH = 2
PAST_LEN = 1024
DEEPNORM_ALPHA = (2 * DEPTH) ** 0.25
LN_EPS = 1e-5
NORM_EPS = 1e-6

LANES = 128
SUBLANES = 8
VMEM_LIMIT = 48 * 1024 * 1024

def _params(sem):
    return pltpu.CompilerParams(dimension_semantics=sem, vmem_limit_bytes=VMEM_LIMIT)


def _dot(a, b):
    return jnp.dot(a, b, preferred_element_type=F32)


def _dot_hi(a, b):
    return jnp.dot(a, b, preferred_element_type=F32, precision=HIGHEST)


def _dot_nt(a, b, precision=None):
    return lax.dot_general(a, b, (((1,), (1,)), ((), ())), preferred_element_type=F32,
                           precision=precision)


def _dot_tn(a, b):
    return lax.dot_general(a, b, (((0,), (0,)), ((), ())), preferred_element_type=F32)


def _sigmoid(x):
    return 1.0 / (1.0 + jnp.exp(-x))


def _full(shape):
    nd = len(shape)
    return pl.BlockSpec(shape, lambda *_: (0,) * nd)


def _proj_even_kernel(x_ref, wu_ref, wqkv_ref, wz_ref, wba_ref, u_ref, qkv_ref, z_ref, ba_ref):
    xb = x_ref[...].astype(BF16)
    u_ref[...] = _dot(xb, wu_ref[...])
    qkv_ref[...] = _dot(xb, wqkv_ref[...])
    z_ref[...] = _dot(xb, wz_ref[...])
    ba_ref[...] = _dot(xb, wba_ref[...])


def _proj_even(x, wu, wqkv, wz, wba, tm):
    t = x.shape[0]
    row = lambda n: pl.BlockSpec((tm, n), lambda i: (i, 0))
    return pl.pallas_call(
        _proj_even_kernel,
        grid=(t // tm,),
        in_specs=[row(D_MODEL), _full(wu.shape), _full(wqkv.shape), _full(wz.shape), _full(wba.shape)],
        out_specs=[row(S5_WIDTH), row(GDN_QKV), row(GDN_HEADS * GDN_DV), row(LANES)],
        out_shape=[jax.ShapeDtypeStruct((t, S5_WIDTH), F32), jax.ShapeDtypeStruct((t, GDN_QKV), F32),
                   jax.ShapeDtypeStruct((t, GDN_HEADS * GDN_DV), F32), jax.ShapeDtypeStruct((t, LANES), F32)],
        compiler_params=_params(("parallel",)),
        name="proj_even",
    )(x, wu, wqkv, wz, wba)


def _proj_odd_kernel(x_ref, w_ref, cos_ref, sin_ref, o_ref):
    j = pl.program_id(0)
    acc = _dot(x_ref[...].astype(BF16), w_ref[...])

    @pl.when(j == 0)
    def _():
        cos, sin = cos_ref[...], sin_ref[...]
        half = RET_DK // 2
        for h in range(2 * RET_HEADS):
            x0 = acc[:, h * RET_DK:h * RET_DK + half]
            x1 = acc[:, h * RET_DK + half:(h + 1) * RET_DK]
            scale = 1.0 if h < RET_HEADS else RET_DK ** -0.5
            o_ref[:, h * RET_DK:h * RET_DK + half] = ((x0 * cos - x1 * sin) * scale).astype(o_ref.dtype)
            o_ref[:, h * RET_DK + half:(h + 1) * RET_DK] = ((x0 * sin + x1 * cos) * scale).astype(o_ref.dtype)

    @pl.when(j != 0)
    def _():
        o_ref[...] = acc.astype(o_ref.dtype)


def _proj_odd(x, w, cos, sin, tm):
    t = x.shape[0]
    nblk = w.shape[1] // 2048
    period = cos.shape[0] // tm
    return pl.pallas_call(
        _proj_odd_kernel,
        grid=(nblk, t // tm),
        in_specs=[pl.BlockSpec((tm, D_MODEL), lambda j, i: (i, 0)),
                  pl.BlockSpec((D_MODEL, 2048), lambda j, i: (0, j)),
                  pl.BlockSpec((tm, LANES), lambda j, i: (i % period, 0)),
                  pl.BlockSpec((tm, LANES), lambda j, i: (i % period, 0))],
        out_specs=pl.BlockSpec((tm, 2048), lambda j, i: (i, j)),
        out_shape=jax.ShapeDtypeStruct((t, w.shape[1]), BF16),
        compiler_params=_params(("parallel", "parallel")),
        name="proj_odd",
    )(x, w, cos, sin)


def _layer_norm(r, g, b):
    mu = jnp.mean(r, -1, keepdims=True)
    d = r - mu
    var = jnp.mean(d * d, -1, keepdims=True)
    return d * lax.rsqrt(var + LN_EPS) * g + b


S5_LANE_BLOCKS = S5_WIDTH // LANES
S5_TILES_PER_BLOCK = S5_TILES // S5_LANE_BLOCKS


def _s5_kernel(u_ref, h0_ref, bmat_ref, cmat_ref, acoef_ref, dskip_ref, wglu_ref, bglu_ref,
               y_ref, hout_ref, utm, sre, sim, ytm, hst, *, nb, lt):
    tb = pl.program_id(0)

    @pl.when(tb == 0)
    def _():
        hst[...] = h0_ref[...]

    for b in range(nb):
        for q in range(S5_LANE_BLOCKS):
            utm[q, pl.ds(b, lt, stride=nb), :] = u_ref[b, :, q * LANES:(q + 1) * LANES]

    def block_body(q, carry):
        bu = _dot(utm[q].astype(BF16), bmat_ref[q])
        for g in range(S5_TILES_PER_BLOCK):
            sre[g] = bu[:, (2 * g) * LANES:(2 * g + 1) * LANES]
            sim[g] = bu[:, (2 * g + 1) * LANES:(2 * g + 2) * LANES]
        js = [q * S5_TILES_PER_BLOCK + g for g in range(S5_TILES_PER_BLOCK)]
        acs = [acoef_ref[j] for j in js]
        hs = [(hst[j, :, 0:LANES], hst[j, :, LANES:2 * LANES]) for j in js]
        for t in range(lt):
            rows = slice(t * nb, (t + 1) * nb)
            for g in range(S5_TILES_PER_BLOCK):
                ar, ai = acs[g][:, :LANES], acs[g][:, LANES:]
                hr, hi = hs[g]
                nhr = ar * hr - ai * hi + sre[g, rows, :]
                nhi = ar * hi + ai * hr + sim[g, rows, :]
                sre[g, rows, :] = nhr
                sim[g, rows, :] = nhi
                hs[g] = (nhr, nhi)
        for g, j in enumerate(js):
            hst[j, :, 0:LANES] = hs[g][0]
            hst[j, :, LANES:2 * LANES] = hs[g][1]
        st = jnp.concatenate([part for g in range(S5_TILES_PER_BLOCK) for part in (sre[g], sim[g])], axis=-1)
        ytm[q] = _dot(st.astype(BF16), cmat_ref[q])
        return carry

    lax.fori_loop(0, S5_LANE_BLOCKS, block_body, 0)
    y = (jnp.concatenate([ytm[q] for q in range(S5_LANE_BLOCKS)], axis=-1)
         + dskip_ref[...] * jnp.concatenate([utm[q] for q in range(S5_LANE_BLOCKS)], axis=-1))
    y = jax.nn.gelu(y)
    y = y * _sigmoid(_dot(y.astype(BF16), wglu_ref[...]) + bglu_ref[...])
    for q in range(S5_LANE_BLOCKS):
        ytm[q] = y[:, q * LANES:(q + 1) * LANES]
    for b in range(nb):
        for q in range(S5_LANE_BLOCKS):
            y_ref[b, :, q * LANES:(q + 1) * LANES] = ytm[q, pl.ds(b, lt, stride=nb), :].astype(y_ref.dtype)
    hout_ref[...] = hst[...]


def _s5(u3, h0, bmat, cmat, acoef, dskip, wglu, bglu, lt):
    nb, L, _ = u3.shape
    rows = nb * lt
    tpb = S5_TILES_PER_BLOCK
    bmat = bmat.reshape(S5_LANE_BLOCKS, tpb, LANES, 2 * LANES).transpose(0, 2, 1, 3).reshape(
        S5_LANE_BLOCKS, LANES, tpb * 2 * LANES)
    cmat = cmat.reshape(S5_LANE_BLOCKS, tpb * 2 * LANES, LANES)
    acoef = jnp.broadcast_to(acoef[:, :1], (S5_TILES, nb, 2 * LANES))
    return pl.pallas_call(
        functools.partial(_s5_kernel, nb=nb, lt=lt),
        grid=(L // lt,),
        in_specs=[pl.BlockSpec((nb, lt, S5_WIDTH), lambda i: (0, i, 0)),
                  _full(h0.shape), _full(bmat.shape), _full(cmat.shape), _full(acoef.shape),
                  _full(dskip.shape), _full(wglu.shape), _full(bglu.shape)],
        out_specs=[pl.BlockSpec((nb, lt, S5_WIDTH), lambda i: (0, i, 0)), _full(h0.shape)],
        out_shape=[jax.ShapeDtypeStruct((nb, L, S5_WIDTH), BF16), jax.ShapeDtypeStruct(h0.shape, F32)],
        scratch_shapes=[pltpu.VMEM((S5_LANE_BLOCKS, rows, LANES), F32), pltpu.VMEM((tpb, rows, LANES), F32),
                        pltpu.VMEM((tpb, rows, LANES), F32), pltpu.VMEM((S5_LANE_BLOCKS, rows, LANES), F32),
                        pltpu.VMEM(h0.shape, F32)],
        compiler_params=_params(("arbitrary",)),
        name="s5_scan",
    )(u3, h0, bmat, cmat, acoef, dskip, wglu, bglu)


def _s5_weights(a_re, a_im, log_dt, b_re, b_im, c_re, c_im):
    dt = jnp.exp(log_dt)[:, None]
    lr, li = a_re * dt, a_im * dt
    mag = jnp.exp(lr)
    ab_re, ab_im = mag * jnp.cos(li), mag * jnp.sin(li)
    den = a_re * a_re + a_im * a_im
    cf_re = ((ab_re - 1.0) * a_re + ab_im * a_im) / den
    cf_im = (ab_im * a_re - (ab_re - 1.0) * a_im) / den
    bb_re = cf_re[..., None] * b_re - cf_im[..., None] * b_im
    bb_im = cf_re[..., None] * b_im + cf_im[..., None] * b_re
    jj = np.arange(S5_TILES)[:, None, None]
    lg = np.arange(8)[None, :, None]
    gi = np.arange(2)[None, None, :]
    sel = jnp.asarray((lg == 2 * (jj % 4) + gi).astype(np.float32))
    tiles = lambda w: w.reshape(S5_TILES, 2, *w.shape[1:])
    bt = lambda w: jnp.einsum('jlg,jgpn->jlngp', sel, tiles(w)).reshape(S5_TILES, LANES, LANES)
    bmat = jnp.concatenate([bt(bb_re), bt(bb_im)], axis=-1)
    ct = lambda w: jnp.einsum('jlg,jgnp->jgpln', sel, tiles(w)).reshape(S5_TILES, LANES, LANES)
    cmat = jnp.concatenate([ct(c_re), -ct(c_im)], axis=1)
    acoef = jnp.concatenate([ab_re.reshape(S5_TILES, LANES), ab_im.reshape(S5_TILES, LANES)], axis=-1)
    acoef = jnp.broadcast_to(acoef[:, None, :], (S5_TILES, SUBLANES, 2 * LANES))
    return bmat.astype(BF16), cmat.astype(BF16), acoef


def _s5_state_in(h_re, h_im):
    nb = h_re.shape[0]
    h = jnp.concatenate([h_re.reshape(nb, S5_TILES, LANES), h_im.reshape(nb, S5_TILES, LANES)], axis=-1)
    return jnp.transpose(h, (1, 0, 2))


def _s5_state_out(h):
    nb = h.shape[1]
    h = jnp.transpose(h, (1, 0, 2))
    return (h[..., :LANES].reshape(nb, S5_GROUPS, S5_STATE), h[..., LANES:].reshape(nb, S5_GROUPS, S5_STATE))


def _split_bf16(a):
    hi = a.astype(BF16)
    return hi, (a - hi.astype(F32)).astype(BF16)


def _bdot(a, b):
    return lax.dot_general(a, b, (((2,), (1,)), ((0,), (0,))), preferred_element_type=F32)


def _bdot_nt(a, b):
    return lax.dot_general(a, b, (((2,), (2,)), ((0,), (0,))), preferred_element_type=F32)


def _bdot_bf16(a, b):
    return _bdot(a.astype(BF16), b.astype(BF16))


def _unit_lower_inverse(nmat):
    lc = nmat.shape[-1]
    ri = lax.broadcasted_iota(jnp.int32, nmat.shape, 1)
    ci = lax.broadcasted_iota(jnp.int32, nmat.shape, 2)
    base = 16
    dmat = jnp.where(ri // base == ci // base, nmat, 0.0)
    inv = jnp.where(ri == ci, 1.0, 0.0) - dmat
    pw = dmat
    for _ in range(3):
        pw = _bdot_bf16(pw, pw)
        inv = inv + _bdot_bf16(inv, pw)
    size = base
    while size < lc:
        off = jnp.where(ri // (2 * size) == ci // (2 * size), jnp.where(ri // size > ci // size, nmat, 0.0), 0.0)
        inv = inv - _bdot_bf16(_bdot_bf16(inv, off), inv)
        size *= 2
    return inv


def _gdn_local_kernel(qkv_ref, ba_ref, ctx_ref, cw_ref, p1_ref, p2_ref, ltri_ref,
                      u0_ref, w_ref, qd_ref, kd_ref, attn_ref, g_ref, cout_ref, xpad, *, lc, cp):
    c = pl.program_id(1)
    rb = lc * cp

    @pl.when(c == 0)
    def _():
        xpad[0:SUBLANES, :] = ctx_ref[0]

    xpad[SUBLANES:SUBLANES + rb, :] = qkv_ref[0]
    cw = cw_ref[...]
    conv = (cw[3:4] * xpad[8:8 + rb, :] + cw[2:3] * xpad[7:7 + rb, :]
            + cw[1:2] * xpad[6:6 + rb, :] + cw[0:1] * xpad[5:5 + rb, :])
    tail = xpad[rb:rb + SUBLANES, :]
    xpad[0:SUBLANES, :] = tail
    cout_ref[0] = tail
    a = conv * _sigmoid(conv)

    ba = ba_ref[0]
    beta_all = _sigmoid(ba)
    sp_in = ba + p2_ref[...]
    softplus = jnp.maximum(sp_in, 0.0) + jnp.log(1.0 + jnp.exp(-jnp.abs(sp_in)))
    g_all = p1_ref[...] * softplus
    g_hi = g_all.astype(BF16)
    g_r = g_all - g_hi.astype(F32)
    g_mid = g_r.astype(BF16)
    g_lo = (g_r - g_mid.astype(F32)).astype(BF16)
    lt = ltri_ref[...]
    G = _dot(lt, g_hi) + _dot(lt, g_mid) + _dot(lt, g_lo)
    g_ref[0] = G
    GT = G.T
    pairs = [(h, cc) for h in range(GDN_HEADS) for cc in range(cp)]
    qs, ks, vs, betas, gcols, grows, glasts = [], [], [], [], [], [], []
    for h in range(GDN_HEADS):
        qa = a[:, h * GDN_DK:(h + 1) * GDN_DK]
        ka = a[:, (GDN_HEADS + h) * GDN_DK:(GDN_HEADS + h + 1) * GDN_DK]
        va = a[:, (2 * GDN_HEADS + h) * GDN_DK:(2 * GDN_HEADS + h + 1) * GDN_DK]
        qa = qa * lax.rsqrt(jnp.sum(qa * qa, -1, keepdims=True) + NORM_EPS) * (GDN_DK ** -0.5)
        ka = ka * lax.rsqrt(jnp.sum(ka * ka, -1, keepdims=True) + NORM_EPS)
        for cc in range(cp):
            rows = slice(cc * lc, (cc + 1) * lc)
            qs.append(qa[rows])
            ks.append(ka[rows])
            vs.append(va[rows])
            betas.append(beta_all[rows, h:h + 1])
            gcols.append(G[rows, GDN_HEADS + h:GDN_HEADS + h + 1])
            grows.append(GT[GDN_HEADS + h:GDN_HEADS + h + 1, cc * lc:(cc + 1) * lc])
            glasts.append(GT[GDN_HEADS + h:GDN_HEADS + h + 1, (cc + 1) * lc - 1:(cc + 1) * lc])
    q3, k3, v3 = jnp.stack(qs), jnp.stack(ks), jnp.stack(vs)
    beta3, gcol3 = jnp.stack(betas), jnp.stack(gcols)
    grow3, glast3 = jnp.stack(grows), jnp.stack(glasts)
    shape3 = (len(pairs), lc, lc)
    ri = lax.broadcasted_iota(jnp.int32, shape3, 1)
    ci = lax.broadcasted_iota(jnp.int32, shape3, 2)
    incl = ri >= ci
    dec3 = jnp.where(incl, jnp.exp(jnp.where(incl, gcol3 - grow3, 0.0)), 0.0)
    eg3 = jnp.exp(gcol3)
    kb3 = k3 * beta3
    kbf3 = k3.astype(BF16)
    nmat3 = jnp.where(ri > ci, _bdot_nt(kb3.astype(BF16), kbf3) * dec3, 0.0)
    inv3 = _unit_lower_inverse(nmat3)
    sol3 = _bdot_bf16(inv3, jnp.concatenate([v3 * beta3, kb3 * eg3], axis=-1))
    w3 = sol3[:, :, GDN_DV:].astype(BF16)
    qd3 = (q3 * eg3).astype(BF16)
    kd3 = (k3 * jnp.exp(glast3 - gcol3)).astype(BF16)
    attn3 = (_bdot_nt(q3.astype(BF16), kbf3) * dec3).astype(BF16)
    for i, (h, cc) in enumerate(pairs):
        rows = slice(cc * lc, (cc + 1) * lc)
        cols = slice(h * GDN_DV, (h + 1) * GDN_DV)
        u0_ref[0, rows, cols] = sol3[i, :, :GDN_DV]
        w_ref[0, rows, cols] = w3[i]
        qd_ref[0, rows, cols] = qd3[i]
        kd_ref[0, rows, cols] = kd3[i]
        attn_ref[0, rows, h * lc:(h + 1) * lc] = attn3[i]


def _gdn_seq_kernel(u0_ref, w_ref, qd_ref, kd_ref, attn_ref, g_ref, z_ref, s0_ref, nw_ref,
                    y_ref, sout_ref, S, *, lc, nbb):
    c = pl.program_id(1)

    @pl.when(c == 0)
    def _():
        S[...] = s0_ref[...]

    nw = nw_ref[...]
    pairs = [(bb, h) for bb in range(nbb) for h in range(GDN_HEADS)]
    hcols = lambda h: slice(h * GDN_DV, (h + 1) * GDN_DV)
    stack = lambda f: jnp.stack([f(bb, h) for bb, h in pairs])
    dlast = jnp.exp(g_ref[:, lc - 1:lc, :])
    S3 = S[...].reshape(len(pairs), GDN_DK, GDN_DV)
    wq3 = stack(lambda bb, h: jnp.concatenate([w_ref[bb, :, hcols(h)], qd_ref[bb, :, hcols(h)]], axis=0))
    r3 = _bdot(wq3, S3.astype(BF16))
    ub3 = (stack(lambda bb, h: u0_ref[bb, :, hcols(h)]) - r3[:, :lc]).astype(BF16)
    o3 = r3[:, lc:] + _bdot(stack(lambda bb, h: attn_ref[bb, :, h * lc:(h + 1) * lc]), ub3)
    d3 = stack(lambda bb, h: dlast[bb, :, GDN_HEADS + h:GDN_HEADS + h + 1])
    kd3 = stack(lambda bb, h: kd_ref[bb, :, hcols(h)])
    kdu3 = lax.dot_general(kd3, ub3, (((1,), (1,)), ((0,), (0,))), preferred_element_type=F32)
    S[...] = (d3 * S3 + kdu3).reshape(S.shape)
    o3 = o3 * lax.rsqrt(jnp.mean(o3 * o3, -1, keepdims=True) + NORM_EPS) * nw
    for i, (bb, h) in enumerate(pairs):
        zh = z_ref[bb, :, hcols(h)]
        y_ref[bb, :, hcols(h)] = (o3[i] * (zh * _sigmoid(zh))).astype(y_ref.dtype)

    @pl.when(c == pl.num_programs(1) - 1)
    def _():
        sout_ref[...] = S[...]


def _gdn(qkv3, z3, ba3, ctx8, s0, cw, p1, p2, nw, lc, cp, nbb):
    nb, L, _ = qkv3.shape
    rb = lc * cp
    hd = GDN_HEADS * GDN_DV
    ltri = jnp.asarray(np.kron(np.eye(cp, dtype=np.float32), np.tril(np.ones((lc, lc), np.float32)))).astype(BF16)
    blk = lambda n: pl.BlockSpec((1, rb, n), lambda b, c: (b, c, 0))
    per_b = lambda shape: pl.BlockSpec((1,) + shape, lambda b, c: (b,) + (0,) * len(shape))
    cst = lambda shape: pl.BlockSpec(shape, lambda b, c: (0,) * len(shape))
    sds = lambda n, dt: jax.ShapeDtypeStruct((nb, L, n), dt)
    u0, w, qd, kd, attn, G, cout = pl.pallas_call(
        functools.partial(_gdn_local_kernel, lc=lc, cp=cp),
        grid=(nb, L // rb),
        in_specs=[blk(GDN_QKV), blk(LANES), per_b((SUBLANES, GDN_QKV)), cst(cw.shape), cst(p1.shape),
                  cst(p2.shape), cst(ltri.shape)],
        out_specs=[blk(hd), blk(hd), blk(hd), blk(hd), blk(GDN_HEADS * lc), blk(LANES),
                   per_b((SUBLANES, GDN_QKV))],
        out_shape=[sds(hd, F32), sds(hd, BF16), sds(hd, BF16), sds(hd, BF16), sds(GDN_HEADS * lc, BF16),
                   sds(LANES, F32), jax.ShapeDtypeStruct((nb, SUBLANES, GDN_QKV), F32)],
        scratch_shapes=[pltpu.VMEM((rb + SUBLANES, GDN_QKV), F32)],
        compiler_params=_params(("parallel", "arbitrary")),
        name="gdn_local",
    )(qkv3, ba3, ctx8, cw, p1, p2, ltri)
    sblk = lambda n: pl.BlockSpec((nbb, lc, n), lambda b, c: (b, c, 0))
    state = pl.BlockSpec((nbb, GDN_HEADS, GDN_DK, GDN_DV), lambda b, c: (b, 0, 0, 0))
    y, s_new = pl.pallas_call(
        functools.partial(_gdn_seq_kernel, lc=lc, nbb=nbb),
        grid=(nb // nbb, L // lc),
        in_specs=[sblk(hd), sblk(hd), sblk(hd), sblk(hd), sblk(GDN_HEADS * lc), sblk(LANES), sblk(hd),
                  state, cst(nw.shape)],
        out_specs=[sblk(hd), state],
        out_shape=[sds(hd, BF16), jax.ShapeDtypeStruct(s0.shape, F32)],
        scratch_shapes=[pltpu.VMEM((nbb, GDN_HEADS, GDN_DK, GDN_DV), F32)],
        compiler_params=_params(("parallel", "arbitrary")),
        name="gdn_seq",
    )(u0, w, qd, kd, attn, G, z3, s0, nw)
    return y, s_new, cout


def _ret_kernel(q_ref, k_ref, v_ref, g_ref, r0_ref, dec_ref, qs_ref, ks_ref, cd_ref, o_ref, rout_ref, R,
                *, nbb):
    c = pl.program_id(1)

    @pl.when(c == 0)
    def _():
        R[...] = r0_ref[...]

    pairs = [(bb, h) for bb in range(nbb) for h in range(RET_HEADS)]
    stack = lambda f: jnp.stack([f(bb, h) for bb, h in pairs])
    kcols = lambda h: slice(h * RET_DK, (h + 1) * RET_DK)
    vcols = lambda h: slice(h * RET_DV, (h + 1) * RET_DV)
    q3 = stack(lambda bb, h: q_ref[bb, :, kcols(h)])
    k3 = stack(lambda bb, h: k_ref[bb, :, kcols(h)])
    v3 = stack(lambda bb, h: v_ref[bb, :, vcols(h)])
    dec3 = stack(lambda bb, h: dec_ref[h])
    qs3 = stack(lambda bb, h: qs_ref[h])
    ks3 = stack(lambda bb, h: ks_ref[h])
    cd3 = stack(lambda bb, h: cd_ref[h])
    R3 = R[...].reshape(len(pairs), RET_DK, RET_DV)
    s3 = _bdot_nt(q3, k3) * dec3
    o3 = _bdot(s3.astype(BF16), v3) + _bdot(q3, R3.astype(BF16)) * qs3
    kv3 = lax.dot_general((k3.astype(F32) * ks3).astype(BF16), v3, (((1,), (1,)), ((0,), (0,))),
                          preferred_element_type=F32)
    R[...] = (cd3 * R3 + kv3).reshape(R.shape)
    mu = jnp.mean(o3, -1, keepdims=True)
    d3 = o3 - mu
    var = jnp.mean(d3 * d3, -1, keepdims=True)
    on3 = d3 * lax.rsqrt(var + LN_EPS)
    for i, (bb, h) in enumerate(pairs):
        gt = g_ref[bb, :, vcols(h)].astype(F32)
        o_ref[bb, :, vcols(h)] = (gt * _sigmoid(gt) * on3[i]).astype(o_ref.dtype)

    @pl.when(c == pl.num_programs(1) - 1)
    def _():
        rout_ref[...] = R[...]


def _retention(proj3, r0, chunk, nbb, nc):
    nb, L, _ = proj3.shape
    lc = chunk * nc
    log_g = np.log(1.0 - 2.0 ** (-5.0 - np.arange(RET_HEADS, dtype=np.float64)))
    idx = np.arange(lc, dtype=np.float64)
    dist = idx[:, None] - idx[None, :]
    which = (idx // chunk)[:, None] - (idx // chunk)[None, :]
    dec = np.where(which >= 0, np.exp(log_g[:, None, None] * np.abs(dist)), 0.0).astype(np.float32)
    qs = np.exp(log_g[:, None] * (idx + 1.0)).astype(np.float32)[..., None]
    ks = np.exp(log_g[:, None] * (lc - 1.0 - idx)).astype(np.float32)[..., None]
    cdec = np.exp(log_g * lc).astype(np.float32)[:, None, None]
    nqk = RET_HEADS * RET_DK
    nv = RET_HEADS * RET_DV
    cst = lambda shape: pl.BlockSpec(shape, lambda b, c: (0,) * len(shape))
    state = pl.BlockSpec((nbb, RET_HEADS, RET_DK, RET_DV), lambda b, c: (b, 0, 0, 0))
    return pl.pallas_call(
        functools.partial(_ret_kernel, nbb=nbb),
        grid=(nb // nbb, L // lc),
        in_specs=[pl.BlockSpec((nbb, lc, nqk), lambda b, c: (b, c, 0)),
                  pl.BlockSpec((nbb, lc, nqk), lambda b, c: (b, c, 1)),
                  pl.BlockSpec((nbb, lc, nv), lambda b, c: (b, c, 1)),
                  pl.BlockSpec((nbb, lc, nv), lambda b, c: (b, c, 2)),
                  state, cst(dec.shape), cst(qs.shape), cst(ks.shape), cst(cdec.shape)],
        out_specs=[pl.BlockSpec((nbb, lc, nv), lambda b, c: (b, c, 0)), state],
        out_shape=[jax.ShapeDtypeStruct((nb, L, nv), BF16), jax.ShapeDtypeStruct(r0.shape, F32)],
        scratch_shapes=[pltpu.VMEM((nbb, RET_HEADS, RET_DK, RET_DV), F32)],
        compiler_params=_params(("parallel", "arbitrary")),
        name="retention",
    )(proj3, proj3, proj3, proj3, r0, jnp.asarray(dec), jnp.asarray(qs), jnp.asarray(ks), jnp.asarray(cdec))


RUN_ALIGN = SUBLANES
RUN_PIECE = 64
MOE_VMEM_LIMIT = 58 * 1024 * 1024


def _moe_params(sem):
    return pltpu.CompilerParams(dimension_semantics=sem, vmem_limit_bytes=MOE_VMEM_LIMIT)


def _tile_cap(tm):
    rows = TOP_K * tm + N_EXPERTS * (RUN_ALIGN - 1)
    return -(-rows // LANES) * LANES


def _route_tile(x, rw_ref, rb_ref, lst_ref, ust_ref, pos_ref, gt_ref, cnt_ref, xs_ref):
    tm = x.shape[0]
    cap = xs_ref.shape[1]
    lane = lax.broadcasted_iota(jnp.int32, (tm, LANES), 1)
    lane_f = lane.astype(F32)
    xh, xl = _split_bf16(x)
    wh, wl = _split_bf16(rw_ref[...])
    logits = _dot(xh, wh) + _dot(xh, wl) + _dot(xl, wh) + rb_ref[...]
    logits = jnp.where(lane < N_EXPERTS, logits, -jnp.inf)
    vals, hots = [], []
    for _ in range(TOP_K):
        m = jnp.max(logits, -1, keepdims=True)
        first = jnp.min(jnp.where(logits == m, lane_f, float(LANES)), -1, keepdims=True)
        hot = lane_f == first
        vals.append(m)
        hots.append(hot)
        logits = jnp.where(hot, -jnp.inf, logits)
    es = [jnp.exp(v - vals[0]) for v in vals]
    den = es[0] + es[1] + es[2] + es[3]
    multi = jnp.zeros((tm, LANES), F32)
    for hot in hots:
        multi = multi + hot.astype(F32)
    counts = jnp.sum(multi, 0, keepdims=True)
    units = jnp.floor((counts + (RUN_ALIGN - 1)) * (1.0 / RUN_ALIGN))
    offs = _dot(jnp.broadcast_to(units, (SUBLANES, LANES)).astype(BF16), ust_ref[...])[0:1] * float(RUN_ALIGN)
    before = _dot(lst_ref[...], multi.astype(BF16))
    slot = offs + before
    pos = jnp.zeros((tm, LANES), F32)
    gt = jnp.zeros((tm, LANES), F32)
    for kk in range(TOP_K):
        pos = jnp.where(lane == kk, jnp.sum(jnp.where(hots[kk], slot, 0.0), -1, keepdims=True), pos)
        gt = jnp.where(lane == kk, es[kk] / den, gt)
    pos = pos.astype(jnp.int32)
    pos_ref[...] = pos
    gt_ref[...] = gt
    cnt_ref[0] = counts
    pos_t = pos.T[0:2 * SUBLANES].astype(jnp.int16)
    row = lax.broadcasted_iota(jnp.int16, (cap, tm), 0)
    sel = jnp.zeros((cap, tm), BF16)
    for kk in range(TOP_K):
        sel = sel + jnp.where(row == pos_t[kk:kk + 1, :], jnp.ones((), BF16), jnp.zeros((), BF16))
    xs_ref[0] = _dot(sel, xh)


def _outproj_route_kernel(*refs, n_in):
    a_refs = refs[:n_in]
    w_refs = refs[n_in:2 * n_in]
    x_ref, g_ref, b_ref = refs[2 * n_in:2 * n_in + 3]
    route_in = refs[2 * n_in + 3:2 * n_in + 7]
    o_ref = refs[2 * n_in + 7]
    route_out = refs[2 * n_in + 8:]
    acc = _dot(a_refs[0][...], w_refs[0][...])
    for a_ref, w_ref in zip(a_refs[1:], w_refs[1:]):
        acc = acc + _dot(a_ref[...], w_ref[...])
    x1 = _layer_norm(DEEPNORM_ALPHA * x_ref[...] + acc, g_ref[...], b_ref[...])
    o_ref[...] = x1
    _route_tile(x1, *route_in, *route_out)


def _outproj_route(acts, ws, x, g, b, rw, rb, tm):
    t = x.shape[0]
    nt = t // tm
    cap = _tile_cap(tm)
    lst = jnp.asarray(np.tril(np.ones((tm, tm), np.float32), -1)).astype(BF16)
    ust = jnp.asarray(np.triu(np.ones((LANES, LANES), np.float32), 1)).astype(BF16)
    row = lambda n: pl.BlockSpec((tm, n), lambda i: (i, 0))
    x1, pos, gt, cnt, xs = pl.pallas_call(
        functools.partial(_outproj_route_kernel, n_in=len(acts)),
        grid=(nt,),
        in_specs=[row(a.shape[1]) for a in acts] + [_full(w.shape) for w in ws]
                 + [row(D_MODEL), _full((1, D_MODEL)), _full((1, D_MODEL)),
                    _full(rw.shape), _full(rb.shape), _full(lst.shape), _full(ust.shape)],
        out_specs=[row(D_MODEL), row(LANES), row(LANES), pl.BlockSpec((1, 1, LANES), lambda i: (i, 0, 0)),
                   pl.BlockSpec((1, cap, D_MODEL), lambda i: (i, 0, 0))],
        out_shape=[jax.ShapeDtypeStruct((t, D_MODEL), F32),
                   jax.ShapeDtypeStruct((t, LANES), jnp.int32), jax.ShapeDtypeStruct((t, LANES), F32),
                   jax.ShapeDtypeStruct((nt, 1, LANES), F32), jax.ShapeDtypeStruct((nt, cap, D_MODEL), F32)],
        compiler_params=_moe_params(("parallel",)),
        name="outproj_route",
    )(*acts, *ws, x, g, b, rw, rb, lst, ust)
    return x1, (pos, gt, cnt, xs)


def _expert_kernel(be_ref, nu_ref, ilo_ref, rows_ref, gs_ref, n8_ref, lo_ref, *refs, blk, tiles):
    ng = len(tiles)
    xs_hbms = refs[:ng]
    w1_ref, b1_ref, w2_ref, b2_ref = refs[ng:ng + 4]
    ys_hbms = refs[ng + 4:2 * ng + 4]
    xbuf, ybuf, w1b, w2b, in_sem, out_sem = refs[2 * ng + 4:]
    nt = sum(tiles)
    firsts = [sum(tiles[:g]) for g in range(ng)]
    j = pl.program_id(0)
    nu = nu_ref[0]

    def for_each_run(jb, fn):
        e = be_ref[jb]
        base = jb * blk
        for g in range(ng):
            end = firsts[g] + tiles[g]

            def cond(i, end=end):
                return (i < end) & (gs_ref[e * nt + jnp.minimum(i, nt - 1)] < base + blk)

            def body(i, g=g):
                g0 = gs_ref[e * nt + i]
                first = jnp.maximum(g0, base)
                last = jnp.minimum(g0 + n8_ref[e * nt + i], base + blk)
                fn(g, i - firsts[g], lo_ref[e * nt + i] + (first - g0), first - base, last - first)
                return i + 1

            lax.while_loop(cond, body, jnp.clip(ilo_ref[jb], firsts[g], end))

    def pieces(length, fn):
        def digits(sizes):
            for size in sizes:
                @pl.when((length & size) != 0)
                def _(size=size):
                    fn(length & ~(2 * size - 1), size)

        sizes = [blk >> k for k in range(blk.bit_length()) if blk >> k >= RUN_ALIGN]

        @pl.when(length > 2 * RUN_PIECE - 1)
        def _():
            digits([z for z in sizes if z > RUN_PIECE])

        digits([z for z in sizes if z <= RUN_PIECE])

    def aligned(v, size):
        return pl.ds(pl.multiple_of(v, RUN_ALIGN), size)

    def copy_in(jb, slot):
        def run(g, ig, src, dst, length):
            def piece(off, size):
                pltpu.make_async_copy(xs_hbms[g].at[ig, aligned(src + off, size)],
                                      xbuf.at[slot, aligned(dst + off, size)], in_sem.at[slot]).start()
            pieces(length, piece)
        for_each_run(jb, run)

    def copy_out(jb, slot):
        def run(g, ig, src, dst, length):
            def piece(off, size):
                pltpu.make_async_copy(ybuf.at[slot, aligned(dst + off, size)],
                                      ys_hbms[g].at[ig, aligned(src + off, size)], out_sem.at[slot]).start()
            pieces(length, piece)
        for_each_run(jb, run)

    def wait_rows(sem, nrows):
        size = blk
        while size >= RUN_ALIGN:
            @pl.when((nrows & size) != 0)
            def _(size=size):
                pltpu.make_async_copy(xbuf.at[1, pl.ds(0, size)], xbuf.at[0, pl.ds(0, size)], sem).wait()
            size //= 2

    @pl.when(j < nu)
    def _():
        slot = j % 2

        @pl.when(j == 0)
        def _():
            xbuf[...] = jnp.zeros_like(xbuf)
            copy_in(0, 0)

        wait_rows(in_sem.at[slot], rows_ref[j])

        @pl.when(j + 1 < nu)
        def _():
            copy_in(j + 1, 1 - slot)

        @pl.when(j >= 2)
        def _():
            wait_rows(out_sem.at[slot], rows_ref[jnp.maximum(j - 2, 0)])

        @pl.when((j == 0) | (be_ref[j] != be_ref[jnp.maximum(j - 1, 0)]))
        def _():
            w1b[...] = w1_ref[0, 0].astype(BF16)
            w2b[...] = w2_ref[0, 0].astype(BF16)

        h = _dot(xbuf[slot].astype(BF16), w1b[...]) + b1_ref[0]
        glu = jnp.minimum(h[:, :D_FF], SWIGLU_LIMIT)
        lin = jnp.clip(h[:, D_FF:], -SWIGLU_LIMIT, SWIGLU_LIMIT)
        act = glu * _sigmoid(SWIGLU_ALPHA * glu) * (lin + 1.0)
        ybuf[slot] = _dot(act.astype(BF16), w2b[...]) + b2_ref[0]
        copy_out(j, slot)

        @pl.when(j == nu - 1)
        def _():
            wait_rows(out_sem.at[slot], rows_ref[j])

            @pl.when(j >= 1)
            def _():
                wait_rows(out_sem.at[1 - slot], rows_ref[jnp.maximum(j - 1, 0)])


def _experts(tables, xs_list, w1, b1, w2, b2, layer, blk):
    ng = len(xs_list)
    n_blk = tables[0].shape[0]
    wspec = lambda shape: pl.BlockSpec((1, 1) + shape, lambda j, be, *_: (layer, be[j], 0, 0))
    bspec = lambda n: pl.BlockSpec((1, 1, n), lambda j, be, *_: (be[j], 0, 0))
    hbm = pl.BlockSpec(memory_space=pl.ANY)
    grid_spec = pltpu.PrefetchScalarGridSpec(
        num_scalar_prefetch=len(tables),
        grid=(n_blk,),
        in_specs=[hbm] * ng + [wspec((D_MODEL, 2 * D_FF)), bspec(2 * D_FF), wspec((D_FF, D_MODEL)), bspec(D_MODEL)],
        out_specs=[hbm] * ng,
        scratch_shapes=[pltpu.VMEM((2, blk, D_MODEL), F32), pltpu.VMEM((2, blk, D_MODEL), F32),
                        pltpu.VMEM((D_MODEL, 2 * D_FF), BF16), pltpu.VMEM((D_FF, D_MODEL), BF16),
                        pltpu.SemaphoreType.DMA((2,)), pltpu.SemaphoreType.DMA((2,))],
    )
    return pl.pallas_call(
        functools.partial(_expert_kernel, blk=blk, tiles=tuple(xs.shape[0] for xs in xs_list)),
        grid_spec=grid_spec,
        out_shape=[jax.ShapeDtypeStruct(xs.shape, F32) for xs in xs_list],
        input_output_aliases={len(tables) + g: g for g in range(ng)},
        compiler_params=_moe_params(("arbitrary",)),
        name="moe_experts",
    )(*tables, *xs_list, w1, b1, w2, b2)


def _combine_kernel(ys_ref, pos_ref, gt_ref, x_ref, p_ref, g_ref, b_ref, plew_ref, gatew_ref, o_ref):
    tm = x_ref.shape[0]
    cap = ys_ref.shape[1]
    pos = pos_ref[...].astype(jnp.int16)
    gt = gt_ref[...].astype(BF16)
    col = lax.broadcasted_iota(jnp.int16, (tm, cap), 1)
    sel = jnp.zeros((tm, cap), BF16)
    for kk in range(TOP_K):
        sel = sel + jnp.where(col == pos[:, kk:kk + 1], gt[:, kk:kk + 1], jnp.zeros((), BF16))
    y = _dot(sel, ys_ref[0].astype(BF16))
    x2 = _layer_norm(DEEPNORM_ALPHA * x_ref[...] + y, g_ref[...], b_ref[...])
    pp = _dot(p_ref[...].astype(BF16), plew_ref[...])
    gg = _sigmoid(_dot(x2.astype(BF16), gatew_ref[...]))
    o_ref[...] = x2 + pp * gg


def _combine(ys, pos, gt, x, p, layer, g, b, plew, gatew, tm):
    t = x.shape[0]
    cap = ys.shape[1]
    row = lambda n: pl.BlockSpec((tm, n), lambda i: (i, 0))
    return pl.pallas_call(
        _combine_kernel,
        grid=(t // tm,),
        in_specs=[pl.BlockSpec((1, cap, D_MODEL), lambda i: (i, 0, 0)), row(LANES), row(LANES),
                  row(D_MODEL), pl.BlockSpec((None, tm, PLE_DIM), lambda i: (layer, i, 0)),
                  _full((1, D_MODEL)), _full((1, D_MODEL)),
                  _full(plew.shape), _full(gatew.shape)],
        out_specs=row(D_MODEL),
        out_shape=jax.ShapeDtypeStruct((t, D_MODEL), F32),
        compiler_params=_moe_params(("parallel",)),
        name="moe_combine",
    )(ys, pos, gt, x, p, g, b, plew, gatew)


def _moe_tables(cnt, t, blk):
    nt = cnt.shape[0]
    n = cnt[:, 0, :N_EXPERTS].astype(jnp.int32)
    n8 = (n + RUN_ALIGN - 1) // RUN_ALIGN * RUN_ALIGN
    lo = jnp.cumsum(n8, axis=1) - n8
    rows_e = jnp.sum(n8, axis=0)
    padded = (rows_e + blk - 1) // blk * blk
    pend = jnp.cumsum(padded)
    pstart = pend - padded
    gstart = pstart[None, :] + jnp.cumsum(n8, axis=0) - n8
    n_blk = -(-(TOP_K * t + nt * N_EXPERTS * (RUN_ALIGN - 1) + N_EXPERTS * (blk - 1)) // blk)
    n_used = (pend[-1] // blk).astype(jnp.int32).reshape(1)
    blk_start = jnp.arange(n_blk, dtype=jnp.int32) * blk
    blk_e = jnp.minimum(jnp.sum((pend[None, :] <= blk_start[:, None]).astype(jnp.int32), axis=1), N_EXPERTS - 1)
    run_end = (gstart + n8)[:, blk_e]
    ilo = jnp.sum((run_end <= blk_start[None, :]).astype(jnp.int32), axis=0)
    rows_b = jnp.clip((pstart + rows_e)[blk_e] - blk_start, 0, blk)
    flat = lambda a: a.T.reshape(-1).astype(jnp.int32)
    i32 = lambda a: a.astype(jnp.int32)
    return (i32(blk_e), n_used, i32(ilo), i32(rows_b), flat(gstart), flat(n8), flat(lo))


def _moe_ple(xs_in, routed, ps, w1, b1, w2, b2, layer, g, b, plew, gatew, tms, blk):
    cnt = jnp.concatenate([r[2] for r in routed], axis=0)
    tables = _moe_tables(cnt, sum(x.shape[0] for x in xs_in), blk)
    ys = _experts(tables, [r[3] for r in routed], w1, b1, w2, b2, layer, blk)
    return [_combine(y, r[0], r[1], x, p, layer, g, b, plew, gatew, tm)
            for y, r, x, p, tm in zip(ys, routed, xs_in, ps, tms)]


_ROT_PERM = np.concatenate([np.arange(0, RET_DK, 2), np.arange(1, RET_DK, 2)])


def _lane_row(vals, offset):
    row = jnp.zeros((1, LANES), F32)
    return row.at[0, offset:offset + vals.shape[0]].set(vals)


def _even_mixer(x, nb, L, s5_re, s5_im, gdn_s, conv_s, W, tm):
    t = nb * L
    lc = L if L <= CHUNK else CHUNK
    u, qkv, z, ba = _proj_even(x, W['wu'], W['wqkv'], W['wz'], W['wba'], tm)
    yA3, h_new = _s5(u.reshape(nb, L, S5_WIDTH), _s5_state_in(s5_re[0].astype(F32), s5_im[0].astype(F32)),
                     W['bmat'], W['cmat'], W['acoef'], W['dskip'], W['wglu'], W['bglu'], lc)
    new_re, new_im = _s5_state_out(h_new)
    ctx8 = jnp.concatenate([jnp.zeros((nb, SUBLANES - (GDN_CONV - 1), GDN_QKV), F32), conv_s[0].astype(F32)], axis=1)
    yB3, new_gdn, cout = _gdn(qkv.reshape(nb, L, GDN_QKV), z.reshape(nb, L, -1), ba.reshape(nb, L, LANES),
                              ctx8, gdn_s[0].astype(F32), W['convw'], W['p1'], W['p2'], W['normw'], lc,
                              cp=min(4, L // lc), nbb=4)
    new_conv = cout[:, SUBLANES - (GDN_CONV - 1):, :]
    x, routed = _outproj_route([yA3.reshape(t, -1), yB3.reshape(t, -1)], [W['wout_a'], W['wout_b']], x,
                               W['ln1_g'][0], W['ln1_b'][0], W['rw'][0], W['rb'][0], tm)
    return x, routed, new_re[None], new_im[None], new_gdn[None], new_conv[None]


def _odd_mixer(x, nb, L, ret_s, pos0, W, tm):
    t = nb * L
    lc = L if L <= CHUNK else CHUNK
    pos = pos0 + jnp.arange(L, dtype=F32)
    freq = 1.0 / (ROPE_BASE ** jnp.linspace(0.0, 1.0, RET_DK // 2, dtype=F32))
    ang = pos[:, None] * freq[None]
    reps = max(1, tm // L)
    cos = jnp.tile(jnp.cos(ang), (reps, 1))
    sin = jnp.tile(jnp.sin(ang), (reps, 1))
    proj = _proj_odd(x, W['win_odd'], cos, sin, tm)
    half = RET_DK // 2
    r0 = ret_s[0].astype(F32).reshape(nb, RET_HEADS, half, 2, RET_DV).swapaxes(2, 3).reshape(
        nb, RET_HEADS, RET_DK, RET_DV)
    o3, r_new = _retention(proj.reshape(nb, L, -1), r0, lc, nbb=2, nc=min(4, L // lc))
    new_ret = r_new.reshape(nb, RET_HEADS, 2, half, RET_DV).swapaxes(2, 3).reshape(nb, RET_HEADS, RET_DK, RET_DV)
    x, routed = _outproj_route([o3.reshape(t, -1)], [W['wout_odd']], x, W['ln1_g'][1], W['ln1_b'][1],
                               W['rw'][1], W['rb'][1], tm)
    return x, routed, new_ret[None]


def kernel(x_prompt, x_sample, state_s5_re, state_s5_im, state_gdn, state_gdn_conv, state_ret, p_prompt, p_sample, w_in_even, s5_a_re, s5_a_im, s5_log_dt, s5_b_re, s5_b_im, s5_c_re, s5_c_im, s5_d, s5_w_glu, s5_b_glu, gdn_conv_w, gdn_a_log, gdn_dt_bias, gdn_norm_w, w_out_even, w_in_odd, w_out_odd, ln1_g, ln1_b, ln2_g, ln2_b, router_w, router_b, moe_w1, moe_b1, moe_w2, moe_b2, ple_w, ple_gate_w):
    o1 = S5_WIDTH
    o2 = o1 + GDN_QKV
    o3 = o2 + GDN_HEADS * GDN_DV
    win = w_in_even[0]
    bmat, cmat, acoef = _s5_weights(s5_a_re[0], s5_a_im[0], s5_log_dt[0], s5_b_re[0], s5_b_im[0],
                                    s5_c_re[0], s5_c_im[0])
    wodd = w_in_odd[0]
    nk = RET_HEADS * RET_DK
    perm_cols = lambda w: w.reshape(D_MODEL, RET_HEADS, RET_DK)[:, :, _ROT_PERM].reshape(D_MODEL, nk)
    W = dict(
        wu=win[:, :o1].astype(BF16), wqkv=win[:, o1:o2].astype(BF16), wz=win[:, o2:o3].astype(BF16),
        wba=jnp.pad(win[:, o3:], ((0, 0), (0, LANES - 2 * GDN_HEADS))).astype(BF16),
        bmat=bmat, cmat=cmat, acoef=acoef, dskip=s5_d[0][None], wglu=s5_w_glu[0].astype(BF16),
        bglu=s5_b_glu[0][None], convw=gdn_conv_w[0],
        p1=_lane_row(-jnp.exp(gdn_a_log[0]), GDN_HEADS), p2=_lane_row(gdn_dt_bias[0], GDN_HEADS),
        normw=gdn_norm_w[0][None],
        wout_a=w_out_even[0][:S5_WIDTH].astype(BF16), wout_b=w_out_even[0][S5_WIDTH:].astype(BF16),
        win_odd=jnp.concatenate([perm_cols(wodd[:, :nk]), perm_cols(wodd[:, nk:2 * nk]), wodd[:, 2 * nk:]],
                                axis=1).astype(BF16),
        wout_odd=w_out_odd[0].astype(BF16),
        ln1_g=ln1_g[:, None], ln1_b=ln1_b[:, None], ln2_g=ln2_g[:, None], ln2_b=ln2_b[:, None],
        rw=jnp.pad(router_w, ((0, 0), (0, 0), (0, LANES - N_EXPERTS))),
        rb=jnp.pad(router_b, ((0, 0), (0, LANES - N_EXPERTS)))[:, None],
        w1=moe_w1, b1=moe_b1[:, :, None], w2=moe_w2, b2=moe_b2[:, :, None],
        plew=ple_w.astype(BF16), gatew=ple_gate_w.astype(BF16),
    )
    bp, lp, _ = x_prompt.shape
    bs, ls, _ = x_sample.shape
    zeros = lambda *s: jnp.zeros(s, F32)
    shapes = [(bp, lp), (bs, ls)]
    tms = [512, 128]
    ps = [p_prompt, p_sample]
    xs = [x_prompt.reshape(bp * lp, D_MODEL).astype(F32), x_sample.reshape(bs * ls, D_MODEL).astype(F32)]
    even_states = [(zeros(1, bp, S5_GROUPS, S5_STATE), zeros(1, bp, S5_GROUPS, S5_STATE),
                    zeros(1, bp, GDN_HEADS, GDN_DK, GDN_DV), zeros(1, bp, GDN_CONV - 1, GDN_QKV)),
                   (state_s5_re, state_s5_im, state_gdn, state_gdn_conv)]
    ret_states = [zeros(1, bp, RET_HEADS, RET_DK, RET_DV), state_ret]
    pos0s = [0.0, float(PAST_LEN)]

    def moe(mixed, layer):
        return _moe_ple([m[0] for m in mixed], [m[1] for m in mixed], [p.reshape(DEPTH, -1, PLE_DIM) for p in ps],
                        W['w1'], W['b1'][layer], W['w2'], W['b2'][layer], layer, W['ln2_g'][layer],
                        W['ln2_b'][layer], W['plew'][layer], W['gatew'][layer], tms, blk=512)

    even = [_even_mixer(x, nb, L, *st, W, tm) for x, (nb, L), st, tm in zip(xs, shapes, even_states, tms)]
    xs = moe(even, 0)
    odd = [_odd_mixer(x, nb, L, st, pos0, W, tm)
           for x, (nb, L), st, pos0, tm in zip(xs, shapes, ret_states, pos0s, tms)]
    xs = moe(odd, 1)
    dp = x_prompt.dtype
    y_p, y_s = xs[0].reshape(bp, lp, D_MODEL), xs[1].reshape(bs, ls, D_MODEL)
    (_, _, p_re, p_im, p_gdn, p_conv), (_, _, s_re, s_im, s_gdn, s_conv) = even
    p_ret, s_ret = odd[0][2], odd[1][2]
    return (y_p.astype(dp), y_s.astype(x_sample.dtype),
            p_re.astype(dp), p_im.astype(dp), p_gdn.astype(dp), p_conv.astype(dp), p_ret.astype(dp),
            s_re.astype(state_s5_re.dtype), s_im.astype(state_s5_im.dtype), s_gdn.astype(state_gdn.dtype),
            s_conv.astype(state_gdn_conv.dtype), s_ret.astype(state_ret.dtype))
```

```python
import functools

import jax
import jax.numpy as jnp
import numpy as np
from jax import lax
from jax.experimental import pallas as pl
from jax.experimental.pallas import tpu as pltpu

F32 = jnp.float32
BF16 = jnp.bfloat16

D_MODEL = 1024
CHUNK = 64
S5_WIDTH = 512
S5_GROUP = 16
S5_GROUPS = 32
S5_STATE = 64
S5_TILES = 16
GDN_HEADS = 4
GDN_DK = 128
GDN_DV = 128
GDN_CONV = 4
GDN_QKV = 1536
RET_HEADS = 4
RET_DK = 256
RET_DV = 512
ROPE_BASE = 10000.0
N_EXPERTS = 32
TOP_K = 4
D_FF = 1024
SWIGLU_LIMIT = 7.0
SWIGLU_ALPHA = 1.702
PLE_DIM = 256
DEPTH = 2
PAST_LEN = 1024
DEEPNORM_ALPHA = (2 * DEPTH) ** 0.25
LN_EPS = 1e-5
NORM_EPS = 1e-6

LANES = 128
SUBLANES = 8
VMEM_LIMIT = 48 * 1024 * 1024

PROMPT_TM = 512
SAMPLE_TM = 128
EXPERT_BLK = 512


def _params(sem):
    return pltpu.CompilerParams(dimension_semantics=sem, vmem_limit_bytes=VMEM_LIMIT)


def _dot(a, b):
    return jnp.dot(a, b, preferred_element_type=F32)


def _sigmoid(x):
    return 1.0 / (1.0 + jnp.exp(-x))


def _full(shape):
    nd = len(shape)
    return pl.BlockSpec(shape, lambda *_: (0,) * nd)


def _proj_even_kernel(x_ref, wu_ref, wqkv_ref, wz_ref, wba_ref, u_ref, qkv_ref, z_ref, ba_ref):
    xb = x_ref[...].astype(BF16)
    u_ref[...] = _dot(xb, wu_ref[...])
    qkv_ref[...] = _dot(xb, wqkv_ref[...])
    z_ref[...] = _dot(xb, wz_ref[...])
    ba_ref[...] = _dot(xb, wba_ref[...])


def _proj_even(x, wu, wqkv, wz, wba, tm):
    t = x.shape[0]
    row = lambda n: pl.BlockSpec((tm, n), lambda i: (i, 0))
    return pl.pallas_call(
        _proj_even_kernel,
        grid=(t // tm,),
        in_specs=[row(D_MODEL), _full(wu.shape), _full(wqkv.shape), _full(wz.shape), _full(wba.shape)],
        out_specs=[row(S5_WIDTH), row(GDN_QKV), row(GDN_HEADS * GDN_DV), row(LANES)],
        out_shape=[jax.ShapeDtypeStruct((t, S5_WIDTH), F32), jax.ShapeDtypeStruct((t, GDN_QKV), F32),
                   jax.ShapeDtypeStruct((t, GDN_HEADS * GDN_DV), F32), jax.ShapeDtypeStruct((t, LANES), F32)],
        compiler_params=_params(("parallel",)),
        name="proj_even",
    )(x, wu, wqkv, wz, wba)


def _proj_odd_kernel(x_ref, w_ref, cos_ref, sin_ref, o_ref):
    j = pl.program_id(0)
    acc = _dot(x_ref[...].astype(BF16), w_ref[...])

    @pl.when(j == 0)
    def _():
        cos, sin = cos_ref[...], sin_ref[...]
        half = RET_DK // 2
        for h in range(2 * RET_HEADS):
            x0 = acc[:, h * RET_DK:h * RET_DK + half]
            x1 = acc[:, h * RET_DK + half:(h + 1) * RET_DK]
            scale = 1.0 if h < RET_HEADS else RET_DK ** -0.5
            o_ref[:, h * RET_DK:h * RET_DK + half] = ((x0 * cos - x1 * sin) * scale).astype(o_ref.dtype)
            o_ref[:, h * RET_DK + half:(h + 1) * RET_DK] = ((x0 * sin + x1 * cos) * scale).astype(o_ref.dtype)

    @pl.when(j != 0)
    def _():
        o_ref[...] = acc.astype(o_ref.dtype)


def _proj_odd(x, w, cos, sin, tm):
    t = x.shape[0]
    nblk = w.shape[1] // 2048
    period = cos.shape[0] // tm
    return pl.pallas_call(
        _proj_odd_kernel,
        grid=(nblk, t // tm),
        in_specs=[pl.BlockSpec((tm, D_MODEL), lambda j, i: (i, 0)),
                  pl.BlockSpec((D_MODEL, 2048), lambda j, i: (0, j)),
                  pl.BlockSpec((tm, LANES), lambda j, i: (i % period, 0)),
                  pl.BlockSpec((tm, LANES), lambda j, i: (i % period, 0))],
        out_specs=pl.BlockSpec((tm, 2048), lambda j, i: (i, j)),
        out_shape=jax.ShapeDtypeStruct((t, w.shape[1]), BF16),
        compiler_params=_params(("parallel", "parallel")),
        name="proj_odd",
    )(x, w, cos, sin)


def _layer_norm(r, g, b):
    mu = jnp.mean(r, -1, keepdims=True)
    d = r - mu
    var = jnp.mean(d * d, -1, keepdims=True)
    return d * lax.rsqrt(var + LN_EPS) * g + b


S5_LANE_BLOCKS = S5_WIDTH // LANES
S5_TILES_PER_BLOCK = S5_TILES // S5_LANE_BLOCKS


def _s5_kernel(u_ref, h0_ref, bmat_ref, cmat_ref, acoef_ref, dskip_ref, wglu_ref, bglu_ref,
               y_ref, hout_ref, utm, sre, sim, ytm, hst, *, nb, lt):
    tb = pl.program_id(0)

    @pl.when(tb == 0)
    def _():
        hst[...] = h0_ref[...]

    for b in range(nb):
        for q in range(S5_LANE_BLOCKS):
            utm[q, pl.ds(b, lt, stride=nb), :] = u_ref[b, :, q * LANES:(q + 1) * LANES]

    def block_body(q, carry):
        bu = _dot(utm[q].astype(BF16), bmat_ref[q])
        for g in range(S5_TILES_PER_BLOCK):
            sre[g] = bu[:, (2 * g) * LANES:(2 * g + 1) * LANES]
            sim[g] = bu[:, (2 * g + 1) * LANES:(2 * g + 2) * LANES]
        js = [q * S5_TILES_PER_BLOCK + g for g in range(S5_TILES_PER_BLOCK)]
        acs = [acoef_ref[j] for j in js]
        hs = [(hst[j, :, 0:LANES], hst[j, :, LANES:2 * LANES]) for j in js]
        for t in range(lt):
            rows = slice(t * nb, (t + 1) * nb)
            for g in range(S5_TILES_PER_BLOCK):
                ar, ai = acs[g][:, :LANES], acs[g][:, LANES:]
                hr, hi = hs[g]
                nhr = ar * hr - ai * hi + sre[g, rows, :]
                nhi = ar * hi + ai * hr + sim[g, rows, :]
                sre[g, rows, :] = nhr
                sim[g, rows, :] = nhi
                hs[g] = (nhr, nhi)
        for g, j in enumerate(js):
            hst[j, :, 0:LANES] = hs[g][0]
            hst[j, :, LANES:2 * LANES] = hs[g][1]
        st = jnp.concatenate([part for g in range(S5_TILES_PER_BLOCK) for part in (sre[g], sim[g])], axis=-1)
        ytm[q] = _dot(st.astype(BF16), cmat_ref[q])
        return carry

    lax.fori_loop(0, S5_LANE_BLOCKS, block_body, 0)
    y = (jnp.concatenate([ytm[q] for q in range(S5_LANE_BLOCKS)], axis=-1)
         + dskip_ref[...] * jnp.concatenate([utm[q] for q in range(S5_LANE_BLOCKS)], axis=-1))
    y = jax.nn.gelu(y)
    y = y * _sigmoid(_dot(y.astype(BF16), wglu_ref[...]) + bglu_ref[...])
    for q in range(S5_LANE_BLOCKS):
        ytm[q] = y[:, q * LANES:(q + 1) * LANES]
    for b in range(nb):
        for q in range(S5_LANE_BLOCKS):
            y_ref[b, :, q * LANES:(q + 1) * LANES] = ytm[q, pl.ds(b, lt, stride=nb), :].astype(y_ref.dtype)
    hout_ref[...] = hst[...]


def _s5(u3, h0, bmat, cmat, acoef, dskip, wglu, bglu, lt):
    nb, L, _ = u3.shape
    rows = nb * lt
    tpb = S5_TILES_PER_BLOCK
    bmat = bmat.reshape(S5_LANE_BLOCKS, tpb, LANES, 2 * LANES).transpose(0, 2, 1, 3).reshape(
        S5_LANE_BLOCKS, LANES, tpb * 2 * LANES)
    cmat = cmat.reshape(S5_LANE_BLOCKS, tpb * 2 * LANES, LANES)
    acoef = jnp.broadcast_to(acoef[:, :1], (S5_TILES, nb, 2 * LANES))
    return pl.pallas_call(
        functools.partial(_s5_kernel, nb=nb, lt=lt),
        grid=(L // lt,),
        in_specs=[pl.BlockSpec((nb, lt, S5_WIDTH), lambda i: (0, i, 0)),
                  _full(h0.shape), _full(bmat.shape), _full(cmat.shape), _full(acoef.shape),
                  _full(dskip.shape), _full(wglu.shape), _full(bglu.shape)],
        out_specs=[pl.BlockSpec((nb, lt, S5_WIDTH), lambda i: (0, i, 0)), _full(h0.shape)],
        out_shape=[jax.ShapeDtypeStruct((nb, L, S5_WIDTH), BF16), jax.ShapeDtypeStruct(h0.shape, F32)],
        scratch_shapes=[pltpu.VMEM((S5_LANE_BLOCKS, rows, LANES), F32), pltpu.VMEM((tpb, rows, LANES), F32),
                        pltpu.VMEM((tpb, rows, LANES), F32), pltpu.VMEM((S5_LANE_BLOCKS, rows, LANES), F32),
                        pltpu.VMEM(h0.shape, F32)],
        compiler_params=_params(("arbitrary",)),
        name="s5_scan",
    )(u3, h0, bmat, cmat, acoef, dskip, wglu, bglu)


def _s5_weights(a_re, a_im, log_dt, b_re, b_im, c_re, c_im):
    dt = jnp.exp(log_dt)[:, None]
    lr, li = a_re * dt, a_im * dt
    mag = jnp.exp(lr)
    ab_re, ab_im = mag * jnp.cos(li), mag * jnp.sin(li)
    den = a_re * a_re + a_im * a_im
    cf_re = ((ab_re - 1.0) * a_re + ab_im * a_im) / den
    cf_im = (ab_im * a_re - (ab_re - 1.0) * a_im) / den
    bb_re = cf_re[..., None] * b_re - cf_im[..., None] * b_im
    bb_im = cf_re[..., None] * b_im + cf_im[..., None] * b_re
    jj = np.arange(S5_TILES)[:, None, None]
    lg = np.arange(8)[None, :, None]
    gi = np.arange(2)[None, None, :]
    sel = jnp.asarray((lg == 2 * (jj % 4) + gi).astype(np.float32))
    tiles = lambda w: w.reshape(S5_TILES, 2, *w.shape[1:])
    bt = lambda w: jnp.einsum('jlg,jgpn->jlngp', sel, tiles(w)).reshape(S5_TILES, LANES, LANES)
    bmat = jnp.concatenate([bt(bb_re), bt(bb_im)], axis=-1)
    ct = lambda w: jnp.einsum('jlg,jgnp->jgpln', sel, tiles(w)).reshape(S5_TILES, LANES, LANES)
    cmat = jnp.concatenate([ct(c_re), -ct(c_im)], axis=1)
    acoef = jnp.concatenate([ab_re.reshape(S5_TILES, LANES), ab_im.reshape(S5_TILES, LANES)], axis=-1)
    acoef = jnp.broadcast_to(acoef[:, None, :], (S5_TILES, SUBLANES, 2 * LANES))
    return bmat.astype(BF16), cmat.astype(BF16), acoef


def _s5_state_in(h_re, h_im):
    nb = h_re.shape[0]
    h = jnp.concatenate([h_re.reshape(nb, S5_TILES, LANES), h_im.reshape(nb, S5_TILES, LANES)], axis=-1)
    return jnp.transpose(h, (1, 0, 2))


def _s5_state_out(h):
    nb = h.shape[1]
    h = jnp.transpose(h, (1, 0, 2))
    return (h[..., :LANES].reshape(nb, S5_GROUPS, S5_STATE), h[..., LANES:].reshape(nb, S5_GROUPS, S5_STATE))


def _split_bf16(a):
    hi = a.astype(BF16)
    return hi, (a - hi.astype(F32)).astype(BF16)


def _bdot(a, b):
    return lax.dot_general(a, b, (((2,), (1,)), ((0,), (0,))), preferred_element_type=F32)


def _bdot_nt(a, b):
    return lax.dot_general(a, b, (((2,), (2,)), ((0,), (0,))), preferred_element_type=F32)


def _bdot_bf16(a, b):
    return _bdot(a.astype(BF16), b.astype(BF16))


def _unit_lower_inverse(nmat):
    lc = nmat.shape[-1]
    ri = lax.broadcasted_iota(jnp.int32, nmat.shape, 1)
    ci = lax.broadcasted_iota(jnp.int32, nmat.shape, 2)
    base = 16
    dmat = jnp.where(ri // base == ci // base, nmat, 0.0)
    inv = jnp.where(ri == ci, 1.0, 0.0) - dmat
    pw = dmat
    for _ in range(3):
        pw = _bdot_bf16(pw, pw)
        inv = inv + _bdot_bf16(inv, pw)
    size = base
    while size < lc:
        off = jnp.where(ri // (2 * size) == ci // (2 * size), jnp.where(ri // size > ci // size, nmat, 0.0), 0.0)
        inv = inv - _bdot_bf16(_bdot_bf16(inv, off), inv)
        size *= 2
    return inv


def _gdn_local_kernel(qkv_ref, ba_ref, ctx_ref, cw_ref, p1_ref, p2_ref, ltri_ref,
                      u0_ref, w_ref, qd_ref, kd_ref, attn_ref, g_ref, cout_ref, xpad, *, lc, cp):
    c = pl.program_id(1)
    rb = lc * cp

    @pl.when(c == 0)
    def _():
        xpad[0:SUBLANES, :] = ctx_ref[0]

    xpad[SUBLANES:SUBLANES + rb, :] = qkv_ref[0]
    cw = cw_ref[...]
    conv = (cw[3:4] * xpad[8:8 + rb, :] + cw[2:3] * xpad[7:7 + rb, :]
            + cw[1:2] * xpad[6:6 + rb, :] + cw[0:1] * xpad[5:5 + rb, :])
    tail = xpad[rb:rb + SUBLANES, :]
    xpad[0:SUBLANES, :] = tail
    cout_ref[0] = tail
    a = conv * _sigmoid(conv)

    ba = ba_ref[0]
    beta_all = _sigmoid(ba)
    sp_in = ba + p2_ref[...]
    softplus = jnp.maximum(sp_in, 0.0) + jnp.log(1.0 + jnp.exp(-jnp.abs(sp_in)))
    g_all = p1_ref[...] * softplus
    g_hi = g_all.astype(BF16)
    g_r = g_all - g_hi.astype(F32)
    g_mid = g_r.astype(BF16)
    g_lo = (g_r - g_mid.astype(F32)).astype(BF16)
    lt = ltri_ref[...]
    G = _dot(lt, g_hi) + _dot(lt, g_mid) + _dot(lt, g_lo)
    g_ref[0] = G
    GT = G.T
    pairs = [(h, cc) for h in range(GDN_HEADS) for cc in range(cp)]
    qs, ks, vs, betas, gcols, grows, glasts = [], [], [], [], [], [], []
    for h in range(GDN_HEADS):
        qa = a[:, h * GDN_DK:(h + 1) * GDN_DK]
        ka = a[:, (GDN_HEADS + h) * GDN_DK:(GDN_HEADS + h + 1) * GDN_DK]
        va = a[:, (2 * GDN_HEADS + h) * GDN_DK:(2 * GDN_HEADS + h + 1) * GDN_DK]
        qa = qa * lax.rsqrt(jnp.sum(qa * qa, -1, keepdims=True) + NORM_EPS) * (GDN_DK ** -0.5)
        ka = ka * lax.rsqrt(jnp.sum(ka * ka, -1, keepdims=True) + NORM_EPS)
        for cc in range(cp):
            rows = slice(cc * lc, (cc + 1) * lc)
            qs.append(qa[rows])
            ks.append(ka[rows])
            vs.append(va[rows])
            betas.append(beta_all[rows, h:h + 1])
            gcols.append(G[rows, GDN_HEADS + h:GDN_HEADS + h + 1])
            grows.append(GT[GDN_HEADS + h:GDN_HEADS + h + 1, cc * lc:(cc + 1) * lc])
            glasts.append(GT[GDN_HEADS + h:GDN_HEADS + h + 1, (cc + 1) * lc - 1:(cc + 1) * lc])
    q3, k3, v3 = jnp.stack(qs), jnp.stack(ks), jnp.stack(vs)
    beta3, gcol3 = jnp.stack(betas), jnp.stack(gcols)
    grow3, glast3 = jnp.stack(grows), jnp.stack(glasts)
    shape3 = (len(pairs), lc, lc)
    ri = lax.broadcasted_iota(jnp.int32, shape3, 1)
    ci = lax.broadcasted_iota(jnp.int32, shape3, 2)
    incl = ri >= ci
    dec3 = jnp.where(incl, jnp.exp(jnp.where(incl, gcol3 - grow3, 0.0)), 0.0)
    eg3 = jnp.exp(gcol3)
    kb3 = k3 * beta3
    kbf3 = k3.astype(BF16)
    nmat3 = jnp.where(ri > ci, _bdot_nt(kb3.astype(BF16), kbf3) * dec3, 0.0)
    inv3 = _unit_lower_inverse(nmat3)
    sol3 = _bdot_bf16(inv3, jnp.concatenate([v3 * beta3, kb3 * eg3], axis=-1))
    w3 = sol3[:, :, GDN_DV:].astype(BF16)
    qd3 = (q3 * eg3).astype(BF16)
    kd3 = (k3 * jnp.exp(glast3 - gcol3)).astype(BF16)
    attn3 = (_bdot_nt(q3.astype(BF16), kbf3) * dec3).astype(BF16)
    for i, (h, cc) in enumerate(pairs):
        rows = slice(cc * lc, (cc + 1) * lc)
        cols = slice(h * GDN_DV, (h + 1) * GDN_DV)
        u0_ref[0, rows, cols] = sol3[i, :, :GDN_DV]
        w_ref[0, rows, cols] = w3[i]
        qd_ref[0, rows, cols] = qd3[i]
        kd_ref[0, rows, cols] = kd3[i]
        attn_ref[0, rows, h * lc:(h + 1) * lc] = attn3[i]


def _gdn_seq_kernel(u0_ref, w_ref, qd_ref, kd_ref, attn_ref, g_ref, z_ref, s0_ref, nw_ref,
                    y_ref, sout_ref, S, *, lc, nbb):
    c = pl.program_id(1)

    @pl.when(c == 0)
    def _():
        S[...] = s0_ref[...]

    nw = nw_ref[...]
    pairs = [(bb, h) for bb in range(nbb) for h in range(GDN_HEADS)]
    hcols = lambda h: slice(h * GDN_DV, (h + 1) * GDN_DV)
    stack = lambda f: jnp.stack([f(bb, h) for bb, h in pairs])
    dlast = jnp.exp(g_ref[:, lc - 1:lc, :])
    S3 = S[...].reshape(len(pairs), GDN_DK, GDN_DV)
    wq3 = stack(lambda bb, h: jnp.concatenate([w_ref[bb, :, hcols(h)], qd_ref[bb, :, hcols(h)]], axis=0))
    r3 = _bdot(wq3, S3.astype(BF16))
    ub3 = (stack(lambda bb, h: u0_ref[bb, :, hcols(h)]) - r3[:, :lc]).astype(BF16)
    o3 = r3[:, lc:] + _bdot(stack(lambda bb, h: attn_ref[bb, :, h * lc:(h + 1) * lc]), ub3)
    d3 = stack(lambda bb, h: dlast[bb, :, GDN_HEADS + h:GDN_HEADS + h + 1])
    kd3 = stack(lambda bb, h: kd_ref[bb, :, hcols(h)])
    kdu3 = lax.dot_general(kd3, ub3, (((1,), (1,)), ((0,), (0,))), preferred_element_type=F32)
    S[...] = (d3 * S3 + kdu3).reshape(S.shape)
    o3 = o3 * lax.rsqrt(jnp.mean(o3 * o3, -1, keepdims=True) + NORM_EPS) * nw
    for i, (bb, h) in enumerate(pairs):
        zh = z_ref[bb, :, hcols(h)]
        y_ref[bb, :, hcols(h)] = (o3[i] * (zh * _sigmoid(zh))).astype(y_ref.dtype)

    @pl.when(c == pl.num_programs(1) - 1)
    def _():
        sout_ref[...] = S[...]


def _gdn(qkv3, z3, ba3, ctx8, s0, cw, p1, p2, nw, lc, cp, nbb):
    nb, L, _ = qkv3.shape
    rb = lc * cp
    hd = GDN_HEADS * GDN_DV
    ltri = jnp.asarray(np.kron(np.eye(cp, dtype=np.float32), np.tril(np.ones((lc, lc), np.float32)))).astype(BF16)
    blk = lambda n: pl.BlockSpec((1, rb, n), lambda b, c: (b, c, 0))
    per_b = lambda shape: pl.BlockSpec((1,) + shape, lambda b, c: (b,) + (0,) * len(shape))
    cst = lambda shape: pl.BlockSpec(shape, lambda b, c: (0,) * len(shape))
    sds = lambda n, dt: jax.ShapeDtypeStruct((nb, L, n), dt)
    u0, w, qd, kd, attn, G, cout = pl.pallas_call(
        functools.partial(_gdn_local_kernel, lc=lc, cp=cp),
        grid=(nb, L // rb),
        in_specs=[blk(GDN_QKV), blk(LANES), per_b((SUBLANES, GDN_QKV)), cst(cw.shape), cst(p1.shape),
                  cst(p2.shape), cst(ltri.shape)],
        out_specs=[blk(hd), blk(hd), blk(hd), blk(hd), blk(GDN_HEADS * lc), blk(LANES),
                   per_b((SUBLANES, GDN_QKV))],
        out_shape=[sds(hd, F32), sds(hd, BF16), sds(hd, BF16), sds(hd, BF16), sds(GDN_HEADS * lc, BF16),
                   sds(LANES, F32), jax.ShapeDtypeStruct((nb, SUBLANES, GDN_QKV), F32)],
        scratch_shapes=[pltpu.VMEM((rb + SUBLANES, GDN_QKV), F32)],
        compiler_params=_params(("parallel", "arbitrary")),
        name="gdn_local",
    )(qkv3, ba3, ctx8, cw, p1, p2, ltri)
    sblk = lambda n: pl.BlockSpec((nbb, lc, n), lambda b, c: (b, c, 0))
    state = pl.BlockSpec((nbb, GDN_HEADS, GDN_DK, GDN_DV), lambda b, c: (b, 0, 0, 0))
    y, s_new = pl.pallas_call(
        functools.partial(_gdn_seq_kernel, lc=lc, nbb=nbb),
        grid=(nb // nbb, L // lc),
        in_specs=[sblk(hd), sblk(hd), sblk(hd), sblk(hd), sblk(GDN_HEADS * lc), sblk(LANES), sblk(hd),
                  state, cst(nw.shape)],
        out_specs=[sblk(hd), state],
        out_shape=[sds(hd, BF16), jax.ShapeDtypeStruct(s0.shape, F32)],
        scratch_shapes=[pltpu.VMEM((nbb, GDN_HEADS, GDN_DK, GDN_DV), F32)],
        compiler_params=_params(("parallel", "arbitrary")),
        name="gdn_seq",
    )(u0, w, qd, kd, attn, G, z3, s0, nw)
    return y, s_new, cout


def _ret_kernel(q_ref, k_ref, v_ref, g_ref, r0_ref, dec_ref, qs_ref, ks_ref, cd_ref, o_ref, rout_ref, R,
                *, nbb):
    c = pl.program_id(1)

    @pl.when(c == 0)
    def _():
        R[...] = r0_ref[...]

    pairs = [(bb, h) for bb in range(nbb) for h in range(RET_HEADS)]
    stack = lambda f: jnp.stack([f(bb, h) for bb, h in pairs])
    kcols = lambda h: slice(h * RET_DK, (h + 1) * RET_DK)
    vcols = lambda h: slice(h * RET_DV, (h + 1) * RET_DV)
    q3 = stack(lambda bb, h: q_ref[bb, :, kcols(h)])
    k3 = stack(lambda bb, h: k_ref[bb, :, kcols(h)])
    v3 = stack(lambda bb, h: v_ref[bb, :, vcols(h)])
    dec3 = stack(lambda bb, h: dec_ref[h])
    qs3 = stack(lambda bb, h: qs_ref[h])
    ks3 = stack(lambda bb, h: ks_ref[h])
    cd3 = stack(lambda bb, h: cd_ref[h])
    R3 = R[...].reshape(len(pairs), RET_DK, RET_DV)
    s3 = _bdot_nt(q3, k3) * dec3
    o3 = _bdot(s3.astype(BF16), v3) + _bdot(q3, R3.astype(BF16)) * qs3
    kv3 = lax.dot_general((k3.astype(F32) * ks3).astype(BF16), v3, (((1,), (1,)), ((0,), (0,))),
                          preferred_element_type=F32)
    R[...] = (cd3 * R3 + kv3).reshape(R.shape)
    mu = jnp.mean(o3, -1, keepdims=True)
    d3 = o3 - mu
    var = jnp.mean(d3 * d3, -1, keepdims=True)
    on3 = d3 * lax.rsqrt(var + LN_EPS)
    for i, (bb, h) in enumerate(pairs):
        gt = g_ref[bb, :, vcols(h)].astype(F32)
        o_ref[bb, :, vcols(h)] = (gt * _sigmoid(gt) * on3[i]).astype(o_ref.dtype)

    @pl.when(c == pl.num_programs(1) - 1)
    def _():
        rout_ref[...] = R[...]


def _retention(proj3, r0, chunk, nbb, nc):
    nb, L, _ = proj3.shape
    lc = chunk * nc
    log_g = np.log(1.0 - 2.0 ** (-5.0 - np.arange(RET_HEADS, dtype=np.float64)))
    idx = np.arange(lc, dtype=np.float64)
    dist = idx[:, None] - idx[None, :]
    which = (idx // chunk)[:, None] - (idx // chunk)[None, :]
    dec = np.where(which >= 0, np.exp(log_g[:, None, None] * np.abs(dist)), 0.0).astype(np.float32)
    qs = np.exp(log_g[:, None] * (idx + 1.0)).astype(np.float32)[..., None]
    ks = np.exp(log_g[:, None] * (lc - 1.0 - idx)).astype(np.float32)[..., None]
    cdec = np.exp(log_g * lc).astype(np.float32)[:, None, None]
    nqk = RET_HEADS * RET_DK
    nv = RET_HEADS * RET_DV
    cst = lambda shape: pl.BlockSpec(shape, lambda b, c: (0,) * len(shape))
    state = pl.BlockSpec((nbb, RET_HEADS, RET_DK, RET_DV), lambda b, c: (b, 0, 0, 0))
    return pl.pallas_call(
        functools.partial(_ret_kernel, nbb=nbb),
        grid=(nb // nbb, L // lc),
        in_specs=[pl.BlockSpec((nbb, lc, nqk), lambda b, c: (b, c, 0)),
                  pl.BlockSpec((nbb, lc, nqk), lambda b, c: (b, c, 1)),
                  pl.BlockSpec((nbb, lc, nv), lambda b, c: (b, c, 1)),
                  pl.BlockSpec((nbb, lc, nv), lambda b, c: (b, c, 2)),
                  state, cst(dec.shape), cst(qs.shape), cst(ks.shape), cst(cdec.shape)],
        out_specs=[pl.BlockSpec((nbb, lc, nv), lambda b, c: (b, c, 0)), state],
        out_shape=[jax.ShapeDtypeStruct((nb, L, nv), BF16), jax.ShapeDtypeStruct(r0.shape, F32)],
        scratch_shapes=[pltpu.VMEM((nbb, RET_HEADS, RET_DK, RET_DV), F32)],
        compiler_params=_params(("parallel", "arbitrary")),
        name="retention",
    )(proj3, proj3, proj3, proj3, r0, jnp.asarray(dec), jnp.asarray(qs), jnp.asarray(ks), jnp.asarray(cdec))


RUN_ALIGN = SUBLANES
RUN_PIECE = 64
MOE_VMEM_LIMIT = 58 * 1024 * 1024


def _moe_params(sem):
    return pltpu.CompilerParams(dimension_semantics=sem, vmem_limit_bytes=MOE_VMEM_LIMIT)


def _tile_cap(tm):
    rows = TOP_K * tm + N_EXPERTS * (RUN_ALIGN - 1)
    return -(-rows // LANES) * LANES


def _route_tile(x, rw_ref, rb_ref, lst_ref, ust_ref, pos_ref, gt_ref, cnt_ref, xs_ref):
    tm = x.shape[0]
    cap = xs_ref.shape[1]
    lane = lax.broadcasted_iota(jnp.int32, (tm, LANES), 1)
    lane_f = lane.astype(F32)
    xh, xl = _split_bf16(x)
    wh, wl = _split_bf16(rw_ref[...])
    logits = _dot(xh, wh) + _dot(xh, wl) + _dot(xl, wh) + rb_ref[...]
    logits = jnp.where(lane < N_EXPERTS, logits, -jnp.inf)
    vals, hots = [], []
    for _ in range(TOP_K):
        m = jnp.max(logits, -1, keepdims=True)
        first = jnp.min(jnp.where(logits == m, lane_f, float(LANES)), -1, keepdims=True)
        hot = lane_f == first
        vals.append(m)
        hots.append(hot)
        logits = jnp.where(hot, -jnp.inf, logits)
    es = [jnp.exp(v - vals[0]) for v in vals]
    den = es[0] + es[1] + es[2] + es[3]
    multi = jnp.zeros((tm, LANES), F32)
    for hot in hots:
        multi = multi + hot.astype(F32)
    counts = jnp.sum(multi, 0, keepdims=True)
    units = jnp.floor((counts + (RUN_ALIGN - 1)) * (1.0 / RUN_ALIGN))
    offs = _dot(jnp.broadcast_to(units, (SUBLANES, LANES)).astype(BF16), ust_ref[...])[0:1] * float(RUN_ALIGN)
    before = _dot(lst_ref[...], multi.astype(BF16))
    slot = offs + before
    pos = jnp.zeros((tm, LANES), F32)
    gt = jnp.zeros((tm, LANES), F32)
    for kk in range(TOP_K):
        pos = jnp.where(lane == kk, jnp.sum(jnp.where(hots[kk], slot, 0.0), -1, keepdims=True), pos)
        gt = jnp.where(lane == kk, es[kk] / den, gt)
    pos = pos.astype(jnp.int32)
    pos_ref[...] = pos
    gt_ref[...] = gt
    cnt_ref[0] = counts
    pos_t = pos.T[0:2 * SUBLANES].astype(jnp.int16)
    row = lax.broadcasted_iota(jnp.int16, (cap, tm), 0)
    sel = jnp.zeros((cap, tm), BF16)
    for kk in range(TOP_K):
        sel = sel + jnp.where(row == pos_t[kk:kk + 1, :], jnp.ones((), BF16), jnp.zeros((), BF16))
    xs_ref[0] = _dot(sel, xh)


def _outproj_route_kernel(*refs, n_in):
    a_refs = refs[:n_in]
    w_refs = refs[n_in:2 * n_in]
    x_ref, g_ref, b_ref = refs[2 * n_in:2 * n_in + 3]
    route_in = refs[2 * n_in + 3:2 * n_in + 7]
    o_ref = refs[2 * n_in + 7]
    route_out = refs[2 * n_in + 8:]
    acc = _dot(a_refs[0][...], w_refs[0][...])
    for a_ref, w_ref in zip(a_refs[1:], w_refs[1:]):
        acc = acc + _dot(a_ref[...], w_ref[...])
    x1 = _layer_norm(DEEPNORM_ALPHA * x_ref[...] + acc, g_ref[...], b_ref[...])
    o_ref[...] = x1
    _route_tile(x1, *route_in, *route_out)


def _outproj_route(acts, ws, x, g, b, rw, rb, tm):
    t = x.shape[0]
    nt = t // tm
    cap = _tile_cap(tm)
    lst = jnp.asarray(np.tril(np.ones((tm, tm), np.float32), -1)).astype(BF16)
    ust = jnp.asarray(np.triu(np.ones((LANES, LANES), np.float32), 1)).astype(BF16)
    row = lambda n: pl.BlockSpec((tm, n), lambda i: (i, 0))
    x1, pos, gt, cnt, xs = pl.pallas_call(
        functools.partial(_outproj_route_kernel, n_in=len(acts)),
        grid=(nt,),
        in_specs=[row(a.shape[1]) for a in acts] + [_full(w.shape) for w in ws]
                 + [row(D_MODEL), _full((1, D_MODEL)), _full((1, D_MODEL)),
                    _full(rw.shape), _full(rb.shape), _full(lst.shape), _full(ust.shape)],
        out_specs=[row(D_MODEL), row(LANES), row(LANES), pl.BlockSpec((1, 1, LANES), lambda i: (i, 0, 0)),
                   pl.BlockSpec((1, cap, D_MODEL), lambda i: (i, 0, 0))],
        out_shape=[jax.ShapeDtypeStruct((t, D_MODEL), F32),
                   jax.ShapeDtypeStruct((t, LANES), jnp.int32), jax.ShapeDtypeStruct((t, LANES), F32),
                   jax.ShapeDtypeStruct((nt, 1, LANES), F32), jax.ShapeDtypeStruct((nt, cap, D_MODEL), F32)],
        compiler_params=_moe_params(("parallel",)),
        name="outproj_route",
    )(*acts, *ws, x, g, b, rw, rb, lst, ust)
    return x1, (pos, gt, cnt, xs)


def _expert_kernel(be_ref, nu_ref, ilo_ref, rows_ref, gs_ref, n8_ref, lo_ref, *refs, blk, tiles):
    ng = len(tiles)
    xs_hbms = refs[:ng]
    w1_ref, b1_ref, w2_ref, b2_ref = refs[ng:ng + 4]
    ys_hbms = refs[ng + 4:2 * ng + 4]
    xbuf, ybuf, w1b, w2b, in_sem, out_sem = refs[2 * ng + 4:]
    nt = sum(tiles)
    firsts = [sum(tiles[:g]) for g in range(ng)]
    j = pl.program_id(0)
    nu = nu_ref[0]

    def for_each_run(jb, fn):
        e = be_ref[jb]
        base = jb * blk
        for g in range(ng):
            end = firsts[g] + tiles[g]

            def cond(i, end=end):
                return (i < end) & (gs_ref[e * nt + jnp.minimum(i, nt - 1)] < base + blk)

            def body(i, g=g):
                g0 = gs_ref[e * nt + i]
                first = jnp.maximum(g0, base)
                last = jnp.minimum(g0 + n8_ref[e * nt + i], base + blk)
                fn(g, i - firsts[g], lo_ref[e * nt + i] + (first - g0), first - base, last - first)
                return i + 1

            lax.while_loop(cond, body, jnp.clip(ilo_ref[jb], firsts[g], end))

    def pieces(length, fn):
        def digits(sizes):
            for size in sizes:
                @pl.when((length & size) != 0)
                def _(size=size):
                    fn(length & ~(2 * size - 1), size)

        sizes = [blk >> k for k in range(blk.bit_length()) if blk >> k >= RUN_ALIGN]

        @pl.when(length > 2 * RUN_PIECE - 1)
        def _():
            digits([z for z in sizes if z > RUN_PIECE])

        digits([z for z in sizes if z <= RUN_PIECE])

    def aligned(v, size):
        return pl.ds(pl.multiple_of(v, RUN_ALIGN), size)

    def copy_in(jb, slot):
        def run(g, ig, src, dst, length):
            def piece(off, size):
                pltpu.make_async_copy(xs_hbms[g].at[ig, aligned(src + off, size)],
                                      xbuf.at[slot, aligned(dst + off, size)], in_sem.at[slot]).start()
            pieces(length, piece)
        for_each_run(jb, run)

    def copy_out(jb, slot):
        def run(g, ig, src, dst, length):
            def piece(off, size):
                pltpu.make_async_copy(ybuf.at[slot, aligned(dst + off, size)],
                                      ys_hbms[g].at[ig, aligned(src + off, size)], out_sem.at[slot]).start()
            pieces(length, piece)
        for_each_run(jb, run)

    def wait_rows(sem, nrows):
        size = blk
        while size >= RUN_ALIGN:
            @pl.when((nrows & size) != 0)
            def _(size=size):
                pltpu.make_async_copy(xbuf.at[1, pl.ds(0, size)], xbuf.at[0, pl.ds(0, size)], sem).wait()
            size //= 2

    @pl.when(j < nu)
    def _():
        slot = j % 2

        @pl.when(j == 0)
        def _():
            xbuf[...] = jnp.zeros_like(xbuf)
            copy_in(0, 0)

        wait_rows(in_sem.at[slot], rows_ref[j])

        @pl.when(j + 1 < nu)
        def _():
            copy_in(j + 1, 1 - slot)

        @pl.when(j >= 2)
        def _():
            wait_rows(out_sem.at[slot], rows_ref[jnp.maximum(j - 2, 0)])

        @pl.when((j == 0) | (be_ref[j] != be_ref[jnp.maximum(j - 1, 0)]))
        def _():
            w1b[...] = w1_ref[0, 0].astype(BF16)
            w2b[...] = w2_ref[0, 0].astype(BF16)

        h = _dot(xbuf[slot].astype(BF16), w1b[...]) + b1_ref[0]
        glu = jnp.minimum(h[:, :D_FF], SWIGLU_LIMIT)
        lin = jnp.clip(h[:, D_FF:], -SWIGLU_LIMIT, SWIGLU_LIMIT)
        act = glu * _sigmoid(SWIGLU_ALPHA * glu) * (lin + 1.0)
        ybuf[slot] = _dot(act.astype(BF16), w2b[...]) + b2_ref[0]
        copy_out(j, slot)

        @pl.when(j == nu - 1)
        def _():
            wait_rows(out_sem.at[slot], rows_ref[j])

            @pl.when(j >= 1)
            def _():
                wait_rows(out_sem.at[1 - slot], rows_ref[jnp.maximum(j - 1, 0)])


def _experts(tables, xs_list, w1, b1, w2, b2, layer, blk):
    ng = len(xs_list)
    n_blk = tables[0].shape[0]
    wspec = lambda shape: pl.BlockSpec((1, 1) + shape, lambda j, be, *_: (layer, be[j], 0, 0))
    bspec = lambda n: pl.BlockSpec((1, 1, n), lambda j, be, *_: (be[j], 0, 0))
    hbm = pl.BlockSpec(memory_space=pl.ANY)
    grid_spec = pltpu.PrefetchScalarGridSpec(
        num_scalar_prefetch=len(tables),
        grid=(n_blk,),
        in_specs=[hbm] * ng + [wspec((D_MODEL, 2 * D_FF)), bspec(2 * D_FF), wspec((D_FF, D_MODEL)), bspec(D_MODEL)],
        out_specs=[hbm] * ng,
        scratch_shapes=[pltpu.VMEM((2, blk, D_MODEL), F32), pltpu.VMEM((2, blk, D_MODEL), F32),
                        pltpu.VMEM((D_MODEL, 2 * D_FF), BF16), pltpu.VMEM((D_FF, D_MODEL), BF16),
                        pltpu.SemaphoreType.DMA((2,)), pltpu.SemaphoreType.DMA((2,))],
    )
    return pl.pallas_call(
        functools.partial(_expert_kernel, blk=blk, tiles=tuple(xs.shape[0] for xs in xs_list)),
        grid_spec=grid_spec,
        out_shape=[jax.ShapeDtypeStruct(xs.shape, F32) for xs in xs_list],
        input_output_aliases={len(tables) + g: g for g in range(ng)},
        compiler_params=_moe_params(("arbitrary",)),
        name="moe_experts",
    )(*tables, *xs_list, w1, b1, w2, b2)


def _combine_kernel(ys_ref, pos_ref, gt_ref, x_ref, p_ref, g_ref, b_ref, plew_ref, gatew_ref, o_ref):
    tm = x_ref.shape[0]
    cap = ys_ref.shape[1]
    pos = pos_ref[...].astype(jnp.int16)
    gt = gt_ref[...].astype(BF16)
    col = lax.broadcasted_iota(jnp.int16, (tm, cap), 1)
    sel = jnp.zeros((tm, cap), BF16)
    for kk in range(TOP_K):
        sel = sel + jnp.where(col == pos[:, kk:kk + 1], gt[:, kk:kk + 1], jnp.zeros((), BF16))
    y = _dot(sel, ys_ref[0].astype(BF16))
    x2 = _layer_norm(DEEPNORM_ALPHA * x_ref[...] + y, g_ref[...], b_ref[...])
    pp = _dot(p_ref[...].astype(BF16), plew_ref[...])
    gg = _sigmoid(_dot(x2.astype(BF16), gatew_ref[...]))
    o_ref[...] = x2 + pp * gg


def _combine(ys, pos, gt, x, p, layer, g, b, plew, gatew, tm):
    t = x.shape[0]
    cap = ys.shape[1]
    row = lambda n: pl.BlockSpec((tm, n), lambda i: (i, 0))
    return pl.pallas_call(
        _combine_kernel,
        grid=(t // tm,),
        in_specs=[pl.BlockSpec((1, cap, D_MODEL), lambda i: (i, 0, 0)), row(LANES), row(LANES),
                  row(D_MODEL), pl.BlockSpec((None, tm, PLE_DIM), lambda i: (layer, i, 0)),
                  _full((1, D_MODEL)), _full((1, D_MODEL)),
                  _full(plew.shape), _full(gatew.shape)],
        out_specs=row(D_MODEL),
        out_shape=jax.ShapeDtypeStruct((t, D_MODEL), F32),
        compiler_params=_moe_params(("parallel",)),
        name="moe_combine",
    )(ys, pos, gt, x, p, g, b, plew, gatew)


def _moe_tables(cnt, t, blk):
    nt = cnt.shape[0]
    n = cnt[:, 0, :N_EXPERTS].astype(jnp.int32)
    n8 = (n + RUN_ALIGN - 1) // RUN_ALIGN * RUN_ALIGN
    lo = jnp.cumsum(n8, axis=1) - n8
    rows_e = jnp.sum(n8, axis=0)
    padded = (rows_e + blk - 1) // blk * blk
    pend = jnp.cumsum(padded)
    pstart = pend - padded
    gstart = pstart[None, :] + jnp.cumsum(n8, axis=0) - n8
    n_blk = -(-(TOP_K * t + nt * N_EXPERTS * (RUN_ALIGN - 1) + N_EXPERTS * (blk - 1)) // blk)
    n_used = (pend[-1] // blk).astype(jnp.int32).reshape(1)
    blk_start = jnp.arange(n_blk, dtype=jnp.int32) * blk
    blk_e = jnp.minimum(jnp.sum((pend[None, :] <= blk_start[:, None]).astype(jnp.int32), axis=1), N_EXPERTS - 1)
    run_end = (gstart + n8)[:, blk_e]
    ilo = jnp.sum((run_end <= blk_start[None, :]).astype(jnp.int32), axis=0)
    rows_b = jnp.clip((pstart + rows_e)[blk_e] - blk_start, 0, blk)
    flat = lambda a: a.T.reshape(-1).astype(jnp.int32)
    i32 = lambda a: a.astype(jnp.int32)
    return (i32(blk_e), n_used, i32(ilo), i32(rows_b), flat(gstart), flat(n8), flat(lo))


def _moe_ple(xs_in, routed, ps, w1, b1, w2, b2, layer, g, b, plew, gatew, tms, blk):
    cnt = jnp.concatenate([r[2] for r in routed], axis=0)
    tables = _moe_tables(cnt, sum(x.shape[0] for x in xs_in), blk)
    ys = _experts(tables, [r[3] for r in routed], w1, b1, w2, b2, layer, blk)
    return [_combine(y, r[0], r[1], x, p, layer, g, b, plew, gatew, tm)
            for y, r, x, p, tm in zip(ys, routed, xs_in, ps, tms)]


_ROT_PERM = np.concatenate([np.arange(0, RET_DK, 2), np.arange(1, RET_DK, 2)])


def _lane_row(vals, offset):
    row = jnp.zeros((1, LANES), F32)
    return row.at[0, offset:offset + vals.shape[0]].set(vals)


def _even_mixer(x, nb, L, s5_re, s5_im, gdn_s, conv_s, W, tm):
    t = nb * L
    lc = L if L <= CHUNK else CHUNK
    u, qkv, z, ba = _proj_even(x, W['wu'], W['wqkv'], W['wz'], W['wba'], tm)
    yA3, h_new = _s5(u.reshape(nb, L, S5_WIDTH), _s5_state_in(s5_re[0].astype(F32), s5_im[0].astype(F32)),
                     W['bmat'], W['cmat'], W['acoef'], W['dskip'], W['wglu'], W['bglu'], lc)
    new_re, new_im = _s5_state_out(h_new)
    ctx8 = jnp.concatenate([jnp.zeros((nb, SUBLANES - (GDN_CONV - 1), GDN_QKV), F32), conv_s[0].astype(F32)], axis=1)
    yB3, new_gdn, cout = _gdn(qkv.reshape(nb, L, GDN_QKV), z.reshape(nb, L, -1), ba.reshape(nb, L, LANES),
                              ctx8, gdn_s[0].astype(F32), W['convw'], W['p1'], W['p2'], W['normw'], lc,
                              cp=min(4, L // lc), nbb=4)
    new_conv = cout[:, SUBLANES - (GDN_CONV - 1):, :]
    x, routed = _outproj_route([yA3.reshape(t, -1), yB3.reshape(t, -1)], [W['wout_a'], W['wout_b']], x,
                               W['ln1_g'][0], W['ln1_b'][0], W['rw'][0], W['rb'][0], tm)
    return x, routed, new_re[None], new_im[None], new_gdn[None], new_conv[None]


def _odd_mixer(x, nb, L, ret_s, pos0, W, tm):
    t = nb * L
    lc = L if L <= CHUNK else CHUNK
    pos = pos0 + jnp.arange(L, dtype=F32)
    freq = 1.0 / (ROPE_BASE ** jnp.linspace(0.0, 1.0, RET_DK // 2, dtype=F32))
    ang = pos[:, None] * freq[None]
    reps = max(1, tm // L)
    cos = jnp.tile(jnp.cos(ang), (reps, 1))
    sin = jnp.tile(jnp.sin(ang), (reps, 1))
    proj = _proj_odd(x, W['win_odd'], cos, sin, tm)
    half = RET_DK // 2
    r0 = ret_s[0].astype(F32).reshape(nb, RET_HEADS, half, 2, RET_DV).swapaxes(2, 3).reshape(
        nb, RET_HEADS, RET_DK, RET_DV)
    o3, r_new = _retention(proj.reshape(nb, L, -1), r0, lc, nbb=2, nc=min(4, L // lc))
    new_ret = r_new.reshape(nb, RET_HEADS, 2, half, RET_DV).swapaxes(2, 3).reshape(nb, RET_HEADS, RET_DK, RET_DV)
    x, routed = _outproj_route([o3.reshape(t, -1)], [W['wout_odd']], x, W['ln1_g'][1], W['ln1_b'][1],
                               W['rw'][1], W['rb'][1], tm)
    return x, routed, new_ret[None]


def kernel(x_prompt, x_sample, state_s5_re, state_s5_im, state_gdn, state_gdn_conv, state_ret, p_prompt, p_sample, w_in_even, s5_a_re, s5_a_im, s5_log_dt, s5_b_re, s5_b_im, s5_c_re, s5_c_im, s5_d, s5_w_glu, s5_b_glu, gdn_conv_w, gdn_a_log, gdn_dt_bias, gdn_norm_w, w_out_even, w_in_odd, w_out_odd, ln1_g, ln1_b, ln2_g, ln2_b, router_w, router_b, moe_w1, moe_b1, moe_w2, moe_b2, ple_w, ple_gate_w):
    o1 = S5_WIDTH
    o2 = o1 + GDN_QKV
    o3 = o2 + GDN_HEADS * GDN_DV
    win = w_in_even[0]
    bmat, cmat, acoef = _s5_weights(s5_a_re[0], s5_a_im[0], s5_log_dt[0], s5_b_re[0], s5_b_im[0],
                                    s5_c_re[0], s5_c_im[0])
    wodd = w_in_odd[0]
    nk = RET_HEADS * RET_DK
    perm_cols = lambda w: w.reshape(D_MODEL, RET_HEADS, RET_DK)[:, :, _ROT_PERM].reshape(D_MODEL, nk)
    W = dict(
        wu=win[:, :o1].astype(BF16), wqkv=win[:, o1:o2].astype(BF16), wz=win[:, o2:o3].astype(BF16),
        wba=jnp.pad(win[:, o3:], ((0, 0), (0, LANES - 2 * GDN_HEADS))).astype(BF16),
        bmat=bmat, cmat=cmat, acoef=acoef, dskip=s5_d[0][None], wglu=s5_w_glu[0].astype(BF16),
        bglu=s5_b_glu[0][None], convw=gdn_conv_w[0],
        p1=_lane_row(-jnp.exp(gdn_a_log[0]), GDN_HEADS), p2=_lane_row(gdn_dt_bias[0], GDN_HEADS),
        normw=gdn_norm_w[0][None],
        wout_a=w_out_even[0][:S5_WIDTH].astype(BF16), wout_b=w_out_even[0][S5_WIDTH:].astype(BF16),
        win_odd=jnp.concatenate([perm_cols(wodd[:, :nk]), perm_cols(wodd[:, nk:2 * nk]), wodd[:, 2 * nk:]],
                                axis=1).astype(BF16),
        wout_odd=w_out_odd[0].astype(BF16),
        ln1_g=ln1_g[:, None], ln1_b=ln1_b[:, None], ln2_g=ln2_g[:, None], ln2_b=ln2_b[:, None],
        rw=jnp.pad(router_w, ((0, 0), (0, 0), (0, LANES - N_EXPERTS))),
        rb=jnp.pad(router_b, ((0, 0), (0, LANES - N_EXPERTS)))[:, None],
        w1=moe_w1, b1=moe_b1[:, :, None], w2=moe_w2, b2=moe_b2[:, :, None],
        plew=ple_w.astype(BF16), gatew=ple_gate_w.astype(BF16),
    )
    bp, lp, _ = x_prompt.shape
    bs, ls, _ = x_sample.shape
    zeros = lambda *s: jnp.zeros(s, F32)
    shapes = [(bp, lp), (bs, ls)]
    tms = [PROMPT_TM, SAMPLE_TM]
    ps = [p_prompt, p_sample]
    xs = [x_prompt.reshape(bp * lp, D_MODEL).astype(F32), x_sample.reshape(bs * ls, D_MODEL).astype(F32)]
    even_states = [(zeros(1, bp, S5_GROUPS, S5_STATE), zeros(1, bp, S5_GROUPS, S5_STATE),
                    zeros(1, bp, GDN_HEADS, GDN_DK, GDN_DV), zeros(1, bp, GDN_CONV - 1, GDN_QKV)),
                   (state_s5_re, state_s5_im, state_gdn, state_gdn_conv)]
    ret_states = [zeros(1, bp, RET_HEADS, RET_DK, RET_DV), state_ret]
    pos0s = [0.0, float(PAST_LEN)]

    def moe(mixed, layer):
        return _moe_ple([m[0] for m in mixed], [m[1] for m in mixed], [p.reshape(DEPTH, -1, PLE_DIM) for p in ps],
                        W['w1'], W['b1'][layer], W['w2'], W['b2'][layer], layer, W['ln2_g'][layer],
                        W['ln2_b'][layer], W['plew'][layer], W['gatew'][layer], tms, blk=EXPERT_BLK)

    even = [_even_mixer(x, nb, L, *st, W, tm) for x, (nb, L), st, tm in zip(xs, shapes, even_states, tms)]
    xs = moe(even, 0)
    odd = [_odd_mixer(x, nb, L, st, pos0, W, tm)
           for x, (nb, L), st, pos0, tm in zip(xs, shapes, ret_states, pos0s, tms)]
    xs = moe(odd, 1)
    dp = x_prompt.dtype
    y_p, y_s = xs[0].reshape(bp, lp, D_MODEL), xs[1].reshape(bs, ls, D_MODEL)
    (_, _, p_re, p_im, p_gdn, p_conv), (_, _, s_re, s_im, s_gdn, s_conv) = even
    p_ret, s_ret = odd[0][2], odd[1][2]
    return (y_p.astype(dp), y_s.astype(x_sample.dtype),
            p_re.astype(dp), p_im.astype(dp), p_gdn.astype(dp), p_conv.astype(dp), p_ret.astype(dp),
            s_re.astype(state_s5_re.dtype), s_im.astype(state_s5_im.dtype), s_gdn.astype(state_gdn.dtype),
            s_conv.astype(state_gdn_conv.dtype), s_ret.astype(state_ret.dtype))
```

```python
import functools

import jax
import jax.numpy as jnp
import numpy as np
from jax import lax
from jax.experimental import pallas as pl
from jax.experimental.pallas import tpu as pltpu

F32 = jnp.float32
BF16 = jnp.bfloat16

D_MODEL = 1024
CHUNK = 64
S5_WIDTH = 512
S5_GROUP = 16
S5_GROUPS = 32
S5_STATE = 64
S5_TILES = 16
GDN_HEADS = 4
GDN_DK = 128
GDN_DV = 128
GDN_CONV = 4
GDN_QKV = 1536
RET_HEADS = 4
RET_DK = 256
RET_DV = 512
ROPE_BASE = 10000.0
N_EXPERTS = 32
TOP_K = 4
D_FF = 1024
SWIGLU_LIMIT = 7.0
SWIGLU_ALPHA = 1.702
PLE_DIM = 256
DEPTH = 2
PAST_LEN = 1024
DEEPNORM_ALPHA = (2 * DEPTH) ** 0.25
LN_EPS = 1e-5
NORM_EPS = 1e-6

LANES = 128
SUBLANES = 8
VMEM_LIMIT = 48 * 1024 * 1024

PROMPT_TM = 512
SAMPLE_TM = 128
EXPERT_BLK = 512


def _params(sem):
    return pltpu.CompilerParams(dimension_semantics=sem, vmem_limit_bytes=VMEM_LIMIT)


def _dot(a, b):
    return jnp.dot(a, b, preferred_element_type=F32)


def _sigmoid(x):
    return 1.0 / (1.0 + jnp.exp(-x))


def _full(shape):
    nd = len(shape)
    return pl.BlockSpec(shape, lambda *_: (0,) * nd)


def _proj_even_kernel(x_ref, wu_ref, wqkv_ref, wz_ref, wba_ref, u_ref, qkv_ref, z_ref, ba_ref):
    xb = x_ref[...].astype(BF16)
    u_ref[...] = _dot(xb, wu_ref[...])
    qkv_ref[...] = _dot(xb, wqkv_ref[...])
    z_ref[...] = _dot(xb, wz_ref[...])
    ba_ref[...] = _dot(xb, wba_ref[...])


def _proj_even(x, wu, wqkv, wz, wba, tm):
    t = x.shape[0]
    row = lambda n: pl.BlockSpec((tm, n), lambda i: (i, 0))
    return pl.pallas_call(
        _proj_even_kernel,
        grid=(t // tm,),
        in_specs=[row(D_MODEL), _full(wu.shape), _full(wqkv.shape), _full(wz.shape), _full(wba.shape)],
        out_specs=[row(S5_WIDTH), row(GDN_QKV), row(GDN_HEADS * GDN_DV), row(LANES)],
        out_shape=[jax.ShapeDtypeStruct((t, S5_WIDTH), F32), jax.ShapeDtypeStruct((t, GDN_QKV), F32),
                   jax.ShapeDtypeStruct((t, GDN_HEADS * GDN_DV), F32), jax.ShapeDtypeStruct((t, LANES), F32)],
        compiler_params=_params(("parallel",)),
        name="proj_even",
    )(x, wu, wqkv, wz, wba)


def _proj_odd_kernel(x_ref, w_ref, cos_ref, sin_ref, o_ref):
    j = pl.program_id(0)
    acc = _dot(x_ref[...].astype(BF16), w_ref[...])

    @pl.when(j == 0)
    def _():
        cos, sin = cos_ref[...], sin_ref[...]
        half = RET_DK // 2
        for h in range(2 * RET_HEADS):
            x0 = acc[:, h * RET_DK:h * RET_DK + half]
            x1 = acc[:, h * RET_DK + half:(h + 1) * RET_DK]
            scale = 1.0 if h < RET_HEADS else RET_DK ** -0.5
            o_ref[:, h * RET_DK:h * RET_DK + half] = ((x0 * cos - x1 * sin) * scale).astype(o_ref.dtype)
            o_ref[:, h * RET_DK + half:(h + 1) * RET_DK] = ((x0 * sin + x1 * cos) * scale).astype(o_ref.dtype)

    @pl.when(j != 0)
    def _():
        o_ref[...] = acc.astype(o_ref.dtype)


def _proj_odd(x, w, cos, sin, tm):
    t = x.shape[0]
    nblk = w.shape[1] // 2048
    period = cos.shape[0] // tm
    return pl.pallas_call(
        _proj_odd_kernel,
        grid=(nblk, t // tm),
        in_specs=[pl.BlockSpec((tm, D_MODEL), lambda j, i: (i, 0)),
                  pl.BlockSpec((D_MODEL, 2048), lambda j, i: (0, j)),
                  pl.BlockSpec((tm, LANES), lambda j, i: (i % period, 0)),
                  pl.BlockSpec((tm, LANES), lambda j, i: (i % period, 0))],
        out_specs=pl.BlockSpec((tm, 2048), lambda j, i: (i, j)),
        out_shape=jax.ShapeDtypeStruct((t, w.shape[1]), BF16),
        compiler_params=_params(("parallel", "parallel")),
        name="proj_odd",
    )(x, w, cos, sin)


def _layer_norm(r, g, b):
    mu = jnp.mean(r, -1, keepdims=True)
    d = r - mu
    var = jnp.mean(d * d, -1, keepdims=True)
    return d * lax.rsqrt(var + LN_EPS) * g + b


S5_LANE_BLOCKS = S5_WIDTH // LANES
S5_TILES_PER_BLOCK = S5_TILES // S5_LANE_BLOCKS


def _s5_kernel(u_ref, h0_ref, bmat_ref, cmat_ref, acoef_ref, dskip_ref, wglu_ref, bglu_ref,
               y_ref, hout_ref, utm, sre, sim, ytm, hst, *, nb, lt):
    tb = pl.program_id(0)

    @pl.when(tb == 0)
    def _():
        hst[...] = h0_ref[...]

    for b in range(nb):
        for q in range(S5_LANE_BLOCKS):
            utm[q, pl.ds(b, lt, stride=nb), :] = u_ref[b, :, q * LANES:(q + 1) * LANES]

    def block_body(q, carry):
        bu = _dot(utm[q].astype(BF16), bmat_ref[q])
        for g in range(S5_TILES_PER_BLOCK):
            sre[g] = bu[:, (2 * g) * LANES:(2 * g + 1) * LANES]
            sim[g] = bu[:, (2 * g + 1) * LANES:(2 * g + 2) * LANES]
        js = [q * S5_TILES_PER_BLOCK + g for g in range(S5_TILES_PER_BLOCK)]
        acs = [acoef_ref[j] for j in js]
        hs = [(hst[j, :, 0:LANES], hst[j, :, LANES:2 * LANES]) for j in js]
        for t in range(lt):
            rows = slice(t * nb, (t + 1) * nb)
            for g in range(S5_TILES_PER_BLOCK):
                ar, ai = acs[g][:, :LANES], acs[g][:, LANES:]
                hr, hi = hs[g]
                nhr = ar * hr - ai * hi + sre[g, rows, :]
                nhi = ar * hi + ai * hr + sim[g, rows, :]
                sre[g, rows, :] = nhr
                sim[g, rows, :] = nhi
                hs[g] = (nhr, nhi)
        for g, j in enumerate(js):
            hst[j, :, 0:LANES] = hs[g][0]
            hst[j, :, LANES:2 * LANES] = hs[g][1]
        st = jnp.concatenate([part for g in range(S5_TILES_PER_BLOCK) for part in (sre[g], sim[g])], axis=-1)
        ytm[q] = _dot(st.astype(BF16), cmat_ref[q])
        return carry

    lax.fori_loop(0, S5_LANE_BLOCKS, block_body, 0)
    y = (jnp.concatenate([ytm[q] for q in range(S5_LANE_BLOCKS)], axis=-1)
         + dskip_ref[...] * jnp.concatenate([utm[q] for q in range(S5_LANE_BLOCKS)], axis=-1))
    y = jax.nn.gelu(y)
    y = y * _sigmoid(_dot(y.astype(BF16), wglu_ref[...]) + bglu_ref[...])
    for q in range(S5_LANE_BLOCKS):
        ytm[q] = y[:, q * LANES:(q + 1) * LANES]
    for b in range(nb):
        for q in range(S5_LANE_BLOCKS):
            y_ref[b, :, q * LANES:(q + 1) * LANES] = ytm[q, pl.ds(b, lt, stride=nb), :].astype(y_ref.dtype)
    hout_ref[...] = hst[...]


def _s5(u3, h0, bmat, cmat, acoef, dskip, wglu, bglu, lt):
    nb, L, _ = u3.shape
    rows = nb * lt
    tpb = S5_TILES_PER_BLOCK
    bmat = bmat.reshape(S5_LANE_BLOCKS, tpb, LANES, 2 * LANES).transpose(0, 2, 1, 3).reshape(
        S5_LANE_BLOCKS, LANES, tpb * 2 * LANES)
    cmat = cmat.reshape(S5_LANE_BLOCKS, tpb * 2 * LANES, LANES)
    acoef = jnp.broadcast_to(acoef[:, :1], (S5_TILES, nb, 2 * LANES))
    return pl.pallas_call(
        functools.partial(_s5_kernel, nb=nb, lt=lt),
        grid=(L // lt,),
        in_specs=[pl.BlockSpec((nb, lt, S5_WIDTH), lambda i: (0, i, 0)),
                  _full(h0.shape), _full(bmat.shape), _full(cmat.shape), _full(acoef.shape),
                  _full(dskip.shape), _full(wglu.shape), _full(bglu.shape)],
        out_specs=[pl.BlockSpec((nb, lt, S5_WIDTH), lambda i: (0, i, 0)), _full(h0.shape)],
        out_shape=[jax.ShapeDtypeStruct((nb, L, S5_WIDTH), BF16), jax.ShapeDtypeStruct(h0.shape, F32)],
        scratch_shapes=[pltpu.VMEM((S5_LANE_BLOCKS, rows, LANES), F32), pltpu.VMEM((tpb, rows, LANES), F32),
                        pltpu.VMEM((tpb, rows, LANES), F32), pltpu.VMEM((S5_LANE_BLOCKS, rows, LANES), F32),
                        pltpu.VMEM(h0.shape, F32)],
        compiler_params=_params(("arbitrary",)),
        name="s5_scan",
    )(u3, h0, bmat, cmat, acoef, dskip, wglu, bglu)


def _s5_weights(a_re, a_im, log_dt, b_re, b_im, c_re, c_im):
    dt = jnp.exp(log_dt)[:, None]
    lr, li = a_re * dt, a_im * dt
    mag = jnp.exp(lr)
    ab_re, ab_im = mag * jnp.cos(li), mag * jnp.sin(li)
    den = a_re * a_re + a_im * a_im
    cf_re = ((ab_re - 1.0) * a_re + ab_im * a_im) / den
    cf_im = (ab_im * a_re - (ab_re - 1.0) * a_im) / den
    bb_re = cf_re[..., None] * b_re - cf_im[..., None] * b_im
    bb_im = cf_re[..., None] * b_im + cf_im[..., None] * b_re
    jj = np.arange(S5_TILES)[:, None, None]
    lg = np.arange(8)[None, :, None]
    gi = np.arange(2)[None, None, :]
    sel = jnp.asarray((lg == 2 * (jj % 4) + gi).astype(np.float32))
    tiles = lambda w: w.reshape(S5_TILES, 2, *w.shape[1:])
    bt = lambda w: jnp.einsum('jlg,jgpn->jlngp', sel, tiles(w)).reshape(S5_TILES, LANES, LANES)
    bmat = jnp.concatenate([bt(bb_re), bt(bb_im)], axis=-1)
    ct = lambda w: jnp.einsum('jlg,jgnp->jgpln', sel, tiles(w)).reshape(S5_TILES, LANES, LANES)
    cmat = jnp.concatenate([ct(c_re), -ct(c_im)], axis=1)
    acoef = jnp.concatenate([ab_re.reshape(S5_TILES, LANES), ab_im.reshape(S5_TILES, LANES)], axis=-1)
    acoef = jnp.broadcast_to(acoef[:, None, :], (S5_TILES, SUBLANES, 2 * LANES))
    return bmat.astype(BF16), cmat.astype(BF16), acoef


def _s5_state_in(h_re, h_im):
    nb = h_re.shape[0]
    h = jnp.concatenate([h_re.reshape(nb, S5_TILES, LANES), h_im.reshape(nb, S5_TILES, LANES)], axis=-1)
    return jnp.transpose(h, (1, 0, 2))


def _s5_state_out(h):
    nb = h.shape[1]
    h = jnp.transpose(h, (1, 0, 2))
    return (h[..., :LANES].reshape(nb, S5_GROUPS, S5_STATE), h[..., LANES:].reshape(nb, S5_GROUPS, S5_STATE))


def _bdot(a, b):
    return lax.dot_general(a, b, (((2,), (1,)), ((0,), (0,))), preferred_element_type=F32)


def _bdot_nt(a, b):
    return lax.dot_general(a, b, (((2,), (2,)), ((0,), (0,))), preferred_element_type=F32)


def _bdot_bf16(a, b):
    return _bdot(a.astype(BF16), b.astype(BF16))


def _unit_lower_inverse(nmat):
    lc = nmat.shape[-1]
    ri = lax.broadcasted_iota(jnp.int32, nmat.shape, 1)
    ci = lax.broadcasted_iota(jnp.int32, nmat.shape, 2)
    base = 16
    dmat = jnp.where(ri // base == ci // base, nmat, 0.0)
    inv = jnp.where(ri == ci, 1.0, 0.0) - dmat
    pw = dmat
    for _ in range(3):
        pw = _bdot_bf16(pw, pw)
        inv = inv + _bdot_bf16(inv, pw)
    size = base
    while size < lc:
        off = jnp.where(ri // (2 * size) == ci // (2 * size), jnp.where(ri // size > ci // size, nmat, 0.0), 0.0)
        inv = inv - _bdot_bf16(_bdot_bf16(inv, off), inv)
        size *= 2
    return inv


def _gdn_local_kernel(qkv_ref, ba_ref, ctx_ref, cw_ref, p1_ref, p2_ref, ltri_ref,
                      u0_ref, w_ref, qd_ref, kd_ref, attn_ref, g_ref, cout_ref, xpad, *, lc, cp):
    c = pl.program_id(1)
    rb = lc * cp

    @pl.when(c == 0)
    def _():
        xpad[0:SUBLANES, :] = ctx_ref[0]

    xpad[SUBLANES:SUBLANES + rb, :] = qkv_ref[0]
    cw = cw_ref[...]
    conv = (cw[3:4] * xpad[8:8 + rb, :] + cw[2:3] * xpad[7:7 + rb, :]
            + cw[1:2] * xpad[6:6 + rb, :] + cw[0:1] * xpad[5:5 + rb, :])
    tail = xpad[rb:rb + SUBLANES, :]
    xpad[0:SUBLANES, :] = tail
    cout_ref[0] = tail
    a = conv * _sigmoid(conv)

    ba = ba_ref[0]
    beta_all = _sigmoid(ba)
    sp_in = ba + p2_ref[...]
    softplus = jnp.maximum(sp_in, 0.0) + jnp.log(1.0 + jnp.exp(-jnp.abs(sp_in)))
    g_all = p1_ref[...] * softplus
    g_hi = g_all.astype(BF16)
    g_r = g_all - g_hi.astype(F32)
    g_mid = g_r.astype(BF16)
    g_lo = (g_r - g_mid.astype(F32)).astype(BF16)
    lt = ltri_ref[...]
    G = _dot(lt, g_hi) + _dot(lt, g_mid) + _dot(lt, g_lo)
    g_ref[0] = G
    GT = G.T
    pairs = [(h, cc) for h in range(GDN_HEADS) for cc in range(cp)]
    qs, ks, vs, betas, gcols, grows, glasts = [], [], [], [], [], [], []
    for h in range(GDN_HEADS):
        qa = a[:, h * GDN_DK:(h + 1) * GDN_DK]
        ka = a[:, (GDN_HEADS + h) * GDN_DK:(GDN_HEADS + h + 1) * GDN_DK]
        va = a[:, (2 * GDN_HEADS + h) * GDN_DK:(2 * GDN_HEADS + h + 1) * GDN_DK]
        qa = qa * lax.rsqrt(jnp.sum(qa * qa, -1, keepdims=True) + NORM_EPS) * (GDN_DK ** -0.5)
        ka = ka * lax.rsqrt(jnp.sum(ka * ka, -1, keepdims=True) + NORM_EPS)
        for cc in range(cp):
            rows = slice(cc * lc, (cc + 1) * lc)
            qs.append(qa[rows])
            ks.append(ka[rows])
            vs.append(va[rows])
            betas.append(beta_all[rows, h:h + 1])
            gcols.append(G[rows, GDN_HEADS + h:GDN_HEADS + h + 1])
            grows.append(GT[GDN_HEADS + h:GDN_HEADS + h + 1, cc * lc:(cc + 1) * lc])
            glasts.append(GT[GDN_HEADS + h:GDN_HEADS + h + 1, (cc + 1) * lc - 1:(cc + 1) * lc])
    q3, k3, v3 = jnp.stack(qs), jnp.stack(ks), jnp.stack(vs)
    beta3, gcol3 = jnp.stack(betas), jnp.stack(gcols)
    grow3, glast3 = jnp.stack(grows), jnp.stack(glasts)
    shape3 = (len(pairs), lc, lc)
    ri = lax.broadcasted_iota(jnp.int32, shape3, 1)
    ci = lax.broadcasted_iota(jnp.int32, shape3, 2)
    incl = ri >= ci
    dec3 = jnp.where(incl, jnp.exp(jnp.where(incl, gcol3 - grow3, 0.0)), 0.0)
    eg3 = jnp.exp(gcol3)
    kb3 = k3 * beta3
    kbf3 = k3.astype(BF16)
    nmat3 = jnp.where(ri > ci, _bdot_nt(kb3.astype(BF16), kbf3) * dec3, 0.0)
    inv3 = _unit_lower_inverse(nmat3)
    sol3 = _bdot_bf16(inv3, jnp.concatenate([v3 * beta3, kb3 * eg3], axis=-1))
    w3 = sol3[:, :, GDN_DV:].astype(BF16)
    qd3 = (q3 * eg3).astype(BF16)
    kd3 = (k3 * jnp.exp(glast3 - gcol3)).astype(BF16)
    attn3 = (_bdot_nt(q3.astype(BF16), kbf3) * dec3).astype(BF16)
    for i, (h, cc) in enumerate(pairs):
        rows = slice(cc * lc, (cc + 1) * lc)
        cols = slice(h * GDN_DV, (h + 1) * GDN_DV)
        u0_ref[0, rows, cols] = sol3[i, :, :GDN_DV]
        w_ref[0, rows, cols] = w3[i]
        qd_ref[0, rows, cols] = qd3[i]
        kd_ref[0, rows, cols] = kd3[i]
        attn_ref[0, rows, h * lc:(h + 1) * lc] = attn3[i]


def _gdn_seq_kernel(u0_ref, w_ref, qd_ref, kd_ref, attn_ref, g_ref, z_ref, s0_ref, nw_ref,
                    y_ref, sout_ref, S, *, lc, nbb):
    c = pl.program_id(1)

    @pl.when(c == 0)
    def _():
        S[...] = s0_ref[...]

    nw = nw_ref[...]
    pairs = [(bb, h) for bb in range(nbb) for h in range(GDN_HEADS)]
    hcols = lambda h: slice(h * GDN_DV, (h + 1) * GDN_DV)
    stack = lambda f: jnp.stack([f(bb, h) for bb, h in pairs])
    dlast = jnp.exp(g_ref[:, lc - 1:lc, :])
    S3 = S[...].reshape(len(pairs), GDN_DK, GDN_DV)
    wq3 = stack(lambda bb, h: jnp.concatenate([w_ref[bb, :, hcols(h)], qd_ref[bb, :, hcols(h)]], axis=0))
    r3 = _bdot(wq3, S3.astype(BF16))
    ub3 = (stack(lambda bb, h: u0_ref[bb, :, hcols(h)]) - r3[:, :lc]).astype(BF16)
    o3 = r3[:, lc:] + _bdot(stack(lambda bb, h: attn_ref[bb, :, h * lc:(h + 1) * lc]), ub3)
    d3 = stack(lambda bb, h: dlast[bb, :, GDN_HEADS + h:GDN_HEADS + h + 1])
    kd3 = stack(lambda bb, h: kd_ref[bb, :, hcols(h)])
    kdu3 = lax.dot_general(kd3, ub3, (((1,), (1,)), ((0,), (0,))), preferred_element_type=F32)
    S[...] = (d3 * S3 + kdu3).reshape(S.shape)
    o3 = o3 * lax.rsqrt(jnp.mean(o3 * o3, -1, keepdims=True) + NORM_EPS) * nw
    for i, (bb, h) in enumerate(pairs):
        zh = z_ref[bb, :, hcols(h)]
        y_ref[bb, :, hcols(h)] = (o3[i] * (zh * _sigmoid(zh))).astype(y_ref.dtype)

    @pl.when(c == pl.num_programs(1) - 1)
    def _():
        sout_ref[...] = S[...]


def _gdn(qkv3, z3, ba3, ctx8, s0, cw, p1, p2, nw, lc, cp, nbb):
    nb, L, _ = qkv3.shape
    rb = lc * cp
    hd = GDN_HEADS * GDN_DV
    ltri = jnp.asarray(np.kron(np.eye(cp, dtype=np.float32), np.tril(np.ones((lc, lc), np.float32)))).astype(BF16)
    blk = lambda n: pl.BlockSpec((1, rb, n), lambda b, c: (b, c, 0))
    per_b = lambda shape: pl.BlockSpec((1,) + shape, lambda b, c: (b,) + (0,) * len(shape))
    cst = lambda shape: pl.BlockSpec(shape, lambda b, c: (0,) * len(shape))
    sds = lambda n, dt: jax.ShapeDtypeStruct((nb, L, n), dt)
    u0, w, qd, kd, attn, G, cout = pl.pallas_call(
        functools.partial(_gdn_local_kernel, lc=lc, cp=cp),
        grid=(nb, L // rb),
        in_specs=[blk(GDN_QKV), blk(LANES), per_b((SUBLANES, GDN_QKV)), cst(cw.shape), cst(p1.shape),
                  cst(p2.shape), cst(ltri.shape)],
        out_specs=[blk(hd), blk(hd), blk(hd), blk(hd), blk(GDN_HEADS * lc), blk(LANES),
                   per_b((SUBLANES, GDN_QKV))],
        out_shape=[sds(hd, F32), sds(hd, BF16), sds(hd, BF16), sds(hd, BF16), sds(GDN_HEADS * lc, BF16),
                   sds(LANES, F32), jax.ShapeDtypeStruct((nb, SUBLANES, GDN_QKV), F32)],
        scratch_shapes=[pltpu.VMEM((rb + SUBLANES, GDN_QKV), F32)],
        compiler_params=_params(("parallel", "arbitrary")),
        name="gdn_local",
    )(qkv3, ba3, ctx8, cw, p1, p2, ltri)
    sblk = lambda n: pl.BlockSpec((nbb, lc, n), lambda b, c: (b, c, 0))
    state = pl.BlockSpec((nbb, GDN_HEADS, GDN_DK, GDN_DV), lambda b, c: (b, 0, 0, 0))
    y, s_new = pl.pallas_call(
        functools.partial(_gdn_seq_kernel, lc=lc, nbb=nbb),
        grid=(nb // nbb, L // lc),
        in_specs=[sblk(hd), sblk(hd), sblk(hd), sblk(hd), sblk(GDN_HEADS * lc), sblk(LANES), sblk(hd),
                  state, cst(nw.shape)],
        out_specs=[sblk(hd), state],
        out_shape=[sds(hd, BF16), jax.ShapeDtypeStruct(s0.shape, F32)],
        scratch_shapes=[pltpu.VMEM((nbb, GDN_HEADS, GDN_DK, GDN_DV), F32)],
        compiler_params=_params(("parallel", "arbitrary")),
        name="gdn_seq",
    )(u0, w, qd, kd, attn, G, z3, s0, nw)
    return y, s_new, cout


def _ret_kernel(q_ref, k_ref, v_ref, g_ref, r0_ref, dec_ref, qs_ref, ks_ref, cd_ref, o_ref, rout_ref, R,
                *, nbb):
    c = pl.program_id(1)

    @pl.when(c == 0)
    def _():
        R[...] = r0_ref[...]

    pairs = [(bb, h) for bb in range(nbb) for h in range(RET_HEADS)]
    stack = lambda f: jnp.stack([f(bb, h) for bb, h in pairs])
    kcols = lambda h: slice(h * RET_DK, (h + 1) * RET_DK)
    vcols = lambda h: slice(h * RET_DV, (h + 1) * RET_DV)
    q3 = stack(lambda bb, h: q_ref[bb, :, kcols(h)])
    k3 = stack(lambda bb, h: k_ref[bb, :, kcols(h)])
    v3 = stack(lambda bb, h: v_ref[bb, :, vcols(h)])
    dec3 = stack(lambda bb, h: dec_ref[h])
    qs3 = stack(lambda bb, h: qs_ref[h])
    ks3 = stack(lambda bb, h: ks_ref[h])
    cd3 = stack(lambda bb, h: cd_ref[h])
    R3 = R[...].reshape(len(pairs), RET_DK, RET_DV)
    s3 = _bdot_nt(q3, k3) * dec3
    o3 = _bdot(s3.astype(BF16), v3) + _bdot(q3, R3.astype(BF16)) * qs3
    kv3 = lax.dot_general((k3.astype(F32) * ks3).astype(BF16), v3, (((1,), (1,)), ((0,), (0,))),
                          preferred_element_type=F32)
    R[...] = (cd3 * R3 + kv3).reshape(R.shape)
    mu = jnp.mean(o3, -1, keepdims=True)
    d3 = o3 - mu
    var = jnp.mean(d3 * d3, -1, keepdims=True)
    on3 = d3 * lax.rsqrt(var + LN_EPS)
    for i, (bb, h) in enumerate(pairs):
        gt = g_ref[bb, :, vcols(h)].astype(F32)
        o_ref[bb, :, vcols(h)] = (gt * _sigmoid(gt) * on3[i]).astype(o_ref.dtype)

    @pl.when(c == pl.num_programs(1) - 1)
    def _():
        rout_ref[...] = R[...]


def _retention(proj3, r0, chunk, nbb, nc):
    nb, L, _ = proj3.shape
    lc = chunk * nc
    log_g = np.log(1.0 - 2.0 ** (-5.0 - np.arange(RET_HEADS, dtype=np.float64)))
    idx = np.arange(lc, dtype=np.float64)
    dist = idx[:, None] - idx[None, :]
    which = (idx // chunk)[:, None] - (idx // chunk)[None, :]
    dec = np.where(which >= 0, np.exp(log_g[:, None, None] * np.abs(dist)), 0.0).astype(np.float32)
    qs = np.exp(log_g[:, None] * (idx + 1.0)).astype(np.float32)[..., None]
    ks = np.exp(log_g[:, None] * (lc - 1.0 - idx)).astype(np.float32)[..., None]
    cdec = np.exp(log_g * lc).astype(np.float32)[:, None, None]
    nqk = RET_HEADS * RET_DK
    nv = RET_HEADS * RET_DV
    cst = lambda shape: pl.BlockSpec(shape, lambda b, c: (0,) * len(shape))
    state = pl.BlockSpec((nbb, RET_HEADS, RET_DK, RET_DV), lambda b, c: (b, 0, 0, 0))
    return pl.pallas_call(
        functools.partial(_ret_kernel, nbb=nbb),
        grid=(nb // nbb, L // lc),
        in_specs=[pl.BlockSpec((nbb, lc, nqk), lambda b, c: (b, c, 0)),
                  pl.BlockSpec((nbb, lc, nqk), lambda b, c: (b, c, 1)),
                  pl.BlockSpec((nbb, lc, nv), lambda b, c: (b, c, 1)),
                  pl.BlockSpec((nbb, lc, nv), lambda b, c: (b, c, 2)),
                  state, cst(dec.shape), cst(qs.shape), cst(ks.shape), cst(cdec.shape)],
        out_specs=[pl.BlockSpec((nbb, lc, nv), lambda b, c: (b, c, 0)), state],
        out_shape=[jax.ShapeDtypeStruct((nb, L, nv), BF16), jax.ShapeDtypeStruct(r0.shape, F32)],
        scratch_shapes=[pltpu.VMEM((nbb, RET_HEADS, RET_DK, RET_DV), F32)],
        compiler_params=_params(("parallel", "arbitrary")),
        name="retention",
    )(proj3, proj3, proj3, proj3, r0, jnp.asarray(dec), jnp.asarray(qs), jnp.asarray(ks), jnp.asarray(cdec))


RUN_ALIGN = SUBLANES
RUN_PIECE = 64
MOE_VMEM_LIMIT = 58 * 1024 * 1024


def _moe_params(sem):
    return pltpu.CompilerParams(dimension_semantics=sem, vmem_limit_bytes=MOE_VMEM_LIMIT)


def _tile_cap(tm):
    rows = TOP_K * tm + N_EXPERTS * (RUN_ALIGN - 1)
    return -(-rows // LANES) * LANES


def _route_tile(x, rw_ref, rb_ref, lst_ref, ust_ref, pos_ref, gt_ref, cnt_ref, xs_ref):
    tm = x.shape[0]
    cap = xs_ref.shape[1]
    lane = lax.broadcasted_iota(jnp.int32, (tm, LANES), 1)
    lane_f = lane.astype(F32)
    xh = x.astype(BF16)
    logits = _dot(xh, rw_ref[...].astype(BF16)) + rb_ref[...]
    logits = jnp.where(lane < N_EXPERTS, logits, -jnp.inf)
    vals, hots = [], []
    for _ in range(TOP_K):
        m = jnp.max(logits, -1, keepdims=True)
        first = jnp.min(jnp.where(logits == m, lane_f, float(LANES)), -1, keepdims=True)
        hot = lane_f == first
        vals.append(m)
        hots.append(hot)
        logits = jnp.where(hot, -jnp.inf, logits)
    es = [jnp.exp(v - vals[0]) for v in vals]
    den = es[0] + es[1] + es[2] + es[3]
    multi = jnp.zeros((tm, LANES), F32)
    for hot in hots:
        multi = multi + hot.astype(F32)
    counts = jnp.sum(multi, 0, keepdims=True)
    units = jnp.floor((counts + (RUN_ALIGN - 1)) * (1.0 / RUN_ALIGN))
    offs = _dot(jnp.broadcast_to(units, (SUBLANES, LANES)).astype(BF16), ust_ref[...])[0:1] * float(RUN_ALIGN)
    before = _dot(lst_ref[...], multi.astype(BF16))
    slot = offs + before
    pos = jnp.zeros((tm, LANES), F32)
    gt = jnp.zeros((tm, LANES), F32)
    for kk in range(TOP_K):
        pos = jnp.where(lane == kk, jnp.sum(jnp.where(hots[kk], slot, 0.0), -1, keepdims=True), pos)
        gt = jnp.where(lane == kk, es[kk] / den, gt)
    pos = pos.astype(jnp.int32)
    pos_ref[...] = pos
    gt_ref[...] = gt
    cnt_ref[0] = counts
    pos_t = pos.T[0:2 * SUBLANES].astype(jnp.int16)
    row = lax.broadcasted_iota(jnp.int16, (cap, tm), 0)
    sel = jnp.zeros((cap, tm), BF16)
    for kk in range(TOP_K):
        sel = sel + jnp.where(row == pos_t[kk:kk + 1, :], jnp.ones((), BF16), jnp.zeros((), BF16))
    xs_ref[0] = _dot(sel, xh)


def _outproj_route_kernel(*refs, n_in):
    a_refs = refs[:n_in]
    w_refs = refs[n_in:2 * n_in]
    x_ref, g_ref, b_ref = refs[2 * n_in:2 * n_in + 3]
    route_in = refs[2 * n_in + 3:2 * n_in + 7]
    o_ref = refs[2 * n_in + 7]
    route_out = refs[2 * n_in + 8:]
    acc = _dot(a_refs[0][...], w_refs[0][...])
    for a_ref, w_ref in zip(a_refs[1:], w_refs[1:]):
        acc = acc + _dot(a_ref[...], w_ref[...])
    x1 = _layer_norm(DEEPNORM_ALPHA * x_ref[...] + acc, g_ref[...], b_ref[...])
    o_ref[...] = x1
    _route_tile(x1, *route_in, *route_out)


def _outproj_route(acts, ws, x, g, b, rw, rb, tm):
    t = x.shape[0]
    nt = t // tm
    cap = _tile_cap(tm)
    lst = jnp.asarray(np.tril(np.ones((tm, tm), np.float32), -1)).astype(BF16)
    ust = jnp.asarray(np.triu(np.ones((LANES, LANES), np.float32), 1)).astype(BF16)
    row = lambda n: pl.BlockSpec((tm, n), lambda i: (i, 0))
    x1, pos, gt, cnt, xs = pl.pallas_call(
        functools.partial(_outproj_route_kernel, n_in=len(acts)),
        grid=(nt,),
        in_specs=[row(a.shape[1]) for a in acts] + [_full(w.shape) for w in ws]
                 + [row(D_MODEL), _full((1, D_MODEL)), _full((1, D_MODEL)),
                    _full(rw.shape), _full(rb.shape), _full(lst.shape), _full(ust.shape)],
        out_specs=[row(D_MODEL), row(LANES), row(LANES), pl.BlockSpec((1, 1, LANES), lambda i: (i, 0, 0)),
                   pl.BlockSpec((1, cap, D_MODEL), lambda i: (i, 0, 0))],
        out_shape=[jax.ShapeDtypeStruct((t, D_MODEL), F32),
                   jax.ShapeDtypeStruct((t, LANES), jnp.int32), jax.ShapeDtypeStruct((t, LANES), F32),
                   jax.ShapeDtypeStruct((nt, 1, LANES), F32), jax.ShapeDtypeStruct((nt, cap, D_MODEL), F32)],
        compiler_params=_moe_params(("parallel",)),
        name="outproj_route",
    )(*acts, *ws, x, g, b, rw, rb, lst, ust)
    return x1, (pos, gt, cnt, xs)


def _expert_kernel(be_ref, nu_ref, ilo_ref, rows_ref, gs_ref, n8_ref, lo_ref, *refs, blk, tiles):
    ng = len(tiles)
    xs_hbms = refs[:ng]
    w1_ref, b1_ref, w2_ref, b2_ref = refs[ng:ng + 4]
    ys_hbms = refs[ng + 4:2 * ng + 4]
    xbuf, ybuf, w1b, w2b, in_sem, out_sem = refs[2 * ng + 4:]
    nt = sum(tiles)
    firsts = [sum(tiles[:g]) for g in range(ng)]
    j = pl.program_id(0)
    nu = nu_ref[0]

    def for_each_run(jb, fn):
        e = be_ref[jb]
        base = jb * blk
        for g in range(ng):
            end = firsts[g] + tiles[g]

            def cond(i, end=end):
                return (i < end) & (gs_ref[e * nt + jnp.minimum(i, nt - 1)] < base + blk)

            def body(i, g=g):
                g0 = gs_ref[e * nt + i]
                first = jnp.maximum(g0, base)
                last = jnp.minimum(g0 + n8_ref[e * nt + i], base + blk)
                fn(g, i - firsts[g], lo_ref[e * nt + i] + (first - g0), first - base, last - first)
                return i + 1

            lax.while_loop(cond, body, jnp.clip(ilo_ref[jb], firsts[g], end))

    def pieces(length, fn):
        def digits(sizes):
            for size in sizes:
                @pl.when((length & size) != 0)
                def _(size=size):
                    fn(length & ~(2 * size - 1), size)

        sizes = [blk >> k for k in range(blk.bit_length()) if blk >> k >= RUN_ALIGN]

        @pl.when(length > 2 * RUN_PIECE - 1)
        def _():
            digits([z for z in sizes if z > RUN_PIECE])

        digits([z for z in sizes if z <= RUN_PIECE])

    def aligned(v, size):
        return pl.ds(pl.multiple_of(v, RUN_ALIGN), size)

    def copy_in(jb, slot):
        def run(g, ig, src, dst, length):
            def piece(off, size):
                pltpu.make_async_copy(xs_hbms[g].at[ig, aligned(src + off, size)],
                                      xbuf.at[slot, aligned(dst + off, size)], in_sem.at[slot]).start()
            pieces(length, piece)
        for_each_run(jb, run)

    def copy_out(jb, slot):
        def run(g, ig, src, dst, length):
            def piece(off, size):
                pltpu.make_async_copy(ybuf.at[slot, aligned(dst + off, size)],
                                      ys_hbms[g].at[ig, aligned(src + off, size)], out_sem.at[slot]).start()
            pieces(length, piece)
        for_each_run(jb, run)

    def wait_rows(sem, nrows):
        size = blk
        while size >= RUN_ALIGN:
            @pl.when((nrows & size) != 0)
            def _(size=size):
                pltpu.make_async_copy(xbuf.at[1, pl.ds(0, size)], xbuf.at[0, pl.ds(0, size)], sem).wait()
            size //= 2

    @pl.when(j < nu)
    def _():
        slot = j % 2

        @pl.when(j == 0)
        def _():
            xbuf[...] = jnp.zeros_like(xbuf)
            copy_in(0, 0)

        wait_rows(in_sem.at[slot], rows_ref[j])

        @pl.when(j + 1 < nu)
        def _():
            copy_in(j + 1, 1 - slot)

        @pl.when(j >= 2)
        def _():
            wait_rows(out_sem.at[slot], rows_ref[jnp.maximum(j - 2, 0)])

        @pl.when((j == 0) | (be_ref[j] != be_ref[jnp.maximum(j - 1, 0)]))
        def _():
            w1b[...] = w1_ref[0, 0].astype(BF16)
            w2b[...] = w2_ref[0, 0].astype(BF16)

        h = _dot(xbuf[slot].astype(BF16), w1b[...]) + b1_ref[0]
        glu = jnp.minimum(h[:, :D_FF], SWIGLU_LIMIT)
        lin = jnp.clip(h[:, D_FF:], -SWIGLU_LIMIT, SWIGLU_LIMIT)
        act = glu * _sigmoid(SWIGLU_ALPHA * glu) * (lin + 1.0)
        ybuf[slot] = _dot(act.astype(BF16), w2b[...]) + b2_ref[0]
        copy_out(j, slot)

        @pl.when(j == nu - 1)
        def _():
            wait_rows(out_sem.at[slot], rows_ref[j])

            @pl.when(j >= 1)
            def _():
                wait_rows(out_sem.at[1 - slot], rows_ref[jnp.maximum(j - 1, 0)])


def _experts(tables, xs_list, w1, b1, w2, b2, layer, blk):
    ng = len(xs_list)
    n_blk = tables[0].shape[0]
    wspec = lambda shape: pl.BlockSpec((1, 1) + shape, lambda j, be, *_: (layer, be[j], 0, 0))
    bspec = lambda n: pl.BlockSpec((1, 1, n), lambda j, be, *_: (be[j], 0, 0))
    hbm = pl.BlockSpec(memory_space=pl.ANY)
    grid_spec = pltpu.PrefetchScalarGridSpec(
        num_scalar_prefetch=len(tables),
        grid=(n_blk,),
        in_specs=[hbm] * ng + [wspec((D_MODEL, 2 * D_FF)), bspec(2 * D_FF), wspec((D_FF, D_MODEL)), bspec(D_MODEL)],
        out_specs=[hbm] * ng,
        scratch_shapes=[pltpu.VMEM((2, blk, D_MODEL), F32), pltpu.VMEM((2, blk, D_MODEL), F32),
                        pltpu.VMEM((D_MODEL, 2 * D_FF), BF16), pltpu.VMEM((D_FF, D_MODEL), BF16),
                        pltpu.SemaphoreType.DMA((2,)), pltpu.SemaphoreType.DMA((2,))],
    )
    return pl.pallas_call(
        functools.partial(_expert_kernel, blk=blk, tiles=tuple(xs.shape[0] for xs in xs_list)),
        grid_spec=grid_spec,
        out_shape=[jax.ShapeDtypeStruct(xs.shape, F32) for xs in xs_list],
        input_output_aliases={len(tables) + g: g for g in range(ng)},
        compiler_params=_moe_params(("arbitrary",)),
        name="moe_experts",
    )(*tables, *xs_list, w1, b1, w2, b2)


def _combine_kernel(ys_ref, pos_ref, gt_ref, x_ref, p_ref, g_ref, b_ref, plew_ref, gatew_ref, o_ref):
    tm = x_ref.shape[0]
    cap = ys_ref.shape[1]
    pos = pos_ref[...].astype(jnp.int16)
    gt = gt_ref[...].astype(BF16)
    col = lax.broadcasted_iota(jnp.int16, (tm, cap), 1)
    sel = jnp.zeros((tm, cap), BF16)
    for kk in range(TOP_K):
        sel = sel + jnp.where(col == pos[:, kk:kk + 1], gt[:, kk:kk + 1], jnp.zeros((), BF16))
    y = _dot(sel, ys_ref[0].astype(BF16))
    x2 = _layer_norm(DEEPNORM_ALPHA * x_ref[...] + y, g_ref[...], b_ref[...])
    pp = _dot(p_ref[...].astype(BF16), plew_ref[...])
    gg = _sigmoid(_dot(x2.astype(BF16), gatew_ref[...]))
    o_ref[...] = x2 + pp * gg


def _combine(ys, pos, gt, x, p, layer, g, b, plew, gatew, tm):
    t = x.shape[0]
    cap = ys.shape[1]
    row = lambda n: pl.BlockSpec((tm, n), lambda i: (i, 0))
    return pl.pallas_call(
        _combine_kernel,
        grid=(t // tm,),
        in_specs=[pl.BlockSpec((1, cap, D_MODEL), lambda i: (i, 0, 0)), row(LANES), row(LANES),
                  row(D_MODEL), pl.BlockSpec((None, tm, PLE_DIM), lambda i: (layer, i, 0)),
                  _full((1, D_MODEL)), _full((1, D_MODEL)),
                  _full(plew.shape), _full(gatew.shape)],
        out_specs=row(D_MODEL),
        out_shape=jax.ShapeDtypeStruct((t, D_MODEL), F32),
        compiler_params=_moe_params(("parallel",)),
        name="moe_combine",
    )(ys, pos, gt, x, p, g, b, plew, gatew)


def _moe_tables(cnt, t, blk):
    nt = cnt.shape[0]
    n = cnt[:, 0, :N_EXPERTS].astype(jnp.int32)
    n8 = (n + RUN_ALIGN - 1) // RUN_ALIGN * RUN_ALIGN
    lo = jnp.cumsum(n8, axis=1) - n8
    rows_e = jnp.sum(n8, axis=0)
    padded = (rows_e + blk - 1) // blk * blk
    pend = jnp.cumsum(padded)
    pstart = pend - padded
    gstart = pstart[None, :] + jnp.cumsum(n8, axis=0) - n8
    n_blk = -(-(TOP_K * t + nt * N_EXPERTS * (RUN_ALIGN - 1) + N_EXPERTS * (blk - 1)) // blk)
    n_used = (pend[-1] // blk).astype(jnp.int32).reshape(1)
    blk_start = jnp.arange(n_blk, dtype=jnp.int32) * blk
    blk_e = jnp.minimum(jnp.sum((pend[None, :] <= blk_start[:, None]).astype(jnp.int32), axis=1), N_EXPERTS - 1)
    run_end = (gstart + n8)[:, blk_e]
    ilo = jnp.sum((run_end <= blk_start[None, :]).astype(jnp.int32), axis=0)
    rows_b = jnp.clip((pstart + rows_e)[blk_e] - blk_start, 0, blk)
    flat = lambda a: a.T.reshape(-1).astype(jnp.int32)
    i32 = lambda a: a.astype(jnp.int32)
    return (i32(blk_e), n_used, i32(ilo), i32(rows_b), flat(gstart), flat(n8), flat(lo))


def _moe_ple(xs_in, routed, ps, w1, b1, w2, b2, layer, g, b, plew, gatew, tms, blk):
    cnt = jnp.concatenate([r[2] for r in routed], axis=0)
    tables = _moe_tables(cnt, sum(x.shape[0] for x in xs_in), blk)
    ys = _experts(tables, [r[3] for r in routed], w1, b1, w2, b2, layer, blk)
    return [_combine(y, r[0], r[1], x, p, layer, g, b, plew, gatew, tm)
            for y, r, x, p, tm in zip(ys, routed, xs_in, ps, tms)]


_ROT_PERM = np.concatenate([np.arange(0, RET_DK, 2), np.arange(1, RET_DK, 2)])


def _lane_row(vals, offset):
    row = jnp.zeros((1, LANES), F32)
    return row.at[0, offset:offset + vals.shape[0]].set(vals)


def _even_mixer(x, nb, L, s5_re, s5_im, gdn_s, conv_s, W, tm):
    t = nb * L
    lc = L if L <= CHUNK else CHUNK
    u, qkv, z, ba = _proj_even(x, W['wu'], W['wqkv'], W['wz'], W['wba'], tm)
    yA3, h_new = _s5(u.reshape(nb, L, S5_WIDTH), _s5_state_in(s5_re[0].astype(F32), s5_im[0].astype(F32)),
                     W['bmat'], W['cmat'], W['acoef'], W['dskip'], W['wglu'], W['bglu'], lc)
    new_re, new_im = _s5_state_out(h_new)
    ctx8 = jnp.concatenate([jnp.zeros((nb, SUBLANES - (GDN_CONV - 1), GDN_QKV), F32), conv_s[0].astype(F32)], axis=1)
    yB3, new_gdn, cout = _gdn(qkv.reshape(nb, L, GDN_QKV), z.reshape(nb, L, -1), ba.reshape(nb, L, LANES),
                              ctx8, gdn_s[0].astype(F32), W['convw'], W['p1'], W['p2'], W['normw'], lc,
                              cp=min(4, L // lc), nbb=4)
    new_conv = cout[:, SUBLANES - (GDN_CONV - 1):, :]
    x, routed = _outproj_route([yA3.reshape(t, -1), yB3.reshape(t, -1)], [W['wout_a'], W['wout_b']], x,
                               W['ln1_g'][0], W['ln1_b'][0], W['rw'][0], W['rb'][0], tm)
    return x, routed, new_re[None], new_im[None], new_gdn[None], new_conv[None]


def _odd_mixer(x, nb, L, ret_s, pos0, W, tm):
    t = nb * L
    lc = L if L <= CHUNK else CHUNK
    pos = pos0 + jnp.arange(L, dtype=F32)
    freq = 1.0 / (ROPE_BASE ** jnp.linspace(0.0, 1.0, RET_DK // 2, dtype=F32))
    ang = pos[:, None] * freq[None]
    reps = max(1, tm // L)
    cos = jnp.tile(jnp.cos(ang), (reps, 1))
    sin = jnp.tile(jnp.sin(ang), (reps, 1))
    proj = _proj_odd(x, W['win_odd'], cos, sin, tm)
    half = RET_DK // 2
    r0 = ret_s[0].astype(F32).reshape(nb, RET_HEADS, half, 2, RET_DV).swapaxes(2, 3).reshape(
        nb, RET_HEADS, RET_DK, RET_DV)
    o3, r_new = _retention(proj.reshape(nb, L, -1), r0, lc, nbb=2, nc=min(4, L // lc))
    new_ret = r_new.reshape(nb, RET_HEADS, 2, half, RET_DV).swapaxes(2, 3).reshape(nb, RET_HEADS, RET_DK, RET_DV)
    x, routed = _outproj_route([o3.reshape(t, -1)], [W['wout_odd']], x, W['ln1_g'][1], W['ln1_b'][1],
                               W['rw'][1], W['rb'][1], tm)
    return x, routed, new_ret[None]


def kernel(x_prompt, x_sample, state_s5_re, state_s5_im, state_gdn, state_gdn_conv, state_ret, p_prompt, p_sample, w_in_even, s5_a_re, s5_a_im, s5_log_dt, s5_b_re, s5_b_im, s5_c_re, s5_c_im, s5_d, s5_w_glu, s5_b_glu, gdn_conv_w, gdn_a_log, gdn_dt_bias, gdn_norm_w, w_out_even, w_in_odd, w_out_odd, ln1_g, ln1_b, ln2_g, ln2_b, router_w, router_b, moe_w1, moe_b1, moe_w2, moe_b2, ple_w, ple_gate_w):
    o1 = S5_WIDTH
    o2 = o1 + GDN_QKV
    o3 = o2 + GDN_HEADS * GDN_DV
    win = w_in_even[0]
    bmat, cmat, acoef = _s5_weights(s5_a_re[0], s5_a_im[0], s5_log_dt[0], s5_b_re[0], s5_b_im[0],
                                    s5_c_re[0], s5_c_im[0])
    wodd = w_in_odd[0]
    nk = RET_HEADS * RET_DK
    perm_cols = lambda w: w.reshape(D_MODEL, RET_HEADS, RET_DK)[:, :, _ROT_PERM].reshape(D_MODEL, nk)
    W = dict(
        wu=win[:, :o1].astype(BF16), wqkv=win[:, o1:o2].astype(BF16), wz=win[:, o2:o3].astype(BF16),
        wba=jnp.pad(win[:, o3:], ((0, 0), (0, LANES - 2 * GDN_HEADS))).astype(BF16),
        bmat=bmat, cmat=cmat, acoef=acoef, dskip=s5_d[0][None], wglu=s5_w_glu[0].astype(BF16),
        bglu=s5_b_glu[0][None], convw=gdn_conv_w[0],
        p1=_lane_row(-jnp.exp(gdn_a_log[0]), GDN_HEADS), p2=_lane_row(gdn_dt_bias[0], GDN_HEADS),
        normw=gdn_norm_w[0][None],
        wout_a=w_out_even[0][:S5_WIDTH].astype(BF16), wout_b=w_out_even[0][S5_WIDTH:].astype(BF16),
        win_odd=jnp.concatenate([perm_cols(wodd[:, :nk]), perm_cols(wodd[:, nk:2 * nk]), wodd[:, 2 * nk:]],
                                axis=1).astype(BF16),
        wout_odd=w_out_odd[0].astype(BF16),
        ln1_g=ln1_g[:, None], ln1_b=ln1_b[:, None], ln2_g=ln2_g[:, None], ln2_b=ln2_b[:, None],
        rw=jnp.pad(router_w, ((0, 0), (0, 0), (0, LANES - N_EXPERTS))),
        rb=jnp.pad(router_b, ((0, 0), (0, LANES - N_EXPERTS)))[:, None],
        w1=moe_w1, b1=moe_b1[:, :, None], w2=moe_w2, b2=moe_b2[:, :, None],
        plew=ple_w.astype(BF16), gatew=ple_gate_w.astype(BF16),
    )
    bp, lp, _ = x_prompt.shape
    bs, ls, _ = x_sample.shape
    zeros = lambda *s: jnp.zeros(s, F32)
    shapes = [(bp, lp), (bs, ls)]
    tms = [PROMPT_TM, SAMPLE_TM]
    ps = [p_prompt, p_sample]
    xs = [x_prompt.reshape(bp * lp, D_MODEL).astype(F32), x_sample.reshape(bs * ls, D_MODEL).astype(F32)]
    even_states = [(zeros(1, bp, S5_GROUPS, S5_STATE), zeros(1, bp, S5_GROUPS, S5_STATE),
                    zeros(1, bp, GDN_HEADS, GDN_DK, GDN_DV), zeros(1, bp, GDN_CONV - 1, GDN_QKV)),
                   (state_s5_re, state_s5_im, state_gdn, state_gdn_conv)]
    ret_states = [zeros(1, bp, RET_HEADS, RET_DK, RET_DV), state_ret]
    pos0s = [0.0, float(PAST_LEN)]

    def moe(mixed, layer):
        return _moe_ple([m[0] for m in mixed], [m[1] for m in mixed], [p.reshape(DEPTH, -1, PLE_DIM) for p in ps],
                        W['w1'], W['b1'][layer], W['w2'], W['b2'][layer], layer, W['ln2_g'][layer],
                        W['ln2_b'][layer], W['plew'][layer], W['gatew'][layer], tms, blk=EXPERT_BLK)

    even = [_even_mixer(x, nb, L, *st, W, tm) for x, (nb, L), st, tm in zip(xs, shapes, even_states, tms)]
    xs = moe(even, 0)
    odd = [_odd_mixer(x, nb, L, st, pos0, W, tm)
           for x, (nb, L), st, pos0, tm in zip(xs, shapes, ret_states, pos0s, tms)]
    xs = moe(odd, 1)
    dp = x_prompt.dtype
    y_p, y_s = xs[0].reshape(bp, lp, D_MODEL), xs[1].reshape(bs, ls, D_MODEL)
    (_, _, p_re, p_im, p_gdn, p_conv), (_, _, s_re, s_im, s_gdn, s_conv) = even
    p_ret, s_ret = odd[0][2], odd[1][2]
    return (y_p.astype(dp), y_s.astype(x_sample.dtype),
            p_re.astype(dp), p_im.astype(dp), p_gdn.astype(dp), p_conv.astype(dp), p_ret.astype(dp),
            s_re.astype(state_s5_re.dtype), s_im.astype(state_s5_im.dtype), s_gdn.astype(state_gdn.dtype),
            s_conv.astype(state_gdn_conv.dtype), s_ret.astype(state_ret.dtype))
```

```python
import functools

import jax
import jax.numpy as jnp
import numpy as np
from jax import lax
from jax.experimental import pallas as pl
from jax.experimental.pallas import tpu as pltpu

F32 = jnp.float32
BF16 = jnp.bfloat16

D_MODEL = 1024
CHUNK = 64
S5_WIDTH = 512
S5_GROUP = 16
S5_GROUPS = 32
S5_STATE = 64
S5_TILES = 16
GDN_HEADS = 4
GDN_DK = 128
GDN_DV = 128
GDN_CONV = 4
GDN_QKV = 1536
RET_HEADS = 4
RET_DK = 256
RET_DV = 512
ROPE_BASE = 10000.0
N_EXPERTS = 32
TOP_K = 4
D_FF = 1024
SWIGLU_LIMIT = 7.0
SWIGLU_ALPHA = 1.702
PLE_DIM = 256
DEPTH = 2
PAST_LEN = 1024
DEEPNORM_ALPHA = (2 * DEPTH) ** 0.25
LN_EPS = 1e-5
NORM_EPS = 1e-6

LANES = 128
SUBLANES = 8
VMEM_LIMIT = 48 * 1024 * 1024

PROMPT_TM = 512
SAMPLE_TM = 128
EXPERT_BLK = 512


def _params(sem):
    return pltpu.CompilerParams(dimension_semantics=sem, vmem_limit_bytes=VMEM_LIMIT)


def _dot(a, b):
    return jnp.dot(a, b, preferred_element_type=F32)


def _sigmoid(x):
    return 1.0 / (1.0 + jnp.exp(-x))


def _full(shape):
    nd = len(shape)
    return pl.BlockSpec(shape, lambda *_: (0,) * nd)


def _proj_even_kernel(x_ref, wu_ref, wqkv_ref, wz_ref, wba_ref, u_ref, qkv_ref, z_ref, ba_ref):
    xb = x_ref[...].astype(BF16)
    u_ref[...] = _dot(xb, wu_ref[...])
    qkv_ref[...] = _dot(xb, wqkv_ref[...])
    z_ref[...] = _dot(xb, wz_ref[...])
    ba_ref[...] = _dot(xb, wba_ref[...])


def _proj_even(x, wu, wqkv, wz, wba, tm):
    t = x.shape[0]
    row = lambda n: pl.BlockSpec((tm, n), lambda i: (i, 0))
    return pl.pallas_call(
        _proj_even_kernel,
        grid=(t // tm,),
        in_specs=[row(D_MODEL), _full(wu.shape), _full(wqkv.shape), _full(wz.shape), _full(wba.shape)],
        out_specs=[row(S5_WIDTH), row(GDN_QKV), row(GDN_HEADS * GDN_DV), row(LANES)],
        out_shape=[jax.ShapeDtypeStruct((t, S5_WIDTH), F32), jax.ShapeDtypeStruct((t, GDN_QKV), F32),
                   jax.ShapeDtypeStruct((t, GDN_HEADS * GDN_DV), F32), jax.ShapeDtypeStruct((t, LANES), F32)],
        compiler_params=_params(("parallel",)),
        name="proj_even",
    )(x, wu, wqkv, wz, wba)


def _proj_odd_kernel(x_ref, w_ref, cos_ref, sin_ref, o_ref):
    j = pl.program_id(0)
    acc = _dot(x_ref[...].astype(BF16), w_ref[...])

    @pl.when(j == 0)
    def _():
        cos, sin = cos_ref[...], sin_ref[...]
        half = RET_DK // 2
        for h in range(2 * RET_HEADS):
            x0 = acc[:, h * RET_DK:h * RET_DK + half]
            x1 = acc[:, h * RET_DK + half:(h + 1) * RET_DK]
            scale = 1.0 if h < RET_HEADS else RET_DK ** -0.5
            o_ref[:, h * RET_DK:h * RET_DK + half] = ((x0 * cos - x1 * sin) * scale).astype(o_ref.dtype)
            o_ref[:, h * RET_DK + half:(h + 1) * RET_DK] = ((x0 * sin + x1 * cos) * scale).astype(o_ref.dtype)

    @pl.when(j != 0)
    def _():
        o_ref[...] = acc.astype(o_ref.dtype)


def _proj_odd(x, w, cos, sin, tm):
    t = x.shape[0]
    nblk = w.shape[1] // 2048
    period = cos.shape[0] // tm
    return pl.pallas_call(
        _proj_odd_kernel,
        grid=(nblk, t // tm),
        in_specs=[pl.BlockSpec((tm, D_MODEL), lambda j, i: (i, 0)),
                  pl.BlockSpec((D_MODEL, 2048), lambda j, i: (0, j)),
                  pl.BlockSpec((tm, LANES), lambda j, i: (i % period, 0)),
                  pl.BlockSpec((tm, LANES), lambda j, i: (i % period, 0))],
        out_specs=pl.BlockSpec((tm, 2048), lambda j, i: (i, j)),
        out_shape=jax.ShapeDtypeStruct((t, w.shape[1]), BF16),
        compiler_params=_params(("parallel", "parallel")),
        name="proj_odd",
    )(x, w, cos, sin)


def _layer_norm(r, g, b):
    mu = jnp.mean(r, -1, keepdims=True)
    d = r - mu
    var = jnp.mean(d * d, -1, keepdims=True)
    return d * lax.rsqrt(var + LN_EPS) * g + b


S5_LANE_BLOCKS = S5_WIDTH // LANES
S5_TILES_PER_BLOCK = S5_TILES // S5_LANE_BLOCKS


def _s5_kernel(u_ref, h0_ref, bmat_ref, cmat_ref, acoef_ref, dskip_ref, wglu_ref, bglu_ref,
               y_ref, hout_ref, utm, sre, sim, ytm, hst, *, nb, lt):
    tb = pl.program_id(0)

    @pl.when(tb == 0)
    def _():
        hst[...] = h0_ref[...]

    for b in range(nb):
        for q in range(S5_LANE_BLOCKS):
            utm[q, pl.ds(b, lt, stride=nb), :] = u_ref[b, :, q * LANES:(q + 1) * LANES]

    def block_body(q, carry):
        bu = _dot(utm[q].astype(BF16), bmat_ref[q])
        for g in range(S5_TILES_PER_BLOCK):
            sre[g] = bu[:, (2 * g) * LANES:(2 * g + 1) * LANES]
            sim[g] = bu[:, (2 * g + 1) * LANES:(2 * g + 2) * LANES]
        js = [q * S5_TILES_PER_BLOCK + g for g in range(S5_TILES_PER_BLOCK)]
        acs = [acoef_ref[j] for j in js]
        hs = [(hst[j, :, 0:LANES], hst[j, :, LANES:2 * LANES]) for j in js]
        for t in range(lt):
            rows = slice(t * nb, (t + 1) * nb)
            for g in range(S5_TILES_PER_BLOCK):
                ar, ai = acs[g][:, :LANES], acs[g][:, LANES:]
                hr, hi = hs[g]
                nhr = ar * hr - ai * hi + sre[g, rows, :]
                nhi = ar * hi + ai * hr + sim[g, rows, :]
                sre[g, rows, :] = nhr
                sim[g, rows, :] = nhi
                hs[g] = (nhr, nhi)
        for g, j in enumerate(js):
            hst[j, :, 0:LANES] = hs[g][0]
            hst[j, :, LANES:2 * LANES] = hs[g][1]
        st = jnp.concatenate([part for g in range(S5_TILES_PER_BLOCK) for part in (sre[g], sim[g])], axis=-1)
        ytm[q] = _dot(st.astype(BF16), cmat_ref[q])
        return carry

    lax.fori_loop(0, S5_LANE_BLOCKS, block_body, 0)
    y = (jnp.concatenate([ytm[q] for q in range(S5_LANE_BLOCKS)], axis=-1)
         + dskip_ref[...] * jnp.concatenate([utm[q] for q in range(S5_LANE_BLOCKS)], axis=-1))
    y = jax.nn.gelu(y)
    y = y * _sigmoid(_dot(y.astype(BF16), wglu_ref[...]) + bglu_ref[...])
    for q in range(S5_LANE_BLOCKS):
        ytm[q] = y[:, q * LANES:(q + 1) * LANES]
    for b in range(nb):
        for q in range(S5_LANE_BLOCKS):
            y_ref[b, :, q * LANES:(q + 1) * LANES] = ytm[q, pl.ds(b, lt, stride=nb), :].astype(y_ref.dtype)
    hout_ref[...] = hst[...]


def _s5(u3, h0, bmat, cmat, acoef, dskip, wglu, bglu, lt):
    nb, L, _ = u3.shape
    rows = nb * lt
    tpb = S5_TILES_PER_BLOCK
    bmat = bmat.reshape(S5_LANE_BLOCKS, tpb, LANES, 2 * LANES).transpose(0, 2, 1, 3).reshape(
        S5_LANE_BLOCKS, LANES, tpb * 2 * LANES)
    cmat = cmat.reshape(S5_LANE_BLOCKS, tpb * 2 * LANES, LANES)
    acoef = jnp.broadcast_to(acoef[:, :1], (S5_TILES, nb, 2 * LANES))
    return pl.pallas_call(
        functools.partial(_s5_kernel, nb=nb, lt=lt),
        grid=(L // lt,),
        in_specs=[pl.BlockSpec((nb, lt, S5_WIDTH), lambda i: (0, i, 0)),
                  _full(h0.shape), _full(bmat.shape), _full(cmat.shape), _full(acoef.shape),
                  _full(dskip.shape), _full(wglu.shape), _full(bglu.shape)],
        out_specs=[pl.BlockSpec((nb, lt, S5_WIDTH), lambda i: (0, i, 0)), _full(h0.shape)],
        out_shape=[jax.ShapeDtypeStruct((nb, L, S5_WIDTH), BF16), jax.ShapeDtypeStruct(h0.shape, F32)],
        scratch_shapes=[pltpu.VMEM((S5_LANE_BLOCKS, rows, LANES), F32), pltpu.VMEM((tpb, rows, LANES), F32),
                        pltpu.VMEM((tpb, rows, LANES), F32), pltpu.VMEM((S5_LANE_BLOCKS, rows, LANES), F32),
                        pltpu.VMEM(h0.shape, F32)],
        compiler_params=_params(("arbitrary",)),
        name="s5_scan",
    )(u3, h0, bmat, cmat, acoef, dskip, wglu, bglu)


def _s5_weights(a_re, a_im, log_dt, b_re, b_im, c_re, c_im):
    dt = jnp.exp(log_dt)[:, None]
    lr, li = a_re * dt, a_im * dt
    mag = jnp.exp(lr)
    ab_re, ab_im = mag * jnp.cos(li), mag * jnp.sin(li)
    den = a_re * a_re + a_im * a_im
    cf_re = ((ab_re - 1.0) * a_re + ab_im * a_im) / den
    cf_im = (ab_im * a_re - (ab_re - 1.0) * a_im) / den
    bb_re = cf_re[..., None] * b_re - cf_im[..., None] * b_im
    bb_im = cf_re[..., None] * b_im + cf_im[..., None] * b_re
    jj = np.arange(S5_TILES)[:, None, None]
    lg = np.arange(8)[None, :, None]
    gi = np.arange(2)[None, None, :]
    sel = jnp.asarray((lg == 2 * (jj % 4) + gi).astype(np.float32))
    tiles = lambda w: w.reshape(S5_TILES, 2, *w.shape[1:])
    bt = lambda w: jnp.einsum('jlg,jgpn->jlngp', sel, tiles(w)).reshape(S5_TILES, LANES, LANES)
    bmat = jnp.concatenate([bt(bb_re), bt(bb_im)], axis=-1)
    ct = lambda w: jnp.einsum('jlg,jgnp->jgpln', sel, tiles(w)).reshape(S5_TILES, LANES, LANES)
    cmat = jnp.concatenate([ct(c_re), -ct(c_im)], axis=1)
    acoef = jnp.concatenate([ab_re.reshape(S5_TILES, LANES), ab_im.reshape(S5_TILES, LANES)], axis=-1)
    acoef = jnp.broadcast_to(acoef[:, None, :], (S5_TILES, SUBLANES, 2 * LANES))
    return bmat.astype(BF16), cmat.astype(BF16), acoef


def _s5_state_in(h_re, h_im):
    nb = h_re.shape[0]
    h = jnp.concatenate([h_re.reshape(nb, S5_TILES, LANES), h_im.reshape(nb, S5_TILES, LANES)], axis=-1)
    return jnp.transpose(h, (1, 0, 2))


def _s5_state_out(h):
    nb = h.shape[1]
    h = jnp.transpose(h, (1, 0, 2))
    return (h[..., :LANES].reshape(nb, S5_GROUPS, S5_STATE), h[..., LANES:].reshape(nb, S5_GROUPS, S5_STATE))


def _bdot(a, b):
    return lax.dot_general(a, b, (((2,), (1,)), ((0,), (0,))), preferred_element_type=F32)


def _bdot_nt(a, b):
    return lax.dot_general(a, b, (((2,), (2,)), ((0,), (0,))), preferred_element_type=F32)


def _bdot_bf16(a, b):
    return _bdot(a.astype(BF16), b.astype(BF16))


def _unit_lower_inverse(nmat):
    lc = nmat.shape[-1]
    ri = lax.broadcasted_iota(jnp.int32, nmat.shape, 1)
    ci = lax.broadcasted_iota(jnp.int32, nmat.shape, 2)
    base = 16
    dmat = jnp.where(ri // base == ci // base, nmat, 0.0)
    inv = jnp.where(ri == ci, 1.0, 0.0) - dmat
    pw = dmat
    for _ in range(3):
        pw = _bdot_bf16(pw, pw)
        inv = inv + _bdot_bf16(inv, pw)
    size = base
    while size < lc:
        off = jnp.where(ri // (2 * size) == ci // (2 * size), jnp.where(ri // size > ci // size, nmat, 0.0), 0.0)
        inv = inv - _bdot_bf16(_bdot_bf16(inv, off), inv)
        size *= 2
    return inv


def _gdn_local_kernel(qkv_ref, ba_ref, ctx_ref, cw_ref, p1_ref, p2_ref, ltri_ref,
                      u0_ref, w_ref, qd_ref, kd_ref, attn_ref, g_ref, cout_ref, xpad, *, lc, cp):
    c = pl.program_id(1)
    rb = lc * cp

    @pl.when(c == 0)
    def _():
        xpad[0:SUBLANES, :] = ctx_ref[0]

    xpad[SUBLANES:SUBLANES + rb, :] = qkv_ref[0]
    cw = cw_ref[...]
    conv = (cw[3:4] * xpad[8:8 + rb, :] + cw[2:3] * xpad[7:7 + rb, :]
            + cw[1:2] * xpad[6:6 + rb, :] + cw[0:1] * xpad[5:5 + rb, :])
    tail = xpad[rb:rb + SUBLANES, :]
    xpad[0:SUBLANES, :] = tail
    cout_ref[0] = tail
    a = conv * _sigmoid(conv)

    ba = ba_ref[0]
    beta_all = _sigmoid(ba)
    sp_in = ba + p2_ref[...]
    softplus = jnp.maximum(sp_in, 0.0) + jnp.log(1.0 + jnp.exp(-jnp.abs(sp_in)))
    g_all = p1_ref[...] * softplus
    g_hi = g_all.astype(BF16)
    g_r = g_all - g_hi.astype(F32)
    g_mid = g_r.astype(BF16)
    g_lo = (g_r - g_mid.astype(F32)).astype(BF16)
    lt = ltri_ref[...]
    G = _dot(lt, g_hi) + _dot(lt, g_mid) + _dot(lt, g_lo)
    g_ref[0] = G
    GT = G.T
    pairs = [(h, cc) for h in range(GDN_HEADS) for cc in range(cp)]
    qs, ks, vs, betas, gcols, grows, glasts = [], [], [], [], [], [], []
    for h in range(GDN_HEADS):
        qa = a[:, h * GDN_DK:(h + 1) * GDN_DK]
        ka = a[:, (GDN_HEADS + h) * GDN_DK:(GDN_HEADS + h + 1) * GDN_DK]
        va = a[:, (2 * GDN_HEADS + h) * GDN_DK:(2 * GDN_HEADS + h + 1) * GDN_DK]
        qa = qa * lax.rsqrt(jnp.sum(qa * qa, -1, keepdims=True) + NORM_EPS) * (GDN_DK ** -0.5)
        ka = ka * lax.rsqrt(jnp.sum(ka * ka, -1, keepdims=True) + NORM_EPS)
        for cc in range(cp):
            rows = slice(cc * lc, (cc + 1) * lc)
            qs.append(qa[rows])
            ks.append(ka[rows])
            vs.append(va[rows])
            betas.append(beta_all[rows, h:h + 1])
            gcols.append(G[rows, GDN_HEADS + h:GDN_HEADS + h + 1])
            grows.append(GT[GDN_HEADS + h:GDN_HEADS + h + 1, cc * lc:(cc + 1) * lc])
            glasts.append(GT[GDN_HEADS + h:GDN_HEADS + h + 1, (cc + 1) * lc - 1:(cc + 1) * lc])
    q3, k3, v3 = jnp.stack(qs), jnp.stack(ks), jnp.stack(vs)
    beta3, gcol3 = jnp.stack(betas), jnp.stack(gcols)
    grow3, glast3 = jnp.stack(grows), jnp.stack(glasts)
    shape3 = (len(pairs), lc, lc)
    ri = lax.broadcasted_iota(jnp.int32, shape3, 1)
    ci = lax.broadcasted_iota(jnp.int32, shape3, 2)
    incl = ri >= ci
    dec3 = jnp.where(incl, jnp.exp(jnp.where(incl, gcol3 - grow3, 0.0)), 0.0)
    eg3 = jnp.exp(gcol3)
    kb3 = k3 * beta3
    kbf3 = k3.astype(BF16)
    nmat3 = jnp.where(ri > ci, _bdot_nt(kb3.astype(BF16), kbf3) * dec3, 0.0)
    inv3 = _unit_lower_inverse(nmat3)
    sol3 = _bdot_bf16(inv3, jnp.concatenate([v3 * beta3, kb3 * eg3], axis=-1))
    w3 = sol3[:, :, GDN_DV:].astype(BF16)
    qd3 = (q3 * eg3).astype(BF16)
    kd3 = (k3 * jnp.exp(glast3 - gcol3)).astype(BF16)
    attn3 = (_bdot_nt(q3.astype(BF16), kbf3) * dec3).astype(BF16)
    for i, (h, cc) in enumerate(pairs):
        rows = slice(cc * lc, (cc + 1) * lc)
        cols = slice(h * GDN_DV, (h + 1) * GDN_DV)
        u0_ref[0, rows, cols] = sol3[i, :, :GDN_DV]
        w_ref[0, rows, cols] = w3[i]
        qd_ref[0, rows, cols] = qd3[i]
        kd_ref[0, rows, cols] = kd3[i]
        attn_ref[0, rows, h * lc:(h + 1) * lc] = attn3[i]


def _gdn_seq_kernel(u0_ref, w_ref, qd_ref, kd_ref, attn_ref, g_ref, z_ref, s0_ref, nw_ref,
                    y_ref, sout_ref, S, *, lc, nbb):
    c = pl.program_id(1)

    @pl.when(c == 0)
    def _():
        S[...] = s0_ref[...]

    nw = nw_ref[...]
    pairs = [(bb, h) for bb in range(nbb) for h in range(GDN_HEADS)]
    hcols = lambda h: slice(h * GDN_DV, (h + 1) * GDN_DV)
    stack = lambda f: jnp.stack([f(bb, h) for bb, h in pairs])
    dlast = jnp.exp(g_ref[:, lc - 1:lc, :])
    S3 = S[...].reshape(len(pairs), GDN_DK, GDN_DV)
    wq3 = stack(lambda bb, h: jnp.concatenate([w_ref[bb, :, hcols(h)], qd_ref[bb, :, hcols(h)]], axis=0))
    r3 = _bdot(wq3, S3.astype(BF16))
    ub3 = (stack(lambda bb, h: u0_ref[bb, :, hcols(h)]) - r3[:, :lc]).astype(BF16)
    o3 = r3[:, lc:] + _bdot(stack(lambda bb, h: attn_ref[bb, :, h * lc:(h + 1) * lc]), ub3)
    d3 = stack(lambda bb, h: dlast[bb, :, GDN_HEADS + h:GDN_HEADS + h + 1])
    kd3 = stack(lambda bb, h: kd_ref[bb, :, hcols(h)])
    kdu3 = lax.dot_general(kd3, ub3, (((1,), (1,)), ((0,), (0,))), preferred_element_type=F32)
    S[...] = (d3 * S3 + kdu3).reshape(S.shape)
    o3 = o3 * lax.rsqrt(jnp.mean(o3 * o3, -1, keepdims=True) + NORM_EPS) * nw
    for i, (bb, h) in enumerate(pairs):
        zh = z_ref[bb, :, hcols(h)]
        y_ref[bb, :, hcols(h)] = (o3[i] * (zh * _sigmoid(zh))).astype(y_ref.dtype)

    @pl.when(c == pl.num_programs(1) - 1)
    def _():
        sout_ref[...] = S[...]


def _gdn(qkv3, z3, ba3, ctx8, s0, cw, p1, p2, nw, lc, cp, nbb):
    nb, L, _ = qkv3.shape
    rb = lc * cp
    hd = GDN_HEADS * GDN_DV
    ltri = jnp.asarray(np.kron(np.eye(cp, dtype=np.float32), np.tril(np.ones((lc, lc), np.float32)))).astype(BF16)
    blk = lambda n: pl.BlockSpec((1, rb, n), lambda b, c: (b, c, 0))
    per_b = lambda shape: pl.BlockSpec((1,) + shape, lambda b, c: (b,) + (0,) * len(shape))
    cst = lambda shape: pl.BlockSpec(shape, lambda b, c: (0,) * len(shape))
    sds = lambda n, dt: jax.ShapeDtypeStruct((nb, L, n), dt)
    u0, w, qd, kd, attn, G, cout = pl.pallas_call(
        functools.partial(_gdn_local_kernel, lc=lc, cp=cp),
        grid=(nb, L // rb),
        in_specs=[blk(GDN_QKV), blk(LANES), per_b((SUBLANES, GDN_QKV)), cst(cw.shape), cst(p1.shape),
                  cst(p2.shape), cst(ltri.shape)],
        out_specs=[blk(hd), blk(hd), blk(hd), blk(hd), blk(GDN_HEADS * lc), blk(LANES),
                   per_b((SUBLANES, GDN_QKV))],
        out_shape=[sds(hd, F32), sds(hd, BF16), sds(hd, BF16), sds(hd, BF16), sds(GDN_HEADS * lc, BF16),
                   sds(LANES, F32), jax.ShapeDtypeStruct((nb, SUBLANES, GDN_QKV), F32)],
        scratch_shapes=[pltpu.VMEM((rb + SUBLANES, GDN_QKV), F32)],
        compiler_params=_params(("parallel", "arbitrary")),
        name="gdn_local",
    )(qkv3, ba3, ctx8, cw, p1, p2, ltri)
    sblk = lambda n: pl.BlockSpec((nbb, lc, n), lambda b, c: (b, c, 0))
    state = pl.BlockSpec((nbb, GDN_HEADS, GDN_DK, GDN_DV), lambda b, c: (b, 0, 0, 0))
    y, s_new = pl.pallas_call(
        functools.partial(_gdn_seq_kernel, lc=lc, nbb=nbb),
        grid=(nb // nbb, L // lc),
        in_specs=[sblk(hd), sblk(hd), sblk(hd), sblk(hd), sblk(GDN_HEADS * lc), sblk(LANES), sblk(hd),
                  state, cst(nw.shape)],
        out_specs=[sblk(hd), state],
        out_shape=[sds(hd, BF16), jax.ShapeDtypeStruct(s0.shape, F32)],
        scratch_shapes=[pltpu.VMEM((nbb, GDN_HEADS, GDN_DK, GDN_DV), F32)],
        compiler_params=_params(("parallel", "arbitrary")),
        name="gdn_seq",
    )(u0, w, qd, kd, attn, G, z3, s0, nw)
    return y, s_new, cout


def _ret_kernel(q_ref, k_ref, v_ref, g_ref, r0_ref, dec_ref, qs_ref, ks_ref, cd_ref, o_ref, rout_ref, R,
                *, nbb):
    c = pl.program_id(1)

    @pl.when(c == 0)
    def _():
        R[...] = r0_ref[...]

    pairs = [(bb, h) for bb in range(nbb) for h in range(RET_HEADS)]
    stack = lambda f: jnp.stack([f(bb, h) for bb, h in pairs])
    kcols = lambda h: slice(h * RET_DK, (h + 1) * RET_DK)
    vcols = lambda h: slice(h * RET_DV, (h + 1) * RET_DV)
    q3 = stack(lambda bb, h: q_ref[bb, :, kcols(h)])
    k3 = stack(lambda bb, h: k_ref[bb, :, kcols(h)])
    v3 = stack(lambda bb, h: v_ref[bb, :, vcols(h)])
    dec3 = stack(lambda bb, h: dec_ref[h])
    qs3 = stack(lambda bb, h: qs_ref[h])
    ks3 = stack(lambda bb, h: ks_ref[h])
    cd3 = stack(lambda bb, h: cd_ref[h])
    R3 = R[...].reshape(len(pairs), RET_DK, RET_DV)
    s3 = _bdot_nt(q3, k3) * dec3
    o3 = _bdot(s3.astype(BF16), v3) + _bdot(q3, R3.astype(BF16)) * qs3
    kv3 = lax.dot_general((k3.astype(F32) * ks3).astype(BF16), v3, (((1,), (1,)), ((0,), (0,))),
                          preferred_element_type=F32)
    R[...] = (cd3 * R3 + kv3).reshape(R.shape)
    mu = jnp.mean(o3, -1, keepdims=True)
    d3 = o3 - mu
    var = jnp.mean(d3 * d3, -1, keepdims=True)
    on3 = d3 * lax.rsqrt(var + LN_EPS)
    for i, (bb, h) in enumerate(pairs):
        gt = g_ref[bb, :, vcols(h)].astype(F32)
        o_ref[bb, :, vcols(h)] = (gt * _sigmoid(gt) * on3[i]).astype(o_ref.dtype)

    @pl.when(c == pl.num_programs(1) - 1)
    def _():
        rout_ref[...] = R[...]


def _retention(proj3, r0, chunk, nbb, nc):
    nb, L, _ = proj3.shape
    lc = chunk * nc
    log_g = np.log(1.0 - 2.0 ** (-5.0 - np.arange(RET_HEADS, dtype=np.float64)))
    idx = np.arange(lc, dtype=np.float64)
    dist = idx[:, None] - idx[None, :]
    which = (idx // chunk)[:, None] - (idx // chunk)[None, :]
    dec = np.where(which >= 0, np.exp(log_g[:, None, None] * np.abs(dist)), 0.0).astype(np.float32)
    qs = np.exp(log_g[:, None] * (idx + 1.0)).astype(np.float32)[..., None]
    ks = np.exp(log_g[:, None] * (lc - 1.0 - idx)).astype(np.float32)[..., None]
    cdec = np.exp(log_g * lc).astype(np.float32)[:, None, None]
    nqk = RET_HEADS * RET_DK
    nv = RET_HEADS * RET_DV
    cst = lambda shape: pl.BlockSpec(shape, lambda b, c: (0,) * len(shape))
    state = pl.BlockSpec((nbb, RET_HEADS, RET_DK, RET_DV), lambda b, c: (b, 0, 0, 0))
    return pl.pallas_call(
        functools.partial(_ret_kernel, nbb=nbb),
        grid=(nb // nbb, L // lc),
        in_specs=[pl.BlockSpec((nbb, lc, nqk), lambda b, c: (b, c, 0)),
                  pl.BlockSpec((nbb, lc, nqk), lambda b, c: (b, c, 1)),
                  pl.BlockSpec((nbb, lc, nv), lambda b, c: (b, c, 1)),
                  pl.BlockSpec((nbb, lc, nv), lambda b, c: (b, c, 2)),
                  state, cst(dec.shape), cst(qs.shape), cst(ks.shape), cst(cdec.shape)],
        out_specs=[pl.BlockSpec((nbb, lc, nv), lambda b, c: (b, c, 0)), state],
        out_shape=[jax.ShapeDtypeStruct((nb, L, nv), BF16), jax.ShapeDtypeStruct(r0.shape, F32)],
        scratch_shapes=[pltpu.VMEM((nbb, RET_HEADS, RET_DK, RET_DV), F32)],
        compiler_params=_params(("parallel", "arbitrary")),
        name="retention",
    )(proj3, proj3, proj3, proj3, r0, jnp.asarray(dec), jnp.asarray(qs), jnp.asarray(ks), jnp.asarray(cdec))


RUN_ALIGN = SUBLANES
RUN_PIECE = 64
MOE_VMEM_LIMIT = 58 * 1024 * 1024


def _moe_params(sem):
    return pltpu.CompilerParams(dimension_semantics=sem, vmem_limit_bytes=MOE_VMEM_LIMIT)


def _tile_cap(tm):
    rows = TOP_K * tm + N_EXPERTS * (RUN_ALIGN - 1)
    return -(-rows // LANES) * LANES


def _route_tile(x, rw_ref, rb_ref, lst_ref, ust_ref, pos_ref, gt_ref, cnt_ref, xs_ref):
    tm = x.shape[0]
    cap = xs_ref.shape[1]
    lane = lax.broadcasted_iota(jnp.int32, (tm, LANES), 1)
    lane_f = lane.astype(F32)
    xh = x.astype(BF16)
    logits = _dot(xh, rw_ref[...].astype(BF16)) + rb_ref[...]
    logits = jnp.where(lane < N_EXPERTS, logits, -jnp.inf)
    vals, hots = [], []
    for _ in range(TOP_K):
        m = jnp.max(logits, -1, keepdims=True)
        first = jnp.min(jnp.where(logits == m, lane_f, float(LANES)), -1, keepdims=True)
        hot = lane_f == first
        vals.append(m)
        hots.append(hot)
        logits = jnp.where(hot, -jnp.inf, logits)
    es = [jnp.exp(v - vals[0]) for v in vals]
    den = es[0] + es[1] + es[2] + es[3]
    multi = jnp.zeros((tm, LANES), F32)
    for hot in hots:
        multi = multi + hot.astype(F32)
    counts = jnp.sum(multi, 0, keepdims=True)
    units = jnp.floor((counts + (RUN_ALIGN - 1)) * (1.0 / RUN_ALIGN))
    offs = _dot(jnp.broadcast_to(units, (SUBLANES, LANES)).astype(BF16), ust_ref[...])[0:1] * float(RUN_ALIGN)
    before = _dot(lst_ref[...], multi.astype(BF16))
    slot = offs + before
    pos = jnp.zeros((tm, LANES), F32)
    gt = jnp.zeros((tm, LANES), F32)
    for kk in range(TOP_K):
        pos = jnp.where(lane == kk, jnp.sum(jnp.where(hots[kk], slot, 0.0), -1, keepdims=True), pos)
        gt = jnp.where(lane == kk, es[kk] / den, gt)
    pos = pos.astype(jnp.int32)
    pos_ref[...] = pos
    gt_ref[...] = gt
    cnt_ref[0] = counts
    pos_t = pos.T[0:2 * SUBLANES].astype(jnp.int16)
    row = lax.broadcasted_iota(jnp.int16, (cap, tm), 0)
    sel = jnp.zeros((cap, tm), BF16)
    for kk in range(TOP_K):
        sel = sel + jnp.where(row == pos_t[kk:kk + 1, :], jnp.ones((), BF16), jnp.zeros((), BF16))
    xs_ref[0] = _dot(sel, xh)


def _outproj_route_kernel(*refs, n_in):
    a_refs = refs[:n_in]
    w_refs = refs[n_in:2 * n_in]
    x_ref, g_ref, b_ref = refs[2 * n_in:2 * n_in + 3]
    route_in = refs[2 * n_in + 3:2 * n_in + 7]
    o_ref = refs[2 * n_in + 7]
    route_out = refs[2 * n_in + 8:]
    acc = _dot(a_refs[0][...], w_refs[0][...])
    for a_ref, w_ref in zip(a_refs[1:], w_refs[1:]):
        acc = acc + _dot(a_ref[...], w_ref[...])
    x1 = _layer_norm(DEEPNORM_ALPHA * x_ref[...] + acc, g_ref[...], b_ref[...])
    o_ref[...] = x1
    _route_tile(x1, *route_in, *route_out)


def _outproj_route(acts, ws, x, g, b, rw, rb, tm):
    t = x.shape[0]
    nt = t // tm
    cap = _tile_cap(tm)
    lst = jnp.asarray(np.tril(np.ones((tm, tm), np.float32), -1)).astype(BF16)
    ust = jnp.asarray(np.triu(np.ones((LANES, LANES), np.float32), 1)).astype(BF16)
    row = lambda n: pl.BlockSpec((tm, n), lambda i: (i, 0))
    x1, pos, gt, cnt, xs = pl.pallas_call(
        functools.partial(_outproj_route_kernel, n_in=len(acts)),
        grid=(nt,),
        in_specs=[row(a.shape[1]) for a in acts] + [_full(w.shape) for w in ws]
                 + [row(D_MODEL), _full((1, D_MODEL)), _full((1, D_MODEL)),
                    _full(rw.shape), _full(rb.shape), _full(lst.shape), _full(ust.shape)],
        out_specs=[row(D_MODEL), row(LANES), row(LANES), pl.BlockSpec((1, 1, LANES), lambda i: (i, 0, 0)),
                   pl.BlockSpec((1, cap, D_MODEL), lambda i: (i, 0, 0))],
        out_shape=[jax.ShapeDtypeStruct((t, D_MODEL), F32),
                   jax.ShapeDtypeStruct((t, LANES), jnp.int32), jax.ShapeDtypeStruct((t, LANES), F32),
                   jax.ShapeDtypeStruct((nt, 1, LANES), F32), jax.ShapeDtypeStruct((nt, cap, D_MODEL), F32)],
        compiler_params=_moe_params(("parallel",)),
        name="outproj_route",
    )(*acts, *ws, x, g, b, rw, rb, lst, ust)
    return x1, (pos, gt, cnt, xs)


def _expert_kernel(be_ref, nu_ref, ilo_ref, rows_ref, gs_ref, n8_ref, lo_ref, *refs, blk, tiles):
    ng = len(tiles)
    xs_hbms = refs[:ng]
    w1_ref, b1_ref, w2_ref, b2_ref = refs[ng:ng + 4]
    ys_hbms = refs[ng + 4:2 * ng + 4]
    xbuf, ybuf, w1b, w2b, in_sem, out_sem = refs[2 * ng + 4:]
    nt = sum(tiles)
    firsts = [sum(tiles[:g]) for g in range(ng)]
    j = pl.program_id(0)
    nu = nu_ref[0]

    def for_each_run(jb, fn):
        e = be_ref[jb]
        base = jb * blk
        for g in range(ng):
            end = firsts[g] + tiles[g]

            def cond(i, end=end):
                return (i < end) & (gs_ref[e * nt + jnp.minimum(i, nt - 1)] < base + blk)

            def body(i, g=g):
                g0 = gs_ref[e * nt + i]
                first = jnp.maximum(g0, base)
                last = jnp.minimum(g0 + n8_ref[e * nt + i], base + blk)
                fn(g, i - firsts[g], lo_ref[e * nt + i] + (first - g0), first - base, last - first)
                return i + 1

            lax.while_loop(cond, body, jnp.clip(ilo_ref[jb], firsts[g], end))

    def pieces(length, fn):
        def digits(sizes):
            for size in sizes:
                @pl.when((length & size) != 0)
                def _(size=size):
                    fn(length & ~(2 * size - 1), size)

        sizes = [blk >> k for k in range(blk.bit_length()) if blk >> k >= RUN_ALIGN]

        @pl.when(length > 2 * RUN_PIECE - 1)
        def _():
            digits([z for z in sizes if z > RUN_PIECE])

        digits([z for z in sizes if z <= RUN_PIECE])

    def aligned(v, size):
        return pl.ds(pl.multiple_of(v, RUN_ALIGN), size)

    def copy_in(jb, slot):
        def run(g, ig, src, dst, length):
            def piece(off, size):
                pltpu.make_async_copy(xs_hbms[g].at[ig, aligned(src + off, size)],
                                      xbuf.at[slot, aligned(dst + off, size)], in_sem.at[slot]).start()
            pieces(length, piece)
        for_each_run(jb, run)

    def copy_out(jb, slot):
        def run(g, ig, src, dst, length):
            def piece(off, size):
                pltpu.make_async_copy(ybuf.at[slot, aligned(dst + off, size)],
                                      ys_hbms[g].at[ig, aligned(src + off, size)], out_sem.at[slot]).start()
            pieces(length, piece)
        for_each_run(jb, run)

    def wait_rows(sem, nrows):
        size = blk
        while size >= RUN_ALIGN:
            @pl.when((nrows & size) != 0)
            def _(size=size):
                pltpu.make_async_copy(xbuf.at[1, pl.ds(0, size)], xbuf.at[0, pl.ds(0, size)], sem).wait()
            size //= 2

    @pl.when(j < nu)
    def _():
        slot = j % 2

        @pl.when(j == 0)
        def _():
            xbuf[...] = jnp.zeros_like(xbuf)
            copy_in(0, 0)

        wait_rows(in_sem.at[slot], rows_ref[j])

        @pl.when(j + 1 < nu)
        def _():
            copy_in(j + 1, 1 - slot)

        @pl.when(j >= 2)
        def _():
            wait_rows(out_sem.at[slot], rows_ref[jnp.maximum(j - 2, 0)])

        @pl.when((j == 0) | (be_ref[j] != be_ref[jnp.maximum(j - 1, 0)]))
        def _():
            w1b[...] = w1_ref[0, 0].astype(BF16)
            w2b[...] = w2_ref[0, 0].astype(BF16)

        h = _dot(xbuf[slot].astype(BF16), w1b[...]) + b1_ref[0]
        glu = jnp.minimum(h[:, :D_FF], SWIGLU_LIMIT)
        lin = jnp.clip(h[:, D_FF:], -SWIGLU_LIMIT, SWIGLU_LIMIT)
        act = glu * _sigmoid(SWIGLU_ALPHA * glu) * (lin + 1.0)
        ybuf[slot] = _dot(act.astype(BF16), w2b[...]) + b2_ref[0]
        copy_out(j, slot)

        @pl.when(j == nu - 1)
        def _():
            wait_rows(out_sem.at[slot], rows_ref[j])

            @pl.when(j >= 1)
            def _():
                wait_rows(out_sem.at[1 - slot], rows_ref[jnp.maximum(j - 1, 0)])


def _experts(tables, xs_list, w1, b1, w2, b2, layer, blk):
    ng = len(xs_list)
    n_blk = tables[0].shape[0]
    wspec = lambda shape: pl.BlockSpec((1, 1) + shape, lambda j, be, *_: (layer, be[j], 0, 0))
    bspec = lambda n: pl.BlockSpec((1, 1, n), lambda j, be, *_: (be[j], 0, 0))
    hbm = pl.BlockSpec(memory_space=pl.ANY)
    grid_spec = pltpu.PrefetchScalarGridSpec(
        num_scalar_prefetch=len(tables),
        grid=(n_blk,),
        in_specs=[hbm] * ng + [wspec((D_MODEL, 2 * D_FF)), bspec(2 * D_FF), wspec((D_FF, D_MODEL)), bspec(D_MODEL)],
        out_specs=[hbm] * ng,
        scratch_shapes=[pltpu.VMEM((2, blk, D_MODEL), F32), pltpu.VMEM((2, blk, D_MODEL), F32),
                        pltpu.VMEM((D_MODEL, 2 * D_FF), BF16), pltpu.VMEM((D_FF, D_MODEL), BF16),
                        pltpu.SemaphoreType.DMA((2,)), pltpu.SemaphoreType.DMA((2,))],
    )
    return pl.pallas_call(
        functools.partial(_expert_kernel, blk=blk, tiles=tuple(xs.shape[0] for xs in xs_list)),
        grid_spec=grid_spec,
        out_shape=[jax.ShapeDtypeStruct(xs.shape, F32) for xs in xs_list],
        input_output_aliases={len(tables) + g: g for g in range(ng)},
        compiler_params=_moe_params(("arbitrary",)),
        name="moe_experts",
    )(*tables, *xs_list, w1, b1, w2, b2)


def _combine_kernel(ys_ref, pos_ref, gt_ref, x_ref, p_ref, g_ref, b_ref, plew_ref, gatew_ref, o_ref):
    tm = x_ref.shape[0]
    cap = ys_ref.shape[1]
    pos = pos_ref[...].astype(jnp.int16)
    gt = gt_ref[...].astype(BF16)
    col = lax.broadcasted_iota(jnp.int16, (tm, cap), 1)
    sel = jnp.zeros((tm, cap), BF16)
    for kk in range(TOP_K):
        sel = sel + jnp.where(col == pos[:, kk:kk + 1], gt[:, kk:kk + 1], jnp.zeros((), BF16))
    y = _dot(sel, ys_ref[0].astype(BF16))
    x2 = _layer_norm(DEEPNORM_ALPHA * x_ref[...] + y, g_ref[...], b_ref[...])
    pp = _dot(p_ref[...].astype(BF16), plew_ref[...])
    gg = _sigmoid(_dot(x2.astype(BF16), gatew_ref[...]))
    o_ref[...] = x2 + pp * gg


def _combine(ys, pos, gt, x, p, layer, g, b, plew, gatew, tm):
    t = x.shape[0]
    cap = ys.shape[1]
    row = lambda n: pl.BlockSpec((tm, n), lambda i: (i, 0))
    return pl.pallas_call(
        _combine_kernel,
        grid=(t // tm,),
        in_specs=[pl.BlockSpec((1, cap, D_MODEL), lambda i: (i, 0, 0)), row(LANES), row(LANES),
                  row(D_MODEL), pl.BlockSpec((None, tm, PLE_DIM), lambda i: (layer, i, 0)),
                  _full((1, D_MODEL)), _full((1, D_MODEL)),
                  _full(plew.shape), _full(gatew.shape)],
        out_specs=row(D_MODEL),
        out_shape=jax.ShapeDtypeStruct((t, D_MODEL), F32),
        compiler_params=_moe_params(("parallel",)),
        name="moe_combine",
    )(ys, pos, gt, x, p, g, b, plew, gatew)


def _moe_tables(cnt, t, blk):
    nt = cnt.shape[0]
    n = cnt[:, 0, :N_EXPERTS].astype(jnp.int32)
    n8 = (n + RUN_ALIGN - 1) // RUN_ALIGN * RUN_ALIGN
    lo = jnp.cumsum(n8, axis=1) - n8
    rows_e = jnp.sum(n8, axis=0)
    padded = (rows_e + blk - 1) // blk * blk
    pend = jnp.cumsum(padded)
    pstart = pend - padded
    gstart = pstart[None, :] + jnp.cumsum(n8, axis=0) - n8
    n_blk = -(-(TOP_K * t + nt * N_EXPERTS * (RUN_ALIGN - 1) + N_EXPERTS * (blk - 1)) // blk)
    n_used = (pend[-1] // blk).astype(jnp.int32).reshape(1)
    blk_start = jnp.arange(n_blk, dtype=jnp.int32) * blk
    blk_e = jnp.minimum(jnp.sum((pend[None, :] <= blk_start[:, None]).astype(jnp.int32), axis=1), N_EXPERTS - 1)
    run_end = (gstart + n8)[:, blk_e]
    ilo = jnp.sum((run_end <= blk_start[None, :]).astype(jnp.int32), axis=0)
    rows_b = jnp.clip((pstart + rows_e)[blk_e] - blk_start, 0, blk)
    flat = lambda a: a.T.reshape(-1).astype(jnp.int32)
    i32 = lambda a: a.astype(jnp.int32)
    return (i32(blk_e), n_used, i32(ilo), i32(rows_b), flat(gstart), flat(n8), flat(lo))


def _moe_ple(xs_in, routed, ps, w1, b1, w2, b2, layer, g, b, plew, gatew, tms, blk):
    cnt = jnp.concatenate([r[2] for r in routed], axis=0)
    tables = _moe_tables(cnt, sum(x.shape[0] for x in xs_in), blk)
    ys = _experts(tables, [r[3] for r in routed], w1, b1, w2, b2, layer, blk)
    return [_combine(y, r[0], r[1], x, p, layer, g, b, plew, gatew, tm)
            for y, r, x, p, tm in zip(ys, routed, xs_in, ps, tms)]


_ROT_PERM = np.concatenate([np.arange(0, RET_DK, 2), np.arange(1, RET_DK, 2)])


def _lane_row(vals, offset):
    row = jnp.zeros((1, LANES), F32)
    return row.at[0, offset:offset + vals.shape[0]].set(vals)


def _even_mixer(x, nb, L, s5_re, s5_im, gdn_s, conv_s, W, tm):
    t = nb * L
    lc = L if L <= CHUNK else CHUNK
    u, qkv, z, ba = _proj_even(x, W['wu'], W['wqkv'], W['wz'], W['wba'], tm)
    yA3, h_new = _s5(u.reshape(nb, L, S5_WIDTH), _s5_state_in(s5_re[0].astype(F32), s5_im[0].astype(F32)),
                     W['bmat'], W['cmat'], W['acoef'], W['dskip'], W['wglu'], W['bglu'], lc)
    new_re, new_im = _s5_state_out(h_new)
    ctx8 = jnp.concatenate([jnp.zeros((nb, SUBLANES - (GDN_CONV - 1), GDN_QKV), F32), conv_s[0].astype(F32)], axis=1)
    yB3, new_gdn, cout = _gdn(qkv.reshape(nb, L, GDN_QKV), z.reshape(nb, L, -1), ba.reshape(nb, L, LANES),
                              ctx8, gdn_s[0].astype(F32), W['convw'], W['p1'], W['p2'], W['normw'], lc,
                              cp=min(8, L // lc), nbb=8)
    new_conv = cout[:, SUBLANES - (GDN_CONV - 1):, :]
    x, routed = _outproj_route([yA3.reshape(t, -1), yB3.reshape(t, -1)], [W['wout_a'], W['wout_b']], x,
                               W['ln1_g'][0], W['ln1_b'][0], W['rw'][0], W['rb'][0], tm)
    return x, routed, new_re[None], new_im[None], new_gdn[None], new_conv[None]


def _odd_mixer(x, nb, L, ret_s, pos0, W, tm):
    t = nb * L
    lc = L if L <= CHUNK else CHUNK
    pos = pos0 + jnp.arange(L, dtype=F32)
    freq = 1.0 / (ROPE_BASE ** jnp.linspace(0.0, 1.0, RET_DK // 2, dtype=F32))
    ang = pos[:, None] * freq[None]
    reps = max(1, tm // L)
    cos = jnp.tile(jnp.cos(ang), (reps, 1))
    sin = jnp.tile(jnp.sin(ang), (reps, 1))
    proj = _proj_odd(x, W['win_odd'], cos, sin, tm)
    half = RET_DK // 2
    r0 = ret_s[0].astype(F32).reshape(nb, RET_HEADS, half, 2, RET_DV).swapaxes(2, 3).reshape(
        nb, RET_HEADS, RET_DK, RET_DV)
    o3, r_new = _retention(proj.reshape(nb, L, -1), r0, lc, nbb=2, nc=min(4, L // lc))
    new_ret = r_new.reshape(nb, RET_HEADS, 2, half, RET_DV).swapaxes(2, 3).reshape(nb, RET_HEADS, RET_DK, RET_DV)
    x, routed = _outproj_route([o3.reshape(t, -1)], [W['wout_odd']], x, W['ln1_g'][1], W['ln1_b'][1],
                               W['rw'][1], W['rb'][1], tm)
    return x, routed, new_ret[None]


def kernel(x_prompt, x_sample, state_s5_re, state_s5_im, state_gdn, state_gdn_conv, state_ret, p_prompt, p_sample, w_in_even, s5_a_re, s5_a_im, s5_log_dt, s5_b_re, s5_b_im, s5_c_re, s5_c_im, s5_d, s5_w_glu, s5_b_glu, gdn_conv_w, gdn_a_log, gdn_dt_bias, gdn_norm_w, w_out_even, w_in_odd, w_out_odd, ln1_g, ln1_b, ln2_g, ln2_b, router_w, router_b, moe_w1, moe_b1, moe_w2, moe_b2, ple_w, ple_gate_w):
    o1 = S5_WIDTH
    o2 = o1 + GDN_QKV
    o3 = o2 + GDN_HEADS * GDN_DV
    win = w_in_even[0]
    bmat, cmat, acoef = _s5_weights(s5_a_re[0], s5_a_im[0], s5_log_dt[0], s5_b_re[0], s5_b_im[0],
                                    s5_c_re[0], s5_c_im[0])
    wodd = w_in_odd[0]
    nk = RET_HEADS * RET_DK
    perm_cols = lambda w: w.reshape(D_MODEL, RET_HEADS, RET_DK)[:, :, _ROT_PERM].reshape(D_MODEL, nk)
    W = dict(
        wu=win[:, :o1].astype(BF16), wqkv=win[:, o1:o2].astype(BF16), wz=win[:, o2:o3].astype(BF16),
        wba=jnp.pad(win[:, o3:], ((0, 0), (0, LANES - 2 * GDN_HEADS))).astype(BF16),
        bmat=bmat, cmat=cmat, acoef=acoef, dskip=s5_d[0][None], wglu=s5_w_glu[0].astype(BF16),
        bglu=s5_b_glu[0][None], convw=gdn_conv_w[0],
        p1=_lane_row(-jnp.exp(gdn_a_log[0]), GDN_HEADS), p2=_lane_row(gdn_dt_bias[0], GDN_HEADS),
        normw=gdn_norm_w[0][None],
        wout_a=w_out_even[0][:S5_WIDTH].astype(BF16), wout_b=w_out_even[0][S5_WIDTH:].astype(BF16),
        win_odd=jnp.concatenate([perm_cols(wodd[:, :nk]), perm_cols(wodd[:, nk:2 * nk]), wodd[:, 2 * nk:]],
                                axis=1).astype(BF16),
        wout_odd=w_out_odd[0].astype(BF16),
        ln1_g=ln1_g[:, None], ln1_b=ln1_b[:, None], ln2_g=ln2_g[:, None], ln2_b=ln2_b[:, None],
        rw=jnp.pad(router_w, ((0, 0), (0, 0), (0, LANES - N_EXPERTS))),
        rb=jnp.pad(router_b, ((0, 0), (0, LANES - N_EXPERTS)))[:, None],
        w1=moe_w1, b1=moe_b1[:, :, None], w2=moe_w2, b2=moe_b2[:, :, None],
        plew=ple_w.astype(BF16), gatew=ple_gate_w.astype(BF16),
    )
    bp, lp, _ = x_prompt.shape
    bs, ls, _ = x_sample.shape
    zeros = lambda *s: jnp.zeros(s, F32)
    shapes = [(bp, lp), (bs, ls)]
    tms = [PROMPT_TM, SAMPLE_TM]
    ps = [p_prompt, p_sample]
    xs = [x_prompt.reshape(bp * lp, D_MODEL).astype(F32), x_sample.reshape(bs * ls, D_MODEL).astype(F32)]
    even_states = [(zeros(1, bp, S5_GROUPS, S5_STATE), zeros(1, bp, S5_GROUPS, S5_STATE),
                    zeros(1, bp, GDN_HEADS, GDN_DK, GDN_DV), zeros(1, bp, GDN_CONV - 1, GDN_QKV)),
                   (state_s5_re, state_s5_im, state_gdn, state_gdn_conv)]
    ret_states = [zeros(1, bp, RET_HEADS, RET_DK, RET_DV), state_ret]
    pos0s = [0.0, float(PAST_LEN)]

    def moe(mixed, layer):
        return _moe_ple([m[0] for m in mixed], [m[1] for m in mixed], [p.reshape(DEPTH, -1, PLE_DIM) for p in ps],
                        W['w1'], W['b1'][layer], W['w2'], W['b2'][layer], layer, W['ln2_g'][layer],
                        W['ln2_b'][layer], W['plew'][layer], W['gatew'][layer], tms, blk=EXPERT_BLK)

    even = [_even_mixer(x, nb, L, *st, W, tm) for x, (nb, L), st, tm in zip(xs, shapes, even_states, tms)]
    xs = moe(even, 0)
    odd = [_odd_mixer(x, nb, L, st, pos0, W, tm)
           for x, (nb, L), st, pos0, tm in zip(xs, shapes, ret_states, pos0s, tms)]
    xs = moe(odd, 1)
    dp = x_prompt.dtype
    y_p, y_s = xs[0].reshape(bp, lp, D_MODEL), xs[1].reshape(bs, ls, D_MODEL)
    (_, _, p_re, p_im, p_gdn, p_conv), (_, _, s_re, s_im, s_gdn, s_conv) = even
    p_ret, s_ret = odd[0][2], odd[1][2]
    return (y_p.astype(dp), y_s.astype(x_sample.dtype),
            p_re.astype(dp), p_im.astype(dp), p_gdn.astype(dp), p_conv.astype(dp), p_ret.astype(dp),
            s_re.astype(state_s5_re.dtype), s_im.astype(state_s5_im.dtype), s_gdn.astype(state_gdn.dtype),
            s_conv.astype(state_gdn_conv.dtype), s_ret.astype(state_ret.dtype))
```
